```python
import jax, jax.numpy as jnp
from jax import lax
import numpy as np

D_MODEL = 1024
BATCH = 2
SEQ = 8192
DEPTH = 1

MEM_LEN = 256
HEAD_DIM = 64
LRU_WIDTH = D_MODEL // 2
LRU_BLOCKS = 8
LRU_BLOCK_W = LRU_WIDTH // LRU_BLOCKS
CONV_W = 4
LRU_C = 8.0
MOBA_HEADS = 4
MOBA_WIDTH = MOBA_HEADS * HEAD_DIM
MOBA_BLOCK = 256
MOBA_TOPK = 3
Q_CHUNK = 128
MEM_HEADS = 4
MEM_WIDTH = MEM_HEADS * HEAD_DIM
MIX_WIDTH = LRU_WIDTH + MOBA_WIDTH + MEM_WIDTH
IN_WIDTH = 2 * LRU_WIDTH + 3 * MOBA_WIDTH + MEM_WIDTH
D_FF = ((8 * D_MODEL // 3 + 127) // 128) * 128
NORM_EPS = 1e-6

kernel_name = "hymba_rglru_moba_memxattn_macaron"


def rms_norm(x, g):
    xf = x.astype(jnp.float32)
    y = xf * lax.rsqrt(jnp.mean(xf * xf, axis=-1, keepdims=True) + NORM_EPS)
    return (y * g.astype(jnp.float32)).astype(x.dtype)


def swiglu(x, w_in, w_out):
    a, b = jnp.split(x @ w_in, 2, axis=-1)
    return (jax.nn.silu(a) * b) @ w_out


def alibi_slopes(n_heads):
    return 2.0 ** (-8.0 * jnp.arange(1, n_heads + 1, dtype=jnp.float32) / n_heads)


def split_heads(t, n_heads):
    b, s, _ = t.shape
    return t.reshape(b, s, n_heads, HEAD_DIM).transpose(0, 2, 1, 3)


def merge_heads(t):
    b, h, s, d = t.shape
    return t.transpose(0, 2, 1, 3).reshape(b, s, h * d)


def causal_depthwise_conv(x, w, bias):
    c = x.shape[-1]
    y = lax.conv_general_dilated(
        x, w[:, None, :].astype(x.dtype), window_strides=(1,), padding=[(CONV_W - 1, 0)],
        dimension_numbers=("NWC", "WIO", "NWC"), feature_group_count=c)
    return y + bias.astype(x.dtype)


def rg_lru(xb, a_w, a_b, x_w, x_b, lam):
    b, s, c = xb.shape
    xf = xb.astype(jnp.float32)
    xblk = xf.reshape(b, s, LRU_BLOCKS, LRU_BLOCK_W)
    r = jax.nn.sigmoid(jnp.einsum("bsnc,ncd->bsnd", xblk, a_w.astype(jnp.float32)) + a_b.astype(jnp.float32)).reshape(b, s, c)
    i = jax.nn.sigmoid(jnp.einsum("bsnc,ncd->bsnd", xblk, x_w.astype(jnp.float32)) + x_b.astype(jnp.float32)).reshape(b, s, c)
    log_a = -LRU_C * r * jax.nn.softplus(-lam.astype(jnp.float32))
    a = jnp.exp(log_a)
    mult = jnp.sqrt(-jnp.expm1(2.0 * log_a))
    mult = jnp.where(jnp.arange(s)[None, :, None] == 0, 1.0, mult)
    bt = mult * i * xf

    def combine(left, right):
        a1, b1 = left
        a2, b2 = right
        return a1 * a2, a2 * b1 + b2

    _, h = lax.associative_scan(combine, (a, bt), axis=1)
    return h


def moba_attention(q, k, v):
    b, h, s, hd = q.shape
    n_blk = -(-s // MOBA_BLOCK)
    s_pad = n_blk * MOBA_BLOCK
    pad = s_pad - s
    if pad:
        widths = ((0, 0), (0, 0), (0, pad), (0, 0))
        q, k, v = jnp.pad(q, widths), jnp.pad(k, widths), jnp.pad(v, widths)
    kb = k.reshape(b, h, n_blk, MOBA_BLOCK, hd)
    vb = v.reshape(b, h, n_blk, MOBA_BLOCK, hd)
    k_mean = jnp.mean(kb.astype(jnp.float32), axis=3)
    gate = jnp.einsum("bhsd,bhnd->bhsn", q.astype(jnp.float32), k_mean)
    q_blk = jnp.arange(s_pad) // MOBA_BLOCK
    fully_past = jnp.arange(n_blk)[None, :] < q_blk[:, None]
    gate = jnp.where(fully_past, gate, -jnp.inf)
    n_sel = min(MOBA_TOPK, n_blk)
    gate_top, sel = lax.top_k(gate, n_sel)
    sel_ok = gate_top > -jnp.inf
    slopes = alibi_slopes(h)
    scale = hd ** -0.5
    offs = jnp.arange(MOBA_BLOCK)
    gather_blocks = jax.vmap(jax.vmap(lambda blocks, idx: blocks[idx]))

    def chunk(c):
        start = c * Q_CHUNK
        qc = lax.dynamic_slice_in_dim(q, start, Q_CHUNK, axis=2)
        selc = lax.dynamic_slice_in_dim(sel, start, Q_CHUNK, axis=2)
        okc = lax.dynamic_slice_in_dim(sel_ok, start, Q_CHUNK, axis=2)
        own = start // MOBA_BLOCK
        k_own = lax.dynamic_index_in_dim(kb, own, axis=2, keepdims=False)
        v_own = lax.dynamic_index_in_dim(vb, own, axis=2, keepdims=False)
        k_sel = gather_blocks(kb, selc)
        v_sel = gather_blocks(vb, selc)
        t_q = start + jnp.arange(Q_CHUNK)
        s_own = jnp.einsum("bhqd,bhkd->bhqk", qc, k_own).astype(jnp.float32) * scale
        d_own = t_q[:, None] - (own * MOBA_BLOCK + offs)[None, :]
        s_own = jnp.where(d_own >= 0, s_own - slopes[:, None, None] * d_own.astype(jnp.float32), -jnp.inf)
        s_sel = jnp.einsum("bhqd,bhqjkd->bhqjk", qc, k_sel).astype(jnp.float32) * scale
        d_sel = t_q[None, None, :, None, None] - (selc[..., None] * MOBA_BLOCK + offs)
        s_sel = jnp.where(okc[..., None], s_sel - slopes[None, :, None, None, None] * d_sel.astype(jnp.float32), -jnp.inf)
        scores = jnp.concatenate([s_sel.reshape(b, h, Q_CHUNK, n_sel * MOBA_BLOCK), s_own], axis=-1)
        p = jax.nn.softmax(scores, axis=-1)
        p_sel = p[..., :n_sel * MOBA_BLOCK].reshape(b, h, Q_CHUNK, n_sel, MOBA_BLOCK).astype(v.dtype)
        p_own = p[..., n_sel * MOBA_BLOCK:].astype(v.dtype)
        return (jnp.einsum("bhqjk,bhqjkd->bhqd", p_sel, v_sel)
                + jnp.einsum("bhqk,bhkd->bhqd", p_own, v_own))

    out = lax.map(chunk, jnp.arange(s_pad // Q_CHUNK))
    out = out.transpose(1, 2, 0, 3, 4).reshape(b, h, s_pad, hd)
    return out[:, :, :s]


def memory_attention(q, k, v):
    scores = jnp.einsum("bhsd,bhmd->bhsm", q, k).astype(jnp.float32) * (HEAD_DIM ** -0.5)
    p = jax.nn.softmax(scores, axis=-1)
    return jnp.einsum("bhsm,bhmd->bhsd", p.astype(v.dtype), v)


def setup_inputs(seed: int = 0) -> dict:
    key = jax.random.key(seed)
    ks = jax.random.split(key, 24)

    def normal(k, shape, scale):
        return jax.random.normal(k, shape, jnp.float32) * scale

    def gain(k, n):
        return 1.0 + 0.02 * jax.random.normal(k, (DEPTH, n), jnp.float32)

    u = jax.random.uniform(ks[12], (DEPTH, LRU_WIDTH), jnp.float32, minval=0.9, maxval=0.999)
    a0 = u ** (1.0 / LRU_C)
    lam = jnp.log(a0) - jnp.log1p(-a0)
    return {
        "x": normal(ks[0], (BATCH, SEQ, D_MODEL), 1.0),
        "mem": normal(ks[1], (BATCH, MEM_LEN, D_MODEL), 1.0),
        "ffn1_norm": gain(ks[2], D_MODEL),
        "ffn1_w_in": normal(ks[3], (DEPTH, D_MODEL, 2 * D_FF), D_MODEL ** -0.5),
        "ffn1_w_out": normal(ks[4], (DEPTH, D_FF, D_MODEL), D_FF ** -0.5),
        "mix_norm": gain(ks[5], D_MODEL),
        "mem_norm": gain(ks[6], D_MODEL),
        "w_in": normal(ks[7], (DEPTH, D_MODEL, IN_WIDTH), D_MODEL ** -0.5),
        "lru_conv_w": normal(ks[8], (DEPTH, CONV_W, LRU_WIDTH), CONV_W ** -0.5),
        "lru_conv_b": normal(ks[9], (DEPTH, LRU_WIDTH), 0.01),
        "lru_a_w": normal(ks[10], (DEPTH, LRU_BLOCKS, LRU_BLOCK_W, LRU_BLOCK_W), LRU_BLOCK_W ** -0.5),
        "lru_a_b": normal(ks[11], (DEPTH, LRU_BLOCKS, LRU_BLOCK_W), 0.01),
        "lru_x_w": normal(ks[13], (DEPTH, LRU_BLOCKS, LRU_BLOCK_W, LRU_BLOCK_W), LRU_BLOCK_W ** -0.5),
        "lru_x_b": normal(ks[14], (DEPTH, LRU_BLOCKS, LRU_BLOCK_W), 0.01),
        "lru_lambda": lam,
        "moba_q_norm": gain(ks[15], HEAD_DIM),
        "moba_k_norm": gain(ks[16], HEAD_DIM),
        "mem_w_kv": normal(ks[17], (DEPTH, D_MODEL, 2 * MEM_WIDTH), D_MODEL ** -0.5),
        "mem_q_norm": gain(ks[18], HEAD_DIM),
        "mem_k_norm": gain(ks[19], HEAD_DIM),
        "w_out": normal(ks[20], (DEPTH, MIX_WIDTH, D_MODEL), MIX_WIDTH ** -0.5),
        "ffn2_norm": gain(ks[21], D_MODEL),
        "ffn2_w_in": normal(ks[22], (DEPTH, D_MODEL, 2 * D_FF), D_MODEL ** -0.5),
        "ffn2_w_out": normal(ks[23], (DEPTH, D_FF, D_MODEL), D_FF ** -0.5),
    }


def reference(x, mem, ffn1_norm, ffn1_w_in, ffn1_w_out, mix_norm, mem_norm, w_in,
              lru_conv_w, lru_conv_b, lru_a_w, lru_a_b, lru_x_w, lru_x_b, lru_lambda,
              moba_q_norm, moba_k_norm, mem_w_kv, mem_q_norm, mem_k_norm, w_out,
              ffn2_norm, ffn2_w_in, ffn2_w_out):
    split_at = np.cumsum([LRU_WIDTH, LRU_WIDTH, MOBA_WIDTH, MOBA_WIDTH, MOBA_WIDTH])
    for l in range(DEPTH):
        x = x + 0.5 * swiglu(rms_norm(x, ffn1_norm[l]), ffn1_w_in[l], ffn1_w_out[l])

        u = rms_norm(x, mix_norm[l]) @ w_in[l]
        lru_x, lru_g, mq, mk, mv, cq = jnp.split(u, split_at, axis=-1)

        xb = causal_depthwise_conv(lru_x, lru_conv_w[l], lru_conv_b[l])
        h_lru = rg_lru(xb, lru_a_w[l], lru_a_b[l], lru_x_w[l], lru_x_b[l], lru_lambda[l])
        y_lru = (h_lru * jax.nn.gelu(lru_g.astype(jnp.float32))).astype(x.dtype)

        q = rms_norm(split_heads(mq, MOBA_HEADS), moba_q_norm[l])
        k = rms_norm(split_heads(mk, MOBA_HEADS), moba_k_norm[l])
        v = split_heads(mv, MOBA_HEADS)
        y_moba = merge_heads(moba_attention(q, k, v))

        mkv = rms_norm(mem, mem_norm[l]) @ mem_w_kv[l]
        mem_k, mem_v = jnp.split(mkv, 2, axis=-1)
        qc = rms_norm(split_heads(cq, MEM_HEADS), mem_q_norm[l])
        kc = rms_norm(split_heads(mem_k, MEM_HEADS), mem_k_norm[l])
        y_mem = merge_heads(memory_attention(qc, kc, split_heads(mem_v, MEM_HEADS)))

        mixed = jnp.concatenate([y_lru, y_moba.astype(x.dtype), y_mem.astype(x.dtype)], axis=-1)
        x = x + mixed @ w_out[l]

        x = x + 0.5 * swiglu(rms_norm(x, ffn2_norm[l]), ffn2_w_in[l], ffn2_w_out[l])
    return x
```

```python
import functools
import math

import jax
import jax.numpy as jnp
from jax import lax
from jax.experimental import pallas as pl
from jax.experimental.pallas import tpu as pltpu

F32 = jnp.float32
BF16 = jnp.bfloat16

HEAD_DIM = 64
CONV_W = 4
LRU_C = 8.0
MOBA_HEADS = 4
MOBA_BLOCK = 256
MOBA_TOPK = 3
MEM_HEADS = 4
NORM_EPS = 1e-6
LOG2E = 1.4426950408889634
NEG_BIG = -1e30

V7X_MXU_DIM = 256
LANES = 128
V7X_VMEM_BYTES = 64 * 1024 * 1024

AUG_W = 128
AUG_ALIBI = 64
AUG_MASK = 80

TOKEN_TILE = 512
LRU_TILE = 512
FFN_CHUNK = 256


def _vmem_limit(resident_bytes):
    return int(min(V7X_VMEM_BYTES - 8 * 1024 * 1024, resident_bytes + 24 * 1024 * 1024))


def _const_spec(shape):
    n = len(shape)
    return pl.BlockSpec(shape, lambda *_: (0,) * n, pipeline_mode=pl.Buffered(1))


def _rms_norm(x, g):
    ms = jnp.mean(x * x, axis=-1, keepdims=True)
    return x * lax.rsqrt(ms + NORM_EPS) * g


def _group_rms_norm(u, bd, g, group):
    sq = u * u
    hi = sq.astype(BF16)
    lo = (sq - hi.astype(F32)).astype(BF16)
    parts = []
    for c in range(u.shape[1] // V7X_MXU_DIM):
        sl = slice(c * V7X_MXU_DIM, (c + 1) * V7X_MXU_DIM)
        parts.append(jnp.dot(hi[:, sl], bd, preferred_element_type=F32)
                     + jnp.dot(lo[:, sl], bd, preferred_element_type=F32))
    ss = parts[0] if len(parts) == 1 else jnp.concatenate(parts, axis=1)
    return u * lax.rsqrt(ss * (1.0 / group) + NORM_EPS) * g


def _swiglu_half_step(x, g_ref, w_in_ref, w_out_ref, act_ref):
    d_ff = w_out_ref.shape[0]
    xn = _rms_norm(x, g_ref[...]).astype(BF16)
    for c in range(d_ff // FFN_CHUNK):
        lo = c * FFN_CHUNK
        a = jnp.dot(xn, w_in_ref[:, lo:lo + FFN_CHUNK], preferred_element_type=F32)
        b = jnp.dot(xn, w_in_ref[:, d_ff + lo:d_ff + lo + FFN_CHUNK], preferred_element_type=F32)
        act_ref[:, lo:lo + FFN_CHUNK] = (a * jax.nn.sigmoid(a) * b).astype(BF16)
    y = jnp.dot(act_ref[...], w_out_ref[...], preferred_element_type=F32)
    return x + 0.5 * y


def _ffn_kernel(x_ref, g_ref, w_in_ref, w_out_ref, o_ref, act_ref):
    o_ref[...] = _swiglu_half_step(x_ref[...], g_ref, w_in_ref, w_out_ref, act_ref)


def _ffn(x, g, w_in, w_out):
    n, d = x.shape
    d_ff = w_out.shape[0]
    tm = TOKEN_TILE
    resident = (w_in.size + w_out.size) * 2
    return pl.pallas_call(
        _ffn_kernel,
        grid=(n // tm,),
        in_specs=[pl.BlockSpec((tm, d), lambda i: (i, 0)),
                  _const_spec((1, d)), _const_spec(w_in.shape), _const_spec(w_out.shape)],
        out_specs=pl.BlockSpec((tm, d), lambda i: (i, 0)),
        out_shape=jax.ShapeDtypeStruct((n, d), F32),
        scratch_shapes=[pltpu.VMEM((tm, d_ff), BF16)],
        compiler_params=pltpu.CompilerParams(dimension_semantics=("arbitrary",),
                                             vmem_limit_bytes=_vmem_limit(resident)),
        name="ffn1",
    )(x, g, w_in, w_out)


def _out_ffn_kernel(x_ref, ylru_ref, ymoba_ref, ymem_ref, wo_ref, g_ref, w_in_ref, w_out_ref,
                    o_ref, act_ref):
    n_lru = ylru_ref.shape[1]
    n_moba = ymoba_ref.shape[1]
    x2 = (x_ref[...]
          + jnp.dot(ylru_ref[...], wo_ref[0:n_lru, :], preferred_element_type=F32)
          + jnp.dot(ymoba_ref[...], wo_ref[n_lru:n_lru + n_moba, :], preferred_element_type=F32)
          + jnp.dot(ymem_ref[...], wo_ref[n_lru + n_moba:, :], preferred_element_type=F32))
    o_ref[...] = _swiglu_half_step(x2, g_ref, w_in_ref, w_out_ref, act_ref)


def _out_ffn(x1, y_lru, y_moba, y_mem, w_o, g, w_in, w_out):
    n, d = x1.shape
    d_ff = w_out.shape[0]
    tm = TOKEN_TILE
    resident = (w_in.size + w_out.size + w_o.size) * 2

    def tile(w):
        return pl.BlockSpec((tm, w), lambda i: (i, 0))

    return pl.pallas_call(
        _out_ffn_kernel,
        grid=(n // tm,),
        in_specs=[tile(d), tile(y_lru.shape[1]), tile(y_moba.shape[1]), tile(y_mem.shape[1]),
                  _const_spec(w_o.shape), _const_spec((1, d)),
                  _const_spec(w_in.shape), _const_spec(w_out.shape)],
        out_specs=tile(d),
        out_shape=jax.ShapeDtypeStruct((n, d), F32),
        scratch_shapes=[pltpu.VMEM((tm, d_ff), BF16)],
        compiler_params=pltpu.CompilerParams(dimension_semantics=("arbitrary",),
                                             vmem_limit_bytes=_vmem_limit(resident)),
        name="out_ffn2",
    )(x1, y_lru, y_moba, y_mem, w_o, g, w_in, w_out)


def _mem_kv_kernel(mem_ref, g_ref, w_ref, bd_ref, gk_ref, kc_ref, vc_ref):
    width = kc_ref.shape[1]
    mn = _rms_norm(mem_ref[...], g_ref[...]).astype(BF16)
    kv = jnp.dot(mn, w_ref[...], preferred_element_type=F32)
    kc_ref[...] = _group_rms_norm(kv[:, :width], bd_ref[...], gk_ref[...], HEAD_DIM).astype(BF16)
    vc_ref[...] = kv[:, width:].astype(BF16)


def _mem_kv(mem2d, g, w_kv, bd64, gk):
    m = mem2d.shape[0]
    width = w_kv.shape[1] // 2
    return pl.pallas_call(
        _mem_kv_kernel,
        out_shape=(jax.ShapeDtypeStruct((m, width), BF16), jax.ShapeDtypeStruct((m, width), BF16)),
        name="mem_kv",
    )(mem2d, g, w_kv, bd64, gk)


def _in_proj_kernel(x_ref, g_ref, w_lru_ref, w_q_ref, w_k_ref, w_v_ref, w_c_ref,
                    gq_ref, gk_ref, gc_ref, bd64_ref, bd128_ref, alibi_ref,
                    lrux_ref, lrug_ref, qt_ref, kaug_ref, vt_ref, kmean_ref, cq_ref, *, seq):
    tm = x_ref.shape[0]
    n_lru = lrux_ref.shape[1]
    xn = _rms_norm(x_ref[...], g_ref[...]).astype(BF16)

    lrux_ref[...] = jnp.dot(xn, w_lru_ref[:, :n_lru], preferred_element_type=F32)
    lrug_ref[...] = jnp.dot(xn, w_lru_ref[:, n_lru:], preferred_element_type=F32)

    cq = jnp.dot(xn, w_c_ref[...], preferred_element_type=F32)
    cq = _group_rms_norm(cq, bd64_ref[...], gc_ref[...], HEAD_DIM)
    cq_ref[...] = (cq * (HEAD_DIM ** -0.5 * LOG2E)).astype(BF16)

    q = jnp.dot(xn, w_q_ref[...], preferred_element_type=F32)
    q = _group_rms_norm(q, bd64_ref[...], gq_ref[...], HEAD_DIM)
    v = jnp.dot(xn, w_v_ref[...], preferred_element_type=F32)
    for r in range(tm // MOBA_BLOCK):
        rows = slice(r * MOBA_BLOCK, (r + 1) * MOBA_BLOCK)
        qt_ref[r] = q[rows, :].T
        vt_ref[r] = v[rows, :].T.astype(BF16)

    k = jnp.dot(xn, w_k_ref[...], preferred_element_type=F32)
    k = _group_rms_norm(k, bd128_ref[...], gk_ref[...], HEAD_DIM)
    kmean_ref[...] = jnp.zeros(kmean_ref.shape, F32)
    for r in range(tm // MOBA_BLOCK):
        rows = slice(r * MOBA_BLOCK, (r + 1) * MOBA_BLOCK)
        kmean_ref[0, r:r + 1, :] = jnp.sum(k[rows, :], axis=0, keepdims=True) * (1.0 / MOBA_BLOCK)

    shape = k.shape
    row = lax.broadcasted_iota(jnp.int32, shape, 0)
    lane = lax.broadcasted_iota(jnp.int32, shape, 1) % AUG_W
    t = (pl.program_id(0) * tm + row) % seq
    bias = t.astype(F32) * alibi_ref[...]
    b_hi = bias.astype(BF16).astype(F32)
    b_mid = (bias - b_hi).astype(BF16).astype(F32)
    b_lo = bias - b_hi - b_mid
    pieces = jnp.where(lane == AUG_ALIBI, b_hi, jnp.where(lane == AUG_ALIBI + 1, b_mid, b_lo))
    onehot = jnp.where(lane - AUG_MASK == t // MOBA_BLOCK, 1.0, 0.0)
    is_alibi = (lane >= AUG_ALIBI) & (lane < AUG_ALIBI + 3)
    kaug_ref[...] = (k + jnp.where(is_alibi, pieces, onehot)).astype(BF16)


def _in_proj(x1, g, w_lru, w_q, w_k, w_v, w_c, gq, gk, gc, bd64, bd128, alibi, seq):
    n, d = x1.shape
    tm = TOKEN_TILE
    rb = tm // MOBA_BLOCK
    nblk = n // MOBA_BLOCK
    n_lru = w_lru.shape[1] // 2
    wq = w_q.shape[1]
    wk = w_k.shape[1]
    consts = (g, w_lru, w_q, w_k, w_v, w_c, gq, gk, gc, bd64, bd128, alibi)
    resident = sum(c.size * c.dtype.itemsize for c in consts)
    return pl.pallas_call(
        functools.partial(_in_proj_kernel, seq=seq),
        grid=(n // tm,),
        in_specs=[pl.BlockSpec((tm, d), lambda i: (i, 0))] + [_const_spec(c.shape) for c in consts],
        out_specs=[pl.BlockSpec((tm, n_lru), lambda i: (i, 0)),
                   pl.BlockSpec((tm, n_lru), lambda i: (i, 0)),
                   pl.BlockSpec((rb, wq, MOBA_BLOCK), lambda i: (i, 0, 0)),
                   pl.BlockSpec((tm, wk), lambda i: (i, 0)),
                   pl.BlockSpec((rb, wq, MOBA_BLOCK), lambda i: (i, 0, 0)),
                   pl.BlockSpec((1, 8, wk), lambda i: (i, 0, 0)),
                   pl.BlockSpec((tm, wq), lambda i: (i, 0))],
        out_shape=(jax.ShapeDtypeStruct((n, n_lru), F32),
                   jax.ShapeDtypeStruct((n, n_lru), F32),
                   jax.ShapeDtypeStruct((nblk, wq, MOBA_BLOCK), F32),
                   jax.ShapeDtypeStruct((n, wk), BF16),
                   jax.ShapeDtypeStruct((nblk, wq, MOBA_BLOCK), BF16),
                   jax.ShapeDtypeStruct((n // tm, 8, wk), F32),
                   jax.ShapeDtypeStruct((n, wq), BF16)),
        compiler_params=pltpu.CompilerParams(dimension_semantics=("arbitrary",),
                                             vmem_limit_bytes=_vmem_limit(resident)),
        name="in_proj",
    )(x1, *consts)


def _gelu_tanh(x):
    c = math.sqrt(2.0 / math.pi)
    return 0.5 * x * (1.0 + jnp.tanh(c * (x + 0.044715 * (x * x * x))))


def _lru_kernel(x_ref, g_ref, cw_ref, cb_ref, wa_ref, ba_ref, wx_ref, bx_ref, lam_ref,
                o_ref, xpad_ref, a_ref, b_ref, hl_ref, p_ref, carry_ref):
    tl, c = x_ref.shape
    seg = tl // 8
    half = V7X_MXU_DIM
    ti = pl.program_id(1)

    @pl.when(ti == 0)
    def _():
        xpad_ref[0:8, :] = jnp.zeros((8, c), F32)
        carry_ref[...] = jnp.zeros(carry_ref.shape, F32)

    xpad_ref[8:8 + tl, :] = x_ref[...]
    xb = jnp.broadcast_to(cb_ref[...], (tl, c))
    for j in range(CONV_W):
        off = 8 - (CONV_W - 1) + j
        xb = xb + cw_ref[j:j + 1, :] * xpad_ref[off:off + tl, :]
    xpad_ref[0:8, :] = x_ref[tl - 8:tl, :]

    xbb = xb.astype(BF16)

    def gate(w_ref, bias_ref):
        z = jnp.concatenate(
            [jnp.dot(xbb[:, :half], w_ref[0], preferred_element_type=F32),
             jnp.dot(xbb[:, half:], w_ref[1], preferred_element_type=F32)], axis=1)
        return jax.nn.sigmoid(z + bias_ref[...])

    r = gate(wa_ref, ba_ref)
    i = gate(wx_ref, bx_ref)
    neg_lam = -lam_ref[...]
    softplus = jnp.maximum(neg_lam, 0.0) + jnp.log1p(jnp.exp(-jnp.abs(neg_lam)))
    log_a = (-LRU_C) * r * softplus
    a = jnp.exp(log_a)
    mult = jnp.sqrt(1.0 - a * a)
    row = lax.broadcasted_iota(jnp.int32, (tl, c), 0)
    start_row = jnp.where(ti == 0, 0, -1)
    mult = jnp.where(row == start_row, 1.0, mult)
    bt = mult * i * xb
    n_lt = c // LANES
    for k in range(n_lt):
        a_ref[k] = a[:, k * LANES:(k + 1) * LANES]
        b_ref[k] = bt[:, k * LANES:(k + 1) * LANES]

    h = [jnp.zeros((8, LANES), F32)] * n_lt
    p = [jnp.ones((8, LANES), F32)] * n_lt
    for j in range(seg):
        idx = pl.ds(j, 8, stride=seg)
        for k in range(n_lt):
            aj = a_ref[k, idx, :]
            h[k] = aj * h[k] + b_ref[k, idx, :]
            p[k] = aj * p[k]
            hl_ref[k, idx, :] = h[k]
            p_ref[k, idx, :] = p[k]

    for k in range(n_lt):
        lanes = slice(k * LANES, (k + 1) * LANES)
        carry = carry_ref[:, lanes]
        for s in range(8):
            rows = slice(s * seg, (s + 1) * seg)
            hs = hl_ref[k, rows, :] + p_ref[k, rows, :] * carry
            o_ref[rows, lanes] = (hs * _gelu_tanh(g_ref[rows, lanes])).astype(o_ref.dtype)
            carry = hs[seg - 1:seg, :]
        carry_ref[:, lanes] = carry


def _lru(lru_x, lru_g, cw, cb, wa, ba, wx, bx, lam, batch, seq):
    n, c = lru_x.shape
    tl = LRU_TILE
    nt = seq // tl
    consts = (cw, cb, wa, ba, wx, bx, lam)
    tile = pl.BlockSpec((tl, c), lambda b, t: (b * nt + t, 0))
    return pl.pallas_call(
        _lru_kernel,
        grid=(batch, nt),
        in_specs=[tile, tile] + [_const_spec(w.shape) for w in consts],
        out_specs=tile,
        out_shape=jax.ShapeDtypeStruct((n, c), BF16),
        scratch_shapes=[pltpu.VMEM((tl + 8, c), F32)]
        + [pltpu.VMEM((c // LANES, tl, LANES), F32)] * 4
        + [pltpu.VMEM((1, c), F32)],
        compiler_params=pltpu.CompilerParams(dimension_semantics=("arbitrary", "arbitrary")),
        name="lru",
    )(lru_x, lru_g, *consts)


def _moba_kernel(qt_ref, kaug_ref, vt_ref, kmean_ref, o_ref, qaug_ref, m_ref, l_ref, acc_ref, ot_ref):
    qi = pl.program_id(1)
    nblk = kmean_ref.shape[1]
    blk = MOBA_BLOCK

    blk_id = lax.broadcasted_iota(jnp.int32, (nblk, blk), 0).astype(F32)
    qi_f = qi.astype(F32)
    for h in range(MOBA_HEADS):
        qt = qt_ref[0, 0, h * HEAD_DIM:(h + 1) * HEAD_DIM, :]
        gate = jnp.dot(kmean_ref[0, :, h * AUG_W:h * AUG_W + HEAD_DIM], qt,
                       preferred_element_type=F32, precision=lax.Precision.HIGHEST)
        gate = jnp.where(blk_id < qi_f, gate, -jnp.inf)
        keep = blk_id == qi_f
        for _ in range(MOBA_TOPK):
            best = jnp.max(gate, axis=0, keepdims=True)
            first = jnp.min(jnp.where(gate == best, blk_id, float(nblk)), axis=0, keepdims=True)
            pick = (blk_id == first) & (best > -jnp.inf)
            keep = keep | pick
            gate = jnp.where(pick, -jnp.inf, gate)
        qaug_ref[h, 0:HEAD_DIM, :] = (qt * (HEAD_DIM ** -0.5 * LOG2E)).astype(BF16)
        sub = lax.broadcasted_iota(jnp.int32, (AUG_MASK - AUG_ALIBI, blk), 0)
        qaug_ref[h, AUG_ALIBI:AUG_MASK, :] = jnp.where(sub < 3, 1.0, 0.0).astype(BF16)
        qaug_ref[h, AUG_MASK:AUG_MASK + nblk, :] = jnp.where(keep, 0.0, NEG_BIG).astype(BF16)
        qaug_ref[h, AUG_MASK + nblk:, :] = jnp.zeros((AUG_W - AUG_MASK - nblk, blk), BF16)

    kk = lax.broadcasted_iota(jnp.int32, (blk, blk), 0)
    qq = lax.broadcasted_iota(jnp.int32, (blk, blk), 1)
    causal = kk <= qq

    for h in range(MOBA_HEADS):
        s = jnp.dot(kaug_ref[0, qi, :, h * AUG_W:(h + 1) * AUG_W], qaug_ref[h],
                    preferred_element_type=F32)
        s = jnp.where(causal, s, NEG_BIG)
        m = jnp.max(s, axis=0, keepdims=True)
        p = jnp.exp2(s - m)
        m_ref[h] = m
        l_ref[h] = jnp.sum(p, axis=0, keepdims=True)
        acc_ref[h] = jnp.dot(vt_ref[0, qi, h * HEAD_DIM:(h + 1) * HEAD_DIM, :], p.astype(BF16),
                             preferred_element_type=F32)

    def past_block(j, carry):
        for h in range(MOBA_HEADS):
            s = jnp.dot(kaug_ref[0, j, :, h * AUG_W:(h + 1) * AUG_W], qaug_ref[h],
                        preferred_element_type=F32)
            m_old = m_ref[h]
            m_new = jnp.maximum(m_old, jnp.max(s, axis=0, keepdims=True))
            alpha = jnp.exp2(m_old - m_new)
            p = jnp.exp2(s - m_new)
            m_ref[h] = m_new
            l_ref[h] = alpha * l_ref[h] + jnp.sum(p, axis=0, keepdims=True)
            acc_ref[h] = alpha * acc_ref[h] + jnp.dot(
                vt_ref[0, j, h * HEAD_DIM:(h + 1) * HEAD_DIM, :], p.astype(BF16),
                preferred_element_type=F32)
        return carry

    lax.fori_loop(0, qi, past_block, 0)

    for h in range(MOBA_HEADS):
        ot_ref[h * HEAD_DIM:(h + 1) * HEAD_DIM, :] = acc_ref[h] / l_ref[h]
    o_ref[...] = ot_ref[...].T.astype(o_ref.dtype)


def _moba(qt, kaug, vt, kmean, batch, seq):
    nblk = seq // MOBA_BLOCK
    width = MOBA_HEADS * HEAD_DIM
    qt = qt.reshape(batch, nblk, width, MOBA_BLOCK)
    kaug = kaug.reshape(batch, nblk, MOBA_BLOCK, MOBA_HEADS * AUG_W)
    vt = vt.reshape(batch, nblk, width, MOBA_BLOCK)
    resident = 2 * (kaug.size // batch + vt.size // batch) * 2
    return pl.pallas_call(
        _moba_kernel,
        grid=(batch, nblk),
        in_specs=[pl.BlockSpec((1, 1, width, MOBA_BLOCK), lambda b, i: (b, i, 0, 0)),
                  pl.BlockSpec((1, nblk, MOBA_BLOCK, MOBA_HEADS * AUG_W), lambda b, i: (b, 0, 0, 0)),
                  pl.BlockSpec((1, nblk, width, MOBA_BLOCK), lambda b, i: (b, 0, 0, 0)),
                  pl.BlockSpec((1, nblk, MOBA_HEADS * AUG_W), lambda b, i: (b, 0, 0))],
        out_specs=pl.BlockSpec((MOBA_BLOCK, width), lambda b, i: (b * nblk + i, 0)),
        out_shape=jax.ShapeDtypeStruct((batch * seq, width), BF16),
        scratch_shapes=[pltpu.VMEM((MOBA_HEADS, AUG_W, MOBA_BLOCK), BF16),
                        pltpu.VMEM((MOBA_HEADS, 1, MOBA_BLOCK), F32),
                        pltpu.VMEM((MOBA_HEADS, 1, MOBA_BLOCK), F32),
                        pltpu.VMEM((MOBA_HEADS, HEAD_DIM, MOBA_BLOCK), F32),
                        pltpu.VMEM((width, MOBA_BLOCK), F32)],
        compiler_params=pltpu.CompilerParams(dimension_semantics=("arbitrary", "arbitrary"),
                                             vmem_limit_bytes=_vmem_limit(resident)),
        name="moba",
    )(qt, kaug, vt, kmean)


def _mem_attn_kernel(q_ref, k_ref, v_ref, o_ref):
    outs = []
    for h in range(MEM_HEADS):
        cols = slice(h * HEAD_DIM, (h + 1) * HEAD_DIM)
        s = lax.dot_general(q_ref[:, cols], k_ref[0, :, cols], (((1,), (1,)), ((), ())),
                            preferred_element_type=F32)
        p = jnp.exp2(s - jnp.max(s, axis=-1, keepdims=True))
        l = jnp.sum(p, axis=-1, keepdims=True)
        outs.append(jnp.dot(p.astype(BF16), v_ref[0, :, cols], preferred_element_type=F32) / l)
    o_ref[...] = jnp.concatenate(outs, axis=1).astype(o_ref.dtype)


def _mem_attn(cq, kc, vc, batch, seq):
    n, width = cq.shape
    tm = TOKEN_TILE
    nt = seq // tm
    mlen = kc.shape[0] // batch
    kc = kc.reshape(batch, mlen, width)
    vc = vc.reshape(batch, mlen, width)
    kv_spec = pl.BlockSpec((1, mlen, width), lambda b, t: (b, 0, 0))
    tile = pl.BlockSpec((tm, width), lambda b, t: (b * nt + t, 0))
    return pl.pallas_call(
        _mem_attn_kernel,
        grid=(batch, nt),
        in_specs=[tile, kv_spec, kv_spec],
        out_specs=tile,
        out_shape=jax.ShapeDtypeStruct((n, width), BF16),
        compiler_params=pltpu.CompilerParams(dimension_semantics=("arbitrary", "arbitrary")),
        name="mem_attn",
    )(cq, kc, vc)


def _block_diag_ones(group):
    idx = jnp.arange(V7X_MXU_DIM) // group
    return (idx[:, None] == idx[None, :]).astype(BF16)


def _pack_block_diag(w):
    nb, bw, _ = w.shape
    per = V7X_MXU_DIM // bw
    w = w.reshape(nb // per, per, bw, bw)
    eye = jnp.eye(per, dtype=w.dtype)
    return jnp.einsum("gpij,pq->gpiqj", w, eye).reshape(nb // per, V7X_MXU_DIM, V7X_MXU_DIM)


def _pad_heads(w, heads):
    lead = w.shape[:-1]
    w = w.reshape(lead + (heads, HEAD_DIM))
    w = jnp.pad(w, [(0, 0)] * len(lead) + [(0, 0), (0, AUG_W - HEAD_DIM)])
    return w.reshape(lead + (heads * AUG_W,))


def _layer(x, mem, ffn1_norm, ffn1_w_in, ffn1_w_out, mix_norm, mem_norm, w_in,
           lru_conv_w, lru_conv_b, lru_a_w, lru_a_b, lru_x_w, lru_x_b, lru_lambda,
           moba_q_norm, moba_k_norm, mem_w_kv, mem_q_norm, mem_k_norm, w_out,
           ffn2_norm, ffn2_w_in, ffn2_w_out):
    batch, seq, d = x.shape
    n = batch * seq
    n_lru = lru_lambda.shape[0]
    wq = MOBA_HEADS * HEAD_DIM
    wc = MEM_HEADS * HEAD_DIM
    row = lambda v: v.reshape(1, -1).astype(F32)

    bd64 = _block_diag_ones(HEAD_DIM)
    bd128 = _block_diag_ones(AUG_W)

    w_in_b = w_in.astype(BF16)
    o = 2 * n_lru
    w_lru = w_in_b[:, :o]
    w_q = w_in_b[:, o:o + wq]
    w_k = _pad_heads(w_in_b[:, o + wq:o + 2 * wq], MOBA_HEADS)
    w_v = w_in_b[:, o + 2 * wq:o + 3 * wq]
    w_c = w_in_b[:, o + 3 * wq:]
    gq = row(jnp.tile(moba_q_norm, MOBA_HEADS))
    gk = row(_pad_heads(jnp.tile(moba_k_norm, MOBA_HEADS), MOBA_HEADS))
    gc = row(jnp.tile(mem_q_norm, MEM_HEADS))
    gck = row(jnp.tile(mem_k_norm, MEM_HEADS))
    slopes = 2.0 ** (-8.0 * jnp.arange(1, MOBA_HEADS + 1, dtype=F32) / MOBA_HEADS)
    lane = jnp.arange(AUG_W)
    is_alibi = (lane >= AUG_ALIBI) & (lane < AUG_ALIBI + 3)
    alibi = row(jnp.where(is_alibi[None, :], slopes[:, None] * LOG2E, 0.0))

    x2d = x.reshape(n, d)
    kc, vc = _mem_kv(mem.reshape(-1, d), row(mem_norm), mem_w_kv.astype(BF16), bd64, gck)

    x1 = _ffn(x2d, row(ffn1_norm), ffn1_w_in.astype(BF16), ffn1_w_out.astype(BF16))

    lru_x, lru_g, qt, kaug, vt, kmean, cq = _in_proj(
        x1, row(mix_norm), w_lru, w_q, w_k, w_v, w_c, gq, gk, gc, bd64, bd128, alibi, seq)
    kmean = kmean[:, :TOKEN_TILE // MOBA_BLOCK, :].reshape(batch, seq // MOBA_BLOCK, -1)

    y_lru = _lru(lru_x, lru_g, lru_conv_w.astype(F32), row(lru_conv_b),
                 _pack_block_diag(lru_a_w).astype(BF16), row(lru_a_b),
                 _pack_block_diag(lru_x_w).astype(BF16), row(lru_x_b), row(lru_lambda), batch, seq)
    y_moba = _moba(qt, kaug, vt, kmean, batch, seq)
    y_mem = _mem_attn(cq, kc, vc, batch, seq)

    out = _out_ffn(x1, y_lru, y_moba, y_mem, w_out.astype(BF16), row(ffn2_norm),
                   ffn2_w_in.astype(BF16), ffn2_w_out.astype(BF16))
    return out.reshape(batch, seq, d)


def kernel(x, mem, ffn1_norm, ffn1_w_in, ffn1_w_out, mix_norm, mem_norm, w_in, lru_conv_w, lru_conv_b,
           lru_a_w, lru_a_b, lru_x_w, lru_x_b, lru_lambda, moba_q_norm, moba_k_norm, mem_w_kv,
           mem_q_norm, mem_k_norm, w_out, ffn2_norm, ffn2_w_in, ffn2_w_out):
    params = (ffn1_norm, ffn1_w_in, ffn1_w_out, mix_norm, mem_norm, w_in, lru_conv_w, lru_conv_b,
              lru_a_w, lru_a_b, lru_x_w, lru_x_b, lru_lambda, moba_q_norm, moba_k_norm, mem_w_kv,
              mem_q_norm, mem_k_norm, w_out, ffn2_norm, ffn2_w_in, ffn2_w_out)
    for layer in range(ffn1_norm.shape[0]):
        x = _layer(x, mem, *(p[layer] for p in params))
    return x
```

```python
import functools
import math

import jax
import jax.numpy as jnp
from jax import lax
from jax.experimental import pallas as pl
from jax.experimental.pallas import tpu as pltpu

F32 = jnp.float32
BF16 = jnp.bfloat16

HEAD_DIM = 64
CONV_W = 4
LRU_C = 8.0
MOBA_HEADS = 4
MOBA_BLOCK = 256
MOBA_TOPK = 3
MEM_HEADS = 4
NORM_EPS = 1e-6
LOG2E = 1.4426950408889634
NEG_BIG = -1e30

V7X_MXU_DIM = 256
LANES = 128
V7X_VMEM_BYTES = 64 * 1024 * 1024

AUG_W = 128
AUG_ALIBI = 64
AUG_MASK = 80
VT_ROWS = HEAD_DIM + 16
SCORE_LOOKAHEAD = 3
PAST_UNROLL = 2

TOKEN_TILE = 512
LRU_TILE = 512
FFN_CHUNK = 256


def _vmem_limit(resident_bytes):
    return int(min(V7X_VMEM_BYTES - 8 * 1024 * 1024, resident_bytes + 24 * 1024 * 1024))


def _const_spec(shape):
    n = len(shape)
    return pl.BlockSpec(shape, lambda *_: (0,) * n, pipeline_mode=pl.Buffered(1))


def _rms_norm(x, g):
    ms = jnp.mean(x * x, axis=-1, keepdims=True)
    return x * lax.rsqrt(ms + NORM_EPS) * g


def _group_rms_norm(u, bd, g, group):
    sq = u * u
    hi = sq.astype(BF16)
    lo = (sq - hi.astype(F32)).astype(BF16)
    parts = []
    for c in range(u.shape[1] // V7X_MXU_DIM):
        sl = slice(c * V7X_MXU_DIM, (c + 1) * V7X_MXU_DIM)
        parts.append(jnp.dot(hi[:, sl], bd, preferred_element_type=F32)
                     + jnp.dot(lo[:, sl], bd, preferred_element_type=F32))
    ss = parts[0] if len(parts) == 1 else jnp.concatenate(parts, axis=1)
    return u * lax.rsqrt(ss * (1.0 / group) + NORM_EPS) * g


def _swiglu_half_step(x, g_ref, w_in_ref, w_out_ref, act_ref):
    d_ff = w_out_ref.shape[0]
    xn = _rms_norm(x, g_ref[...]).astype(BF16)
    for c in range(d_ff // FFN_CHUNK):
        lo = c * FFN_CHUNK
        a = jnp.dot(xn, w_in_ref[:, lo:lo + FFN_CHUNK], preferred_element_type=F32)
        b = jnp.dot(xn, w_in_ref[:, d_ff + lo:d_ff + lo + FFN_CHUNK], preferred_element_type=F32)
        act_ref[:, lo:lo + FFN_CHUNK] = (a * jax.nn.sigmoid(a) * b).astype(BF16)
    y = jnp.dot(act_ref[...], w_out_ref[...], preferred_element_type=F32)
    return x + 0.5 * y


def _ffn_kernel(x_ref, g_ref, w_in_ref, w_out_ref, o_ref, act_ref):
    o_ref[...] = _swiglu_half_step(x_ref[...], g_ref, w_in_ref, w_out_ref, act_ref)


def _ffn(x, g, w_in, w_out):
    n, d = x.shape
    d_ff = w_out.shape[0]
    tm = TOKEN_TILE
    resident = (w_in.size + w_out.size) * 2
    return pl.pallas_call(
        _ffn_kernel,
        grid=(n // tm,),
        in_specs=[pl.BlockSpec((tm, d), lambda i: (i, 0)),
                  _const_spec((1, d)), _const_spec(w_in.shape), _const_spec(w_out.shape)],
        out_specs=pl.BlockSpec((tm, d), lambda i: (i, 0)),
        out_shape=jax.ShapeDtypeStruct((n, d), F32),
        scratch_shapes=[pltpu.VMEM((tm, d_ff), BF16)],
        compiler_params=pltpu.CompilerParams(dimension_semantics=("arbitrary",),
                                             vmem_limit_bytes=_vmem_limit(resident)),
        name="ffn1",
    )(x, g, w_in, w_out)


def _out_ffn_kernel(x_ref, ylru_ref, ymoba_ref, ymem_ref, wo_ref, g_ref, w_in_ref, w_out_ref,
                    o_ref, act_ref):
    n_lru = ylru_ref.shape[1]
    n_moba = ymoba_ref.shape[1]
    x2 = (x_ref[...]
          + jnp.dot(ylru_ref[...], wo_ref[0:n_lru, :], preferred_element_type=F32)
          + jnp.dot(ymoba_ref[...], wo_ref[n_lru:n_lru + n_moba, :], preferred_element_type=F32)
          + jnp.dot(ymem_ref[...], wo_ref[n_lru + n_moba:, :], preferred_element_type=F32))
    o_ref[...] = _swiglu_half_step(x2, g_ref, w_in_ref, w_out_ref, act_ref)


def _out_ffn(x1, y_lru, y_moba, y_mem, w_o, g, w_in, w_out):
    n, d = x1.shape
    d_ff = w_out.shape[0]
    tm = TOKEN_TILE
    resident = (w_in.size + w_out.size + w_o.size) * 2

    def tile(w):
        return pl.BlockSpec((tm, w), lambda i: (i, 0))

    return pl.pallas_call(
        _out_ffn_kernel,
        grid=(n // tm,),
        in_specs=[tile(d), tile(y_lru.shape[1]), tile(y_moba.shape[1]), tile(y_mem.shape[1]),
                  _const_spec(w_o.shape), _const_spec((1, d)),
                  _const_spec(w_in.shape), _const_spec(w_out.shape)],
        out_specs=tile(d),
        out_shape=jax.ShapeDtypeStruct((n, d), F32),
        scratch_shapes=[pltpu.VMEM((tm, d_ff), BF16)],
        compiler_params=pltpu.CompilerParams(dimension_semantics=("arbitrary",),
                                             vmem_limit_bytes=_vmem_limit(resident)),
        name="out_ffn2",
    )(x1, y_lru, y_moba, y_mem, w_o, g, w_in, w_out)


def _mem_kv_kernel(mem_ref, g_ref, w_ref, bd_ref, gk_ref, kc_ref, vc_ref):
    width = kc_ref.shape[1]
    mn = _rms_norm(mem_ref[...], g_ref[...]).astype(BF16)
    kv = jnp.dot(mn, w_ref[...], preferred_element_type=F32)
    kc_ref[...] = _group_rms_norm(kv[:, :width], bd_ref[...], gk_ref[...], HEAD_DIM).astype(BF16)
    vc_ref[...] = kv[:, width:].astype(BF16)


def _mem_kv(mem2d, g, w_kv, bd64, gk):
    m = mem2d.shape[0]
    width = w_kv.shape[1] // 2
    return pl.pallas_call(
        _mem_kv_kernel,
        out_shape=(jax.ShapeDtypeStruct((m, width), BF16), jax.ShapeDtypeStruct((m, width), BF16)),
        name="mem_kv",
    )(mem2d, g, w_kv, bd64, gk)


def _in_proj_kernel(x_ref, g_ref, w_lru_ref, w_q_ref, w_k_ref, w_v_ref, w_c_ref,
                    gq_ref, gk_ref, gc_ref, bd64_ref, bd128_ref, alibi_ref,
                    lrux_ref, lrug_ref, qt_ref, kaug_ref, vt_ref, kmean_ref, cq_ref, *, seq):
    tm = x_ref.shape[0]
    n_lru = lrux_ref.shape[1]
    xn = _rms_norm(x_ref[...], g_ref[...]).astype(BF16)

    lrux_ref[...] = jnp.dot(xn, w_lru_ref[:, :n_lru], preferred_element_type=F32)
    lrug_ref[...] = jnp.dot(xn, w_lru_ref[:, n_lru:], preferred_element_type=F32)

    cq = jnp.dot(xn, w_c_ref[...], preferred_element_type=F32)
    cq = _group_rms_norm(cq, bd64_ref[...], gc_ref[...], HEAD_DIM)
    cq_ref[...] = (cq * (HEAD_DIM ** -0.5 * LOG2E)).astype(BF16)

    q = jnp.dot(xn, w_q_ref[...], preferred_element_type=F32)
    q = _group_rms_norm(q, bd64_ref[...], gq_ref[...], HEAD_DIM)
    v = jnp.dot(xn, w_v_ref[...], preferred_element_type=F32)
    ones = jnp.ones((VT_ROWS - HEAD_DIM, MOBA_BLOCK), BF16)
    for r in range(tm // MOBA_BLOCK):
        rows = slice(r * MOBA_BLOCK, (r + 1) * MOBA_BLOCK)
        qt_ref[r] = q[rows, :].T
        vt = v[rows, :].T.astype(BF16)
        for h in range(MOBA_HEADS):
            vt_ref[r, h * VT_ROWS:h * VT_ROWS + HEAD_DIM, :] = vt[h * HEAD_DIM:(h + 1) * HEAD_DIM, :]
            vt_ref[r, h * VT_ROWS + HEAD_DIM:(h + 1) * VT_ROWS, :] = ones

    k = jnp.dot(xn, w_k_ref[...], preferred_element_type=F32)
    k = _group_rms_norm(k, bd128_ref[...], gk_ref[...], HEAD_DIM)
    kmean_ref[...] = jnp.zeros(kmean_ref.shape, F32)
    for r in range(tm // MOBA_BLOCK):
        rows = slice(r * MOBA_BLOCK, (r + 1) * MOBA_BLOCK)
        kmean_ref[0, r:r + 1, :] = jnp.sum(k[rows, :], axis=0, keepdims=True) * (1.0 / MOBA_BLOCK)

    shape = k.shape
    row = lax.broadcasted_iota(jnp.int32, shape, 0)
    lane = lax.broadcasted_iota(jnp.int32, shape, 1) % AUG_W
    t = (pl.program_id(0) * tm + row) % seq
    bias = t.astype(F32) * alibi_ref[...]
    b_hi = bias.astype(BF16).astype(F32)
    b_mid = (bias - b_hi).astype(BF16).astype(F32)
    b_lo = bias - b_hi - b_mid
    pieces = jnp.where(lane == AUG_ALIBI, b_hi, jnp.where(lane == AUG_ALIBI + 1, b_mid, b_lo))
    onehot = jnp.where(lane - AUG_MASK == t // MOBA_BLOCK, 1.0, 0.0)
    is_alibi = (lane >= AUG_ALIBI) & (lane < AUG_ALIBI + 3)
    kaug_ref[...] = (k + jnp.where(is_alibi, pieces, onehot)).astype(BF16)


def _in_proj(x1, g, w_lru, w_q, w_k, w_v, w_c, gq, gk, gc, bd64, bd128, alibi, seq):
    n, d = x1.shape
    tm = TOKEN_TILE
    rb = tm // MOBA_BLOCK
    nblk = n // MOBA_BLOCK
    n_lru = w_lru.shape[1] // 2
    wq = w_q.shape[1]
    wk = w_k.shape[1]
    consts = (g, w_lru, w_q, w_k, w_v, w_c, gq, gk, gc, bd64, bd128, alibi)
    resident = sum(c.size * c.dtype.itemsize for c in consts)
    return pl.pallas_call(
        functools.partial(_in_proj_kernel, seq=seq),
        grid=(n // tm,),
        in_specs=[pl.BlockSpec((tm, d), lambda i: (i, 0))] + [_const_spec(c.shape) for c in consts],
        out_specs=[pl.BlockSpec((tm, n_lru), lambda i: (i, 0)),
                   pl.BlockSpec((tm, n_lru), lambda i: (i, 0)),
                   pl.BlockSpec((rb, wq, MOBA_BLOCK), lambda i: (i, 0, 0)),
                   pl.BlockSpec((tm, wk), lambda i: (i, 0)),
                   pl.BlockSpec((rb, MOBA_HEADS * VT_ROWS, MOBA_BLOCK), lambda i: (i, 0, 0)),
                   pl.BlockSpec((1, 8, wk), lambda i: (i, 0, 0)),
                   pl.BlockSpec((tm, wq), lambda i: (i, 0))],
        out_shape=(jax.ShapeDtypeStruct((n, n_lru), F32),
                   jax.ShapeDtypeStruct((n, n_lru), F32),
                   jax.ShapeDtypeStruct((nblk, wq, MOBA_BLOCK), F32),
                   jax.ShapeDtypeStruct((n, wk), BF16),
                   jax.ShapeDtypeStruct((nblk, MOBA_HEADS * VT_ROWS, MOBA_BLOCK), BF16),
                   jax.ShapeDtypeStruct((n // tm, 8, wk), F32),
                   jax.ShapeDtypeStruct((n, wq), BF16)),
        compiler_params=pltpu.CompilerParams(dimension_semantics=("arbitrary",),
                                             vmem_limit_bytes=_vmem_limit(resident)),
        name="in_proj",
    )(x1, *consts)


def _gelu_tanh(x):
    c = math.sqrt(2.0 / math.pi)
    return 0.5 * x * (1.0 + jnp.tanh(c * (x + 0.044715 * (x * x * x))))


def _lru_kernel(x_ref, g_ref, cw_ref, cb_ref, wa_ref, ba_ref, wx_ref, bx_ref, lam_ref,
                o_ref, xpad_ref, a_ref, b_ref, hl_ref, p_ref, carry_ref):
    tl, c = x_ref.shape
    seg = tl // 8
    half = V7X_MXU_DIM
    ti = pl.program_id(1)

    @pl.when(ti == 0)
    def _():
        xpad_ref[0:8, :] = jnp.zeros((8, c), F32)
        carry_ref[...] = jnp.zeros(carry_ref.shape, F32)

    xpad_ref[8:8 + tl, :] = x_ref[...]
    xb = jnp.broadcast_to(cb_ref[...], (tl, c))
    for j in range(CONV_W):
        off = 8 - (CONV_W - 1) + j
        xb = xb + cw_ref[j:j + 1, :] * xpad_ref[off:off + tl, :]
    xpad_ref[0:8, :] = x_ref[tl - 8:tl, :]

    xbb = xb.astype(BF16)

    def gate(w_ref, bias_ref):
        z = jnp.concatenate(
            [jnp.dot(xbb[:, :half], w_ref[0], preferred_element_type=F32),
             jnp.dot(xbb[:, half:], w_ref[1], preferred_element_type=F32)], axis=1)
        return jax.nn.sigmoid(z + bias_ref[...])

    r = gate(wa_ref, ba_ref)
    i = gate(wx_ref, bx_ref)
    neg_lam = -lam_ref[...]
    softplus = jnp.maximum(neg_lam, 0.0) + jnp.log1p(jnp.exp(-jnp.abs(neg_lam)))
    log_a = (-LRU_C) * r * softplus
    a = jnp.exp(log_a)
    mult = jnp.sqrt(1.0 - a * a)
    row = lax.broadcasted_iota(jnp.int32, (tl, c), 0)
    start_row = jnp.where(ti == 0, 0, -1)
    mult = jnp.where(row == start_row, 1.0, mult)
    bt = mult * i * xb
    n_lt = c // LANES
    for k in range(n_lt):
        a_ref[k] = a[:, k * LANES:(k + 1) * LANES]
        b_ref[k] = bt[:, k * LANES:(k + 1) * LANES]

    h = [jnp.zeros((8, LANES), F32)] * n_lt
    p = [jnp.ones((8, LANES), F32)] * n_lt
    for j in range(seg):
        idx = pl.ds(j, 8, stride=seg)
        for k in range(n_lt):
            aj = a_ref[k, idx, :]
            h[k] = aj * h[k] + b_ref[k, idx, :]
            p[k] = aj * p[k]
            hl_ref[k, idx, :] = h[k]
            p_ref[k, idx, :] = p[k]

    for k in range(n_lt):
        lanes = slice(k * LANES, (k + 1) * LANES)
        carry = carry_ref[:, lanes]
        for s in range(8):
            rows = slice(s * seg, (s + 1) * seg)
            hs = hl_ref[k, rows, :] + p_ref[k, rows, :] * carry
            o_ref[rows, lanes] = (hs * _gelu_tanh(g_ref[rows, lanes])).astype(o_ref.dtype)
            carry = hs[seg - 1:seg, :]
        carry_ref[:, lanes] = carry


def _lru(lru_x, lru_g, cw, cb, wa, ba, wx, bx, lam, batch, seq):
    n, c = lru_x.shape
    tl = LRU_TILE
    nt = seq // tl
    consts = (cw, cb, wa, ba, wx, bx, lam)
    tile = pl.BlockSpec((tl, c), lambda b, t: (b * nt + t, 0))
    return pl.pallas_call(
        _lru_kernel,
        grid=(batch, nt),
        in_specs=[tile, tile] + [_const_spec(w.shape) for w in consts],
        out_specs=tile,
        out_shape=jax.ShapeDtypeStruct((n, c), BF16),
        scratch_shapes=[pltpu.VMEM((tl + 8, c), F32)]
        + [pltpu.VMEM((c // LANES, tl, LANES), F32)] * 4
        + [pltpu.VMEM((1, c), F32)],
        compiler_params=pltpu.CompilerParams(dimension_semantics=("arbitrary", "arbitrary")),
        name="lru",
    )(lru_x, lru_g, *consts)


def _moba_kernel(qt_ref, kaug_ref, vt_ref, kmean_ref, o_ref,
                 qaug_ref, qown_ref, s_ref, m_ref, acc_ref, ot_ref):
    qi = pl.program_id(1)
    nblk = kmean_ref.shape[1]
    blk = MOBA_BLOCK

    blk_id = lax.broadcasted_iota(jnp.int32, (nblk, blk), 0).astype(F32)
    qi_f = qi.astype(F32)
    for h in range(MOBA_HEADS):
        qt = qt_ref[0, 0, h * HEAD_DIM:(h + 1) * HEAD_DIM, :]
        gate = jnp.dot(kmean_ref[0, :, h * AUG_W:h * AUG_W + HEAD_DIM], qt,
                       preferred_element_type=F32, precision=lax.Precision.HIGHEST)
        gate = jnp.where(blk_id < qi_f, gate, -jnp.inf)
        keep = blk_id < 0.0
        for _ in range(MOBA_TOPK):
            best = jnp.max(gate, axis=0, keepdims=True)
            first = jnp.min(jnp.where(gate == best, blk_id, float(nblk)), axis=0, keepdims=True)
            pick = (blk_id == first) & (best > -jnp.inf)
            keep = keep | pick
            gate = jnp.where(pick, -jnp.inf, gate)
        qaug_ref[h, 0:HEAD_DIM, :] = (qt * (HEAD_DIM ** -0.5 * LOG2E)).astype(BF16)
        sub = lax.broadcasted_iota(jnp.int32, (AUG_MASK - AUG_ALIBI, blk), 0)
        qaug_ref[h, AUG_ALIBI:AUG_MASK, :] = jnp.where(sub < 3, 1.0, 0.0).astype(BF16)
        qaug_ref[h, AUG_MASK:AUG_MASK + nblk, :] = jnp.where(keep, 0.0, NEG_BIG).astype(BF16)
        qaug_ref[h, AUG_MASK + nblk:, :] = jnp.zeros((AUG_W - AUG_MASK - nblk, blk), BF16)
        qown_ref[h, 0:AUG_MASK, :] = qaug_ref[h, 0:AUG_MASK, :]
        qown_ref[h, AUG_MASK:, :] = jnp.zeros((AUG_W - AUG_MASK, blk), BF16)

    kk = lax.broadcasted_iota(jnp.int32, (blk, blk), 0)
    qq = lax.broadcasted_iota(jnp.int32, (blk, blk), 1)
    causal = kk <= qq

    def scores(j, h, own):
        q_ref = qown_ref if own else qaug_ref
        s_ref[h] = jnp.dot(kaug_ref[0, j, :, h * AUG_W:(h + 1) * AUG_W], q_ref[h],
                           preferred_element_type=F32)

    def softmax_pv(j, h, own):
        s = s_ref[h]
        vt = vt_ref[0, j, h * VT_ROWS:(h + 1) * VT_ROWS, :]
        if own:
            s = jnp.where(causal, s, NEG_BIG)
            m_new = jnp.max(s, axis=0, keepdims=True)
            p = jnp.exp2(s - m_new).astype(BF16)
            acc_ref[h] = jnp.dot(vt, p, preferred_element_type=F32)
        else:
            m_old = m_ref[h]
            m_new = jnp.maximum(m_old, jnp.max(s, axis=0, keepdims=True))
            p = jnp.exp2(s - m_new).astype(BF16)
            acc_ref[h] = (jnp.exp2(m_old - m_new) * acc_ref[h]
                          + jnp.dot(vt, p, preferred_element_type=F32))
        m_ref[h] = m_new

    def sweep(blocks, next_block, own):
        items = [(j, h) for j in blocks for h in range(MOBA_HEADS)]
        for n, (j, h) in enumerate(items):
            ahead = n + SCORE_LOOKAHEAD
            if ahead < len(items):
                scores(*items[ahead], own)
            else:
                scores(next_block, ahead - len(items), False)
            softmax_pv(j, h, own)

    for h in range(SCORE_LOOKAHEAD):
        scores(qi, h, True)
    sweep([qi], 0, own=True)

    def past_blocks(t, carry):
        first = t * PAST_UNROLL
        blocks = [jnp.minimum(first + u, nblk - 1) for u in range(PAST_UNROLL)]
        sweep(blocks, jnp.minimum(first + PAST_UNROLL, nblk - 1), own=False)
        return carry

    lax.fori_loop(0, (qi + PAST_UNROLL - 1) // PAST_UNROLL, past_blocks, 0)

    for h in range(MOBA_HEADS):
        ot_ref[h * HEAD_DIM:(h + 1) * HEAD_DIM, :] = (acc_ref[h, 0:HEAD_DIM, :]
                                                      / acc_ref[h, HEAD_DIM:HEAD_DIM + 1, :])
    o_ref[...] = ot_ref[...].T.astype(o_ref.dtype)


def _moba(qt, kaug, vt, kmean, batch, seq):
    nblk = seq // MOBA_BLOCK
    width = MOBA_HEADS * HEAD_DIM
    qt = qt.reshape(batch, nblk, width, MOBA_BLOCK)
    kaug = kaug.reshape(batch, nblk, MOBA_BLOCK, MOBA_HEADS * AUG_W)
    vt_rows = MOBA_HEADS * VT_ROWS
    vt = vt.reshape(batch, nblk, vt_rows, MOBA_BLOCK)
    resident = 2 * (kaug.size // batch + vt.size // batch) * 2
    return pl.pallas_call(
        _moba_kernel,
        grid=(batch, nblk),
        in_specs=[pl.BlockSpec((1, 1, width, MOBA_BLOCK), lambda b, i: (b, i, 0, 0)),
                  pl.BlockSpec((1, nblk, MOBA_BLOCK, MOBA_HEADS * AUG_W), lambda b, i: (b, 0, 0, 0)),
                  pl.BlockSpec((1, nblk, vt_rows, MOBA_BLOCK), lambda b, i: (b, 0, 0, 0)),
                  pl.BlockSpec((1, nblk, MOBA_HEADS * AUG_W), lambda b, i: (b, 0, 0))],
        out_specs=pl.BlockSpec((MOBA_BLOCK, width), lambda b, i: (b * nblk + i, 0)),
        out_shape=jax.ShapeDtypeStruct((batch * seq, width), BF16),
        scratch_shapes=[pltpu.VMEM((MOBA_HEADS, AUG_W, MOBA_BLOCK), BF16),
                        pltpu.VMEM((MOBA_HEADS, AUG_W, MOBA_BLOCK), BF16),
                        pltpu.VMEM((MOBA_HEADS, MOBA_BLOCK, MOBA_BLOCK), F32),
                        pltpu.VMEM((MOBA_HEADS, 1, MOBA_BLOCK), F32),
                        pltpu.VMEM((MOBA_HEADS, VT_ROWS, MOBA_BLOCK), F32),
                        pltpu.VMEM((width, MOBA_BLOCK), F32)],
        compiler_params=pltpu.CompilerParams(dimension_semantics=("arbitrary", "arbitrary"),
                                             vmem_limit_bytes=_vmem_limit(resident)),
        name="moba",
    )(qt, kaug, vt, kmean)


def _mem_attn_kernel(q_ref, k_ref, v_ref, o_ref):
    outs = []
    for h in range(MEM_HEADS):
        cols = slice(h * HEAD_DIM, (h + 1) * HEAD_DIM)
        s = lax.dot_general(q_ref[:, cols], k_ref[0, :, cols], (((1,), (1,)), ((), ())),
                            preferred_element_type=F32)
        p = jnp.exp2(s - jnp.max(s, axis=-1, keepdims=True))
        l = jnp.sum(p, axis=-1, keepdims=True)
        outs.append(jnp.dot(p.astype(BF16), v_ref[0, :, cols], preferred_element_type=F32) / l)
    o_ref[...] = jnp.concatenate(outs, axis=1).astype(o_ref.dtype)


def _mem_attn(cq, kc, vc, batch, seq):
    n, width = cq.shape
    tm = TOKEN_TILE
    nt = seq // tm
    mlen = kc.shape[0] // batch
    kc = kc.reshape(batch, mlen, width)
    vc = vc.reshape(batch, mlen, width)
    kv_spec = pl.BlockSpec((1, mlen, width), lambda b, t: (b, 0, 0))
    tile = pl.BlockSpec((tm, width), lambda b, t: (b * nt + t, 0))
    return pl.pallas_call(
        _mem_attn_kernel,
        grid=(batch, nt),
        in_specs=[tile, kv_spec, kv_spec],
        out_specs=tile,
        out_shape=jax.ShapeDtypeStruct((n, width), BF16),
        compiler_params=pltpu.CompilerParams(dimension_semantics=("arbitrary", "arbitrary")),
        name="mem_attn",
    )(cq, kc, vc)


def _block_diag_ones(group):
    idx = jnp.arange(V7X_MXU_DIM) // group
    return (idx[:, None] == idx[None, :]).astype(BF16)


def _pack_block_diag(w):
    nb, bw, _ = w.shape
    per = V7X_MXU_DIM // bw
    w = w.reshape(nb // per, per, bw, bw)
    eye = jnp.eye(per, dtype=w.dtype)
    return jnp.einsum("gpij,pq->gpiqj", w, eye).reshape(nb // per, V7X_MXU_DIM, V7X_MXU_DIM)


def _pad_heads(w, heads):
    lead = w.shape[:-1]
    w = w.reshape(lead + (heads, HEAD_DIM))
    w = jnp.pad(w, [(0, 0)] * len(lead) + [(0, 0), (0, AUG_W - HEAD_DIM)])
    return w.reshape(lead + (heads * AUG_W,))


def _layer(x, mem, ffn1_norm, ffn1_w_in, ffn1_w_out, mix_norm, mem_norm, w_in,
           lru_conv_w, lru_conv_b, lru_a_w, lru_a_b, lru_x_w, lru_x_b, lru_lambda,
           moba_q_norm, moba_k_norm, mem_w_kv, mem_q_norm, mem_k_norm, w_out,
           ffn2_norm, ffn2_w_in, ffn2_w_out):
    batch, seq, d = x.shape
    n = batch * seq
    n_lru = lru_lambda.shape[0]
    wq = MOBA_HEADS * HEAD_DIM
    wc = MEM_HEADS * HEAD_DIM
    row = lambda v: v.reshape(1, -1).astype(F32)

    bd64 = _block_diag_ones(HEAD_DIM)
    bd128 = _block_diag_ones(AUG_W)

    w_in_b = w_in.astype(BF16)
    o = 2 * n_lru
    w_lru = w_in_b[:, :o]
    w_q = w_in_b[:, o:o + wq]
    w_k = _pad_heads(w_in_b[:, o + wq:o + 2 * wq], MOBA_HEADS)
    w_v = w_in_b[:, o + 2 * wq:o + 3 * wq]
    w_c = w_in_b[:, o + 3 * wq:]
    gq = row(jnp.tile(moba_q_norm, MOBA_HEADS))
    gk = row(_pad_heads(jnp.tile(moba_k_norm, MOBA_HEADS), MOBA_HEADS))
    gc = row(jnp.tile(mem_q_norm, MEM_HEADS))
    gck = row(jnp.tile(mem_k_norm, MEM_HEADS))
    slopes = 2.0 ** (-8.0 * jnp.arange(1, MOBA_HEADS + 1, dtype=F32) / MOBA_HEADS)
    lane = jnp.arange(AUG_W)
    is_alibi = (lane >= AUG_ALIBI) & (lane < AUG_ALIBI + 3)
    alibi = row(jnp.where(is_alibi[None, :], slopes[:, None] * LOG2E, 0.0))

    x2d = x.reshape(n, d)
    kc, vc = _mem_kv(mem.reshape(-1, d), row(mem_norm), mem_w_kv.astype(BF16), bd64, gck)

    x1 = _ffn(x2d, row(ffn1_norm), ffn1_w_in.astype(BF16), ffn1_w_out.astype(BF16))

    lru_x, lru_g, qt, kaug, vt, kmean, cq = _in_proj(
        x1, row(mix_norm), w_lru, w_q, w_k, w_v, w_c, gq, gk, gc, bd64, bd128, alibi, seq)
    kmean = kmean[:, :TOKEN_TILE // MOBA_BLOCK, :].reshape(batch, seq // MOBA_BLOCK, -1)

    y_lru = _lru(lru_x, lru_g, lru_conv_w.astype(F32), row(lru_conv_b),
                 _pack_block_diag(lru_a_w).astype(BF16), row(lru_a_b),
                 _pack_block_diag(lru_x_w).astype(BF16), row(lru_x_b), row(lru_lambda), batch, seq)
    y_moba = _moba(qt, kaug, vt, kmean, batch, seq)
    y_mem = _mem_attn(cq, kc, vc, batch, seq)

    out = _out_ffn(x1, y_lru, y_moba, y_mem, w_out.astype(BF16), row(ffn2_norm),
                   ffn2_w_in.astype(BF16), ffn2_w_out.astype(BF16))
    return out.reshape(batch, seq, d)


def kernel(x, mem, ffn1_norm, ffn1_w_in, ffn1_w_out, mix_norm, mem_norm, w_in, lru_conv_w, lru_conv_b,
           lru_a_w, lru_a_b, lru_x_w, lru_x_b, lru_lambda, moba_q_norm, moba_k_norm, mem_w_kv,
           mem_q_norm, mem_k_norm, w_out, ffn2_norm, ffn2_w_in, ffn2_w_out):
    params = (ffn1_norm, ffn1_w_in, ffn1_w_out, mix_norm, mem_norm, w_in, lru_conv_w, lru_conv_b,
              lru_a_w, lru_a_b, lru_x_w, lru_x_b, lru_lambda, moba_q_norm, moba_k_norm, mem_w_kv,
              mem_q_norm, mem_k_norm, w_out, ffn2_norm, ffn2_w_in, ffn2_w_out)
    for layer in range(ffn1_norm.shape[0]):
        x = _layer(x, mem, *(p[layer] for p in params))
    return x
```

```python
import functools
import math

import jax
import jax.numpy as jnp
from jax import lax
from jax.experimental import pallas as pl
from jax.experimental.pallas import tpu as pltpu

F32 = jnp.float32
BF16 = jnp.bfloat16

HEAD_DIM = 64
CONV_W = 4
LRU_C = 8.0
MOBA_HEADS = 4
MOBA_BLOCK = 256
MOBA_TOPK = 3
MEM_HEADS = 4
NORM_EPS = 1e-6
LOG2E = 1.4426950408889634
NEG_BIG = -1e30

V7X_MXU_DIM = 256
LANES = 128
SUBLANES = 8
V7X_VMEM_BYTES = 64 * 1024 * 1024

AUG_W = 128
AUG_ALIBI = 64
AUG_MASK = 80
VT_ROWS = HEAD_DIM + 16
SCORE_LOOKAHEAD = 3
PAST_UNROLL = 2

TOKEN_TILE = 512
LRU_TILE = 512
FFN_CHUNK = 256


def _vmem_limit(resident_bytes):
    return int(min(V7X_VMEM_BYTES - 8 * 1024 * 1024, resident_bytes + 24 * 1024 * 1024))


def _const_spec(shape):
    n = len(shape)
    return pl.BlockSpec(shape, lambda *_: (0,) * n, pipeline_mode=pl.Buffered(1))


def _rms_norm(x, g):
    ms = jnp.mean(x * x, axis=-1, keepdims=True)
    return x * lax.rsqrt(ms + NORM_EPS) * g


def _group_rms_norm(u, bd, g, group):
    sq = u * u
    hi = sq.astype(BF16)
    lo = (sq - hi.astype(F32)).astype(BF16)
    parts = []
    for c in range(u.shape[1] // V7X_MXU_DIM):
        sl = slice(c * V7X_MXU_DIM, (c + 1) * V7X_MXU_DIM)
        parts.append(jnp.dot(hi[:, sl], bd, preferred_element_type=F32)
                     + jnp.dot(lo[:, sl], bd, preferred_element_type=F32))
    ss = parts[0] if len(parts) == 1 else jnp.concatenate(parts, axis=1)
    return u * lax.rsqrt(ss * (1.0 / group) + NORM_EPS) * g


def _swiglu_half_step(x, g_ref, w_in_ref, w_out_ref, act_ref):
    d_ff = w_out_ref.shape[0]
    xn = _rms_norm(x, g_ref[...]).astype(BF16)
    for c in range(d_ff // FFN_CHUNK):
        lo = c * FFN_CHUNK
        a = jnp.dot(xn, w_in_ref[:, lo:lo + FFN_CHUNK], preferred_element_type=F32)
        b = jnp.dot(xn, w_in_ref[:, d_ff + lo:d_ff + lo + FFN_CHUNK], preferred_element_type=F32)
        act_ref[:, lo:lo + FFN_CHUNK] = (a * jax.nn.sigmoid(a) * b).astype(BF16)
    y = jnp.dot(act_ref[...], w_out_ref[...], preferred_element_type=F32)
    return x + 0.5 * y


def _ffn_kernel(x_ref, g_ref, w_in_ref, w_out_ref, o_ref, act_ref):
    o_ref[...] = _swiglu_half_step(x_ref[...], g_ref, w_in_ref, w_out_ref, act_ref)


def _ffn(x, g, w_in, w_out):
    n, d = x.shape
    d_ff = w_out.shape[0]
    tm = TOKEN_TILE
    resident = (w_in.size + w_out.size) * 2
    return pl.pallas_call(
        _ffn_kernel,
        grid=(n // tm,),
        in_specs=[pl.BlockSpec((tm, d), lambda i: (i, 0)),
                  _const_spec((1, d)), _const_spec(w_in.shape), _const_spec(w_out.shape)],
        out_specs=pl.BlockSpec((tm, d), lambda i: (i, 0)),
        out_shape=jax.ShapeDtypeStruct((n, d), F32),
        scratch_shapes=[pltpu.VMEM((tm, d_ff), BF16)],
        compiler_params=pltpu.CompilerParams(dimension_semantics=("arbitrary",),
                                             vmem_limit_bytes=_vmem_limit(resident)),
        name="ffn1",
    )(x, g, w_in, w_out)


def _out_ffn_kernel(x_ref, ylru_ref, ymoba_ref, ymem_ref, wo_ref, g_ref, w_in_ref, w_out_ref,
                    o_ref, act_ref):
    n_lru = ylru_ref.shape[1]
    n_moba = ymoba_ref.shape[1]
    x2 = (x_ref[...]
          + jnp.dot(ylru_ref[...], wo_ref[0:n_lru, :], preferred_element_type=F32)
          + jnp.dot(ymoba_ref[...], wo_ref[n_lru:n_lru + n_moba, :], preferred_element_type=F32)
          + jnp.dot(ymem_ref[...], wo_ref[n_lru + n_moba:, :], preferred_element_type=F32))
    o_ref[...] = _swiglu_half_step(x2, g_ref, w_in_ref, w_out_ref, act_ref)


def _out_ffn(x1, y_lru, y_moba, y_mem, w_o, g, w_in, w_out):
    n, d = x1.shape
    d_ff = w_out.shape[0]
    tm = TOKEN_TILE
    resident = (w_in.size + w_out.size + w_o.size) * 2

    def tile(w):
        return pl.BlockSpec((tm, w), lambda i: (i, 0))

    return pl.pallas_call(
        _out_ffn_kernel,
        grid=(n // tm,),
        in_specs=[tile(d), tile(y_lru.shape[1]), tile(y_moba.shape[1]), tile(y_mem.shape[1]),
                  _const_spec(w_o.shape), _const_spec((1, d)),
                  _const_spec(w_in.shape), _const_spec(w_out.shape)],
        out_specs=tile(d),
        out_shape=jax.ShapeDtypeStruct((n, d), F32),
        scratch_shapes=[pltpu.VMEM((tm, d_ff), BF16)],
        compiler_params=pltpu.CompilerParams(dimension_semantics=("arbitrary",),
                                             vmem_limit_bytes=_vmem_limit(resident)),
        name="out_ffn2",
    )(x1, y_lru, y_moba, y_mem, w_o, g, w_in, w_out)


def _mem_kv_kernel(mem_ref, g_ref, w_ref, bd_ref, gk_ref, kc_ref, vc_ref):
    width = kc_ref.shape[1]
    mn = _rms_norm(mem_ref[...], g_ref[...]).astype(BF16)
    kv = jnp.dot(mn, w_ref[...], preferred_element_type=F32)
    kc_ref[...] = _group_rms_norm(kv[:, :width], bd_ref[...], gk_ref[...], HEAD_DIM).astype(BF16)
    vc_ref[...] = kv[:, width:].astype(BF16)


def _mem_kv(mem2d, g, w_kv, bd64, gk):
    m = mem2d.shape[0]
    width = w_kv.shape[1] // 2
    return pl.pallas_call(
        _mem_kv_kernel,
        out_shape=(jax.ShapeDtypeStruct((m, width), BF16), jax.ShapeDtypeStruct((m, width), BF16)),
        name="mem_kv",
    )(mem2d, g, w_kv, bd64, gk)


def _in_proj_kernel(x_ref, g_ref, w_lru_ref, w_q_ref, w_k_ref, w_v_ref, w_c_ref,
                    gq_ref, gk_ref, gc_ref, bd64_ref, bd128_ref, alibi_ref,
                    lrux_ref, lrug_ref, qt_ref, kaug_ref, vt_ref, kmean_ref, cq_ref, *, seq):
    tm = x_ref.shape[0]
    n_lru = lrux_ref.shape[1]
    xn = _rms_norm(x_ref[...], g_ref[...]).astype(BF16)

    lrux_ref[...] = jnp.dot(xn, w_lru_ref[:, :n_lru], preferred_element_type=F32)
    lrug_ref[...] = jnp.dot(xn, w_lru_ref[:, n_lru:], preferred_element_type=F32)

    cq = jnp.dot(xn, w_c_ref[...], preferred_element_type=F32)
    cq = _group_rms_norm(cq, bd64_ref[...], gc_ref[...], HEAD_DIM)
    cq_ref[...] = (cq * (HEAD_DIM ** -0.5 * LOG2E)).astype(BF16)

    q = jnp.dot(xn, w_q_ref[...], preferred_element_type=F32)
    q = _group_rms_norm(q, bd64_ref[...], gq_ref[...], HEAD_DIM)
    v = jnp.dot(xn, w_v_ref[...], preferred_element_type=F32)
    ones = jnp.ones((VT_ROWS - HEAD_DIM, MOBA_BLOCK), BF16)
    for r in range(tm // MOBA_BLOCK):
        rows = slice(r * MOBA_BLOCK, (r + 1) * MOBA_BLOCK)
        qt_ref[r] = q[rows, :].T
        vt = v[rows, :].T.astype(BF16)
        for h in range(MOBA_HEADS):
            vt_ref[r, h * VT_ROWS:h * VT_ROWS + HEAD_DIM, :] = vt[h * HEAD_DIM:(h + 1) * HEAD_DIM, :]
            vt_ref[r, h * VT_ROWS + HEAD_DIM:(h + 1) * VT_ROWS, :] = ones

    k = jnp.dot(xn, w_k_ref[...], preferred_element_type=F32)
    k = _group_rms_norm(k, bd128_ref[...], gk_ref[...], HEAD_DIM)
    kmean_ref[...] = jnp.zeros(kmean_ref.shape, F32)
    for r in range(tm // MOBA_BLOCK):
        rows = slice(r * MOBA_BLOCK, (r + 1) * MOBA_BLOCK)
        kmean_ref[0, r:r + 1, :] = jnp.sum(k[rows, :], axis=0, keepdims=True) * (1.0 / MOBA_BLOCK)

    shape = k.shape
    row = lax.broadcasted_iota(jnp.int32, shape, 0)
    lane = lax.broadcasted_iota(jnp.int32, shape, 1) % AUG_W
    t = (pl.program_id(0) * tm + row) % seq
    bias = t.astype(F32) * alibi_ref[...]
    b_hi = bias.astype(BF16).astype(F32)
    b_mid = (bias - b_hi).astype(BF16).astype(F32)
    b_lo = bias - b_hi - b_mid
    pieces = jnp.where(lane == AUG_ALIBI, b_hi, jnp.where(lane == AUG_ALIBI + 1, b_mid, b_lo))
    onehot = jnp.where(lane - AUG_MASK == t // MOBA_BLOCK, 1.0, 0.0)
    is_alibi = (lane >= AUG_ALIBI) & (lane < AUG_ALIBI + 3)
    kaug_ref[...] = (k + jnp.where(is_alibi, pieces, onehot)).astype(BF16)


def _in_proj(x1, g, w_lru, w_q, w_k, w_v, w_c, gq, gk, gc, bd64, bd128, alibi, seq):
    n, d = x1.shape
    tm = TOKEN_TILE
    rb = tm // MOBA_BLOCK
    nblk = n // MOBA_BLOCK
    n_lru = w_lru.shape[1] // 2
    wq = w_q.shape[1]
    wk = w_k.shape[1]
    consts = (g, w_lru, w_q, w_k, w_v, w_c, gq, gk, gc, bd64, bd128, alibi)
    resident = sum(c.size * c.dtype.itemsize for c in consts)
    return pl.pallas_call(
        functools.partial(_in_proj_kernel, seq=seq),
        grid=(n // tm,),
        in_specs=[pl.BlockSpec((tm, d), lambda i: (i, 0))] + [_const_spec(c.shape) for c in consts],
        out_specs=[pl.BlockSpec((tm, n_lru), lambda i: (i, 0)),
                   pl.BlockSpec((tm, n_lru), lambda i: (i, 0)),
                   pl.BlockSpec((rb, wq, MOBA_BLOCK), lambda i: (i, 0, 0)),
                   pl.BlockSpec((tm, wk), lambda i: (i, 0)),
                   pl.BlockSpec((rb, MOBA_HEADS * VT_ROWS, MOBA_BLOCK), lambda i: (i, 0, 0)),
                   pl.BlockSpec((1, 8, wk), lambda i: (i, 0, 0)),
                   pl.BlockSpec((tm, wq), lambda i: (i, 0))],
        out_shape=(jax.ShapeDtypeStruct((n, n_lru), F32),
                   jax.ShapeDtypeStruct((n, n_lru), F32),
                   jax.ShapeDtypeStruct((nblk, wq, MOBA_BLOCK), F32),
                   jax.ShapeDtypeStruct((n, wk), BF16),
                   jax.ShapeDtypeStruct((nblk, MOBA_HEADS * VT_ROWS, MOBA_BLOCK), BF16),
                   jax.ShapeDtypeStruct((n // tm, 8, wk), F32),
                   jax.ShapeDtypeStruct((n, wq), BF16)),
        compiler_params=pltpu.CompilerParams(dimension_semantics=("arbitrary",),
                                             vmem_limit_bytes=_vmem_limit(resident)),
        name="in_proj",
    )(x1, *consts)


def _gelu_tanh(x):
    c = math.sqrt(2.0 / math.pi)
    return 0.5 * x * (1.0 + jnp.tanh(c * (x + 0.044715 * (x * x * x))))


def _lru_kernel(x_ref, g_ref, cw_ref, cb_ref, wa_ref, ba_ref, wx_ref, bx_ref, lam_ref,
                o_ref, xpad_ref, a_ref, b_ref, hl_ref, carry_ref):
    tl, c = x_ref.shape
    half = V7X_MXU_DIM
    ti = pl.program_id(1)

    @pl.when(ti == 0)
    def _():
        xpad_ref[0:8, :] = jnp.zeros((8, c), F32)
        carry_ref[...] = jnp.zeros(carry_ref.shape, F32)

    xpad_ref[8:8 + tl, :] = x_ref[...]
    xb = jnp.broadcast_to(cb_ref[...], (tl, c))
    for j in range(CONV_W):
        off = 8 - (CONV_W - 1) + j
        xb = xb + cw_ref[j:j + 1, :] * xpad_ref[off:off + tl, :]
    xpad_ref[0:8, :] = x_ref[tl - 8:tl, :]

    xbb = xb.astype(BF16)

    def gate(w_ref, bias_ref):
        z = jnp.concatenate(
            [jnp.dot(xbb[:, :half], w_ref[0], preferred_element_type=F32),
             jnp.dot(xbb[:, half:], w_ref[1], preferred_element_type=F32)], axis=1)
        return jax.nn.sigmoid(z + bias_ref[...])

    r = gate(wa_ref, ba_ref)
    i = gate(wx_ref, bx_ref)
    neg_lam = -lam_ref[...]
    softplus = jnp.maximum(neg_lam, 0.0) + jnp.log1p(jnp.exp(-jnp.abs(neg_lam)))
    log_a = (-LRU_C) * r * softplus
    a = jnp.exp(log_a)
    mult = jnp.sqrt(1.0 - a * a)
    row = lax.broadcasted_iota(jnp.int32, (tl, c), 0)
    start_row = jnp.where(ti == 0, 0, -1)
    mult = jnp.where(row == start_row, 1.0, mult)
    bt = mult * i * xb

    groups = tl // SUBLANES
    sub = lax.broadcasted_iota(jnp.int32, (groups, SUBLANES, LANES), 1)
    for k in range(c // LANES):
        lanes = slice(k * LANES, (k + 1) * LANES)
        a3 = a[:, lanes].reshape(groups, SUBLANES, LANES)
        b3 = bt[:, lanes].reshape(groups, SUBLANES, LANES)
        d = 1
        while d < SUBLANES:
            has_prev = sub >= d
            a_prev = pltpu.roll(a3, d, axis=1)
            b_prev = pltpu.roll(b3, d, axis=1)
            b3 = b3 + jnp.where(has_prev, a3, 0.0) * b_prev
            a3 = jnp.where(has_prev, a3 * a_prev, a3)
            d *= 2
        a_ref[:, lanes] = a3.reshape(tl, LANES)
        b_ref[:, lanes] = b3.reshape(tl, LANES)

    h = carry_ref[...]
    for grp in range(groups):
        rows = slice(grp * SUBLANES, (grp + 1) * SUBLANES)
        h_last = jnp.broadcast_to(h[SUBLANES - 1:SUBLANES, :], (SUBLANES, c))
        h = a_ref[rows, :] * h_last + b_ref[rows, :]
        hl_ref[rows, :] = h
    carry_ref[...] = h

    o_ref[...] = (hl_ref[...] * _gelu_tanh(g_ref[...])).astype(o_ref.dtype)


def _lru(lru_x, lru_g, cw, cb, wa, ba, wx, bx, lam, batch, seq):
    n, c = lru_x.shape
    tl = LRU_TILE
    nt = seq // tl
    consts = (cw, cb, wa, ba, wx, bx, lam)
    tile = pl.BlockSpec((tl, c), lambda b, t: (b * nt + t, 0))
    return pl.pallas_call(
        _lru_kernel,
        grid=(batch, nt),
        in_specs=[tile, tile] + [_const_spec(w.shape) for w in consts],
        out_specs=tile,
        out_shape=jax.ShapeDtypeStruct((n, c), BF16),
        scratch_shapes=[pltpu.VMEM((tl + SUBLANES, c), F32)]
        + [pltpu.VMEM((tl, c), F32)] * 3
        + [pltpu.VMEM((SUBLANES, c), F32)],
        compiler_params=pltpu.CompilerParams(dimension_semantics=("arbitrary", "arbitrary")),
        name="lru",
    )(lru_x, lru_g, *consts)


def _moba_kernel(qt_ref, kaug_ref, vt_ref, kmean_ref, o_ref,
                 qaug_ref, qown_ref, s_ref, m_ref, acc_ref, ot_ref):
    qi = pl.program_id(1)
    nblk = kmean_ref.shape[1]
    blk = MOBA_BLOCK

    blk_id = lax.broadcasted_iota(jnp.int32, (nblk, blk), 0).astype(F32)
    qi_f = qi.astype(F32)
    for h in range(MOBA_HEADS):
        qt = qt_ref[0, 0, h * HEAD_DIM:(h + 1) * HEAD_DIM, :]
        gate = jnp.dot(kmean_ref[0, :, h * AUG_W:h * AUG_W + HEAD_DIM], qt,
                       preferred_element_type=F32, precision=lax.Precision.HIGHEST)
        gate = jnp.where(blk_id < qi_f, gate, -jnp.inf)
        keep = blk_id < 0.0
        for _ in range(MOBA_TOPK):
            best = jnp.max(gate, axis=0, keepdims=True)
            first = jnp.min(jnp.where(gate == best, blk_id, float(nblk)), axis=0, keepdims=True)
            pick = (blk_id == first) & (best > -jnp.inf)
            keep = keep | pick
            gate = jnp.where(pick, -jnp.inf, gate)
        qaug_ref[h, 0:HEAD_DIM, :] = (qt * (HEAD_DIM ** -0.5 * LOG2E)).astype(BF16)
        sub = lax.broadcasted_iota(jnp.int32, (AUG_MASK - AUG_ALIBI, blk), 0)
        qaug_ref[h, AUG_ALIBI:AUG_MASK, :] = jnp.where(sub < 3, 1.0, 0.0).astype(BF16)
        qaug_ref[h, AUG_MASK:AUG_MASK + nblk, :] = jnp.where(keep, 0.0, NEG_BIG).astype(BF16)
        qaug_ref[h, AUG_MASK + nblk:, :] = jnp.zeros((AUG_W - AUG_MASK - nblk, blk), BF16)
        qown_ref[h, 0:AUG_MASK, :] = qaug_ref[h, 0:AUG_MASK, :]
        qown_ref[h, AUG_MASK:, :] = jnp.zeros((AUG_W - AUG_MASK, blk), BF16)

    kk = lax.broadcasted_iota(jnp.int32, (blk, blk), 0)
    qq = lax.broadcasted_iota(jnp.int32, (blk, blk), 1)
    causal = kk <= qq

    def scores(j, h, own):
        q_ref = qown_ref if own else qaug_ref
        s_ref[h] = jnp.dot(kaug_ref[0, j, :, h * AUG_W:(h + 1) * AUG_W], q_ref[h],
                           preferred_element_type=F32)

    def softmax_pv(j, h, own):
        s = s_ref[h]
        vt = vt_ref[0, j, h * VT_ROWS:(h + 1) * VT_ROWS, :]
        if own:
            s = jnp.where(causal, s, NEG_BIG)
            m_new = jnp.max(s, axis=0, keepdims=True)
            p = jnp.exp2(s - m_new).astype(BF16)
            acc_ref[h] = jnp.dot(vt, p, preferred_element_type=F32)
        else:
            m_old = m_ref[h]
            m_new = jnp.maximum(m_old, jnp.max(s, axis=0, keepdims=True))
            p = jnp.exp2(s - m_new).astype(BF16)
            acc_ref[h] = (jnp.exp2(m_old - m_new) * acc_ref[h]
                          + jnp.dot(vt, p, preferred_element_type=F32))
        m_ref[h] = m_new

    def sweep(blocks, next_block, own):
        items = [(j, h) for j in blocks for h in range(MOBA_HEADS)]
        for n, (j, h) in enumerate(items):
            ahead = n + SCORE_LOOKAHEAD
            if ahead < len(items):
                scores(*items[ahead], own)
            else:
                scores(next_block, ahead - len(items), False)
            softmax_pv(j, h, own)

    for h in range(SCORE_LOOKAHEAD):
        scores(qi, h, True)
    sweep([qi], 0, own=True)

    def past_blocks(t, carry):
        first = t * PAST_UNROLL
        blocks = [jnp.minimum(first + u, nblk - 1) for u in range(PAST_UNROLL)]
        sweep(blocks, jnp.minimum(first + PAST_UNROLL, nblk - 1), own=False)
        return carry

    lax.fori_loop(0, (qi + PAST_UNROLL - 1) // PAST_UNROLL, past_blocks, 0)

    for h in range(MOBA_HEADS):
        ot_ref[h * HEAD_DIM:(h + 1) * HEAD_DIM, :] = (acc_ref[h, 0:HEAD_DIM, :]
                                                      / acc_ref[h, HEAD_DIM:HEAD_DIM + 1, :])
    o_ref[...] = ot_ref[...].T.astype(o_ref.dtype)


def _moba(qt, kaug, vt, kmean, batch, seq):
    nblk = seq // MOBA_BLOCK
    width = MOBA_HEADS * HEAD_DIM
    qt = qt.reshape(batch, nblk, width, MOBA_BLOCK)
    kaug = kaug.reshape(batch, nblk, MOBA_BLOCK, MOBA_HEADS * AUG_W)
    vt_rows = MOBA_HEADS * VT_ROWS
    vt = vt.reshape(batch, nblk, vt_rows, MOBA_BLOCK)
    resident = 2 * (kaug.size // batch + vt.size // batch) * 2
    return pl.pallas_call(
        _moba_kernel,
        grid=(batch, nblk),
        in_specs=[pl.BlockSpec((1, 1, width, MOBA_BLOCK), lambda b, i: (b, i, 0, 0)),
                  pl.BlockSpec((1, nblk, MOBA_BLOCK, MOBA_HEADS * AUG_W), lambda b, i: (b, 0, 0, 0)),
                  pl.BlockSpec((1, nblk, vt_rows, MOBA_BLOCK), lambda b, i: (b, 0, 0, 0)),
                  pl.BlockSpec((1, nblk, MOBA_HEADS * AUG_W), lambda b, i: (b, 0, 0))],
        out_specs=pl.BlockSpec((MOBA_BLOCK, width), lambda b, i: (b * nblk + i, 0)),
        out_shape=jax.ShapeDtypeStruct((batch * seq, width), BF16),
        scratch_shapes=[pltpu.VMEM((MOBA_HEADS, AUG_W, MOBA_BLOCK), BF16),
                        pltpu.VMEM((MOBA_HEADS, AUG_W, MOBA_BLOCK), BF16),
                        pltpu.VMEM((MOBA_HEADS, MOBA_BLOCK, MOBA_BLOCK), F32),
                        pltpu.VMEM((MOBA_HEADS, 1, MOBA_BLOCK), F32),
                        pltpu.VMEM((MOBA_HEADS, VT_ROWS, MOBA_BLOCK), F32),
                        pltpu.VMEM((width, MOBA_BLOCK), F32)],
        compiler_params=pltpu.CompilerParams(dimension_semantics=("arbitrary", "arbitrary"),
                                             vmem_limit_bytes=_vmem_limit(resident)),
        name="moba",
    )(qt, kaug, vt, kmean)


def _mem_attn_kernel(q_ref, k_ref, v_ref, o_ref):
    outs = []
    for h in range(MEM_HEADS):
        cols = slice(h * HEAD_DIM, (h + 1) * HEAD_DIM)
        s = lax.dot_general(q_ref[:, cols], k_ref[0, :, cols], (((1,), (1,)), ((), ())),
                            preferred_element_type=F32)
        p = jnp.exp2(s - jnp.max(s, axis=-1, keepdims=True))
        l = jnp.sum(p, axis=-1, keepdims=True)
        outs.append(jnp.dot(p.astype(BF16), v_ref[0, :, cols], preferred_element_type=F32) / l)
    o_ref[...] = jnp.concatenate(outs, axis=1).astype(o_ref.dtype)


def _mem_attn(cq, kc, vc, batch, seq):
    n, width = cq.shape
    tm = TOKEN_TILE
    nt = seq // tm
    mlen = kc.shape[0] // batch
    kc = kc.reshape(batch, mlen, width)
    vc = vc.reshape(batch, mlen, width)
    kv_spec = pl.BlockSpec((1, mlen, width), lambda b, t: (b, 0, 0))
    tile = pl.BlockSpec((tm, width), lambda b, t: (b * nt + t, 0))
    return pl.pallas_call(
        _mem_attn_kernel,
        grid=(batch, nt),
        in_specs=[tile, kv_spec, kv_spec],
        out_specs=tile,
        out_shape=jax.ShapeDtypeStruct((n, width), BF16),
        compiler_params=pltpu.CompilerParams(dimension_semantics=("arbitrary", "arbitrary")),
        name="mem_attn",
    )(cq, kc, vc)


def _block_diag_ones(group):
    idx = jnp.arange(V7X_MXU_DIM) // group
    return (idx[:, None] == idx[None, :]).astype(BF16)


def _pack_block_diag(w):
    nb, bw, _ = w.shape
    per = V7X_MXU_DIM // bw
    w = w.reshape(nb // per, per, bw, bw)
    eye = jnp.eye(per, dtype=w.dtype)
    return jnp.einsum("gpij,pq->gpiqj", w, eye).reshape(nb // per, V7X_MXU_DIM, V7X_MXU_DIM)


def _pad_heads(w, heads):
    lead = w.shape[:-1]
    w = w.reshape(lead + (heads, HEAD_DIM))
    w = jnp.pad(w, [(0, 0)] * len(lead) + [(0, 0), (0, AUG_W - HEAD_DIM)])
    return w.reshape(lead + (heads * AUG_W,))


def _layer(x, mem, ffn1_norm, ffn1_w_in, ffn1_w_out, mix_norm, mem_norm, w_in,
           lru_conv_w, lru_conv_b, lru_a_w, lru_a_b, lru_x_w, lru_x_b, lru_lambda,
           moba_q_norm, moba_k_norm, mem_w_kv, mem_q_norm, mem_k_norm, w_out,
           ffn2_norm, ffn2_w_in, ffn2_w_out):
    batch, seq, d = x.shape
    n = batch * seq
    n_lru = lru_lambda.shape[0]
    wq = MOBA_HEADS * HEAD_DIM
    wc = MEM_HEADS * HEAD_DIM
    row = lambda v: v.reshape(1, -1).astype(F32)

    bd64 = _block_diag_ones(HEAD_DIM)
    bd128 = _block_diag_ones(AUG_W)

    w_in_b = w_in.astype(BF16)
    o = 2 * n_lru
    w_lru = w_in_b[:, :o]
    w_q = w_in_b[:, o:o + wq]
    w_k = _pad_heads(w_in_b[:, o + wq:o + 2 * wq], MOBA_HEADS)
    w_v = w_in_b[:, o + 2 * wq:o + 3 * wq]
    w_c = w_in_b[:, o + 3 * wq:]
    gq = row(jnp.tile(moba_q_norm, MOBA_HEADS))
    gk = row(_pad_heads(jnp.tile(moba_k_norm, MOBA_HEADS), MOBA_HEADS))
    gc = row(jnp.tile(mem_q_norm, MEM_HEADS))
    gck = row(jnp.tile(mem_k_norm, MEM_HEADS))
    slopes = 2.0 ** (-8.0 * jnp.arange(1, MOBA_HEADS + 1, dtype=F32) / MOBA_HEADS)
    lane = jnp.arange(AUG_W)
    is_alibi = (lane >= AUG_ALIBI) & (lane < AUG_ALIBI + 3)
    alibi = row(jnp.where(is_alibi[None, :], slopes[:, None] * LOG2E, 0.0))

    x2d = x.reshape(n, d)
    kc, vc = _mem_kv(mem.reshape(-1, d), row(mem_norm), mem_w_kv.astype(BF16), bd64, gck)

    x1 = _ffn(x2d, row(ffn1_norm), ffn1_w_in.astype(BF16), ffn1_w_out.astype(BF16))

    lru_x, lru_g, qt, kaug, vt, kmean, cq = _in_proj(
        x1, row(mix_norm), w_lru, w_q, w_k, w_v, w_c, gq, gk, gc, bd64, bd128, alibi, seq)
    kmean = kmean[:, :TOKEN_TILE // MOBA_BLOCK, :].reshape(batch, seq // MOBA_BLOCK, -1)

    y_lru = _lru(lru_x, lru_g, lru_conv_w.astype(F32), row(lru_conv_b),
                 _pack_block_diag(lru_a_w).astype(BF16), row(lru_a_b),
                 _pack_block_diag(lru_x_w).astype(BF16), row(lru_x_b), row(lru_lambda), batch, seq)
    y_moba = _moba(qt, kaug, vt, kmean, batch, seq)
    y_mem = _mem_attn(cq, kc, vc, batch, seq)

    out = _out_ffn(x1, y_lru, y_moba, y_mem, w_out.astype(BF16), row(ffn2_norm),
                   ffn2_w_in.astype(BF16), ffn2_w_out.astype(BF16))
    return out.reshape(batch, seq, d)


def kernel(x, mem, ffn1_norm, ffn1_w_in, ffn1_w_out, mix_norm, mem_norm, w_in, lru_conv_w, lru_conv_b,
           lru_a_w, lru_a_b, lru_x_w, lru_x_b, lru_lambda, moba_q_norm, moba_k_norm, mem_w_kv,
           mem_q_norm, mem_k_norm, w_out, ffn2_norm, ffn2_w_in, ffn2_w_out):
    params = (ffn1_norm, ffn1_w_in, ffn1_w_out, mix_norm, mem_norm, w_in, lru_conv_w, lru_conv_b,
              lru_a_w, lru_a_b, lru_x_w, lru_x_b, lru_lambda, moba_q_norm, moba_k_norm, mem_w_kv,
              mem_q_norm, mem_k_norm, w_out, ffn2_norm, ffn2_w_in, ffn2_w_out)
    for layer in range(ffn1_norm.shape[0]):
        x = _layer(x, mem, *(p[layer] for p in params))
    return x
```

```python
import functools
import math

import jax
import jax.numpy as jnp
from jax import lax
from jax.experimental import pallas as pl
from jax.experimental.pallas import tpu as pltpu

F32 = jnp.float32
BF16 = jnp.bfloat16

HEAD_DIM = 64
CONV_W = 4
LRU_C = 8.0
MOBA_HEADS = 4
MOBA_BLOCK = 256
MOBA_TOPK = 3
MEM_HEADS = 4
NORM_EPS = 1e-6
LOG2E = 1.4426950408889634
NEG_BIG = -1e30

V7X_MXU_DIM = 256
LANES = 128
SUBLANES = 8
V7X_VMEM_BYTES = 64 * 1024 * 1024

AUG_W = 128
AUG_ALIBI = 64
AUG_MASK = 80
VT_ROWS = HEAD_DIM + 16
SCORE_LOOKAHEAD = 5
PAST_UNROLL = 4
SCORE_SLOTS = 8
assert MOBA_HEADS <= SCORE_LOOKAHEAD < SCORE_SLOTS and (PAST_UNROLL * MOBA_HEADS) % SCORE_SLOTS == 0
PROB_SLOTS = 2

TOKEN_TILE = 512
LRU_TILE = 512
FFN_CHUNK = 256


def _vmem_limit(resident_bytes):
    return int(min(V7X_VMEM_BYTES - 8 * 1024 * 1024, resident_bytes + 24 * 1024 * 1024))


def _const_spec(shape):
    n = len(shape)
    return pl.BlockSpec(shape, lambda *_: (0,) * n, pipeline_mode=pl.Buffered(1))


def _rms_norm(x, g):
    ms = jnp.mean(x * x, axis=-1, keepdims=True)
    return x * lax.rsqrt(ms + NORM_EPS) * g


def _group_rms_norm(u, bd, g, group):
    sq = u * u
    hi = sq.astype(BF16)
    lo = (sq - hi.astype(F32)).astype(BF16)
    parts = []
    for c in range(u.shape[1] // V7X_MXU_DIM):
        sl = slice(c * V7X_MXU_DIM, (c + 1) * V7X_MXU_DIM)
        parts.append(jnp.dot(hi[:, sl], bd, preferred_element_type=F32)
                     + jnp.dot(lo[:, sl], bd, preferred_element_type=F32))
    ss = parts[0] if len(parts) == 1 else jnp.concatenate(parts, axis=1)
    return u * lax.rsqrt(ss * (1.0 / group) + NORM_EPS) * g


def _swiglu_half_step(x, g_ref, w_in_ref, w_out_ref, act_ref):
    d_ff = w_out_ref.shape[0]
    xn = _rms_norm(x, g_ref[...]).astype(BF16)
    for c in range(d_ff // FFN_CHUNK):
        lo = c * FFN_CHUNK
        a = jnp.dot(xn, w_in_ref[:, lo:lo + FFN_CHUNK], preferred_element_type=F32)
        b = jnp.dot(xn, w_in_ref[:, d_ff + lo:d_ff + lo + FFN_CHUNK], preferred_element_type=F32)
        act_ref[:, lo:lo + FFN_CHUNK] = (a * jax.nn.sigmoid(a) * b).astype(BF16)
    y = jnp.dot(act_ref[...], w_out_ref[...], preferred_element_type=F32)
    return x + 0.5 * y


def _ffn_kernel(x_ref, g_ref, w_in_ref, w_out_ref, o_ref, act_ref):
    o_ref[...] = _swiglu_half_step(x_ref[...], g_ref, w_in_ref, w_out_ref, act_ref)


def _ffn(x, g, w_in, w_out):
    n, d = x.shape
    d_ff = w_out.shape[0]
    tm = TOKEN_TILE
    resident = (w_in.size + w_out.size) * 2
    return pl.pallas_call(
        _ffn_kernel,
        grid=(n // tm,),
        in_specs=[pl.BlockSpec((tm, d), lambda i: (i, 0)),
                  _const_spec((1, d)), _const_spec(w_in.shape), _const_spec(w_out.shape)],
        out_specs=pl.BlockSpec((tm, d), lambda i: (i, 0)),
        out_shape=jax.ShapeDtypeStruct((n, d), F32),
        scratch_shapes=[pltpu.VMEM((tm, d_ff), BF16)],
        compiler_params=pltpu.CompilerParams(dimension_semantics=("arbitrary",),
                                             vmem_limit_bytes=_vmem_limit(resident)),
        name="ffn1",
    )(x, g, w_in, w_out)


def _out_ffn_kernel(x_ref, ylru_ref, ymoba_ref, ymem_ref, wo_ref, g_ref, w_in_ref, w_out_ref,
                    o_ref, act_ref):
    n_lru = ylru_ref.shape[1]
    n_moba = ymoba_ref.shape[1]
    x2 = (x_ref[...]
          + jnp.dot(ylru_ref[...], wo_ref[0:n_lru, :], preferred_element_type=F32)
          + jnp.dot(ymoba_ref[...], wo_ref[n_lru:n_lru + n_moba, :], preferred_element_type=F32)
          + jnp.dot(ymem_ref[...], wo_ref[n_lru + n_moba:, :], preferred_element_type=F32))
    o_ref[...] = _swiglu_half_step(x2, g_ref, w_in_ref, w_out_ref, act_ref)


def _out_ffn(x1, y_lru, y_moba, y_mem, w_o, g, w_in, w_out):
    n, d = x1.shape
    d_ff = w_out.shape[0]
    tm = TOKEN_TILE
    resident = (w_in.size + w_out.size + w_o.size) * 2

    def tile(w):
        return pl.BlockSpec((tm, w), lambda i: (i, 0))

    return pl.pallas_call(
        _out_ffn_kernel,
        grid=(n // tm,),
        in_specs=[tile(d), tile(y_lru.shape[1]), tile(y_moba.shape[1]), tile(y_mem.shape[1]),
                  _const_spec(w_o.shape), _const_spec((1, d)),
                  _const_spec(w_in.shape), _const_spec(w_out.shape)],
        out_specs=tile(d),
        out_shape=jax.ShapeDtypeStruct((n, d), F32),
        scratch_shapes=[pltpu.VMEM((tm, d_ff), BF16)],
        compiler_params=pltpu.CompilerParams(dimension_semantics=("arbitrary",),
                                             vmem_limit_bytes=_vmem_limit(resident)),
        name="out_ffn2",
    )(x1, y_lru, y_moba, y_mem, w_o, g, w_in, w_out)


def _mem_kv_kernel(mem_ref, g_ref, w_ref, bd_ref, gk_ref, kc_ref, vc_ref):
    width = kc_ref.shape[1]
    mn = _rms_norm(mem_ref[...], g_ref[...]).astype(BF16)
    kv = jnp.dot(mn, w_ref[...], preferred_element_type=F32)
    kc_ref[...] = _group_rms_norm(kv[:, :width], bd_ref[...], gk_ref[...], HEAD_DIM).astype(BF16)
    vc_ref[...] = kv[:, width:].astype(BF16)


def _mem_kv(mem2d, g, w_kv, bd64, gk):
    m = mem2d.shape[0]
    width = w_kv.shape[1] // 2
    return pl.pallas_call(
        _mem_kv_kernel,
        out_shape=(jax.ShapeDtypeStruct((m, width), BF16), jax.ShapeDtypeStruct((m, width), BF16)),
        name="mem_kv",
    )(mem2d, g, w_kv, bd64, gk)


def _in_proj_kernel(x_ref, g_ref, w_lru_ref, w_q_ref, w_k_ref, w_v_ref, w_c_ref,
                    gq_ref, gk_ref, gc_ref, bd64_ref, bd128_ref, alibi_ref,
                    lrux_ref, lrug_ref, qt_ref, kaug_ref, vt_ref, kmean_ref, cq_ref, *, seq):
    tm = x_ref.shape[0]
    n_lru = lrux_ref.shape[1]
    xn = _rms_norm(x_ref[...], g_ref[...]).astype(BF16)

    lrux_ref[...] = jnp.dot(xn, w_lru_ref[:, :n_lru], preferred_element_type=F32)
    lrug_ref[...] = jnp.dot(xn, w_lru_ref[:, n_lru:], preferred_element_type=F32)

    cq = jnp.dot(xn, w_c_ref[...], preferred_element_type=F32)
    cq = _group_rms_norm(cq, bd64_ref[...], gc_ref[...], HEAD_DIM)
    cq_ref[...] = (cq * (HEAD_DIM ** -0.5 * LOG2E)).astype(BF16)

    q = jnp.dot(xn, w_q_ref[...], preferred_element_type=F32)
    q = _group_rms_norm(q, bd64_ref[...], gq_ref[...], HEAD_DIM)
    v = jnp.dot(xn, w_v_ref[...], preferred_element_type=F32)
    ones = jnp.ones((VT_ROWS - HEAD_DIM, MOBA_BLOCK), BF16)
    for r in range(tm // MOBA_BLOCK):
        rows = slice(r * MOBA_BLOCK, (r + 1) * MOBA_BLOCK)
        qt_ref[r] = q[rows, :].T
        vt = v[rows, :].T.astype(BF16)
        for h in range(MOBA_HEADS):
            vt_ref[r, h * VT_ROWS:h * VT_ROWS + HEAD_DIM, :] = vt[h * HEAD_DIM:(h + 1) * HEAD_DIM, :]
            vt_ref[r, h * VT_ROWS + HEAD_DIM:(h + 1) * VT_ROWS, :] = ones

    k = jnp.dot(xn, w_k_ref[...], preferred_element_type=F32)
    k = _group_rms_norm(k, bd128_ref[...], gk_ref[...], HEAD_DIM)
    kmean_ref[...] = jnp.zeros(kmean_ref.shape, F32)
    for r in range(tm // MOBA_BLOCK):
        rows = slice(r * MOBA_BLOCK, (r + 1) * MOBA_BLOCK)
        kmean_ref[0, r:r + 1, :] = jnp.sum(k[rows, :], axis=0, keepdims=True) * (1.0 / MOBA_BLOCK)

    shape = k.shape
    row = lax.broadcasted_iota(jnp.int32, shape, 0)
    lane = lax.broadcasted_iota(jnp.int32, shape, 1) % AUG_W
    t = (pl.program_id(0) * tm + row) % seq
    bias = t.astype(F32) * alibi_ref[...]
    b_hi = bias.astype(BF16).astype(F32)
    b_mid = (bias - b_hi).astype(BF16).astype(F32)
    b_lo = bias - b_hi - b_mid
    pieces = jnp.where(lane == AUG_ALIBI, b_hi, jnp.where(lane == AUG_ALIBI + 1, b_mid, b_lo))
    onehot = jnp.where(lane - AUG_MASK == t // MOBA_BLOCK, 1.0, 0.0)
    is_alibi = (lane >= AUG_ALIBI) & (lane < AUG_ALIBI + 3)
    kaug_ref[...] = (k + jnp.where(is_alibi, pieces, onehot)).astype(BF16)


def _in_proj(x1, g, w_lru, w_q, w_k, w_v, w_c, gq, gk, gc, bd64, bd128, alibi, seq):
    n, d = x1.shape
    tm = TOKEN_TILE
    rb = tm // MOBA_BLOCK
    nblk = n // MOBA_BLOCK
    n_lru = w_lru.shape[1] // 2
    wq = w_q.shape[1]
    wk = w_k.shape[1]
    consts = (g, w_lru, w_q, w_k, w_v, w_c, gq, gk, gc, bd64, bd128, alibi)
    resident = sum(c.size * c.dtype.itemsize for c in consts)
    return pl.pallas_call(
        functools.partial(_in_proj_kernel, seq=seq),
        grid=(n // tm,),
        in_specs=[pl.BlockSpec((tm, d), lambda i: (i, 0))] + [_const_spec(c.shape) for c in consts],
        out_specs=[pl.BlockSpec((tm, n_lru), lambda i: (i, 0)),
                   pl.BlockSpec((tm, n_lru), lambda i: (i, 0)),
                   pl.BlockSpec((rb, wq, MOBA_BLOCK), lambda i: (i, 0, 0)),
                   pl.BlockSpec((tm, wk), lambda i: (i, 0)),
                   pl.BlockSpec((rb, MOBA_HEADS * VT_ROWS, MOBA_BLOCK), lambda i: (i, 0, 0)),
                   pl.BlockSpec((1, 8, wk), lambda i: (i, 0, 0)),
                   pl.BlockSpec((tm, wq), lambda i: (i, 0))],
        out_shape=(jax.ShapeDtypeStruct((n, n_lru), F32),
                   jax.ShapeDtypeStruct((n, n_lru), F32),
                   jax.ShapeDtypeStruct((nblk, wq, MOBA_BLOCK), F32),
                   jax.ShapeDtypeStruct((n, wk), BF16),
                   jax.ShapeDtypeStruct((nblk, MOBA_HEADS * VT_ROWS, MOBA_BLOCK), BF16),
                   jax.ShapeDtypeStruct((n // tm, 8, wk), F32),
                   jax.ShapeDtypeStruct((n, wq), BF16)),
        compiler_params=pltpu.CompilerParams(dimension_semantics=("arbitrary",),
                                             vmem_limit_bytes=_vmem_limit(resident)),
        name="in_proj",
    )(x1, *consts)


def _gelu_tanh(x):
    c = math.sqrt(2.0 / math.pi)
    return 0.5 * x * (1.0 + jnp.tanh(c * (x + 0.044715 * (x * x * x))))


def _lru_kernel(x_ref, g_ref, cw_ref, cb_ref, wa_ref, ba_ref, wx_ref, bx_ref, lam_ref,
                o_ref, xpad_ref, a_ref, b_ref, hl_ref, carry_ref):
    tl, c = x_ref.shape
    half = V7X_MXU_DIM
    ti = pl.program_id(1)

    @pl.when(ti == 0)
    def _():
        xpad_ref[0:8, :] = jnp.zeros((8, c), F32)
        carry_ref[...] = jnp.zeros(carry_ref.shape, F32)

    xpad_ref[8:8 + tl, :] = x_ref[...]
    xb = jnp.broadcast_to(cb_ref[...], (tl, c))
    for j in range(CONV_W):
        off = 8 - (CONV_W - 1) + j
        xb = xb + cw_ref[j:j + 1, :] * xpad_ref[off:off + tl, :]
    xpad_ref[0:8, :] = x_ref[tl - 8:tl, :]

    xbb = xb.astype(BF16)

    def gate(w_ref, bias_ref):
        z = jnp.concatenate(
            [jnp.dot(xbb[:, :half], w_ref[0], preferred_element_type=F32),
             jnp.dot(xbb[:, half:], w_ref[1], preferred_element_type=F32)], axis=1)
        return jax.nn.sigmoid(z + bias_ref[...])

    r = gate(wa_ref, ba_ref)
    i = gate(wx_ref, bx_ref)
    neg_lam = -lam_ref[...]
    softplus = jnp.maximum(neg_lam, 0.0) + jnp.log1p(jnp.exp(-jnp.abs(neg_lam)))
    log_a = (-LRU_C) * r * softplus
    a = jnp.exp(log_a)
    mult = jnp.sqrt(1.0 - a * a)
    row = lax.broadcasted_iota(jnp.int32, (tl, c), 0)
    start_row = jnp.where(ti == 0, 0, -1)
    mult = jnp.where(row == start_row, 1.0, mult)
    bt = mult * i * xb

    groups = tl // SUBLANES
    sub = lax.broadcasted_iota(jnp.int32, (groups, SUBLANES, LANES), 1)
    for k in range(c // LANES):
        lanes = slice(k * LANES, (k + 1) * LANES)
        a3 = a[:, lanes].reshape(groups, SUBLANES, LANES)
        b3 = bt[:, lanes].reshape(groups, SUBLANES, LANES)
        d = 1
        while d < SUBLANES:
            has_prev = sub >= d
            a_prev = pltpu.roll(a3, d, axis=1)
            b_prev = pltpu.roll(b3, d, axis=1)
            b3 = b3 + jnp.where(has_prev, a3, 0.0) * b_prev
            a3 = jnp.where(has_prev, a3 * a_prev, a3)
            d *= 2
        a_ref[:, lanes] = a3.reshape(tl, LANES)
        b_ref[:, lanes] = b3.reshape(tl, LANES)

    h = carry_ref[...]
    for grp in range(groups):
        rows = slice(grp * SUBLANES, (grp + 1) * SUBLANES)
        h_last = jnp.broadcast_to(h[SUBLANES - 1:SUBLANES, :], (SUBLANES, c))
        h = a_ref[rows, :] * h_last + b_ref[rows, :]
        hl_ref[rows, :] = h
    carry_ref[...] = h

    o_ref[...] = (hl_ref[...] * _gelu_tanh(g_ref[...])).astype(o_ref.dtype)


def _lru(lru_x, lru_g, cw, cb, wa, ba, wx, bx, lam, batch, seq):
    n, c = lru_x.shape
    tl = LRU_TILE
    nt = seq // tl
    consts = (cw, cb, wa, ba, wx, bx, lam)
    tile = pl.BlockSpec((tl, c), lambda b, t: (b * nt + t, 0))
    return pl.pallas_call(
        _lru_kernel,
        grid=(batch, nt),
        in_specs=[tile, tile] + [_const_spec(w.shape) for w in consts],
        out_specs=tile,
        out_shape=jax.ShapeDtypeStruct((n, c), BF16),
        scratch_shapes=[pltpu.VMEM((tl + SUBLANES, c), F32)]
        + [pltpu.VMEM((tl, c), F32)] * 3
        + [pltpu.VMEM((SUBLANES, c), F32)],
        compiler_params=pltpu.CompilerParams(dimension_semantics=("arbitrary", "arbitrary")),
        name="lru",
    )(lru_x, lru_g, *consts)


def _moba_kernel(qt_ref, kaug_ref, vt_ref, kmean_ref, o_ref,
                 qaug_ref, qown_ref, s_ref, p_ref, alpha_ref, m_ref, acc_ref, ot_ref):
    qi = pl.program_id(1)
    nblk = kmean_ref.shape[1]
    blk = MOBA_BLOCK

    def build_query_operands(h):
        qt = qt_ref[0, 0, h * HEAD_DIM:(h + 1) * HEAD_DIM, :]
        qs = (qt * (HEAD_DIM ** -0.5 * LOG2E)).astype(BF16)
        sub = lax.broadcasted_iota(jnp.int32, (AUG_MASK - AUG_ALIBI, blk), 0)
        ones = jnp.where(sub < 3, 1.0, 0.0).astype(BF16)
        for ref in (qaug_ref, qown_ref):
            ref[h, 0:HEAD_DIM, :] = qs
            ref[h, AUG_ALIBI:AUG_MASK, :] = ones
        qown_ref[h, AUG_MASK:, :] = jnp.zeros((AUG_W - AUG_MASK, blk), BF16)

    def build_block_mask(h):
        blk_id = lax.broadcasted_iota(jnp.int32, (nblk, blk), 0).astype(F32)
        qt = qt_ref[0, 0, h * HEAD_DIM:(h + 1) * HEAD_DIM, :]
        gate = jnp.dot(kmean_ref[0, :, h * AUG_W:h * AUG_W + HEAD_DIM], qt,
                       preferred_element_type=F32, precision=lax.Precision.HIGHEST)
        gate = jnp.where(blk_id < qi.astype(F32), gate, -jnp.inf)
        keep = blk_id < 0.0
        for _ in range(MOBA_TOPK):
            best = jnp.max(gate, axis=0, keepdims=True)
            first = jnp.min(jnp.where(gate == best, blk_id, float(nblk)), axis=0, keepdims=True)
            pick = (blk_id == first) & (best > -jnp.inf)
            keep = keep | pick
            gate = jnp.where(pick, -jnp.inf, gate)
        qaug_ref[h, AUG_MASK:AUG_MASK + nblk, :] = jnp.where(keep, 0.0, NEG_BIG).astype(BF16)
        qaug_ref[h, AUG_MASK + nblk:, :] = jnp.zeros((AUG_W - AUG_MASK - nblk, blk), BF16)

    kk = lax.broadcasted_iota(jnp.int32, (blk, blk), 0)
    qq = lax.broadcasted_iota(jnp.int32, (blk, blk), 1)
    causal = kk <= qq

    per_trip = PAST_UNROLL * MOBA_HEADS

    def past_block(trip, u):
        return jnp.minimum(trip * PAST_UNROLL + u, nblk - 1)

    def stream_item(trip, n):
        if n < 0:
            return qi, n + MOBA_HEADS, True, (n + MOBA_HEADS) % SCORE_SLOTS
        if n >= per_trip:
            trip, n = trip + 1, n - per_trip
        return past_block(trip, n // MOBA_HEADS), n % MOBA_HEADS, False, (n + MOBA_HEADS) % SCORE_SLOTS

    def scores(j, h, own, slot):
        q_ref = qown_ref if own else qaug_ref
        s_ref[slot] = jnp.dot(kaug_ref[0, j, :, h * AUG_W:(h + 1) * AUG_W], q_ref[h],
                              preferred_element_type=F32)

    def probabilities(j, h, own, slot):
        s = s_ref[slot]
        if own:
            s = jnp.where(causal, s, NEG_BIG)
            m_new = jnp.max(s, axis=0, keepdims=True)
        else:
            m_old = m_ref[h]
            m_new = jnp.maximum(m_old, jnp.max(s, axis=0, keepdims=True))
            alpha_ref[slot % PROB_SLOTS] = jnp.exp2(m_old - m_new)
        p_ref[slot % PROB_SLOTS] = jnp.exp2(s - m_new).astype(BF16)
        m_ref[h] = m_new

    def accumulate(j, h, own, slot):
        pv = jnp.dot(vt_ref[0, j, h * VT_ROWS:(h + 1) * VT_ROWS, :], p_ref[slot % PROB_SLOTS],
                     preferred_element_type=F32)
        acc_ref[h] = pv if own else alpha_ref[slot % PROB_SLOTS] * acc_ref[h] + pv

    for h in range(MOBA_HEADS):
        build_query_operands(h)
        build_block_mask(h)
    for n in range(-MOBA_HEADS, -MOBA_HEADS + SCORE_LOOKAHEAD):
        scores(*stream_item(0, n))
    probabilities(*stream_item(0, -MOBA_HEADS))
    for n in range(-MOBA_HEADS, 0):
        scores(*stream_item(0, n + SCORE_LOOKAHEAD))
        probabilities(*stream_item(0, n + 1))
        accumulate(*stream_item(0, n))

    def past_trip(trip, carry):
        for n in range(per_trip):
            scores(*stream_item(trip, n + SCORE_LOOKAHEAD))
            probabilities(*stream_item(trip, n + 1))
            accumulate(*stream_item(trip, n))
        return carry

    lax.fori_loop(0, (qi + PAST_UNROLL - 1) // PAST_UNROLL, past_trip, 0)

    for h in range(MOBA_HEADS):
        ot_ref[h * HEAD_DIM:(h + 1) * HEAD_DIM, :] = (acc_ref[h, 0:HEAD_DIM, :]
                                                      / acc_ref[h, HEAD_DIM:HEAD_DIM + 1, :])
    o_ref[...] = ot_ref[...].T.astype(o_ref.dtype)


def _moba(qt, kaug, vt, kmean, batch, seq):
    nblk = seq // MOBA_BLOCK
    width = MOBA_HEADS * HEAD_DIM
    qt = qt.reshape(batch, nblk, width, MOBA_BLOCK)
    kaug = kaug.reshape(batch, nblk, MOBA_BLOCK, MOBA_HEADS * AUG_W)
    vt_rows = MOBA_HEADS * VT_ROWS
    vt = vt.reshape(batch, nblk, vt_rows, MOBA_BLOCK)
    resident = 2 * (kaug.size // batch + vt.size // batch) * 2
    return pl.pallas_call(
        _moba_kernel,
        grid=(batch, nblk),
        in_specs=[pl.BlockSpec((1, 1, width, MOBA_BLOCK), lambda b, i: (b, i, 0, 0)),
                  pl.BlockSpec((1, nblk, MOBA_BLOCK, MOBA_HEADS * AUG_W), lambda b, i: (b, 0, 0, 0)),
                  pl.BlockSpec((1, nblk, vt_rows, MOBA_BLOCK), lambda b, i: (b, 0, 0, 0)),
                  pl.BlockSpec((1, nblk, MOBA_HEADS * AUG_W), lambda b, i: (b, 0, 0))],
        out_specs=pl.BlockSpec((MOBA_BLOCK, width), lambda b, i: (b * nblk + i, 0)),
        out_shape=jax.ShapeDtypeStruct((batch * seq, width), BF16),
        scratch_shapes=[pltpu.VMEM((MOBA_HEADS, AUG_W, MOBA_BLOCK), BF16),
                        pltpu.VMEM((MOBA_HEADS, AUG_W, MOBA_BLOCK), BF16),
                        pltpu.VMEM((SCORE_SLOTS, MOBA_BLOCK, MOBA_BLOCK), F32),
                        pltpu.VMEM((PROB_SLOTS, MOBA_BLOCK, MOBA_BLOCK), BF16),
                        pltpu.VMEM((PROB_SLOTS, 1, MOBA_BLOCK), F32),
                        pltpu.VMEM((MOBA_HEADS, 1, MOBA_BLOCK), F32),
                        pltpu.VMEM((MOBA_HEADS, VT_ROWS, MOBA_BLOCK), F32),
                        pltpu.VMEM((width, MOBA_BLOCK), F32)],
        compiler_params=pltpu.CompilerParams(dimension_semantics=("arbitrary", "arbitrary"),
                                             vmem_limit_bytes=_vmem_limit(resident)),
        name="moba",
    )(qt, kaug, vt, kmean)


def _mem_attn_kernel(q_ref, k_ref, v_ref, o_ref):
    outs = []
    for h in range(MEM_HEADS):
        cols = slice(h * HEAD_DIM, (h + 1) * HEAD_DIM)
        s = lax.dot_general(q_ref[:, cols], k_ref[0, :, cols], (((1,), (1,)), ((), ())),
                            preferred_element_type=F32)
        p = jnp.exp2(s - jnp.max(s, axis=-1, keepdims=True))
        l = jnp.sum(p, axis=-1, keepdims=True)
        outs.append(jnp.dot(p.astype(BF16), v_ref[0, :, cols], preferred_element_type=F32) / l)
    o_ref[...] = jnp.concatenate(outs, axis=1).astype(o_ref.dtype)


def _mem_attn(cq, kc, vc, batch, seq):
    n, width = cq.shape
    tm = TOKEN_TILE
    nt = seq // tm
    mlen = kc.shape[0] // batch
    kc = kc.reshape(batch, mlen, width)
    vc = vc.reshape(batch, mlen, width)
    kv_spec = pl.BlockSpec((1, mlen, width), lambda b, t: (b, 0, 0))
    tile = pl.BlockSpec((tm, width), lambda b, t: (b * nt + t, 0))
    return pl.pallas_call(
        _mem_attn_kernel,
        grid=(batch, nt),
        in_specs=[tile, kv_spec, kv_spec],
        out_specs=tile,
        out_shape=jax.ShapeDtypeStruct((n, width), BF16),
        compiler_params=pltpu.CompilerParams(dimension_semantics=("arbitrary", "arbitrary")),
        name="mem_attn",
    )(cq, kc, vc)


def _block_diag_ones(group):
    idx = jnp.arange(V7X_MXU_DIM) // group
    return (idx[:, None] == idx[None, :]).astype(BF16)


def _pack_block_diag(w):
    nb, bw, _ = w.shape
    per = V7X_MXU_DIM // bw
    w = w.reshape(nb // per, per, bw, bw)
    eye = jnp.eye(per, dtype=w.dtype)
    return jnp.einsum("gpij,pq->gpiqj", w, eye).reshape(nb // per, V7X_MXU_DIM, V7X_MXU_DIM)


def _pad_heads(w, heads):
    lead = w.shape[:-1]
    w = w.reshape(lead + (heads, HEAD_DIM))
    w = jnp.pad(w, [(0, 0)] * len(lead) + [(0, 0), (0, AUG_W - HEAD_DIM)])
    return w.reshape(lead + (heads * AUG_W,))


def _layer(x, mem, ffn1_norm, ffn1_w_in, ffn1_w_out, mix_norm, mem_norm, w_in,
           lru_conv_w, lru_conv_b, lru_a_w, lru_a_b, lru_x_w, lru_x_b, lru_lambda,
           moba_q_norm, moba_k_norm, mem_w_kv, mem_q_norm, mem_k_norm, w_out,
           ffn2_norm, ffn2_w_in, ffn2_w_out):
    batch, seq, d = x.shape
    n = batch * seq
    n_lru = lru_lambda.shape[0]
    wq = MOBA_HEADS * HEAD_DIM
    wc = MEM_HEADS * HEAD_DIM
    row = lambda v: v.reshape(1, -1).astype(F32)

    bd64 = _block_diag_ones(HEAD_DIM)
    bd128 = _block_diag_ones(AUG_W)

    w_in_b = w_in.astype(BF16)
    o = 2 * n_lru
    w_lru = w_in_b[:, :o]
    w_q = w_in_b[:, o:o + wq]
    w_k = _pad_heads(w_in_b[:, o + wq:o + 2 * wq], MOBA_HEADS)
    w_v = w_in_b[:, o + 2 * wq:o + 3 * wq]
    w_c = w_in_b[:, o + 3 * wq:]
    gq = row(jnp.tile(moba_q_norm, MOBA_HEADS))
    gk = row(_pad_heads(jnp.tile(moba_k_norm, MOBA_HEADS), MOBA_HEADS))
    gc = row(jnp.tile(mem_q_norm, MEM_HEADS))
    gck = row(jnp.tile(mem_k_norm, MEM_HEADS))
    slopes = 2.0 ** (-8.0 * jnp.arange(1, MOBA_HEADS + 1, dtype=F32) / MOBA_HEADS)
    lane = jnp.arange(AUG_W)
    is_alibi = (lane >= AUG_ALIBI) & (lane < AUG_ALIBI + 3)
    alibi = row(jnp.where(is_alibi[None, :], slopes[:, None] * LOG2E, 0.0))

    x2d = x.reshape(n, d)
    kc, vc = _mem_kv(mem.reshape(-1, d), row(mem_norm), mem_w_kv.astype(BF16), bd64, gck)

    x1 = _ffn(x2d, row(ffn1_norm), ffn1_w_in.astype(BF16), ffn1_w_out.astype(BF16))

    lru_x, lru_g, qt, kaug, vt, kmean, cq = _in_proj(
        x1, row(mix_norm), w_lru, w_q, w_k, w_v, w_c, gq, gk, gc, bd64, bd128, alibi, seq)
    kmean = kmean[:, :TOKEN_TILE // MOBA_BLOCK, :].reshape(batch, seq // MOBA_BLOCK, -1)

    y_lru = _lru(lru_x, lru_g, lru_conv_w.astype(F32), row(lru_conv_b),
                 _pack_block_diag(lru_a_w).astype(BF16), row(lru_a_b),
                 _pack_block_diag(lru_x_w).astype(BF16), row(lru_x_b), row(lru_lambda), batch, seq)
    y_moba = _moba(qt, kaug, vt, kmean, batch, seq)
    y_mem = _mem_attn(cq, kc, vc, batch, seq)

    out = _out_ffn(x1, y_lru, y_moba, y_mem, w_out.astype(BF16), row(ffn2_norm),
                   ffn2_w_in.astype(BF16), ffn2_w_out.astype(BF16))
    return out.reshape(batch, seq, d)


def kernel(x, mem, ffn1_norm, ffn1_w_in, ffn1_w_out, mix_norm, mem_norm, w_in, lru_conv_w, lru_conv_b,
           lru_a_w, lru_a_b, lru_x_w, lru_x_b, lru_lambda, moba_q_norm, moba_k_norm, mem_w_kv,
           mem_q_norm, mem_k_norm, w_out, ffn2_norm, ffn2_w_in, ffn2_w_out):
    params = (ffn1_norm, ffn1_w_in, ffn1_w_out, mix_norm, mem_norm, w_in, lru_conv_w, lru_conv_b,
              lru_a_w, lru_a_b, lru_x_w, lru_x_b, lru_lambda, moba_q_norm, moba_k_norm, mem_w_kv,
              mem_q_norm, mem_k_norm, w_out, ffn2_norm, ffn2_w_in, ffn2_w_out)
    for layer in range(ffn1_norm.shape[0]):
        x = _layer(x, mem, *(p[layer] for p in params))
    return x
```

```python
import functools
import math

import jax
import jax.numpy as jnp
from jax import lax
from jax.experimental import pallas as pl
from jax.experimental.pallas import tpu as pltpu

F32 = jnp.float32
BF16 = jnp.bfloat16

HEAD_DIM = 64
CONV_W = 4
LRU_C = 8.0
MOBA_HEADS = 4
MOBA_BLOCK = 256
MOBA_TOPK = 3
MEM_HEADS = 4
NORM_EPS = 1e-6
LOG2E = 1.4426950408889634
NEG_BIG = -1e30

V7X_MXU_DIM = 256
LANES = 128
SUBLANES = 8
V7X_VMEM_BYTES = 64 * 1024 * 1024

AUG_W = 128
AUG_ALIBI = 64
AUG_MASK = 80
VT_ROWS = HEAD_DIM + 16
SCORE_LOOKAHEAD = 5
PAST_UNROLL = 4
SCORE_SLOTS = 8
assert MOBA_HEADS <= SCORE_LOOKAHEAD < SCORE_SLOTS and (PAST_UNROLL * MOBA_HEADS) % SCORE_SLOTS == 0
PROB_SLOTS = 2

TOKEN_TILE = 512
FFN_CHUNK = 256


def _vmem_limit(resident_bytes):
    return int(min(V7X_VMEM_BYTES - 8 * 1024 * 1024, resident_bytes + 24 * 1024 * 1024))


def _const_spec(shape):
    n = len(shape)
    return pl.BlockSpec(shape, lambda *_: (0,) * n, pipeline_mode=pl.Buffered(1))


def _rms_norm(x, g):
    ms = jnp.mean(x * x, axis=-1, keepdims=True)
    return x * lax.rsqrt(ms + NORM_EPS) * g


def _group_rms_norm(u, bd, g, group):
    sq = u * u
    hi = sq.astype(BF16)
    lo = (sq - hi.astype(F32)).astype(BF16)
    parts = []
    for c in range(u.shape[1] // V7X_MXU_DIM):
        sl = slice(c * V7X_MXU_DIM, (c + 1) * V7X_MXU_DIM)
        parts.append(jnp.dot(hi[:, sl], bd, preferred_element_type=F32)
                     + jnp.dot(lo[:, sl], bd, preferred_element_type=F32))
    ss = parts[0] if len(parts) == 1 else jnp.concatenate(parts, axis=1)
    return u * lax.rsqrt(ss * (1.0 / group) + NORM_EPS) * g


def _swiglu_half_step(x, g_ref, w_in_ref, w_out_ref, act_ref):
    d_ff = w_out_ref.shape[0]
    xn = _rms_norm(x, g_ref[...]).astype(BF16)
    for c in range(d_ff // FFN_CHUNK):
        lo = c * FFN_CHUNK
        a = jnp.dot(xn, w_in_ref[:, lo:lo + FFN_CHUNK], preferred_element_type=F32)
        b = jnp.dot(xn, w_in_ref[:, d_ff + lo:d_ff + lo + FFN_CHUNK], preferred_element_type=F32)
        act_ref[:, lo:lo + FFN_CHUNK] = (a * jax.nn.sigmoid(a) * b).astype(BF16)
    y = jnp.dot(act_ref[...], w_out_ref[...], preferred_element_type=F32)
    return x + 0.5 * y


def _ffn_kernel(x_ref, g_ref, w_in_ref, w_out_ref, o_ref, act_ref):
    o_ref[...] = _swiglu_half_step(x_ref[...], g_ref, w_in_ref, w_out_ref, act_ref)


def _ffn(x, g, w_in, w_out):
    n, d = x.shape
    d_ff = w_out.shape[0]
    tm = TOKEN_TILE
    resident = (w_in.size + w_out.size) * 2
    return pl.pallas_call(
        _ffn_kernel,
        grid=(n // tm,),
        in_specs=[pl.BlockSpec((tm, d), lambda i: (i, 0)),
                  _const_spec((1, d)), _const_spec(w_in.shape), _const_spec(w_out.shape)],
        out_specs=pl.BlockSpec((tm, d), lambda i: (i, 0)),
        out_shape=jax.ShapeDtypeStruct((n, d), F32),
        scratch_shapes=[pltpu.VMEM((tm, d_ff), BF16)],
        compiler_params=pltpu.CompilerParams(dimension_semantics=("arbitrary",),
                                             vmem_limit_bytes=_vmem_limit(resident)),
        name="ffn1",
    )(x, g, w_in, w_out)


def _out_ffn_kernel(x_ref, ylru_ref, ymoba_ref, ymem_ref, wo_ref, g_ref, w_in_ref, w_out_ref,
                    o_ref, act_ref):
    n_lru = ylru_ref.shape[1]
    n_moba = ymoba_ref.shape[1]
    x2 = (x_ref[...]
          + jnp.dot(ylru_ref[...], wo_ref[0:n_lru, :], preferred_element_type=F32)
          + jnp.dot(ymoba_ref[...], wo_ref[n_lru:n_lru + n_moba, :], preferred_element_type=F32)
          + jnp.dot(ymem_ref[...], wo_ref[n_lru + n_moba:, :], preferred_element_type=F32))
    o_ref[...] = _swiglu_half_step(x2, g_ref, w_in_ref, w_out_ref, act_ref)


def _out_ffn(x1, y_lru, y_moba, y_mem, w_o, g, w_in, w_out):
    n, d = x1.shape
    d_ff = w_out.shape[0]
    tm = TOKEN_TILE
    resident = (w_in.size + w_out.size + w_o.size) * 2

    def tile(w):
        return pl.BlockSpec((tm, w), lambda i: (i, 0))

    return pl.pallas_call(
        _out_ffn_kernel,
        grid=(n // tm,),
        in_specs=[tile(d), tile(y_lru.shape[1]), tile(y_moba.shape[1]), tile(y_mem.shape[1]),
                  _const_spec(w_o.shape), _const_spec((1, d)),
                  _const_spec(w_in.shape), _const_spec(w_out.shape)],
        out_specs=tile(d),
        out_shape=jax.ShapeDtypeStruct((n, d), F32),
        scratch_shapes=[pltpu.VMEM((tm, d_ff), BF16)],
        compiler_params=pltpu.CompilerParams(dimension_semantics=("arbitrary",),
                                             vmem_limit_bytes=_vmem_limit(resident)),
        name="out_ffn2",
    )(x1, y_lru, y_moba, y_mem, w_o, g, w_in, w_out)


def _mem_kv_kernel(mem_ref, g_ref, w_ref, bd_ref, gk_ref, kc_ref, vc_ref):
    width = kc_ref.shape[1]
    mn = _rms_norm(mem_ref[...], g_ref[...]).astype(BF16)
    kv = jnp.dot(mn, w_ref[...], preferred_element_type=F32)
    kc_ref[...] = _group_rms_norm(kv[:, :width], bd_ref[...], gk_ref[...], HEAD_DIM).astype(BF16)
    vc_ref[...] = kv[:, width:].astype(BF16)


def _mem_kv(mem2d, g, w_kv, bd64, gk):
    m = mem2d.shape[0]
    width = w_kv.shape[1] // 2
    return pl.pallas_call(
        _mem_kv_kernel,
        out_shape=(jax.ShapeDtypeStruct((m, width), BF16), jax.ShapeDtypeStruct((m, width), BF16)),
        name="mem_kv",
    )(mem2d, g, w_kv, bd64, gk)


def _in_proj_kernel(x_ref, g_ref, w_lru_ref, w_q_ref, w_k_ref, w_v_ref, w_c_ref,
                    gq_ref, gk_ref, gc_ref, bd64_ref, bd128_ref, alibi_ref,
                    cw_ref, cb_ref, wa_ref, ba_ref, wx_ref, bx_ref, lam_ref,
                    ylru_ref, qt_ref, kaug_ref, vt_ref, kmean_ref, cq_ref,
                    xpad_ref, a_ref, b_ref, hl_ref, carry_ref, *, seq):
    tm = x_ref.shape[0]
    n_lru = ylru_ref.shape[1]
    xn = _rms_norm(x_ref[...], g_ref[...]).astype(BF16)

    lru_x = jnp.dot(xn, w_lru_ref[:, :n_lru], preferred_element_type=F32)
    lru_g = jnp.dot(xn, w_lru_ref[:, n_lru:], preferred_element_type=F32)
    seq_start = pl.program_id(0) % (seq // tm) == 0
    ylru_ref[...] = _lru_tile(lru_x, lru_g, seq_start, cw_ref, cb_ref, wa_ref, ba_ref, wx_ref, bx_ref,
                              lam_ref, xpad_ref, a_ref, b_ref, hl_ref, carry_ref).astype(BF16)

    cq = jnp.dot(xn, w_c_ref[...], preferred_element_type=F32)
    cq = _group_rms_norm(cq, bd64_ref[...], gc_ref[...], HEAD_DIM)
    cq_ref[...] = (cq * (HEAD_DIM ** -0.5 * LOG2E)).astype(BF16)

    q = jnp.dot(xn, w_q_ref[...], preferred_element_type=F32)
    q = _group_rms_norm(q, bd64_ref[...], gq_ref[...], HEAD_DIM)
    v = jnp.dot(xn, w_v_ref[...], preferred_element_type=F32)
    ones = jnp.ones((VT_ROWS - HEAD_DIM, MOBA_BLOCK), BF16)
    for r in range(tm // MOBA_BLOCK):
        rows = slice(r * MOBA_BLOCK, (r + 1) * MOBA_BLOCK)
        qt_ref[r] = q[rows, :].T
        vt = v[rows, :].T.astype(BF16)
        for h in range(MOBA_HEADS):
            vt_ref[r, h * VT_ROWS:h * VT_ROWS + HEAD_DIM, :] = vt[h * HEAD_DIM:(h + 1) * HEAD_DIM, :]
            vt_ref[r, h * VT_ROWS + HEAD_DIM:(h + 1) * VT_ROWS, :] = ones

    k = jnp.dot(xn, w_k_ref[...], preferred_element_type=F32)
    k = _group_rms_norm(k, bd128_ref[...], gk_ref[...], HEAD_DIM)
    kmean_ref[...] = jnp.zeros(kmean_ref.shape, F32)
    for r in range(tm // MOBA_BLOCK):
        rows = slice(r * MOBA_BLOCK, (r + 1) * MOBA_BLOCK)
        kmean_ref[0, r:r + 1, :] = jnp.sum(k[rows, :], axis=0, keepdims=True) * (1.0 / MOBA_BLOCK)

    shape = k.shape
    row = lax.broadcasted_iota(jnp.int32, shape, 0)
    lane = lax.broadcasted_iota(jnp.int32, shape, 1) % AUG_W
    t = (pl.program_id(0) * tm + row) % seq
    bias = t.astype(F32) * alibi_ref[...]
    b_hi = bias.astype(BF16).astype(F32)
    b_mid = (bias - b_hi).astype(BF16).astype(F32)
    b_lo = bias - b_hi - b_mid
    pieces = jnp.where(lane == AUG_ALIBI, b_hi, jnp.where(lane == AUG_ALIBI + 1, b_mid, b_lo))
    onehot = jnp.where(lane - AUG_MASK == t // MOBA_BLOCK, 1.0, 0.0)
    is_alibi = (lane >= AUG_ALIBI) & (lane < AUG_ALIBI + 3)
    kaug_ref[...] = (k + jnp.where(is_alibi, pieces, onehot)).astype(BF16)


def _in_proj(x1, g, w_lru, w_q, w_k, w_v, w_c, gq, gk, gc, bd64, bd128, alibi, lru_consts, seq):
    n, d = x1.shape
    tm = TOKEN_TILE
    rb = tm // MOBA_BLOCK
    nblk = n // MOBA_BLOCK
    n_lru = w_lru.shape[1] // 2
    wq = w_q.shape[1]
    wk = w_k.shape[1]
    consts = (g, w_lru, w_q, w_k, w_v, w_c, gq, gk, gc, bd64, bd128, alibi) + tuple(lru_consts)
    resident = sum(c.size * c.dtype.itemsize for c in consts)
    return pl.pallas_call(
        functools.partial(_in_proj_kernel, seq=seq),
        grid=(n // tm,),
        in_specs=[pl.BlockSpec((tm, d), lambda i: (i, 0))] + [_const_spec(c.shape) for c in consts],
        out_specs=[pl.BlockSpec((tm, n_lru), lambda i: (i, 0)),
                   pl.BlockSpec((rb, wq, MOBA_BLOCK), lambda i: (i, 0, 0)),
                   pl.BlockSpec((tm, wk), lambda i: (i, 0)),
                   pl.BlockSpec((rb, MOBA_HEADS * VT_ROWS, MOBA_BLOCK), lambda i: (i, 0, 0)),
                   pl.BlockSpec((1, 8, wk), lambda i: (i, 0, 0)),
                   pl.BlockSpec((tm, wq), lambda i: (i, 0))],
        out_shape=(jax.ShapeDtypeStruct((n, n_lru), BF16),
                   jax.ShapeDtypeStruct((nblk, wq, MOBA_BLOCK), F32),
                   jax.ShapeDtypeStruct((n, wk), BF16),
                   jax.ShapeDtypeStruct((nblk, MOBA_HEADS * VT_ROWS, MOBA_BLOCK), BF16),
                   jax.ShapeDtypeStruct((n // tm, 8, wk), F32),
                   jax.ShapeDtypeStruct((n, wq), BF16)),
        scratch_shapes=_lru_scratch(tm, n_lru),
        compiler_params=pltpu.CompilerParams(dimension_semantics=("arbitrary",),
                                             vmem_limit_bytes=_vmem_limit(resident)),
        name="in_proj",
    )(x1, *consts)


def _gelu_tanh(x):
    c = math.sqrt(2.0 / math.pi)
    return 0.5 * x * (1.0 + jnp.tanh(c * (x + 0.044715 * (x * x * x))))


def _lru_tile(x, g, seq_start, cw_ref, cb_ref, wa_ref, ba_ref, wx_ref, bx_ref, lam_ref,
              xpad_ref, a_ref, b_ref, hl_ref, carry_ref):
    tl, c = x.shape
    half = V7X_MXU_DIM

    @pl.when(seq_start)
    def _():
        xpad_ref[0:SUBLANES, :] = jnp.zeros((SUBLANES, c), F32)
        carry_ref[...] = jnp.zeros(carry_ref.shape, F32)

    xpad_ref[SUBLANES:SUBLANES + tl, :] = x
    xb = jnp.broadcast_to(cb_ref[...], (tl, c))
    for j in range(CONV_W):
        off = SUBLANES - (CONV_W - 1) + j
        xb = xb + cw_ref[j:j + 1, :] * xpad_ref[off:off + tl, :]
    xpad_ref[0:SUBLANES, :] = x[tl - SUBLANES:tl, :]

    xbb = xb.astype(BF16)

    def gate(w_ref, bias_ref):
        z = jnp.concatenate(
            [jnp.dot(xbb[:, :half], w_ref[0], preferred_element_type=F32),
             jnp.dot(xbb[:, half:], w_ref[1], preferred_element_type=F32)], axis=1)
        return jax.nn.sigmoid(z + bias_ref[...])

    r = gate(wa_ref, ba_ref)
    i = gate(wx_ref, bx_ref)
    neg_lam = -lam_ref[...]
    softplus = jnp.maximum(neg_lam, 0.0) + jnp.log1p(jnp.exp(-jnp.abs(neg_lam)))
    log_a = (-LRU_C) * r * softplus
    a = jnp.exp(log_a)
    mult = jnp.sqrt(1.0 - a * a)
    row = lax.broadcasted_iota(jnp.int32, (tl, c), 0)
    start_row = jnp.where(seq_start, 0, -1)
    mult = jnp.where(row == start_row, 1.0, mult)
    bt = mult * i * xb

    groups = tl // SUBLANES
    sub = lax.broadcasted_iota(jnp.int32, (groups, SUBLANES, LANES), 1)
    for k in range(c // LANES):
        lanes = slice(k * LANES, (k + 1) * LANES)
        a3 = a[:, lanes].reshape(groups, SUBLANES, LANES)
        b3 = bt[:, lanes].reshape(groups, SUBLANES, LANES)
        d = 1
        while d < SUBLANES:
            has_prev = sub >= d
            a_prev = pltpu.roll(a3, d, axis=1)
            b_prev = pltpu.roll(b3, d, axis=1)
            b3 = b3 + jnp.where(has_prev, a3, 0.0) * b_prev
            a3 = jnp.where(has_prev, a3 * a_prev, a3)
            d *= 2
        a_ref[:, lanes] = a3.reshape(tl, LANES)
        b_ref[:, lanes] = b3.reshape(tl, LANES)

    h = carry_ref[...]
    for grp in range(groups):
        rows = slice(grp * SUBLANES, (grp + 1) * SUBLANES)
        h_last = jnp.broadcast_to(h[SUBLANES - 1:SUBLANES, :], (SUBLANES, c))
        h = a_ref[rows, :] * h_last + b_ref[rows, :]
        hl_ref[rows, :] = h
    carry_ref[...] = h

    return hl_ref[...] * _gelu_tanh(g)


def _lru_scratch(tl, c):
    return ([pltpu.VMEM((tl + SUBLANES, c), F32)] + [pltpu.VMEM((tl, c), F32)] * 3
            + [pltpu.VMEM((SUBLANES, c), F32)])


def _moba_kernel(qt_ref, kaug_ref, vt_ref, kmean_ref, o_ref,
                 qaug_ref, qown_ref, s_ref, p_ref, alpha_ref, m_ref, acc_ref, ot_ref):
    qi = pl.program_id(1)
    nblk = kmean_ref.shape[1]
    blk = MOBA_BLOCK

    def build_query_operands(h):
        qt = qt_ref[0, 0, h * HEAD_DIM:(h + 1) * HEAD_DIM, :]
        qs = (qt * (HEAD_DIM ** -0.5 * LOG2E)).astype(BF16)
        sub = lax.broadcasted_iota(jnp.int32, (AUG_MASK - AUG_ALIBI, blk), 0)
        ones = jnp.where(sub < 3, 1.0, 0.0).astype(BF16)
        for ref in (qaug_ref, qown_ref):
            ref[h, 0:HEAD_DIM, :] = qs
            ref[h, AUG_ALIBI:AUG_MASK, :] = ones
        qown_ref[h, AUG_MASK:, :] = jnp.zeros((AUG_W - AUG_MASK, blk), BF16)

    def build_block_mask(h):
        blk_id = lax.broadcasted_iota(jnp.int32, (nblk, blk), 0).astype(F32)
        qt = qt_ref[0, 0, h * HEAD_DIM:(h + 1) * HEAD_DIM, :]
        gate = jnp.dot(kmean_ref[0, :, h * AUG_W:h * AUG_W + HEAD_DIM], qt,
                       preferred_element_type=F32, precision=lax.Precision.HIGHEST)
        gate = jnp.where(blk_id < qi.astype(F32), gate, -jnp.inf)
        keep = blk_id < 0.0
        for _ in range(MOBA_TOPK):
            best = jnp.max(gate, axis=0, keepdims=True)
            first = jnp.min(jnp.where(gate == best, blk_id, float(nblk)), axis=0, keepdims=True)
            pick = (blk_id == first) & (best > -jnp.inf)
            keep = keep | pick
            gate = jnp.where(pick, -jnp.inf, gate)
        qaug_ref[h, AUG_MASK:AUG_MASK + nblk, :] = jnp.where(keep, 0.0, NEG_BIG).astype(BF16)
        qaug_ref[h, AUG_MASK + nblk:, :] = jnp.zeros((AUG_W - AUG_MASK - nblk, blk), BF16)

    kk = lax.broadcasted_iota(jnp.int32, (blk, blk), 0)
    qq = lax.broadcasted_iota(jnp.int32, (blk, blk), 1)
    causal = kk <= qq

    per_trip = PAST_UNROLL * MOBA_HEADS

    def past_block(trip, u):
        return jnp.minimum(trip * PAST_UNROLL + u, nblk - 1)

    def stream_item(trip, n):
        if n < 0:
            return qi, n + MOBA_HEADS, True, (n + MOBA_HEADS) % SCORE_SLOTS
        if n >= per_trip:
            trip, n = trip + 1, n - per_trip
        return past_block(trip, n // MOBA_HEADS), n % MOBA_HEADS, False, (n + MOBA_HEADS) % SCORE_SLOTS

    def scores(j, h, own, slot):
        q_ref = qown_ref if own else qaug_ref
        s_ref[slot] = jnp.dot(kaug_ref[0, j, :, h * AUG_W:(h + 1) * AUG_W], q_ref[h],
                              preferred_element_type=F32)

    def probabilities(j, h, own, slot):
        s = s_ref[slot]
        if own:
            s = jnp.where(causal, s, NEG_BIG)
            m_new = jnp.max(s, axis=0, keepdims=True)
        else:
            m_old = m_ref[h]
            m_new = jnp.maximum(m_old, jnp.max(s, axis=0, keepdims=True))
            alpha_ref[slot % PROB_SLOTS] = jnp.exp2(m_old - m_new)
        p_ref[slot % PROB_SLOTS] = jnp.exp2(s - m_new).astype(BF16)
        m_ref[h] = m_new

    def accumulate(j, h, own, slot):
        pv = jnp.dot(vt_ref[0, j, h * VT_ROWS:(h + 1) * VT_ROWS, :], p_ref[slot % PROB_SLOTS],
                     preferred_element_type=F32)
        acc_ref[h] = pv if own else alpha_ref[slot % PROB_SLOTS] * acc_ref[h] + pv

    for h in range(MOBA_HEADS):
        build_query_operands(h)
        build_block_mask(h)
    for n in range(-MOBA_HEADS, -MOBA_HEADS + SCORE_LOOKAHEAD):
        scores(*stream_item(0, n))
    probabilities(*stream_item(0, -MOBA_HEADS))
    for n in range(-MOBA_HEADS, 0):
        scores(*stream_item(0, n + SCORE_LOOKAHEAD))
        probabilities(*stream_item(0, n + 1))
        accumulate(*stream_item(0, n))

    def past_trip(trip, carry):
        for n in range(per_trip):
            scores(*stream_item(trip, n + SCORE_LOOKAHEAD))
            probabilities(*stream_item(trip, n + 1))
            accumulate(*stream_item(trip, n))
        return carry

    lax.fori_loop(0, (qi + PAST_UNROLL - 1) // PAST_UNROLL, past_trip, 0)

    for h in range(MOBA_HEADS):
        ot_ref[h * HEAD_DIM:(h + 1) * HEAD_DIM, :] = (acc_ref[h, 0:HEAD_DIM, :]
                                                      / acc_ref[h, HEAD_DIM:HEAD_DIM + 1, :])
    o_ref[...] = ot_ref[...].T.astype(o_ref.dtype)


def _moba(qt, kaug, vt, kmean, batch, seq):
    nblk = seq // MOBA_BLOCK
    width = MOBA_HEADS * HEAD_DIM
    qt = qt.reshape(batch, nblk, width, MOBA_BLOCK)
    kaug = kaug.reshape(batch, nblk, MOBA_BLOCK, MOBA_HEADS * AUG_W)
    vt_rows = MOBA_HEADS * VT_ROWS
    vt = vt.reshape(batch, nblk, vt_rows, MOBA_BLOCK)
    resident = 2 * (kaug.size // batch + vt.size // batch) * 2
    return pl.pallas_call(
        _moba_kernel,
        grid=(batch, nblk),
        in_specs=[pl.BlockSpec((1, 1, width, MOBA_BLOCK), lambda b, i: (b, i, 0, 0)),
                  pl.BlockSpec((1, nblk, MOBA_BLOCK, MOBA_HEADS * AUG_W), lambda b, i: (b, 0, 0, 0)),
                  pl.BlockSpec((1, nblk, vt_rows, MOBA_BLOCK), lambda b, i: (b, 0, 0, 0)),
                  pl.BlockSpec((1, nblk, MOBA_HEADS * AUG_W), lambda b, i: (b, 0, 0))],
        out_specs=pl.BlockSpec((MOBA_BLOCK, width), lambda b, i: (b * nblk + i, 0)),
        out_shape=jax.ShapeDtypeStruct((batch * seq, width), BF16),
        scratch_shapes=[pltpu.VMEM((MOBA_HEADS, AUG_W, MOBA_BLOCK), BF16),
                        pltpu.VMEM((MOBA_HEADS, AUG_W, MOBA_BLOCK), BF16),
                        pltpu.VMEM((SCORE_SLOTS, MOBA_BLOCK, MOBA_BLOCK), F32),
                        pltpu.VMEM((PROB_SLOTS, MOBA_BLOCK, MOBA_BLOCK), BF16),
                        pltpu.VMEM((PROB_SLOTS, 1, MOBA_BLOCK), F32),
                        pltpu.VMEM((MOBA_HEADS, 1, MOBA_BLOCK), F32),
                        pltpu.VMEM((MOBA_HEADS, VT_ROWS, MOBA_BLOCK), F32),
                        pltpu.VMEM((width, MOBA_BLOCK), F32)],
        compiler_params=pltpu.CompilerParams(dimension_semantics=("arbitrary", "arbitrary"),
                                             vmem_limit_bytes=_vmem_limit(resident)),
        name="moba",
    )(qt, kaug, vt, kmean)


def _mem_attn_kernel(q_ref, k_ref, v_ref, o_ref):
    outs = []
    for h in range(MEM_HEADS):
        cols = slice(h * HEAD_DIM, (h + 1) * HEAD_DIM)
        s = lax.dot_general(q_ref[:, cols], k_ref[0, :, cols], (((1,), (1,)), ((), ())),
                            preferred_element_type=F32)
        p = jnp.exp2(s - jnp.max(s, axis=-1, keepdims=True))
        l = jnp.sum(p, axis=-1, keepdims=True)
        outs.append(jnp.dot(p.astype(BF16), v_ref[0, :, cols], preferred_element_type=F32) / l)
    o_ref[...] = jnp.concatenate(outs, axis=1).astype(o_ref.dtype)


def _mem_attn(cq, kc, vc, batch, seq):
    n, width = cq.shape
    tm = TOKEN_TILE
    nt = seq // tm
    mlen = kc.shape[0] // batch
    kc = kc.reshape(batch, mlen, width)
    vc = vc.reshape(batch, mlen, width)
    kv_spec = pl.BlockSpec((1, mlen, width), lambda b, t: (b, 0, 0))
    tile = pl.BlockSpec((tm, width), lambda b, t: (b * nt + t, 0))
    return pl.pallas_call(
        _mem_attn_kernel,
        grid=(batch, nt),
        in_specs=[tile, kv_spec, kv_spec],
        out_specs=tile,
        out_shape=jax.ShapeDtypeStruct((n, width), BF16),
        compiler_params=pltpu.CompilerParams(dimension_semantics=("arbitrary", "arbitrary")),
        name="mem_attn",
    )(cq, kc, vc)


def _block_diag_ones(group):
    idx = jnp.arange(V7X_MXU_DIM) // group
    return (idx[:, None] == idx[None, :]).astype(BF16)


def _pack_block_diag(w):
    nb, bw, _ = w.shape
    per = V7X_MXU_DIM // bw
    w = w.reshape(nb // per, per, bw, bw)
    eye = jnp.eye(per, dtype=w.dtype)
    return jnp.einsum("gpij,pq->gpiqj", w, eye).reshape(nb // per, V7X_MXU_DIM, V7X_MXU_DIM)


def _pad_heads(w, heads):
    lead = w.shape[:-1]
    w = w.reshape(lead + (heads, HEAD_DIM))
    w = jnp.pad(w, [(0, 0)] * len(lead) + [(0, 0), (0, AUG_W - HEAD_DIM)])
    return w.reshape(lead + (heads * AUG_W,))


def _layer(x, mem, ffn1_norm, ffn1_w_in, ffn1_w_out, mix_norm, mem_norm, w_in,
           lru_conv_w, lru_conv_b, lru_a_w, lru_a_b, lru_x_w, lru_x_b, lru_lambda,
           moba_q_norm, moba_k_norm, mem_w_kv, mem_q_norm, mem_k_norm, w_out,
           ffn2_norm, ffn2_w_in, ffn2_w_out):
    batch, seq, d = x.shape
    n = batch * seq
    n_lru = lru_lambda.shape[0]
    wq = MOBA_HEADS * HEAD_DIM
    wc = MEM_HEADS * HEAD_DIM
    row = lambda v: v.reshape(1, -1).astype(F32)

    bd64 = _block_diag_ones(HEAD_DIM)
    bd128 = _block_diag_ones(AUG_W)

    w_in_b = w_in.astype(BF16)
    o = 2 * n_lru
    w_lru = w_in_b[:, :o]
    w_q = w_in_b[:, o:o + wq]
    w_k = _pad_heads(w_in_b[:, o + wq:o + 2 * wq], MOBA_HEADS)
    w_v = w_in_b[:, o + 2 * wq:o + 3 * wq]
    w_c = w_in_b[:, o + 3 * wq:]
    gq = row(jnp.tile(moba_q_norm, MOBA_HEADS))
    gk = row(_pad_heads(jnp.tile(moba_k_norm, MOBA_HEADS), MOBA_HEADS))
    gc = row(jnp.tile(mem_q_norm, MEM_HEADS))
    gck = row(jnp.tile(mem_k_norm, MEM_HEADS))
    slopes = 2.0 ** (-8.0 * jnp.arange(1, MOBA_HEADS + 1, dtype=F32) / MOBA_HEADS)
    lane = jnp.arange(AUG_W)
    is_alibi = (lane >= AUG_ALIBI) & (lane < AUG_ALIBI + 3)
    alibi = row(jnp.where(is_alibi[None, :], slopes[:, None] * LOG2E, 0.0))

    x2d = x.reshape(n, d)
    kc, vc = _mem_kv(mem.reshape(-1, d), row(mem_norm), mem_w_kv.astype(BF16), bd64, gck)

    x1 = _ffn(x2d, row(ffn1_norm), ffn1_w_in.astype(BF16), ffn1_w_out.astype(BF16))

    lru_consts = (lru_conv_w.astype(F32), row(lru_conv_b),
                  _pack_block_diag(lru_a_w).astype(BF16), row(lru_a_b),
                  _pack_block_diag(lru_x_w).astype(BF16), row(lru_x_b), row(lru_lambda))
    y_lru, qt, kaug, vt, kmean, cq = _in_proj(
        x1, row(mix_norm), w_lru, w_q, w_k, w_v, w_c, gq, gk, gc, bd64, bd128, alibi, lru_consts, seq)
    kmean = kmean[:, :TOKEN_TILE // MOBA_BLOCK, :].reshape(batch, seq // MOBA_BLOCK, -1)

    y_moba = _moba(qt, kaug, vt, kmean, batch, seq)
    y_mem = _mem_attn(cq, kc, vc, batch, seq)

    out = _out_ffn(x1, y_lru, y_moba, y_mem, w_out.astype(BF16), row(ffn2_norm),
                   ffn2_w_in.astype(BF16), ffn2_w_out.astype(BF16))
    return out.reshape(batch, seq, d)


def kernel(x, mem, ffn1_norm, ffn1_w_in, ffn1_w_out, mix_norm, mem_norm, w_in, lru_conv_w, lru_conv_b,
           lru_a_w, lru_a_b, lru_x_w, lru_x_b, lru_lambda, moba_q_norm, moba_k_norm, mem_w_kv,
           mem_q_norm, mem_k_norm, w_out, ffn2_norm, ffn2_w_in, ffn2_w_out):
    params = (ffn1_norm, ffn1_w_in, ffn1_w_out, mix_norm, mem_norm, w_in, lru_conv_w, lru_conv_b,
              lru_a_w, lru_a_b, lru_x_w, lru_x_b, lru_lambda, moba_q_norm, moba_k_norm, mem_w_kv,
              mem_q_norm, mem_k_norm, w_out, ffn2_norm, ffn2_w_in, ffn2_w_out)
    for layer in range(ffn1_norm.shape[0]):
        x = _layer(x, mem, *(p[layer] for p in params))
    return x
```

```python
import functools
import math

import jax
import jax.numpy as jnp
from jax import lax
from jax.experimental import pallas as pl
from jax.experimental.pallas import tpu as pltpu

F32 = jnp.float32
BF16 = jnp.bfloat16

HEAD_DIM = 64
CONV_W = 4
LRU_C = 8.0
MOBA_HEADS = 4
MOBA_BLOCK = 256
MOBA_TOPK = 3
MEM_HEADS = 4
NORM_EPS = 1e-6
LOG2E = 1.4426950408889634
NEG_BIG = -1e30

V7X_MXU_DIM = 256
LANES = 128
SUBLANES = 8
V7X_VMEM_BYTES = 64 * 1024 * 1024

AUG_W = 128
AUG_ALIBI = 64
AUG_MASK = 80
VT_ROWS = HEAD_DIM + 16
SCORE_LOOKAHEAD = 5
PAST_UNROLL = 4
SCORE_SLOTS = 8
assert MOBA_HEADS <= SCORE_LOOKAHEAD < SCORE_SLOTS and (PAST_UNROLL * MOBA_HEADS) % SCORE_SLOTS == 0
PROB_SLOTS = 2

TOKEN_TILE = 512
LRU_ROW_CHUNK = 128
FFN_CHUNK = 256


def _vmem_limit(resident_bytes):
    return int(min(V7X_VMEM_BYTES - 8 * 1024 * 1024, resident_bytes + 24 * 1024 * 1024))


def _const_spec(shape):
    n = len(shape)
    return pl.BlockSpec(shape, lambda *_: (0,) * n, pipeline_mode=pl.Buffered(1))


def _rms_norm(x, g):
    ms = jnp.mean(x * x, axis=-1, keepdims=True)
    return x * lax.rsqrt(ms + NORM_EPS) * g


def _group_rms_norm(u, bd, g, group):
    sq = u * u
    hi = sq.astype(BF16)
    lo = (sq - hi.astype(F32)).astype(BF16)
    parts = []
    for c in range(u.shape[1] // V7X_MXU_DIM):
        sl = slice(c * V7X_MXU_DIM, (c + 1) * V7X_MXU_DIM)
        parts.append(jnp.dot(hi[:, sl], bd, preferred_element_type=F32)
                     + jnp.dot(lo[:, sl], bd, preferred_element_type=F32))
    ss = parts[0] if len(parts) == 1 else jnp.concatenate(parts, axis=1)
    return u * lax.rsqrt(ss * (1.0 / group) + NORM_EPS) * g


def _swiglu_half_step(x, g_ref, w_in_ref, w_out_ref, act_ref):
    d_ff = w_out_ref.shape[0]
    xn = _rms_norm(x, g_ref[...]).astype(BF16)
    for c in range(d_ff // FFN_CHUNK):
        lo = c * FFN_CHUNK
        a = jnp.dot(xn, w_in_ref[:, lo:lo + FFN_CHUNK], preferred_element_type=F32)
        b = jnp.dot(xn, w_in_ref[:, d_ff + lo:d_ff + lo + FFN_CHUNK], preferred_element_type=F32)
        act_ref[:, lo:lo + FFN_CHUNK] = (a * jax.nn.sigmoid(a) * b).astype(BF16)
    y = jnp.dot(act_ref[...], w_out_ref[...], preferred_element_type=F32)
    return x + 0.5 * y


def _ffn_kernel(x_ref, g_ref, w_in_ref, w_out_ref, o_ref, act_ref):
    o_ref[...] = _swiglu_half_step(x_ref[...], g_ref, w_in_ref, w_out_ref, act_ref)


def _ffn(x, g, w_in, w_out):
    n, d = x.shape
    d_ff = w_out.shape[0]
    tm = TOKEN_TILE
    resident = (w_in.size + w_out.size) * 2
    return pl.pallas_call(
        _ffn_kernel,
        grid=(n // tm,),
        in_specs=[pl.BlockSpec((tm, d), lambda i: (i, 0)),
                  _const_spec((1, d)), _const_spec(w_in.shape), _const_spec(w_out.shape)],
        out_specs=pl.BlockSpec((tm, d), lambda i: (i, 0)),
        out_shape=jax.ShapeDtypeStruct((n, d), F32),
        scratch_shapes=[pltpu.VMEM((tm, d_ff), BF16)],
        compiler_params=pltpu.CompilerParams(dimension_semantics=("arbitrary",),
                                             vmem_limit_bytes=_vmem_limit(resident)),
        name="ffn1",
    )(x, g, w_in, w_out)


def _out_ffn_kernel(x_ref, ylru_ref, ymoba_ref, ymem_ref, wo_ref, g_ref, w_in_ref, w_out_ref,
                    o_ref, act_ref):
    n_lru = ylru_ref.shape[1]
    n_moba = ymoba_ref.shape[1]
    x2 = (x_ref[...]
          + jnp.dot(ylru_ref[...], wo_ref[0:n_lru, :], preferred_element_type=F32)
          + jnp.dot(ymoba_ref[...], wo_ref[n_lru:n_lru + n_moba, :], preferred_element_type=F32)
          + jnp.dot(ymem_ref[...], wo_ref[n_lru + n_moba:, :], preferred_element_type=F32))
    o_ref[...] = _swiglu_half_step(x2, g_ref, w_in_ref, w_out_ref, act_ref)


def _out_ffn(x1, y_lru, y_moba, y_mem, w_o, g, w_in, w_out):
    n, d = x1.shape
    d_ff = w_out.shape[0]
    tm = TOKEN_TILE
    resident = (w_in.size + w_out.size + w_o.size) * 2

    def tile(w):
        return pl.BlockSpec((tm, w), lambda i: (i, 0))

    return pl.pallas_call(
        _out_ffn_kernel,
        grid=(n // tm,),
        in_specs=[tile(d), tile(y_lru.shape[1]), tile(y_moba.shape[1]), tile(y_mem.shape[1]),
                  _const_spec(w_o.shape), _const_spec((1, d)),
                  _const_spec(w_in.shape), _const_spec(w_out.shape)],
        out_specs=tile(d),
        out_shape=jax.ShapeDtypeStruct((n, d), F32),
        scratch_shapes=[pltpu.VMEM((tm, d_ff), BF16)],
        compiler_params=pltpu.CompilerParams(dimension_semantics=("arbitrary",),
                                             vmem_limit_bytes=_vmem_limit(resident)),
        name="out_ffn2",
    )(x1, y_lru, y_moba, y_mem, w_o, g, w_in, w_out)


def _mem_kv_kernel(mem_ref, g_ref, w_ref, bd_ref, gk_ref, kc_ref, vc_ref):
    width = kc_ref.shape[1]
    mn = _rms_norm(mem_ref[...], g_ref[...]).astype(BF16)
    kv = jnp.dot(mn, w_ref[...], preferred_element_type=F32)
    kc_ref[...] = _group_rms_norm(kv[:, :width], bd_ref[...], gk_ref[...], HEAD_DIM).astype(BF16)
    vc_ref[...] = kv[:, width:].astype(BF16)


def _mem_kv(mem2d, g, w_kv, bd64, gk):
    m = mem2d.shape[0]
    width = w_kv.shape[1] // 2
    return pl.pallas_call(
        _mem_kv_kernel,
        out_shape=(jax.ShapeDtypeStruct((m, width), BF16), jax.ShapeDtypeStruct((m, width), BF16)),
        name="mem_kv",
    )(mem2d, g, w_kv, bd64, gk)


def _in_proj_kernel(x_ref, g_ref, w_lru_ref, w_q_ref, w_k_ref, w_v_ref, w_c_ref,
                    gq_ref, gk_ref, gc_ref, bd64_ref, bd128_ref, kpos_ref,
                    cw_ref, cb_ref, wa_ref, ba_ref, wx_ref, bx_ref, lam_ref,
                    ylru_ref, qt_ref, kaug_ref, vt_ref, kmean_ref, cq_ref,
                    xpad_ref, gg_ref, a_ref, b_ref, hl_ref, carry_ref, *, seq):
    tm = x_ref.shape[0]
    n_lru = ylru_ref.shape[1]
    xn = _rms_norm(x_ref[...], g_ref[...]).astype(BF16)

    seq_start = pl.program_id(0) % (seq // tm) == 0
    _lru_begin_tile(jnp.dot(xn, w_lru_ref[:, :n_lru], preferred_element_type=F32),
                    seq_start, xpad_ref, carry_ref)

    def recurrent_lane_tile(k):
        _lru_lane_tile(k, seq_start, cw_ref, cb_ref, wa_ref, ba_ref, wx_ref, bx_ref, lam_ref,
                       xpad_ref, a_ref, b_ref, hl_ref, carry_ref)

    def gate_branch():
        gg_ref[...] = _gelu_tanh(jnp.dot(xn, w_lru_ref[:, n_lru:], preferred_element_type=F32))

    def mem_queries():
        cq = jnp.dot(xn, w_c_ref[...], preferred_element_type=F32)
        cq = _group_rms_norm(cq, bd64_ref[...], gc_ref[...], HEAD_DIM)
        cq_ref[...] = (cq * (HEAD_DIM ** -0.5 * LOG2E)).astype(BF16)

    def moba_queries():
        q = jnp.dot(xn, w_q_ref[...], preferred_element_type=F32)
        q = _group_rms_norm(q, bd64_ref[...], gq_ref[...], HEAD_DIM)
        for r in range(tm // MOBA_BLOCK):
            qt_ref[r] = q[r * MOBA_BLOCK:(r + 1) * MOBA_BLOCK, :].T

    def moba_values():
        v = jnp.dot(xn, w_v_ref[...], preferred_element_type=F32)
        ones = jnp.ones((VT_ROWS - HEAD_DIM, MOBA_BLOCK), BF16)
        for r in range(tm // MOBA_BLOCK):
            vt = v[r * MOBA_BLOCK:(r + 1) * MOBA_BLOCK, :].T.astype(BF16)
            for h in range(MOBA_HEADS):
                vt_ref[r, h * VT_ROWS:h * VT_ROWS + HEAD_DIM, :] = vt[h * HEAD_DIM:(h + 1) * HEAD_DIM, :]
                vt_ref[r, h * VT_ROWS + HEAD_DIM:(h + 1) * VT_ROWS, :] = ones

    def moba_keys():
        k = jnp.dot(xn, w_k_ref[...], preferred_element_type=F32)
        k = _group_rms_norm(k, bd128_ref[...], gk_ref[...], HEAD_DIM)
        kmean_ref[...] = jnp.zeros(kmean_ref.shape, F32)
        for r in range(tm // MOBA_BLOCK):
            rows = slice(r * MOBA_BLOCK, (r + 1) * MOBA_BLOCK)
            kmean_ref[0, r:r + 1, :] = jnp.sum(k[rows, :], axis=0, keepdims=True) * (1.0 / MOBA_BLOCK)
        kaug_ref[...] = k.astype(BF16) + kpos_ref[...]

    for k, projection in enumerate((gate_branch, mem_queries, moba_queries, moba_values)):
        recurrent_lane_tile(k)
        projection()
    ylru_ref[...] = (hl_ref[...] * gg_ref[...]).astype(ylru_ref.dtype)
    moba_keys()


def _in_proj(x1, g, w_lru, w_q, w_k, w_v, w_c, gq, gk, gc, bd64, bd128, kpos, lru_consts, seq):
    n, d = x1.shape
    tm = TOKEN_TILE
    rb = tm // MOBA_BLOCK
    nblk = n // MOBA_BLOCK
    n_lru = w_lru.shape[1] // 2
    wq = w_q.shape[1]
    wk = w_k.shape[1]
    assert n_lru // LANES == 4, "the kernel pairs each recurrence lane tile with one projection"
    head = (g, w_lru, w_q, w_k, w_v, w_c, gq, gk, gc, bd64, bd128)
    tail = tuple(lru_consts)
    resident = sum(c.size * c.dtype.itemsize for c in head + tail)
    tiles_per_seq = seq // tm
    return pl.pallas_call(
        functools.partial(_in_proj_kernel, seq=seq),
        grid=(n // tm,),
        in_specs=([pl.BlockSpec((tm, d), lambda i: (i, 0))] + [_const_spec(c.shape) for c in head]
                  + [pl.BlockSpec((tm, wk), lambda i: (i % tiles_per_seq, 0))]
                  + [_const_spec(c.shape) for c in tail]),
        out_specs=[pl.BlockSpec((tm, n_lru), lambda i: (i, 0)),
                   pl.BlockSpec((rb, wq, MOBA_BLOCK), lambda i: (i, 0, 0)),
                   pl.BlockSpec((tm, wk), lambda i: (i, 0)),
                   pl.BlockSpec((rb, MOBA_HEADS * VT_ROWS, MOBA_BLOCK), lambda i: (i, 0, 0)),
                   pl.BlockSpec((1, 8, wk), lambda i: (i, 0, 0)),
                   pl.BlockSpec((tm, wq), lambda i: (i, 0))],
        out_shape=(jax.ShapeDtypeStruct((n, n_lru), BF16),
                   jax.ShapeDtypeStruct((nblk, wq, MOBA_BLOCK), F32),
                   jax.ShapeDtypeStruct((n, wk), BF16),
                   jax.ShapeDtypeStruct((nblk, MOBA_HEADS * VT_ROWS, MOBA_BLOCK), BF16),
                   jax.ShapeDtypeStruct((n // tm, 8, wk), F32),
                   jax.ShapeDtypeStruct((n, wq), BF16)),
        scratch_shapes=_lru_scratch(tm, n_lru),
        compiler_params=pltpu.CompilerParams(dimension_semantics=("arbitrary",),
                                             vmem_limit_bytes=_vmem_limit(resident)),
        name="in_proj",
    )(x1, *head, kpos, *tail)


def _gelu_tanh(x):
    c = math.sqrt(2.0 / math.pi)
    return 0.5 * x * (1.0 + jnp.tanh(c * (x + 0.044715 * (x * x * x))))


def _lru_begin_tile(x, seq_start, xpad_ref, carry_ref):
    tl, c = x.shape

    @pl.when(seq_start)
    def _():
        xpad_ref[0:SUBLANES, :] = jnp.zeros((SUBLANES, c), F32)
        carry_ref[...] = jnp.zeros(carry_ref.shape, F32)

    xpad_ref[SUBLANES:SUBLANES + tl, :] = x


def _lru_lane_tile(k, seq_start, cw_ref, cb_ref, wa_ref, ba_ref, wx_ref, bx_ref, lam_ref,
                   xpad_ref, a_ref, b_ref, hl_ref, carry_ref):
    tl = hl_ref.shape[0]
    lanes = slice(k * LANES, (k + 1) * LANES)

    neg_lam = -lam_ref[:, lanes]
    softplus = jnp.maximum(neg_lam, 0.0) + jnp.log1p(jnp.exp(-jnp.abs(neg_lam)))
    start_row = jnp.where(seq_start, 0, -1)
    groups = tl // SUBLANES
    rc_groups = LRU_ROW_CHUNK // SUBLANES
    sub = lax.broadcasted_iota(jnp.int32, (rc_groups, SUBLANES, LANES), 1)

    for r0 in range(0, tl, LRU_ROW_CHUNK):
        rows = slice(r0, r0 + LRU_ROW_CHUNK)
        xb = jnp.broadcast_to(cb_ref[:, lanes], (LRU_ROW_CHUNK, LANES))
        for j in range(CONV_W):
            off = r0 + SUBLANES - (CONV_W - 1) + j
            xb = xb + cw_ref[j:j + 1, lanes] * xpad_ref[off:off + LRU_ROW_CHUNK, lanes]

        xbb = xb.astype(BF16)
        r = jax.nn.sigmoid(jnp.dot(xbb, wa_ref[k], preferred_element_type=F32) + ba_ref[:, lanes])
        i = jax.nn.sigmoid(jnp.dot(xbb, wx_ref[k], preferred_element_type=F32) + bx_ref[:, lanes])
        log_a = (-LRU_C) * r * softplus
        a = jnp.exp(log_a)
        mult = jnp.sqrt(1.0 - a * a)
        row = lax.broadcasted_iota(jnp.int32, (LRU_ROW_CHUNK, LANES), 0) + r0
        mult = jnp.where(row == start_row, 1.0, mult)
        bt = mult * i * xb

        a3 = a.reshape(rc_groups, SUBLANES, LANES)
        b3 = bt.reshape(rc_groups, SUBLANES, LANES)
        d = 1
        while d < SUBLANES:
            has_prev = sub >= d
            a_prev = pltpu.roll(a3, d, axis=1)
            b_prev = pltpu.roll(b3, d, axis=1)
            b3 = b3 + jnp.where(has_prev, a3, 0.0) * b_prev
            a3 = jnp.where(has_prev, a3 * a_prev, a3)
            d *= 2
        a_ref[rows, lanes] = a3.reshape(LRU_ROW_CHUNK, LANES)
        b_ref[rows, lanes] = b3.reshape(LRU_ROW_CHUNK, LANES)
    xpad_ref[0:SUBLANES, lanes] = xpad_ref[tl:tl + SUBLANES, lanes]

    h = carry_ref[:, lanes]
    for grp in range(groups):
        rows = slice(grp * SUBLANES, (grp + 1) * SUBLANES)
        h_last = jnp.broadcast_to(h[SUBLANES - 1:SUBLANES, :], (SUBLANES, LANES))
        h = a_ref[rows, lanes] * h_last + b_ref[rows, lanes]
        hl_ref[rows, lanes] = h
    carry_ref[:, lanes] = h


def _lru_scratch(tl, c):
    return ([pltpu.VMEM((tl + SUBLANES, c), F32)] + [pltpu.VMEM((tl, c), F32)] * 4
            + [pltpu.VMEM((SUBLANES, c), F32)])


def _moba_kernel(qt_ref, kaug_ref, vt_ref, kmean_ref, o_ref,
                 qaug_ref, qown_ref, s_ref, p_ref, alpha_ref, m_ref, acc_ref, ot_ref):
    qi = pl.program_id(1)
    nblk = kmean_ref.shape[1]
    blk = MOBA_BLOCK

    def build_query_operands(h):
        qt = qt_ref[0, 0, h * HEAD_DIM:(h + 1) * HEAD_DIM, :]
        qs = (qt * (HEAD_DIM ** -0.5 * LOG2E)).astype(BF16)
        sub = lax.broadcasted_iota(jnp.int32, (AUG_MASK - AUG_ALIBI, blk), 0)
        ones = jnp.where(sub < 3, 1.0, 0.0).astype(BF16)
        for ref in (qaug_ref, qown_ref):
            ref[h, 0:HEAD_DIM, :] = qs
            ref[h, AUG_ALIBI:AUG_MASK, :] = ones
        qown_ref[h, AUG_MASK:, :] = jnp.zeros((AUG_W - AUG_MASK, blk), BF16)

    def build_block_mask(h):
        blk_id = lax.broadcasted_iota(jnp.int32, (nblk, blk), 0).astype(F32)
        qt = qt_ref[0, 0, h * HEAD_DIM:(h + 1) * HEAD_DIM, :]
        gate = jnp.dot(kmean_ref[0, :, h * AUG_W:h * AUG_W + HEAD_DIM], qt,
                       preferred_element_type=F32, precision=lax.Precision.HIGHEST)
        gate = jnp.where(blk_id < qi.astype(F32), gate, -jnp.inf)
        keep = blk_id < 0.0
        for _ in range(MOBA_TOPK):
            best = jnp.max(gate, axis=0, keepdims=True)
            first = jnp.min(jnp.where(gate == best, blk_id, float(nblk)), axis=0, keepdims=True)
            pick = (blk_id == first) & (best > -jnp.inf)
            keep = keep | pick
            gate = jnp.where(pick, -jnp.inf, gate)
        qaug_ref[h, AUG_MASK:AUG_MASK + nblk, :] = jnp.where(keep, 0.0, NEG_BIG).astype(BF16)
        qaug_ref[h, AUG_MASK + nblk:, :] = jnp.zeros((AUG_W - AUG_MASK - nblk, blk), BF16)

    kk = lax.broadcasted_iota(jnp.int32, (blk, blk), 0)
    qq = lax.broadcasted_iota(jnp.int32, (blk, blk), 1)
    causal = kk <= qq

    per_trip = PAST_UNROLL * MOBA_HEADS

    def past_block(trip, u):
        return jnp.minimum(trip * PAST_UNROLL + u, nblk - 1)

    def stream_item(trip, n):
        if n < 0:
            return qi, n + MOBA_HEADS, True, (n + MOBA_HEADS) % SCORE_SLOTS
        if n >= per_trip:
            trip, n = trip + 1, n - per_trip
        return past_block(trip, n // MOBA_HEADS), n % MOBA_HEADS, False, (n + MOBA_HEADS) % SCORE_SLOTS

    def scores(j, h, own, slot):
        q_ref = qown_ref if own else qaug_ref
        s_ref[slot] = jnp.dot(kaug_ref[0, j, :, h * AUG_W:(h + 1) * AUG_W], q_ref[h],
                              preferred_element_type=F32)

    def probabilities(j, h, own, slot):
        s = s_ref[slot]
        if own:
            s = jnp.where(causal, s, NEG_BIG)
            m_new = jnp.max(s, axis=0, keepdims=True)
        else:
            m_old = m_ref[h]
            m_new = jnp.maximum(m_old, jnp.max(s, axis=0, keepdims=True))
            alpha_ref[slot % PROB_SLOTS] = jnp.exp2(m_old - m_new)
        p_ref[slot % PROB_SLOTS] = jnp.exp2(s - m_new).astype(BF16)
        m_ref[h] = m_new

    def accumulate(j, h, own, slot):
        pv = jnp.dot(vt_ref[0, j, h * VT_ROWS:(h + 1) * VT_ROWS, :], p_ref[slot % PROB_SLOTS],
                     preferred_element_type=F32)
        acc_ref[h] = pv if own else alpha_ref[slot % PROB_SLOTS] * acc_ref[h] + pv

    for h in range(MOBA_HEADS):
        build_query_operands(h)
        build_block_mask(h)
    for n in range(-MOBA_HEADS, -MOBA_HEADS + SCORE_LOOKAHEAD):
        scores(*stream_item(0, n))
    probabilities(*stream_item(0, -MOBA_HEADS))
    for n in range(-MOBA_HEADS, 0):
        scores(*stream_item(0, n + SCORE_LOOKAHEAD))
        probabilities(*stream_item(0, n + 1))
        accumulate(*stream_item(0, n))

    def past_trip(trip, carry):
        for n in range(per_trip):
            scores(*stream_item(trip, n + SCORE_LOOKAHEAD))
            probabilities(*stream_item(trip, n + 1))
            accumulate(*stream_item(trip, n))
        return carry

    lax.fori_loop(0, (qi + PAST_UNROLL - 1) // PAST_UNROLL, past_trip, 0)

    for h in range(MOBA_HEADS):
        ot_ref[h * HEAD_DIM:(h + 1) * HEAD_DIM, :] = (acc_ref[h, 0:HEAD_DIM, :]
                                                      / acc_ref[h, HEAD_DIM:HEAD_DIM + 1, :])
    o_ref[...] = ot_ref[...].T.astype(o_ref.dtype)


def _moba(qt, kaug, vt, kmean, batch, seq):
    nblk = seq // MOBA_BLOCK
    width = MOBA_HEADS * HEAD_DIM
    qt = qt.reshape(batch, nblk, width, MOBA_BLOCK)
    kaug = kaug.reshape(batch, nblk, MOBA_BLOCK, MOBA_HEADS * AUG_W)
    vt_rows = MOBA_HEADS * VT_ROWS
    vt = vt.reshape(batch, nblk, vt_rows, MOBA_BLOCK)
    resident = 2 * (kaug.size // batch + vt.size // batch) * 2
    return pl.pallas_call(
        _moba_kernel,
        grid=(batch, nblk),
        in_specs=[pl.BlockSpec((1, 1, width, MOBA_BLOCK), lambda b, i: (b, i, 0, 0)),
                  pl.BlockSpec((1, nblk, MOBA_BLOCK, MOBA_HEADS * AUG_W), lambda b, i: (b, 0, 0, 0)),
                  pl.BlockSpec((1, nblk, vt_rows, MOBA_BLOCK), lambda b, i: (b, 0, 0, 0)),
                  pl.BlockSpec((1, nblk, MOBA_HEADS * AUG_W), lambda b, i: (b, 0, 0))],
        out_specs=pl.BlockSpec((MOBA_BLOCK, width), lambda b, i: (b * nblk + i, 0)),
        out_shape=jax.ShapeDtypeStruct((batch * seq, width), BF16),
        scratch_shapes=[pltpu.VMEM((MOBA_HEADS, AUG_W, MOBA_BLOCK), BF16),
                        pltpu.VMEM((MOBA_HEADS, AUG_W, MOBA_BLOCK), BF16),
                        pltpu.VMEM((SCORE_SLOTS, MOBA_BLOCK, MOBA_BLOCK), F32),
                        pltpu.VMEM((PROB_SLOTS, MOBA_BLOCK, MOBA_BLOCK), BF16),
                        pltpu.VMEM((PROB_SLOTS, 1, MOBA_BLOCK), F32),
                        pltpu.VMEM((MOBA_HEADS, 1, MOBA_BLOCK), F32),
                        pltpu.VMEM((MOBA_HEADS, VT_ROWS, MOBA_BLOCK), F32),
                        pltpu.VMEM((width, MOBA_BLOCK), F32)],
        compiler_params=pltpu.CompilerParams(dimension_semantics=("arbitrary", "arbitrary"),
                                             vmem_limit_bytes=_vmem_limit(resident)),
        name="moba",
    )(qt, kaug, vt, kmean)


def _mem_attn_kernel(q_ref, k_ref, v_ref, o_ref):
    outs = []
    for h in range(MEM_HEADS):
        cols = slice(h * HEAD_DIM, (h + 1) * HEAD_DIM)
        s = lax.dot_general(q_ref[:, cols], k_ref[0, :, cols], (((1,), (1,)), ((), ())),
                            preferred_element_type=F32)
        p = jnp.exp2(s - jnp.max(s, axis=-1, keepdims=True))
        l = jnp.sum(p, axis=-1, keepdims=True)
        outs.append(jnp.dot(p.astype(BF16), v_ref[0, :, cols], preferred_element_type=F32) / l)
    o_ref[...] = jnp.concatenate(outs, axis=1).astype(o_ref.dtype)


def _mem_attn(cq, kc, vc, batch, seq):
    n, width = cq.shape
    tm = TOKEN_TILE
    nt = seq // tm
    mlen = kc.shape[0] // batch
    kc = kc.reshape(batch, mlen, width)
    vc = vc.reshape(batch, mlen, width)
    kv_spec = pl.BlockSpec((1, mlen, width), lambda b, t: (b, 0, 0))
    tile = pl.BlockSpec((tm, width), lambda b, t: (b * nt + t, 0))
    return pl.pallas_call(
        _mem_attn_kernel,
        grid=(batch, nt),
        in_specs=[tile, kv_spec, kv_spec],
        out_specs=tile,
        out_shape=jax.ShapeDtypeStruct((n, width), BF16),
        compiler_params=pltpu.CompilerParams(dimension_semantics=("arbitrary", "arbitrary")),
        name="mem_attn",
    )(cq, kc, vc)


def _block_diag_ones(group):
    idx = jnp.arange(V7X_MXU_DIM) // group
    return (idx[:, None] == idx[None, :]).astype(BF16)


def _pack_block_diag(w, tile):
    nb, bw, _ = w.shape
    per = tile // bw
    w = w.reshape(nb // per, per, bw, bw)
    eye = jnp.eye(per, dtype=w.dtype)
    return jnp.einsum("gpij,pq->gpiqj", w, eye).reshape(nb // per, tile, tile)


def _key_position_table(seq):
    slopes = 2.0 ** (-8.0 * jnp.arange(1, MOBA_HEADS + 1, dtype=F32) / MOBA_HEADS)
    t = jnp.arange(seq, dtype=jnp.int32)
    bias = t.astype(F32)[:, None] * (slopes * LOG2E)[None, :]
    to_bf16_grid = functools.partial(lax.reduce_precision, exponent_bits=8, mantissa_bits=7)
    b_hi = to_bf16_grid(bias)
    b_mid = to_bf16_grid(bias - b_hi)
    b_lo = (bias - b_hi - b_mid).astype(BF16)
    b_hi, b_mid = b_hi.astype(BF16), b_mid.astype(BF16)
    table = jnp.zeros((seq, MOBA_HEADS, AUG_W), BF16)
    table = table.at[:, :, AUG_ALIBI].set(b_hi).at[:, :, AUG_ALIBI + 1].set(b_mid)
    table = table.at[:, :, AUG_ALIBI + 2].set(b_lo)
    onehot = (t[:, None] // MOBA_BLOCK == jnp.arange(seq // MOBA_BLOCK)[None, :]).astype(BF16)
    table = table.at[:, :, AUG_MASK:AUG_MASK + seq // MOBA_BLOCK].set(onehot[:, None, :])
    return table.reshape(seq, MOBA_HEADS * AUG_W)


def _pad_heads(w, heads):
    lead = w.shape[:-1]
    w = w.reshape(lead + (heads, HEAD_DIM))
    w = jnp.pad(w, [(0, 0)] * len(lead) + [(0, 0), (0, AUG_W - HEAD_DIM)])
    return w.reshape(lead + (heads * AUG_W,))


def _layer(x, mem, ffn1_norm, ffn1_w_in, ffn1_w_out, mix_norm, mem_norm, w_in,
           lru_conv_w, lru_conv_b, lru_a_w, lru_a_b, lru_x_w, lru_x_b, lru_lambda,
           moba_q_norm, moba_k_norm, mem_w_kv, mem_q_norm, mem_k_norm, w_out,
           ffn2_norm, ffn2_w_in, ffn2_w_out):
    batch, seq, d = x.shape
    n = batch * seq
    n_lru = lru_lambda.shape[0]
    wq = MOBA_HEADS * HEAD_DIM
    wc = MEM_HEADS * HEAD_DIM
    row = lambda v: v.reshape(1, -1).astype(F32)

    bd64 = _block_diag_ones(HEAD_DIM)
    bd128 = _block_diag_ones(AUG_W)

    w_in_b = w_in.astype(BF16)
    o = 2 * n_lru
    w_lru = w_in_b[:, :o]
    w_q = w_in_b[:, o:o + wq]
    w_k = _pad_heads(w_in_b[:, o + wq:o + 2 * wq], MOBA_HEADS)
    w_v = w_in_b[:, o + 2 * wq:o + 3 * wq]
    w_c = w_in_b[:, o + 3 * wq:]
    gq = row(jnp.tile(moba_q_norm, MOBA_HEADS))
    gk = row(_pad_heads(jnp.tile(moba_k_norm, MOBA_HEADS), MOBA_HEADS))
    gc = row(jnp.tile(mem_q_norm, MEM_HEADS))
    gck = row(jnp.tile(mem_k_norm, MEM_HEADS))
    kpos = _key_position_table(seq)

    x2d = x.reshape(n, d)
    kc, vc = _mem_kv(mem.reshape(-1, d), row(mem_norm), mem_w_kv.astype(BF16), bd64, gck)

    x1 = _ffn(x2d, row(ffn1_norm), ffn1_w_in.astype(BF16), ffn1_w_out.astype(BF16))

    lru_consts = (lru_conv_w.astype(F32), row(lru_conv_b),
                  _pack_block_diag(lru_a_w, LANES).astype(BF16), row(lru_a_b),
                  _pack_block_diag(lru_x_w, LANES).astype(BF16), row(lru_x_b), row(lru_lambda))
    y_lru, qt, kaug, vt, kmean, cq = _in_proj(
        x1, row(mix_norm), w_lru, w_q, w_k, w_v, w_c, gq, gk, gc, bd64, bd128, kpos, lru_consts, seq)
    kmean = kmean[:, :TOKEN_TILE // MOBA_BLOCK, :].reshape(batch, seq // MOBA_BLOCK, -1)

    y_moba = _moba(qt, kaug, vt, kmean, batch, seq)
    y_mem = _mem_attn(cq, kc, vc, batch, seq)

    out = _out_ffn(x1, y_lru, y_moba, y_mem, w_out.astype(BF16), row(ffn2_norm),
                   ffn2_w_in.astype(BF16), ffn2_w_out.astype(BF16))
    return out.reshape(batch, seq, d)


def kernel(x, mem, ffn1_norm, ffn1_w_in, ffn1_w_out, mix_norm, mem_norm, w_in, lru_conv_w, lru_conv_b,
           lru_a_w, lru_a_b, lru_x_w, lru_x_b, lru_lambda, moba_q_norm, moba_k_norm, mem_w_kv,
           mem_q_norm, mem_k_norm, w_out, ffn2_norm, ffn2_w_in, ffn2_w_out):
    params = (ffn1_norm, ffn1_w_in, ffn1_w_out, mix_norm, mem_norm, w_in, lru_conv_w, lru_conv_b,
              lru_a_w, lru_a_b, lru_x_w, lru_x_b, lru_lambda, moba_q_norm, moba_k_norm, mem_w_kv,
              mem_q_norm, mem_k_norm, w_out, ffn2_norm, ffn2_w_in, ffn2_w_out)
    for layer in range(ffn1_norm.shape[0]):
        x = _layer(x, mem, *(p[layer] for p in params))
    return x
```

```python
import functools
import math

import jax
import jax.numpy as jnp
from jax import lax
from jax.experimental import pallas as pl
from jax.experimental.pallas import tpu as pltpu

F32 = jnp.float32
BF16 = jnp.bfloat16

HEAD_DIM = 64
CONV_W = 4
LRU_C = 8.0
MOBA_HEADS = 4
MOBA_BLOCK = 256
MOBA_TOPK = 3
MEM_HEADS = 4
NORM_EPS = 1e-6
LOG2E = 1.4426950408889634
NEG_BIG = -1e30

V7X_MXU_DIM = 256
LANES = 128
SUBLANES = 8
V7X_VMEM_BYTES = 64 * 1024 * 1024

AUG_W = 128
AUG_ALIBI = 64
AUG_MASK = 80
VT_ROWS = HEAD_DIM + 16
SCORE_LOOKAHEAD = 5
PAST_UNROLL = 4
SCORE_SLOTS = 8
assert MOBA_HEADS <= SCORE_LOOKAHEAD < SCORE_SLOTS and (PAST_UNROLL * MOBA_HEADS) % SCORE_SLOTS == 0
assert (MOBA_HEADS + MEM_HEADS) % SCORE_SLOTS == 0
PROB_SLOTS = 4

TOKEN_TILE = 512
LRU_TILE = 512
FFN_CHUNK = 256


def _vmem_limit(resident_bytes):
    return int(min(V7X_VMEM_BYTES - 8 * 1024 * 1024, resident_bytes + 24 * 1024 * 1024))


def _const_spec(shape):
    n = len(shape)
    return pl.BlockSpec(shape, lambda *_: (0,) * n, pipeline_mode=pl.Buffered(1))


def _rms_norm(x, g):
    ms = jnp.mean(x * x, axis=-1, keepdims=True)
    return x * lax.rsqrt(ms + NORM_EPS) * g


def _group_rms_norm(u, bd, g, group):
    sq = u * u
    hi = sq.astype(BF16)
    lo = (sq - hi.astype(F32)).astype(BF16)
    parts = []
    for c in range(u.shape[1] // V7X_MXU_DIM):
        sl = slice(c * V7X_MXU_DIM, (c + 1) * V7X_MXU_DIM)
        parts.append(jnp.dot(hi[:, sl], bd, preferred_element_type=F32)
                     + jnp.dot(lo[:, sl], bd, preferred_element_type=F32))
    ss = parts[0] if len(parts) == 1 else jnp.concatenate(parts, axis=1)
    return u * lax.rsqrt(ss * (1.0 / group) + NORM_EPS) * g


def _swiglu_half_step(x, g_ref, w_in_ref, w_out_ref, act_ref):
    d_ff = w_out_ref.shape[0]
    xn = _rms_norm(x, g_ref[...]).astype(BF16)
    for c in range(d_ff // FFN_CHUNK):
        lo = c * FFN_CHUNK
        a = jnp.dot(xn, w_in_ref[:, lo:lo + FFN_CHUNK], preferred_element_type=F32)
        b = jnp.dot(xn, w_in_ref[:, d_ff + lo:d_ff + lo + FFN_CHUNK], preferred_element_type=F32)
        act_ref[:, lo:lo + FFN_CHUNK] = (a * jax.nn.sigmoid(a) * b).astype(BF16)
    y = jnp.dot(act_ref[...], w_out_ref[...], preferred_element_type=F32)
    return x + 0.5 * y


def _ffn_kernel(x_ref, g_ref, w_in_ref, w_out_ref, o_ref, act_ref):
    o_ref[...] = _swiglu_half_step(x_ref[...], g_ref, w_in_ref, w_out_ref, act_ref)


def _ffn(x, g, w_in, w_out):
    n, d = x.shape
    d_ff = w_out.shape[0]
    tm = TOKEN_TILE
    resident = (w_in.size + w_out.size) * 2
    return pl.pallas_call(
        _ffn_kernel,
        grid=(n // tm,),
        in_specs=[pl.BlockSpec((tm, d), lambda i: (i, 0)),
                  _const_spec((1, d)), _const_spec(w_in.shape), _const_spec(w_out.shape)],
        out_specs=pl.BlockSpec((tm, d), lambda i: (i, 0)),
        out_shape=jax.ShapeDtypeStruct((n, d), F32),
        scratch_shapes=[pltpu.VMEM((tm, d_ff), BF16)],
        compiler_params=pltpu.CompilerParams(dimension_semantics=("arbitrary",),
                                             vmem_limit_bytes=_vmem_limit(resident)),
        name="ffn1",
    )(x, g, w_in, w_out)


def _out_ffn_kernel(x_ref, ylru_ref, ymoba_ref, ymem_ref, wo_ref, g_ref, w_in_ref, w_out_ref,
                    o_ref, act_ref):
    n_lru = ylru_ref.shape[1]
    n_moba = ymoba_ref.shape[1]
    x2 = (x_ref[...]
          + jnp.dot(ylru_ref[...], wo_ref[0:n_lru, :], preferred_element_type=F32)
          + jnp.dot(ymoba_ref[...], wo_ref[n_lru:n_lru + n_moba, :], preferred_element_type=F32)
          + jnp.dot(ymem_ref[...], wo_ref[n_lru + n_moba:, :], preferred_element_type=F32))
    o_ref[...] = _swiglu_half_step(x2, g_ref, w_in_ref, w_out_ref, act_ref)


def _out_ffn(x1, y_lru, y_moba, y_mem, w_o, g, w_in, w_out):
    n, d = x1.shape
    d_ff = w_out.shape[0]
    tm = TOKEN_TILE
    resident = (w_in.size + w_out.size + w_o.size) * 2

    def tile(w):
        return pl.BlockSpec((tm, w), lambda i: (i, 0))

    return pl.pallas_call(
        _out_ffn_kernel,
        grid=(n // tm,),
        in_specs=[tile(d), tile(y_lru.shape[1]), tile(y_moba.shape[1]), tile(y_mem.shape[1]),
                  _const_spec(w_o.shape), _const_spec((1, d)),
                  _const_spec(w_in.shape), _const_spec(w_out.shape)],
        out_specs=tile(d),
        out_shape=jax.ShapeDtypeStruct((n, d), F32),
        scratch_shapes=[pltpu.VMEM((tm, d_ff), BF16)],
        compiler_params=pltpu.CompilerParams(dimension_semantics=("arbitrary",),
                                             vmem_limit_bytes=_vmem_limit(resident)),
        name="out_ffn2",
    )(x1, y_lru, y_moba, y_mem, w_o, g, w_in, w_out)


def _mem_kv_kernel(mem_ref, g_ref, wk_ref, wv_ref, bd_ref, gk_ref, kc_ref, vct_ref):
    batch, _, mlen = vct_ref.shape
    mn = _rms_norm(mem_ref[...], g_ref[...]).astype(BF16)
    k = jnp.dot(mn, wk_ref[...], preferred_element_type=F32)
    kc_ref[...] = _group_rms_norm(k, bd_ref[...], gk_ref[...], HEAD_DIM).astype(BF16)
    v = jnp.dot(mn, wv_ref[...], preferred_element_type=F32)
    ones = jnp.ones((VT_ROWS - HEAD_DIM, mlen), BF16)
    for b in range(batch):
        vt = v[b * mlen:(b + 1) * mlen, :].T.astype(BF16)
        for h in range(MEM_HEADS):
            vct_ref[b, h * VT_ROWS:h * VT_ROWS + HEAD_DIM, :] = vt[h * HEAD_DIM:(h + 1) * HEAD_DIM, :]
            vct_ref[b, h * VT_ROWS + HEAD_DIM:(h + 1) * VT_ROWS, :] = ones


def _mem_kv(mem2d, g, w_k, w_v, bd128, gk, batch):
    m = mem2d.shape[0]
    return pl.pallas_call(
        _mem_kv_kernel,
        out_shape=(jax.ShapeDtypeStruct((m, w_k.shape[1]), BF16),
                   jax.ShapeDtypeStruct((batch, MEM_HEADS * VT_ROWS, m // batch), BF16)),
        name="mem_kv",
    )(mem2d, g, w_k, w_v, bd128, gk)


def _in_proj_kernel(x_ref, g_ref, w_lru_ref, w_q_ref, w_k_ref, w_v_ref, w_c_ref,
                    gq_ref, gk_ref, gc_ref, bd64_ref, bd128_ref, alibi_ref,
                    lrux_ref, lrug_ref, qt_ref, kaug_ref, vt_ref, kmean_ref, cqt_ref, *, seq):
    tm = x_ref.shape[0]
    n_lru = lrux_ref.shape[1]
    xn = _rms_norm(x_ref[...], g_ref[...]).astype(BF16)

    lrux_ref[...] = jnp.dot(xn, w_lru_ref[:, :n_lru], preferred_element_type=F32)
    lrug_ref[...] = jnp.dot(xn, w_lru_ref[:, n_lru:], preferred_element_type=F32)

    cq = jnp.dot(xn, w_c_ref[...], preferred_element_type=F32)
    cq = _group_rms_norm(cq, bd64_ref[...], gc_ref[...], HEAD_DIM) * (HEAD_DIM ** -0.5 * LOG2E)
    for r in range(tm // MOBA_BLOCK):
        cqt_ref[r] = cq[r * MOBA_BLOCK:(r + 1) * MOBA_BLOCK, :].T.astype(BF16)

    q = jnp.dot(xn, w_q_ref[...], preferred_element_type=F32)
    q = _group_rms_norm(q, bd64_ref[...], gq_ref[...], HEAD_DIM)
    v = jnp.dot(xn, w_v_ref[...], preferred_element_type=F32)
    ones = jnp.ones((VT_ROWS - HEAD_DIM, MOBA_BLOCK), BF16)
    for r in range(tm // MOBA_BLOCK):
        rows = slice(r * MOBA_BLOCK, (r + 1) * MOBA_BLOCK)
        qt_ref[r] = q[rows, :].T
        vt = v[rows, :].T.astype(BF16)
        for h in range(MOBA_HEADS):
            vt_ref[r, h * VT_ROWS:h * VT_ROWS + HEAD_DIM, :] = vt[h * HEAD_DIM:(h + 1) * HEAD_DIM, :]
            vt_ref[r, h * VT_ROWS + HEAD_DIM:(h + 1) * VT_ROWS, :] = ones

    k = jnp.dot(xn, w_k_ref[...], preferred_element_type=F32)
    k = _group_rms_norm(k, bd128_ref[...], gk_ref[...], HEAD_DIM)
    kmean_ref[...] = jnp.zeros(kmean_ref.shape, F32)
    for r in range(tm // MOBA_BLOCK):
        rows = slice(r * MOBA_BLOCK, (r + 1) * MOBA_BLOCK)
        kmean_ref[0, r:r + 1, :] = jnp.sum(k[rows, :], axis=0, keepdims=True) * (1.0 / MOBA_BLOCK)

    shape = k.shape
    row = lax.broadcasted_iota(jnp.int32, shape, 0)
    lane = lax.broadcasted_iota(jnp.int32, shape, 1) % AUG_W
    t = (pl.program_id(0) * tm + row) % seq
    bias = t.astype(F32) * alibi_ref[...]
    b_hi = bias.astype(BF16).astype(F32)
    b_mid = (bias - b_hi).astype(BF16).astype(F32)
    b_lo = bias - b_hi - b_mid
    pieces = jnp.where(lane == AUG_ALIBI, b_hi, jnp.where(lane == AUG_ALIBI + 1, b_mid, b_lo))
    onehot = jnp.where(lane - AUG_MASK == t // MOBA_BLOCK, 1.0, 0.0)
    is_alibi = (lane >= AUG_ALIBI) & (lane < AUG_ALIBI + 3)
    kaug_ref[...] = (k + jnp.where(is_alibi, pieces, onehot)).astype(BF16)


def _in_proj(x1, g, w_lru, w_q, w_k, w_v, w_c, gq, gk, gc, bd64, bd128, alibi, seq):
    n, d = x1.shape
    tm = TOKEN_TILE
    rb = tm // MOBA_BLOCK
    nblk = n // MOBA_BLOCK
    n_lru = w_lru.shape[1] // 2
    wq = w_q.shape[1]
    wk = w_k.shape[1]
    consts = (g, w_lru, w_q, w_k, w_v, w_c, gq, gk, gc, bd64, bd128, alibi)
    resident = sum(c.size * c.dtype.itemsize for c in consts)
    return pl.pallas_call(
        functools.partial(_in_proj_kernel, seq=seq),
        grid=(n // tm,),
        in_specs=[pl.BlockSpec((tm, d), lambda i: (i, 0))] + [_const_spec(c.shape) for c in consts],
        out_specs=[pl.BlockSpec((tm, n_lru), lambda i: (i, 0)),
                   pl.BlockSpec((tm, n_lru), lambda i: (i, 0)),
                   pl.BlockSpec((rb, wq, MOBA_BLOCK), lambda i: (i, 0, 0)),
                   pl.BlockSpec((tm, wk), lambda i: (i, 0)),
                   pl.BlockSpec((rb, MOBA_HEADS * VT_ROWS, MOBA_BLOCK), lambda i: (i, 0, 0)),
                   pl.BlockSpec((1, 8, wk), lambda i: (i, 0, 0)),
                   pl.BlockSpec((rb, wq, MOBA_BLOCK), lambda i: (i, 0, 0))],
        out_shape=(jax.ShapeDtypeStruct((n, n_lru), F32),
                   jax.ShapeDtypeStruct((n, n_lru), F32),
                   jax.ShapeDtypeStruct((nblk, wq, MOBA_BLOCK), F32),
                   jax.ShapeDtypeStruct((n, wk), BF16),
                   jax.ShapeDtypeStruct((nblk, MOBA_HEADS * VT_ROWS, MOBA_BLOCK), BF16),
                   jax.ShapeDtypeStruct((n // tm, 8, wk), F32),
                   jax.ShapeDtypeStruct((nblk, wq, MOBA_BLOCK), BF16)),
        compiler_params=pltpu.CompilerParams(dimension_semantics=("arbitrary",),
                                             vmem_limit_bytes=_vmem_limit(resident)),
        name="in_proj",
    )(x1, *consts)


def _gelu_tanh(x):
    c = math.sqrt(2.0 / math.pi)
    return 0.5 * x * (1.0 + jnp.tanh(c * (x + 0.044715 * (x * x * x))))


def _lru_kernel(x_ref, g_ref, cw_ref, cb_ref, wa_ref, ba_ref, wx_ref, bx_ref, lam_ref,
                o_ref, xpad_ref, a_ref, b_ref, hl_ref, carry_ref):
    tl, c = x_ref.shape
    half = V7X_MXU_DIM
    ti = pl.program_id(1)

    @pl.when(ti == 0)
    def _():
        xpad_ref[0:8, :] = jnp.zeros((8, c), F32)
        carry_ref[...] = jnp.zeros(carry_ref.shape, F32)

    xpad_ref[8:8 + tl, :] = x_ref[...]
    xb = jnp.broadcast_to(cb_ref[...], (tl, c))
    for j in range(CONV_W):
        off = 8 - (CONV_W - 1) + j
        xb = xb + cw_ref[j:j + 1, :] * xpad_ref[off:off + tl, :]
    xpad_ref[0:8, :] = x_ref[tl - 8:tl, :]

    xbb = xb.astype(BF16)

    def gate(w_ref, bias_ref):
        z = jnp.concatenate(
            [jnp.dot(xbb[:, :half], w_ref[0], preferred_element_type=F32),
             jnp.dot(xbb[:, half:], w_ref[1], preferred_element_type=F32)], axis=1)
        return jax.nn.sigmoid(z + bias_ref[...])

    r = gate(wa_ref, ba_ref)
    i = gate(wx_ref, bx_ref)
    neg_lam = -lam_ref[...]
    softplus = jnp.maximum(neg_lam, 0.0) + jnp.log1p(jnp.exp(-jnp.abs(neg_lam)))
    log_a = (-LRU_C) * r * softplus
    a = jnp.exp(log_a)
    mult = jnp.sqrt(1.0 - a * a)
    row = lax.broadcasted_iota(jnp.int32, (tl, c), 0)
    start_row = jnp.where(ti == 0, 0, -1)
    mult = jnp.where(row == start_row, 1.0, mult)
    bt = mult * i * xb

    groups = tl // SUBLANES
    sub = lax.broadcasted_iota(jnp.int32, (groups, SUBLANES, LANES), 1)
    for k in range(c // LANES):
        lanes = slice(k * LANES, (k + 1) * LANES)
        a3 = a[:, lanes].reshape(groups, SUBLANES, LANES)
        b3 = bt[:, lanes].reshape(groups, SUBLANES, LANES)
        d = 1
        while d < SUBLANES:
            has_prev = sub >= d
            a_prev = pltpu.roll(a3, d, axis=1)
            b_prev = pltpu.roll(b3, d, axis=1)
            b3 = b3 + jnp.where(has_prev, a3, 0.0) * b_prev
            a3 = jnp.where(has_prev, a3 * a_prev, a3)
            d *= 2
        a_ref[:, lanes] = a3.reshape(tl, LANES)
        b_ref[:, lanes] = b3.reshape(tl, LANES)

    h = carry_ref[...]
    for grp in range(groups):
        rows = slice(grp * SUBLANES, (grp + 1) * SUBLANES)
        h_last = jnp.broadcast_to(h[SUBLANES - 1:SUBLANES, :], (SUBLANES, c))
        h = a_ref[rows, :] * h_last + b_ref[rows, :]
        hl_ref[rows, :] = h
    carry_ref[...] = h

    o_ref[...] = (hl_ref[...] * _gelu_tanh(g_ref[...])).astype(o_ref.dtype)


def _lru(lru_x, lru_g, cw, cb, wa, ba, wx, bx, lam, batch, seq):
    n, c = lru_x.shape
    tl = LRU_TILE
    nt = seq // tl
    consts = (cw, cb, wa, ba, wx, bx, lam)
    tile = pl.BlockSpec((tl, c), lambda b, t: (b * nt + t, 0))
    return pl.pallas_call(
        _lru_kernel,
        grid=(batch, nt),
        in_specs=[tile, tile] + [_const_spec(w.shape) for w in consts],
        out_specs=tile,
        out_shape=jax.ShapeDtypeStruct((n, c), BF16),
        scratch_shapes=[pltpu.VMEM((tl + SUBLANES, c), F32)]
        + [pltpu.VMEM((tl, c), F32)] * 3
        + [pltpu.VMEM((SUBLANES, c), F32)],
        compiler_params=pltpu.CompilerParams(dimension_semantics=("arbitrary", "arbitrary")),
        name="lru",
    )(lru_x, lru_g, *consts)


def _moba_kernel(qt_ref, cqt_ref, kaug_ref, vt_ref, kmean_ref, kc_ref, vct_ref, o_ref, om_ref,
                 qaug_ref, qown_ref, qmem_ref, s_ref, p_ref, alpha_ref, m_ref, acc_ref, accm_ref,
                 ot_ref, otm_ref):
    qi = pl.program_id(1)
    nblk = kmean_ref.shape[1]
    blk = MOBA_BLOCK

    def build_query_operands(h):
        qt = qt_ref[0, 0, h * HEAD_DIM:(h + 1) * HEAD_DIM, :]
        qs = (qt * (HEAD_DIM ** -0.5 * LOG2E)).astype(BF16)
        sub = lax.broadcasted_iota(jnp.int32, (AUG_MASK - AUG_ALIBI, blk), 0)
        ones = jnp.where(sub < 3, 1.0, 0.0).astype(BF16)
        for ref in (qaug_ref, qown_ref):
            ref[h, 0:HEAD_DIM, :] = qs
            ref[h, AUG_ALIBI:AUG_MASK, :] = ones
        qown_ref[h, AUG_MASK:, :] = jnp.zeros((AUG_W - AUG_MASK, blk), BF16)
        qmem_ref[h, 0:HEAD_DIM, :] = cqt_ref[0, 0, h * HEAD_DIM:(h + 1) * HEAD_DIM, :]
        qmem_ref[h, HEAD_DIM:, :] = jnp.zeros((AUG_W - HEAD_DIM, blk), BF16)

    def build_block_mask(h):
        blk_id = lax.broadcasted_iota(jnp.int32, (nblk, blk), 0).astype(F32)
        qt = qt_ref[0, 0, h * HEAD_DIM:(h + 1) * HEAD_DIM, :]
        gate = jnp.dot(kmean_ref[0, :, h * AUG_W:h * AUG_W + HEAD_DIM], qt,
                       preferred_element_type=F32, precision=lax.Precision.HIGHEST)
        gate = jnp.where(blk_id < qi.astype(F32), gate, -jnp.inf)
        keep = blk_id < 0.0
        for _ in range(MOBA_TOPK):
            best = jnp.max(gate, axis=0, keepdims=True)
            first = jnp.min(jnp.where(gate == best, blk_id, float(nblk)), axis=0, keepdims=True)
            pick = (blk_id == first) & (best > -jnp.inf)
            keep = keep | pick
            gate = jnp.where(pick, -jnp.inf, gate)
        qaug_ref[h, AUG_MASK:AUG_MASK + nblk, :] = jnp.where(keep, 0.0, NEG_BIG).astype(BF16)
        qaug_ref[h, AUG_MASK + nblk:, :] = jnp.zeros((AUG_W - AUG_MASK - nblk, blk), BF16)

    kk = lax.broadcasted_iota(jnp.int32, (blk, blk), 0)
    qq = lax.broadcasted_iota(jnp.int32, (blk, blk), 1)
    causal = kk <= qq

    per_trip = PAST_UNROLL * MOBA_HEADS
    lead = MOBA_HEADS + MEM_HEADS

    def past_block(trip, u):
        return jnp.minimum(trip * PAST_UNROLL + u, nblk - 1)

    def stream_item(trip, n):
        if n < 0:
            idx = n + lead
            if idx < MOBA_HEADS:
                return "own", qi, idx, idx % SCORE_SLOTS
            return "mem", 0, idx - MOBA_HEADS, idx % SCORE_SLOTS
        if n >= per_trip:
            trip, n = trip + 1, n - per_trip
        return "past", past_block(trip, n // MOBA_HEADS), n % MOBA_HEADS, (n + lead) % SCORE_SLOTS

    def scores(kind, j, h, slot):
        lanes = slice(h * AUG_W, (h + 1) * AUG_W)
        if kind == "mem":
            keys, q_ref = kc_ref[0, :, lanes], qmem_ref
        else:
            keys, q_ref = kaug_ref[0, j, :, lanes], (qown_ref if kind == "own" else qaug_ref)
        s_ref[slot] = jnp.dot(keys, q_ref[h], preferred_element_type=F32)

    def probabilities(kind, j, h, slot):
        s = s_ref[slot]
        if kind == "past":
            m_old = m_ref[h]
            m_new = jnp.maximum(m_old, jnp.max(s, axis=0, keepdims=True))
            alpha_ref[slot % PROB_SLOTS] = jnp.exp2(m_old - m_new)
        else:
            if kind == "own":
                s = jnp.where(causal, s, NEG_BIG)
            m_new = jnp.max(s, axis=0, keepdims=True)
        p_ref[slot % PROB_SLOTS] = jnp.exp2(s - m_new).astype(BF16)
        if kind != "mem":
            m_ref[h] = m_new

    def accumulate(kind, j, h, slot):
        rows = slice(h * VT_ROWS, (h + 1) * VT_ROWS)
        p = p_ref[slot % PROB_SLOTS]
        if kind == "mem":
            accm_ref[h] = jnp.dot(vct_ref[0, rows, :], p, preferred_element_type=F32)
        else:
            pv = jnp.dot(vt_ref[0, j, rows, :], p, preferred_element_type=F32)
            acc_ref[h] = pv if kind == "own" else alpha_ref[slot % PROB_SLOTS] * acc_ref[h] + pv

    for h in range(MOBA_HEADS):
        build_query_operands(h)
        build_block_mask(h)
    for n in range(-lead, -lead + SCORE_LOOKAHEAD):
        scores(*stream_item(0, n))
    probabilities(*stream_item(0, -lead))
    for n in range(-lead, 0):
        scores(*stream_item(0, n + SCORE_LOOKAHEAD))
        probabilities(*stream_item(0, n + 1))
        accumulate(*stream_item(0, n))

    def past_trip(trip, carry):
        for n in range(per_trip):
            scores(*stream_item(trip, n + SCORE_LOOKAHEAD))
            probabilities(*stream_item(trip, n + 1))
            accumulate(*stream_item(trip, n))
        return carry

    lax.fori_loop(0, (qi + PAST_UNROLL - 1) // PAST_UNROLL, past_trip, 0)

    for ref, t_ref, out_ref in ((acc_ref, ot_ref, o_ref), (accm_ref, otm_ref, om_ref)):
        for h in range(MOBA_HEADS):
            t_ref[h * HEAD_DIM:(h + 1) * HEAD_DIM, :] = (ref[h, 0:HEAD_DIM, :]
                                                         / ref[h, HEAD_DIM:HEAD_DIM + 1, :])
        out_ref[...] = t_ref[...].T.astype(out_ref.dtype)


def _moba(qt, cqt, kaug, vt, kmean, kc, vct, batch, seq):
    nblk = seq // MOBA_BLOCK
    width = MOBA_HEADS * HEAD_DIM
    aug = MOBA_HEADS * AUG_W
    vt_rows = MOBA_HEADS * VT_ROWS
    mlen = vct.shape[2]
    assert mlen == MOBA_BLOCK and MEM_HEADS == MOBA_HEADS, "memory items reuse the key-block buffers"
    qt = qt.reshape(batch, nblk, width, MOBA_BLOCK)
    cqt = cqt.reshape(batch, nblk, width, MOBA_BLOCK)
    kaug = kaug.reshape(batch, nblk, MOBA_BLOCK, aug)
    vt = vt.reshape(batch, nblk, vt_rows, MOBA_BLOCK)
    kc = kc.reshape(batch, mlen, aug)
    resident = 2 * (kaug.size // batch + vt.size // batch) * 2
    qblock = pl.BlockSpec((1, 1, width, MOBA_BLOCK), lambda b, i: (b, i, 0, 0))
    out_block = pl.BlockSpec((MOBA_BLOCK, width), lambda b, i: (b * nblk + i, 0))
    out_shape = jax.ShapeDtypeStruct((batch * seq, width), BF16)
    return pl.pallas_call(
        _moba_kernel,
        grid=(batch, nblk),
        in_specs=[qblock, qblock,
                  pl.BlockSpec((1, nblk, MOBA_BLOCK, aug), lambda b, i: (b, 0, 0, 0)),
                  pl.BlockSpec((1, nblk, vt_rows, MOBA_BLOCK), lambda b, i: (b, 0, 0, 0)),
                  pl.BlockSpec((1, nblk, aug), lambda b, i: (b, 0, 0)),
                  pl.BlockSpec((1, mlen, aug), lambda b, i: (b, 0, 0)),
                  pl.BlockSpec((1, vt_rows, mlen), lambda b, i: (b, 0, 0))],
        out_specs=(out_block, out_block),
        out_shape=(out_shape, out_shape),
        scratch_shapes=[pltpu.VMEM((MOBA_HEADS, AUG_W, MOBA_BLOCK), BF16),
                        pltpu.VMEM((MOBA_HEADS, AUG_W, MOBA_BLOCK), BF16),
                        pltpu.VMEM((MOBA_HEADS, AUG_W, MOBA_BLOCK), BF16),
                        pltpu.VMEM((SCORE_SLOTS, MOBA_BLOCK, MOBA_BLOCK), F32),
                        pltpu.VMEM((PROB_SLOTS, MOBA_BLOCK, MOBA_BLOCK), BF16),
                        pltpu.VMEM((PROB_SLOTS, 1, MOBA_BLOCK), F32),
                        pltpu.VMEM((MOBA_HEADS, 1, MOBA_BLOCK), F32),
                        pltpu.VMEM((MOBA_HEADS, VT_ROWS, MOBA_BLOCK), F32),
                        pltpu.VMEM((MOBA_HEADS, VT_ROWS, MOBA_BLOCK), F32),
                        pltpu.VMEM((width, MOBA_BLOCK), F32),
                        pltpu.VMEM((width, MOBA_BLOCK), F32)],
        compiler_params=pltpu.CompilerParams(dimension_semantics=("arbitrary", "arbitrary"),
                                             vmem_limit_bytes=_vmem_limit(resident)),
        name="moba",
    )(qt, cqt, kaug, vt, kmean, kc, vct)


def _block_diag_ones(group):
    idx = jnp.arange(V7X_MXU_DIM) // group
    return (idx[:, None] == idx[None, :]).astype(BF16)


def _pack_block_diag(w):
    nb, bw, _ = w.shape
    per = V7X_MXU_DIM // bw
    w = w.reshape(nb // per, per, bw, bw)
    eye = jnp.eye(per, dtype=w.dtype)
    return jnp.einsum("gpij,pq->gpiqj", w, eye).reshape(nb // per, V7X_MXU_DIM, V7X_MXU_DIM)


def _pad_heads(w, heads):
    lead = w.shape[:-1]
    w = w.reshape(lead + (heads, HEAD_DIM))
    w = jnp.pad(w, [(0, 0)] * len(lead) + [(0, 0), (0, AUG_W - HEAD_DIM)])
    return w.reshape(lead + (heads * AUG_W,))


def _layer(x, mem, ffn1_norm, ffn1_w_in, ffn1_w_out, mix_norm, mem_norm, w_in,
           lru_conv_w, lru_conv_b, lru_a_w, lru_a_b, lru_x_w, lru_x_b, lru_lambda,
           moba_q_norm, moba_k_norm, mem_w_kv, mem_q_norm, mem_k_norm, w_out,
           ffn2_norm, ffn2_w_in, ffn2_w_out):
    batch, seq, d = x.shape
    n = batch * seq
    n_lru = lru_lambda.shape[0]
    wq = MOBA_HEADS * HEAD_DIM
    wc = MEM_HEADS * HEAD_DIM
    row = lambda v: v.reshape(1, -1).astype(F32)

    bd64 = _block_diag_ones(HEAD_DIM)
    bd128 = _block_diag_ones(AUG_W)

    w_in_b = w_in.astype(BF16)
    o = 2 * n_lru
    w_lru = w_in_b[:, :o]
    w_q = w_in_b[:, o:o + wq]
    w_k = _pad_heads(w_in_b[:, o + wq:o + 2 * wq], MOBA_HEADS)
    w_v = w_in_b[:, o + 2 * wq:o + 3 * wq]
    w_c = w_in_b[:, o + 3 * wq:]
    gq = row(jnp.tile(moba_q_norm, MOBA_HEADS))
    gk = row(_pad_heads(jnp.tile(moba_k_norm, MOBA_HEADS), MOBA_HEADS))
    gc = row(jnp.tile(mem_q_norm, MEM_HEADS))
    gck = row(_pad_heads(jnp.tile(mem_k_norm, MEM_HEADS), MEM_HEADS))
    w_kv_b = mem_w_kv.astype(BF16)
    w_ck = _pad_heads(w_kv_b[:, :wc], MEM_HEADS)
    w_cv = w_kv_b[:, wc:]
    slopes = 2.0 ** (-8.0 * jnp.arange(1, MOBA_HEADS + 1, dtype=F32) / MOBA_HEADS)
    lane = jnp.arange(AUG_W)
    is_alibi = (lane >= AUG_ALIBI) & (lane < AUG_ALIBI + 3)
    alibi = row(jnp.where(is_alibi[None, :], slopes[:, None] * LOG2E, 0.0))

    x2d = x.reshape(n, d)
    kc, vct = _mem_kv(mem.reshape(-1, d), row(mem_norm), w_ck, w_cv, bd128, gck, batch)

    x1 = _ffn(x2d, row(ffn1_norm), ffn1_w_in.astype(BF16), ffn1_w_out.astype(BF16))

    lru_x, lru_g, qt, kaug, vt, kmean, cqt = _in_proj(
        x1, row(mix_norm), w_lru, w_q, w_k, w_v, w_c, gq, gk, gc, bd64, bd128, alibi, seq)
    kmean = kmean[:, :TOKEN_TILE // MOBA_BLOCK, :].reshape(batch, seq // MOBA_BLOCK, -1)

    y_lru = _lru(lru_x, lru_g, lru_conv_w.astype(F32), row(lru_conv_b),
                 _pack_block_diag(lru_a_w).astype(BF16), row(lru_a_b),
                 _pack_block_diag(lru_x_w).astype(BF16), row(lru_x_b), row(lru_lambda), batch, seq)
    y_moba, y_mem = _moba(qt, cqt, kaug, vt, kmean, kc, vct, batch, seq)

    out = _out_ffn(x1, y_lru, y_moba, y_mem, w_out.astype(BF16), row(ffn2_norm),
                   ffn2_w_in.astype(BF16), ffn2_w_out.astype(BF16))
    return out.reshape(batch, seq, d)


def kernel(x, mem, ffn1_norm, ffn1_w_in, ffn1_w_out, mix_norm, mem_norm, w_in, lru_conv_w, lru_conv_b,
           lru_a_w, lru_a_b, lru_x_w, lru_x_b, lru_lambda, moba_q_norm, moba_k_norm, mem_w_kv,
           mem_q_norm, mem_k_norm, w_out, ffn2_norm, ffn2_w_in, ffn2_w_out):
    params = (ffn1_norm, ffn1_w_in, ffn1_w_out, mix_norm, mem_norm, w_in, lru_conv_w, lru_conv_b,
              lru_a_w, lru_a_b, lru_x_w, lru_x_b, lru_lambda, moba_q_norm, moba_k_norm, mem_w_kv,
              mem_q_norm, mem_k_norm, w_out, ffn2_norm, ffn2_w_in, ffn2_w_out)
    for layer in range(ffn1_norm.shape[0]):
        x = _layer(x, mem, *(p[layer] for p in params))
    return x
```

```python
import functools
import math

import jax
import jax.numpy as jnp
from jax import lax
from jax.experimental import pallas as pl
from jax.experimental.pallas import tpu as pltpu

F32 = jnp.float32
BF16 = jnp.bfloat16

HEAD_DIM = 64
CONV_W = 4
LRU_C = 8.0
MOBA_HEADS = 4
MOBA_BLOCK = 256
MOBA_TOPK = 3
MEM_HEADS = 4
NORM_EPS = 1e-6
LOG2E = 1.4426950408889634
NEG_BIG = -1e30

V7X_MXU_DIM = 256
LANES = 128
SUBLANES = 8
V7X_VMEM_BYTES = 64 * 1024 * 1024

AUG_W = 128
AUG_ALIBI = 64
AUG_MASK = 80
VT_ROWS = HEAD_DIM + 16
SCORE_LOOKAHEAD = 5
PAST_UNROLL = 4
SCORE_SLOTS = 8
assert MOBA_HEADS <= SCORE_LOOKAHEAD < SCORE_SLOTS and (PAST_UNROLL * MOBA_HEADS) % SCORE_SLOTS == 0
assert (MOBA_HEADS + MEM_HEADS) % SCORE_SLOTS == 0
PROB_SLOTS = SCORE_SLOTS
PROB_LOOKAHEAD = 2
assert PROB_LOOKAHEAD < SCORE_LOOKAHEAD

TOKEN_TILE = 512
LRU_TILE = 512
FFN_CHUNK = 256


def _vmem_limit(resident_bytes):
    return int(min(V7X_VMEM_BYTES - 8 * 1024 * 1024, resident_bytes + 24 * 1024 * 1024))


def _const_spec(shape):
    n = len(shape)
    return pl.BlockSpec(shape, lambda *_: (0,) * n, pipeline_mode=pl.Buffered(1))


def _rms_norm(x, g):
    ms = jnp.mean(x * x, axis=-1, keepdims=True)
    return x * lax.rsqrt(ms + NORM_EPS) * g


def _group_rms_norm(u, bd, g, group):
    sq = u * u
    hi = sq.astype(BF16)
    lo = (sq - hi.astype(F32)).astype(BF16)
    parts = []
    for c in range(u.shape[1] // V7X_MXU_DIM):
        sl = slice(c * V7X_MXU_DIM, (c + 1) * V7X_MXU_DIM)
        parts.append(jnp.dot(hi[:, sl], bd, preferred_element_type=F32)
                     + jnp.dot(lo[:, sl], bd, preferred_element_type=F32))
    ss = parts[0] if len(parts) == 1 else jnp.concatenate(parts, axis=1)
    return u * lax.rsqrt(ss * (1.0 / group) + NORM_EPS) * g


def _swiglu_half_step(x, g_ref, w_in_ref, w_out_ref, act_ref):
    d_ff = w_out_ref.shape[0]
    xn = _rms_norm(x, g_ref[...]).astype(BF16)
    for c in range(d_ff // FFN_CHUNK):
        lo = c * FFN_CHUNK
        a = jnp.dot(xn, w_in_ref[:, lo:lo + FFN_CHUNK], preferred_element_type=F32)
        b = jnp.dot(xn, w_in_ref[:, d_ff + lo:d_ff + lo + FFN_CHUNK], preferred_element_type=F32)
        act_ref[:, lo:lo + FFN_CHUNK] = (a * jax.nn.sigmoid(a) * b).astype(BF16)
    y = jnp.dot(act_ref[...], w_out_ref[...], preferred_element_type=F32)
    return x + 0.5 * y


def _ffn_kernel(x_ref, g_ref, w_in_ref, w_out_ref, o_ref, act_ref):
    o_ref[...] = _swiglu_half_step(x_ref[...], g_ref, w_in_ref, w_out_ref, act_ref)


def _ffn(x, g, w_in, w_out):
    n, d = x.shape
    d_ff = w_out.shape[0]
    tm = TOKEN_TILE
    resident = (w_in.size + w_out.size) * 2
    return pl.pallas_call(
        _ffn_kernel,
        grid=(n // tm,),
        in_specs=[pl.BlockSpec((tm, d), lambda i: (i, 0)),
                  _const_spec((1, d)), _const_spec(w_in.shape), _const_spec(w_out.shape)],
        out_specs=pl.BlockSpec((tm, d), lambda i: (i, 0)),
        out_shape=jax.ShapeDtypeStruct((n, d), F32),
        scratch_shapes=[pltpu.VMEM((tm, d_ff), BF16)],
        compiler_params=pltpu.CompilerParams(dimension_semantics=("arbitrary",),
                                             vmem_limit_bytes=_vmem_limit(resident)),
        name="ffn1",
    )(x, g, w_in, w_out)


def _out_ffn_kernel(x_ref, ylru_ref, ymoba_ref, ymem_ref, wo_ref, g_ref, w_in_ref, w_out_ref,
                    o_ref, act_ref):
    n_lru = ylru_ref.shape[1]
    n_moba = ymoba_ref.shape[1]
    x2 = (x_ref[...]
          + jnp.dot(ylru_ref[...], wo_ref[0:n_lru, :], preferred_element_type=F32)
          + jnp.dot(ymoba_ref[...], wo_ref[n_lru:n_lru + n_moba, :], preferred_element_type=F32)
          + jnp.dot(ymem_ref[...], wo_ref[n_lru + n_moba:, :], preferred_element_type=F32))
    o_ref[...] = _swiglu_half_step(x2, g_ref, w_in_ref, w_out_ref, act_ref)


def _out_ffn(x1, y_lru, y_moba, y_mem, w_o, g, w_in, w_out):
    n, d = x1.shape
    d_ff = w_out.shape[0]
    tm = TOKEN_TILE
    resident = (w_in.size + w_out.size + w_o.size) * 2

    def tile(w):
        return pl.BlockSpec((tm, w), lambda i: (i, 0))

    return pl.pallas_call(
        _out_ffn_kernel,
        grid=(n // tm,),
        in_specs=[tile(d), tile(y_lru.shape[1]), tile(y_moba.shape[1]), tile(y_mem.shape[1]),
                  _const_spec(w_o.shape), _const_spec((1, d)),
                  _const_spec(w_in.shape), _const_spec(w_out.shape)],
        out_specs=tile(d),
        out_shape=jax.ShapeDtypeStruct((n, d), F32),
        scratch_shapes=[pltpu.VMEM((tm, d_ff), BF16)],
        compiler_params=pltpu.CompilerParams(dimension_semantics=("arbitrary",),
                                             vmem_limit_bytes=_vmem_limit(resident)),
        name="out_ffn2",
    )(x1, y_lru, y_moba, y_mem, w_o, g, w_in, w_out)


def _mem_kv_kernel(mem_ref, g_ref, wk_ref, wv_ref, bd_ref, gk_ref, kc_ref, vct_ref):
    batch, _, mlen = vct_ref.shape
    mn = _rms_norm(mem_ref[...], g_ref[...]).astype(BF16)
    k = jnp.dot(mn, wk_ref[...], preferred_element_type=F32)
    kc_ref[...] = _group_rms_norm(k, bd_ref[...], gk_ref[...], HEAD_DIM).astype(BF16)
    v = jnp.dot(mn, wv_ref[...], preferred_element_type=F32)
    ones = jnp.ones((VT_ROWS - HEAD_DIM, mlen), BF16)
    for b in range(batch):
        vt = v[b * mlen:(b + 1) * mlen, :].T.astype(BF16)
        for h in range(MEM_HEADS):
            vct_ref[b, h * VT_ROWS:h * VT_ROWS + HEAD_DIM, :] = vt[h * HEAD_DIM:(h + 1) * HEAD_DIM, :]
            vct_ref[b, h * VT_ROWS + HEAD_DIM:(h + 1) * VT_ROWS, :] = ones


def _mem_kv(mem2d, g, w_k, w_v, bd128, gk, batch):
    m = mem2d.shape[0]
    return pl.pallas_call(
        _mem_kv_kernel,
        out_shape=(jax.ShapeDtypeStruct((m, w_k.shape[1]), BF16),
                   jax.ShapeDtypeStruct((batch, MEM_HEADS * VT_ROWS, m // batch), BF16)),
        name="mem_kv",
    )(mem2d, g, w_k, w_v, bd128, gk)


def _in_proj_kernel(x_ref, g_ref, w_lru_ref, w_qvc_ref, w_k_ref,
                    gq_ref, gk_ref, gc_ref, bd64_ref, bd128_ref, alibi_ref,
                    lrux_ref, lrug_ref, qt_ref, kaug_ref, vt_ref, kmean_ref, cqt_ref, *, seq):
    tm = x_ref.shape[0]
    n_lru = lrux_ref.shape[1]
    xn = _rms_norm(x_ref[...], g_ref[...]).astype(BF16)

    lrux_ref[...] = jnp.dot(xn, w_lru_ref[:, :n_lru], preferred_element_type=F32)
    lrug_ref[...] = jnp.dot(xn, w_lru_ref[:, n_lru:], preferred_element_type=F32)

    wq = qt_ref.shape[1]
    qvc = jnp.dot(xn, w_qvc_ref[...], preferred_element_type=F32)

    cq = _group_rms_norm(qvc[:, 2 * wq:], bd64_ref[...], gc_ref[...], HEAD_DIM) * (HEAD_DIM ** -0.5 * LOG2E)
    for r in range(tm // MOBA_BLOCK):
        cqt_ref[r] = cq[r * MOBA_BLOCK:(r + 1) * MOBA_BLOCK, :].T.astype(BF16)

    q = _group_rms_norm(qvc[:, :wq], bd64_ref[...], gq_ref[...], HEAD_DIM)
    v = qvc[:, wq:2 * wq]
    ones = jnp.ones((VT_ROWS - HEAD_DIM, MOBA_BLOCK), BF16)
    for r in range(tm // MOBA_BLOCK):
        rows = slice(r * MOBA_BLOCK, (r + 1) * MOBA_BLOCK)
        qt_ref[r] = q[rows, :].T
        vt = v[rows, :].T.astype(BF16)
        for h in range(MOBA_HEADS):
            vt_ref[r, h * VT_ROWS:h * VT_ROWS + HEAD_DIM, :] = vt[h * HEAD_DIM:(h + 1) * HEAD_DIM, :]
            vt_ref[r, h * VT_ROWS + HEAD_DIM:(h + 1) * VT_ROWS, :] = ones

    k = jnp.dot(xn, w_k_ref[...], preferred_element_type=F32)
    k = _group_rms_norm(k, bd128_ref[...], gk_ref[...], HEAD_DIM)
    kmean_ref[...] = jnp.zeros(kmean_ref.shape, F32)
    for r in range(tm // MOBA_BLOCK):
        rows = slice(r * MOBA_BLOCK, (r + 1) * MOBA_BLOCK)
        kmean_ref[0, r:r + 1, :] = jnp.sum(k[rows, :], axis=0, keepdims=True) * (1.0 / MOBA_BLOCK)

    shape = k.shape
    row = lax.broadcasted_iota(jnp.int32, shape, 0)
    lane = lax.broadcasted_iota(jnp.int32, shape, 1) % AUG_W
    t = (pl.program_id(0) * tm + row) % seq
    bias = t.astype(F32) * alibi_ref[...]
    b_hi = bias.astype(BF16).astype(F32)
    b_mid = (bias - b_hi).astype(BF16).astype(F32)
    b_lo = bias - b_hi - b_mid
    pieces = jnp.where(lane == AUG_ALIBI, b_hi, jnp.where(lane == AUG_ALIBI + 1, b_mid, b_lo))
    onehot = jnp.where(lane - AUG_MASK == t // MOBA_BLOCK, 1.0, 0.0)
    is_alibi = (lane >= AUG_ALIBI) & (lane < AUG_ALIBI + 3)
    kaug_ref[...] = (k + jnp.where(is_alibi, pieces, onehot)).astype(BF16)


def _in_proj(x1, g, w_lru, w_qvc, w_k, gq, gk, gc, bd64, bd128, alibi, seq):
    n, d = x1.shape
    tm = TOKEN_TILE
    rb = tm // MOBA_BLOCK
    nblk = n // MOBA_BLOCK
    n_lru = w_lru.shape[1] // 2
    wq = w_qvc.shape[1] // 3
    wk = w_k.shape[1]
    consts = (g, w_lru, w_qvc, w_k, gq, gk, gc, bd64, bd128, alibi)
    resident = sum(c.size * c.dtype.itemsize for c in consts)
    return pl.pallas_call(
        functools.partial(_in_proj_kernel, seq=seq),
        grid=(n // tm,),
        in_specs=[pl.BlockSpec((tm, d), lambda i: (i, 0))] + [_const_spec(c.shape) for c in consts],
        out_specs=[pl.BlockSpec((tm, n_lru), lambda i: (i, 0)),
                   pl.BlockSpec((tm, n_lru), lambda i: (i, 0)),
                   pl.BlockSpec((rb, wq, MOBA_BLOCK), lambda i: (i, 0, 0)),
                   pl.BlockSpec((tm, wk), lambda i: (i, 0)),
                   pl.BlockSpec((rb, MOBA_HEADS * VT_ROWS, MOBA_BLOCK), lambda i: (i, 0, 0)),
                   pl.BlockSpec((1, 8, wk), lambda i: (i, 0, 0)),
                   pl.BlockSpec((rb, wq, MOBA_BLOCK), lambda i: (i, 0, 0))],
        out_shape=(jax.ShapeDtypeStruct((n, n_lru), F32),
                   jax.ShapeDtypeStruct((n, n_lru), F32),
                   jax.ShapeDtypeStruct((nblk, wq, MOBA_BLOCK), F32),
                   jax.ShapeDtypeStruct((n, wk), BF16),
                   jax.ShapeDtypeStruct((nblk, MOBA_HEADS * VT_ROWS, MOBA_BLOCK), BF16),
                   jax.ShapeDtypeStruct((n // tm, 8, wk), F32),
                   jax.ShapeDtypeStruct((nblk, wq, MOBA_BLOCK), BF16)),
        compiler_params=pltpu.CompilerParams(dimension_semantics=("arbitrary",),
                                             vmem_limit_bytes=_vmem_limit(resident)),
        name="in_proj",
    )(x1, *consts)


def _gelu_tanh(x):
    c = math.sqrt(2.0 / math.pi)
    return 0.5 * x * (1.0 + jnp.tanh(c * (x + 0.044715 * (x * x * x))))


def _lru_kernel(x_ref, g_ref, cw_ref, cb_ref, wa_ref, ba_ref, wx_ref, bx_ref, lam_ref,
                o_ref, xpad_ref, a_ref, b_ref, hl_ref, carry_ref):
    tl, c = x_ref.shape
    half = V7X_MXU_DIM
    ti = pl.program_id(1)

    @pl.when(ti == 0)
    def _():
        xpad_ref[0:8, :] = jnp.zeros((8, c), F32)
        carry_ref[...] = jnp.zeros(carry_ref.shape, F32)

    xpad_ref[8:8 + tl, :] = x_ref[...]
    xb = jnp.broadcast_to(cb_ref[...], (tl, c))
    for j in range(CONV_W):
        off = 8 - (CONV_W - 1) + j
        xb = xb + cw_ref[j:j + 1, :] * xpad_ref[off:off + tl, :]
    xpad_ref[0:8, :] = x_ref[tl - 8:tl, :]

    xbb = xb.astype(BF16)

    def gate(w_ref, bias_ref):
        z = jnp.concatenate(
            [jnp.dot(xbb[:, :half], w_ref[0], preferred_element_type=F32),
             jnp.dot(xbb[:, half:], w_ref[1], preferred_element_type=F32)], axis=1)
        return jax.nn.sigmoid(z + bias_ref[...])

    r = gate(wa_ref, ba_ref)
    i = gate(wx_ref, bx_ref)
    neg_lam = -lam_ref[...]
    softplus = jnp.maximum(neg_lam, 0.0) + jnp.log1p(jnp.exp(-jnp.abs(neg_lam)))
    log_a = (-LRU_C) * r * softplus
    a = jnp.exp(log_a)
    mult = jnp.sqrt(1.0 - a * a)
    row = lax.broadcasted_iota(jnp.int32, (tl, c), 0)
    start_row = jnp.where(ti == 0, 0, -1)
    mult = jnp.where(row == start_row, 1.0, mult)
    bt = mult * i * xb

    groups = tl // SUBLANES
    sub = lax.broadcasted_iota(jnp.int32, (groups, SUBLANES, LANES), 1)
    for k in range(c // LANES):
        lanes = slice(k * LANES, (k + 1) * LANES)
        a3 = a[:, lanes].reshape(groups, SUBLANES, LANES)
        b3 = bt[:, lanes].reshape(groups, SUBLANES, LANES)
        d = 1
        while d < SUBLANES:
            has_prev = sub >= d
            a_prev = pltpu.roll(a3, d, axis=1)
            b_prev = pltpu.roll(b3, d, axis=1)
            b3 = b3 + jnp.where(has_prev, a3, 0.0) * b_prev
            a3 = jnp.where(has_prev, a3 * a_prev, a3)
            d *= 2
        a_ref[:, lanes] = a3.reshape(tl, LANES)
        b_ref[:, lanes] = b3.reshape(tl, LANES)

    h = carry_ref[...]
    for grp in range(groups):
        rows = slice(grp * SUBLANES, (grp + 1) * SUBLANES)
        h_last = jnp.broadcast_to(h[SUBLANES - 1:SUBLANES, :], (SUBLANES, c))
        h = a_ref[rows, :] * h_last + b_ref[rows, :]
        hl_ref[rows, :] = h
    carry_ref[...] = h

    o_ref[...] = (hl_ref[...] * _gelu_tanh(g_ref[...])).astype(o_ref.dtype)


def _lru(lru_x, lru_g, cw, cb, wa, ba, wx, bx, lam, batch, seq):
    n, c = lru_x.shape
    tl = LRU_TILE
    nt = seq // tl
    consts = (cw, cb, wa, ba, wx, bx, lam)
    tile = pl.BlockSpec((tl, c), lambda b, t: (b * nt + t, 0))
    return pl.pallas_call(
        _lru_kernel,
        grid=(batch, nt),
        in_specs=[tile, tile] + [_const_spec(w.shape) for w in consts],
        out_specs=tile,
        out_shape=jax.ShapeDtypeStruct((n, c), BF16),
        scratch_shapes=[pltpu.VMEM((tl + SUBLANES, c), F32)]
        + [pltpu.VMEM((tl, c), F32)] * 3
        + [pltpu.VMEM((SUBLANES, c), F32)],
        compiler_params=pltpu.CompilerParams(dimension_semantics=("arbitrary", "arbitrary")),
        name="lru",
    )(lru_x, lru_g, *consts)


def _moba_kernel(qt_ref, cqt_ref, kaug_ref, vt_ref, kmean_ref, kc_ref, vct_ref, o_ref, om_ref,
                 qaug_ref, qown_ref, qmem_ref, s_ref, p_ref, alpha_ref, m_ref, acc_ref, accm_ref,
                 ot_ref, otm_ref):
    qi = pl.program_id(1)
    nblk = kmean_ref.shape[1]
    blk = MOBA_BLOCK

    def build_query_operands(h):
        qt = qt_ref[0, 0, h * HEAD_DIM:(h + 1) * HEAD_DIM, :]
        qs = (qt * (HEAD_DIM ** -0.5 * LOG2E)).astype(BF16)
        sub = lax.broadcasted_iota(jnp.int32, (AUG_MASK - AUG_ALIBI, blk), 0)
        ones = jnp.where(sub < 3, 1.0, 0.0).astype(BF16)
        for ref in (qaug_ref, qown_ref):
            ref[h, 0:HEAD_DIM, :] = qs
            ref[h, AUG_ALIBI:AUG_MASK, :] = ones
        qown_ref[h, AUG_MASK:, :] = jnp.zeros((AUG_W - AUG_MASK, blk), BF16)
        qmem_ref[h, 0:HEAD_DIM, :] = cqt_ref[0, 0, h * HEAD_DIM:(h + 1) * HEAD_DIM, :]
        qmem_ref[h, HEAD_DIM:, :] = jnp.zeros((AUG_W - HEAD_DIM, blk), BF16)

    def build_block_mask(h):
        blk_id = lax.broadcasted_iota(jnp.int32, (nblk, blk), 0).astype(F32)
        qt = qt_ref[0, 0, h * HEAD_DIM:(h + 1) * HEAD_DIM, :]
        gate = jnp.dot(kmean_ref[0, :, h * AUG_W:h * AUG_W + HEAD_DIM], qt,
                       preferred_element_type=F32, precision=lax.Precision.HIGHEST)
        gate = jnp.where(blk_id < qi.astype(F32), gate, -jnp.inf)
        keep = blk_id < 0.0
        for _ in range(MOBA_TOPK):
            best = jnp.max(gate, axis=0, keepdims=True)
            first = jnp.min(jnp.where(gate == best, blk_id, float(nblk)), axis=0, keepdims=True)
            pick = (blk_id == first) & (best > -jnp.inf)
            keep = keep | pick
            gate = jnp.where(pick, -jnp.inf, gate)
        qaug_ref[h, AUG_MASK:AUG_MASK + nblk, :] = jnp.where(keep, 0.0, NEG_BIG).astype(BF16)
        qaug_ref[h, AUG_MASK + nblk:, :] = jnp.zeros((AUG_W - AUG_MASK - nblk, blk), BF16)

    kk = lax.broadcasted_iota(jnp.int32, (blk, blk), 0)
    qq = lax.broadcasted_iota(jnp.int32, (blk, blk), 1)
    causal = kk <= qq

    per_trip = PAST_UNROLL * MOBA_HEADS
    lead = MOBA_HEADS + MEM_HEADS

    def past_block(trip, u):
        return jnp.minimum(trip * PAST_UNROLL + u, nblk - 1)

    def stream_item(trip, n):
        if n < 0:
            idx = n + lead
            if idx < MOBA_HEADS:
                return "own", qi, idx, idx % SCORE_SLOTS
            return "mem", 0, idx - MOBA_HEADS, idx % SCORE_SLOTS
        if n >= per_trip:
            trip, n = trip + 1, n - per_trip
        return "past", past_block(trip, n // MOBA_HEADS), n % MOBA_HEADS, (n + lead) % SCORE_SLOTS

    def scores(kind, j, h, slot):
        lanes = slice(h * AUG_W, (h + 1) * AUG_W)
        if kind == "mem":
            keys, q_ref = kc_ref[0, :, lanes], qmem_ref
        else:
            keys, q_ref = kaug_ref[0, j, :, lanes], (qown_ref if kind == "own" else qaug_ref)
        s_ref[slot] = jnp.dot(keys, q_ref[h], preferred_element_type=F32)

    def probabilities(kind, j, h, slot):
        s = s_ref[slot]
        if kind == "past":
            m_old = m_ref[h]
            m_new = jnp.maximum(m_old, jnp.max(s, axis=0, keepdims=True))
            alpha_ref[slot % PROB_SLOTS] = jnp.exp2(m_old - m_new)
        else:
            if kind == "own":
                s = jnp.where(causal, s, NEG_BIG)
            m_new = jnp.max(s, axis=0, keepdims=True)
        p_ref[slot % PROB_SLOTS] = jnp.exp2(s - m_new).astype(BF16)
        if kind != "mem":
            m_ref[h] = m_new

    def accumulate(kind, j, h, slot):
        rows = slice(h * VT_ROWS, (h + 1) * VT_ROWS)
        p = p_ref[slot % PROB_SLOTS]
        if kind == "mem":
            accm_ref[h] = jnp.dot(vct_ref[0, rows, :], p, preferred_element_type=F32)
        else:
            pv = jnp.dot(vt_ref[0, j, rows, :], p, preferred_element_type=F32)
            acc_ref[h] = pv if kind == "own" else alpha_ref[slot % PROB_SLOTS] * acc_ref[h] + pv

    for h in range(MOBA_HEADS):
        build_query_operands(h)
        build_block_mask(h)
    for n in range(-lead, -lead + SCORE_LOOKAHEAD):
        scores(*stream_item(0, n))
    for n in range(-lead, -lead + PROB_LOOKAHEAD):
        probabilities(*stream_item(0, n))
    for n in range(-lead, 0):
        scores(*stream_item(0, n + SCORE_LOOKAHEAD))
        probabilities(*stream_item(0, n + PROB_LOOKAHEAD))
        accumulate(*stream_item(0, n))

    def past_trip(trip, carry):
        for n in range(per_trip):
            scores(*stream_item(trip, n + SCORE_LOOKAHEAD))
            probabilities(*stream_item(trip, n + PROB_LOOKAHEAD))
            accumulate(*stream_item(trip, n))
        return carry

    lax.fori_loop(0, (qi + PAST_UNROLL - 1) // PAST_UNROLL, past_trip, 0)

    for ref, t_ref, out_ref in ((acc_ref, ot_ref, o_ref), (accm_ref, otm_ref, om_ref)):
        for h in range(MOBA_HEADS):
            t_ref[h * HEAD_DIM:(h + 1) * HEAD_DIM, :] = (ref[h, 0:HEAD_DIM, :]
                                                         / ref[h, HEAD_DIM:HEAD_DIM + 1, :])
        out_ref[...] = t_ref[...].T.astype(out_ref.dtype)


def _moba(qt, cqt, kaug, vt, kmean, kc, vct, batch, seq):
    nblk = seq // MOBA_BLOCK
    width = MOBA_HEADS * HEAD_DIM
    aug = MOBA_HEADS * AUG_W
    vt_rows = MOBA_HEADS * VT_ROWS
    mlen = vct.shape[2]
    assert mlen == MOBA_BLOCK and MEM_HEADS == MOBA_HEADS, "memory items reuse the key-block buffers"
    qt = qt.reshape(batch, nblk, width, MOBA_BLOCK)
    cqt = cqt.reshape(batch, nblk, width, MOBA_BLOCK)
    kaug = kaug.reshape(batch, nblk, MOBA_BLOCK, aug)
    vt = vt.reshape(batch, nblk, vt_rows, MOBA_BLOCK)
    kc = kc.reshape(batch, mlen, aug)
    resident = 2 * (kaug.size // batch + vt.size // batch) * 2
    qblock = pl.BlockSpec((1, 1, width, MOBA_BLOCK), lambda b, i: (b, i, 0, 0))
    out_block = pl.BlockSpec((MOBA_BLOCK, width), lambda b, i: (b * nblk + i, 0))
    out_shape = jax.ShapeDtypeStruct((batch * seq, width), BF16)
    return pl.pallas_call(
        _moba_kernel,
        grid=(batch, nblk),
        in_specs=[qblock, qblock,
                  pl.BlockSpec((1, nblk, MOBA_BLOCK, aug), lambda b, i: (b, 0, 0, 0)),
                  pl.BlockSpec((1, nblk, vt_rows, MOBA_BLOCK), lambda b, i: (b, 0, 0, 0)),
                  pl.BlockSpec((1, nblk, aug), lambda b, i: (b, 0, 0)),
                  pl.BlockSpec((1, mlen, aug), lambda b, i: (b, 0, 0)),
                  pl.BlockSpec((1, vt_rows, mlen), lambda b, i: (b, 0, 0))],
        out_specs=(out_block, out_block),
        out_shape=(out_shape, out_shape),
        scratch_shapes=[pltpu.VMEM((MOBA_HEADS, AUG_W, MOBA_BLOCK), BF16),
                        pltpu.VMEM((MOBA_HEADS, AUG_W, MOBA_BLOCK), BF16),
                        pltpu.VMEM((MOBA_HEADS, AUG_W, MOBA_BLOCK), BF16),
                        pltpu.VMEM((SCORE_SLOTS, MOBA_BLOCK, MOBA_BLOCK), F32),
                        pltpu.VMEM((PROB_SLOTS, MOBA_BLOCK, MOBA_BLOCK), BF16),
                        pltpu.VMEM((PROB_SLOTS, 1, MOBA_BLOCK), F32),
                        pltpu.VMEM((MOBA_HEADS, 1, MOBA_BLOCK), F32),
                        pltpu.VMEM((MOBA_HEADS, VT_ROWS, MOBA_BLOCK), F32),
                        pltpu.VMEM((MOBA_HEADS, VT_ROWS, MOBA_BLOCK), F32),
                        pltpu.VMEM((width, MOBA_BLOCK), F32),
                        pltpu.VMEM((width, MOBA_BLOCK), F32)],
        compiler_params=pltpu.CompilerParams(dimension_semantics=("arbitrary", "arbitrary"),
                                             vmem_limit_bytes=_vmem_limit(resident)),
        name="moba",
    )(qt, cqt, kaug, vt, kmean, kc, vct)


def _block_diag_ones(group):
    idx = jnp.arange(V7X_MXU_DIM) // group
    return (idx[:, None] == idx[None, :]).astype(BF16)


def _pack_block_diag(w):
    nb, bw, _ = w.shape
    per = V7X_MXU_DIM // bw
    w = w.reshape(nb // per, per, bw, bw)
    eye = jnp.eye(per, dtype=w.dtype)
    return jnp.einsum("gpij,pq->gpiqj", w, eye).reshape(nb // per, V7X_MXU_DIM, V7X_MXU_DIM)


def _pad_heads(w, heads):
    lead = w.shape[:-1]
    w = w.reshape(lead + (heads, HEAD_DIM))
    w = jnp.pad(w, [(0, 0)] * len(lead) + [(0, 0), (0, AUG_W - HEAD_DIM)])
    return w.reshape(lead + (heads * AUG_W,))


def _layer(x, mem, ffn1_norm, ffn1_w_in, ffn1_w_out, mix_norm, mem_norm, w_in,
           lru_conv_w, lru_conv_b, lru_a_w, lru_a_b, lru_x_w, lru_x_b, lru_lambda,
           moba_q_norm, moba_k_norm, mem_w_kv, mem_q_norm, mem_k_norm, w_out,
           ffn2_norm, ffn2_w_in, ffn2_w_out):
    batch, seq, d = x.shape
    n = batch * seq
    n_lru = lru_lambda.shape[0]
    wq = MOBA_HEADS * HEAD_DIM
    wc = MEM_HEADS * HEAD_DIM
    row = lambda v: v.reshape(1, -1).astype(F32)

    bd64 = _block_diag_ones(HEAD_DIM)
    bd128 = _block_diag_ones(AUG_W)

    w_in_b = w_in.astype(BF16)
    o = 2 * n_lru
    w_lru = w_in_b[:, :o]
    w_k = _pad_heads(w_in_b[:, o + wq:o + 2 * wq], MOBA_HEADS)
    w_qvc = jnp.concatenate([w_in_b[:, o:o + wq], w_in_b[:, o + 2 * wq:]], axis=1)
    gq = row(jnp.tile(moba_q_norm, MOBA_HEADS))
    gk = row(_pad_heads(jnp.tile(moba_k_norm, MOBA_HEADS), MOBA_HEADS))
    gc = row(jnp.tile(mem_q_norm, MEM_HEADS))
    gck = row(_pad_heads(jnp.tile(mem_k_norm, MEM_HEADS), MEM_HEADS))
    w_kv_b = mem_w_kv.astype(BF16)
    w_ck = _pad_heads(w_kv_b[:, :wc], MEM_HEADS)
    w_cv = w_kv_b[:, wc:]
    slopes = 2.0 ** (-8.0 * jnp.arange(1, MOBA_HEADS + 1, dtype=F32) / MOBA_HEADS)
    lane = jnp.arange(AUG_W)
    is_alibi = (lane >= AUG_ALIBI) & (lane < AUG_ALIBI + 3)
    alibi = row(jnp.where(is_alibi[None, :], slopes[:, None] * LOG2E, 0.0))

    x2d = x.reshape(n, d)
    kc, vct = _mem_kv(mem.reshape(-1, d), row(mem_norm), w_ck, w_cv, bd128, gck, batch)

    x1 = _ffn(x2d, row(ffn1_norm), ffn1_w_in.astype(BF16), ffn1_w_out.astype(BF16))

    lru_x, lru_g, qt, kaug, vt, kmean, cqt = _in_proj(
        x1, row(mix_norm), w_lru, w_qvc, w_k, gq, gk, gc, bd64, bd128, alibi, seq)
    kmean = kmean[:, :TOKEN_TILE // MOBA_BLOCK, :].reshape(batch, seq // MOBA_BLOCK, -1)

    y_lru = _lru(lru_x, lru_g, lru_conv_w.astype(F32), row(lru_conv_b),
                 _pack_block_diag(lru_a_w).astype(BF16), row(lru_a_b),
                 _pack_block_diag(lru_x_w).astype(BF16), row(lru_x_b), row(lru_lambda), batch, seq)
    y_moba, y_mem = _moba(qt, cqt, kaug, vt, kmean, kc, vct, batch, seq)

    out = _out_ffn(x1, y_lru, y_moba, y_mem, w_out.astype(BF16), row(ffn2_norm),
                   ffn2_w_in.astype(BF16), ffn2_w_out.astype(BF16))
    return out.reshape(batch, seq, d)


def kernel(x, mem, ffn1_norm, ffn1_w_in, ffn1_w_out, mix_norm, mem_norm, w_in, lru_conv_w, lru_conv_b,
           lru_a_w, lru_a_b, lru_x_w, lru_x_b, lru_lambda, moba_q_norm, moba_k_norm, mem_w_kv,
           mem_q_norm, mem_k_norm, w_out, ffn2_norm, ffn2_w_in, ffn2_w_out):
    params = (ffn1_norm, ffn1_w_in, ffn1_w_out, mix_norm, mem_norm, w_in, lru_conv_w, lru_conv_b,
              lru_a_w, lru_a_b, lru_x_w, lru_x_b, lru_lambda, moba_q_norm, moba_k_norm, mem_w_kv,
              mem_q_norm, mem_k_norm, w_out, ffn2_norm, ffn2_w_in, ffn2_w_out)
    for layer in range(ffn1_norm.shape[0]):
        x = _layer(x, mem, *(p[layer] for p in params))
    return x
```

```python
import functools
import math

import jax
import jax.numpy as jnp
from jax import lax
from jax.experimental import pallas as pl
from jax.experimental.pallas import tpu as pltpu

F32 = jnp.float32
BF16 = jnp.bfloat16

HEAD_DIM = 64
CONV_W = 4
LRU_C = 8.0
MOBA_HEADS = 4
MOBA_BLOCK = 256
MOBA_TOPK = 3
MEM_HEADS = 4
NORM_EPS = 1e-6
LOG2E = 1.4426950408889634
NEG_BIG = -1e30

V7X_MXU_DIM = 256
LANES = 128
SUBLANES = 8
V7X_VMEM_BYTES = 64 * 1024 * 1024

AUG_W = 128
AUG_ALIBI = 64
AUG_MASK = 80
VT_ROWS = HEAD_DIM + 16
SCORE_LOOKAHEAD = 5
PAST_UNROLL = 4
SCORE_SLOTS = 8
assert MOBA_HEADS <= SCORE_LOOKAHEAD < SCORE_SLOTS and (PAST_UNROLL * MOBA_HEADS) % SCORE_SLOTS == 0
assert (MOBA_HEADS + MEM_HEADS) % SCORE_SLOTS == 0
PROB_SLOTS = 4
PROB_LOOKAHEAD = 1
assert PROB_LOOKAHEAD < SCORE_LOOKAHEAD

TOKEN_TILE = 512
LRU_TILE = 512
FFN_CHUNK = 256


def _vmem_limit(resident_bytes):
    return int(min(V7X_VMEM_BYTES - 8 * 1024 * 1024, resident_bytes + 24 * 1024 * 1024))


def _const_spec(shape):
    n = len(shape)
    return pl.BlockSpec(shape, lambda *_: (0,) * n, pipeline_mode=pl.Buffered(1))


def _rms_norm(x, g):
    ms = jnp.mean(x * x, axis=-1, keepdims=True)
    return x * lax.rsqrt(ms + NORM_EPS) * g


def _group_rms_norm(u, bd, g, group):
    sq = u * u
    hi = sq.astype(BF16)
    lo = (sq - hi.astype(F32)).astype(BF16)
    parts = []
    for c in range(u.shape[1] // V7X_MXU_DIM):
        sl = slice(c * V7X_MXU_DIM, (c + 1) * V7X_MXU_DIM)
        parts.append(jnp.dot(hi[:, sl], bd, preferred_element_type=F32)
                     + jnp.dot(lo[:, sl], bd, preferred_element_type=F32))
    ss = parts[0] if len(parts) == 1 else jnp.concatenate(parts, axis=1)
    return u * lax.rsqrt(ss * (1.0 / group) + NORM_EPS) * g


def _swiglu_half_step(x, g_ref, w_in_ref, w_out_ref, act_ref):
    d_ff = w_out_ref.shape[0]
    xn = _rms_norm(x, g_ref[...]).astype(BF16)
    for c in range(d_ff // FFN_CHUNK):
        lo = c * FFN_CHUNK
        a = jnp.dot(xn, w_in_ref[:, lo:lo + FFN_CHUNK], preferred_element_type=F32)
        b = jnp.dot(xn, w_in_ref[:, d_ff + lo:d_ff + lo + FFN_CHUNK], preferred_element_type=F32)
        act_ref[:, lo:lo + FFN_CHUNK] = (a * jax.nn.sigmoid(a) * b).astype(BF16)
    y = jnp.dot(act_ref[...], w_out_ref[...], preferred_element_type=F32)
    return x + 0.5 * y


def _ffn_kernel(x_ref, g_ref, w_in_ref, w_out_ref, o_ref, act_ref):
    o_ref[...] = _swiglu_half_step(x_ref[...], g_ref, w_in_ref, w_out_ref, act_ref)


def _ffn(x, g, w_in, w_out):
    n, d = x.shape
    d_ff = w_out.shape[0]
    tm = TOKEN_TILE
    resident = (w_in.size + w_out.size) * 2
    return pl.pallas_call(
        _ffn_kernel,
        grid=(n // tm,),
        in_specs=[pl.BlockSpec((tm, d), lambda i: (i, 0)),
                  _const_spec((1, d)), _const_spec(w_in.shape), _const_spec(w_out.shape)],
        out_specs=pl.BlockSpec((tm, d), lambda i: (i, 0)),
        out_shape=jax.ShapeDtypeStruct((n, d), F32),
        scratch_shapes=[pltpu.VMEM((tm, d_ff), BF16)],
        compiler_params=pltpu.CompilerParams(dimension_semantics=("arbitrary",),
                                             vmem_limit_bytes=_vmem_limit(resident)),
        name="ffn1",
    )(x, g, w_in, w_out)


def _out_ffn_kernel(x_ref, ylru_ref, ymoba_ref, ymem_ref, wo_ref, g_ref, w_in_ref, w_out_ref,
                    o_ref, act_ref):
    n_lru = ylru_ref.shape[1]
    n_moba = ymoba_ref.shape[1]
    x2 = (x_ref[...]
          + jnp.dot(ylru_ref[...], wo_ref[0:n_lru, :], preferred_element_type=F32)
          + jnp.dot(ymoba_ref[...], wo_ref[n_lru:n_lru + n_moba, :], preferred_element_type=F32)
          + jnp.dot(ymem_ref[...], wo_ref[n_lru + n_moba:, :], preferred_element_type=F32))
    o_ref[...] = _swiglu_half_step(x2, g_ref, w_in_ref, w_out_ref, act_ref)


def _out_ffn(x1, y_lru, y_moba, y_mem, w_o, g, w_in, w_out):
    n, d = x1.shape
    d_ff = w_out.shape[0]
    tm = TOKEN_TILE
    resident = (w_in.size + w_out.size + w_o.size) * 2

    def tile(w):
        return pl.BlockSpec((tm, w), lambda i: (i, 0))

    return pl.pallas_call(
        _out_ffn_kernel,
        grid=(n // tm,),
        in_specs=[tile(d), tile(y_lru.shape[1]), tile(y_moba.shape[1]), tile(y_mem.shape[1]),
                  _const_spec(w_o.shape), _const_spec((1, d)),
                  _const_spec(w_in.shape), _const_spec(w_out.shape)],
        out_specs=tile(d),
        out_shape=jax.ShapeDtypeStruct((n, d), F32),
        scratch_shapes=[pltpu.VMEM((tm, d_ff), BF16)],
        compiler_params=pltpu.CompilerParams(dimension_semantics=("arbitrary",),
                                             vmem_limit_bytes=_vmem_limit(resident)),
        name="out_ffn2",
    )(x1, y_lru, y_moba, y_mem, w_o, g, w_in, w_out)


def _mem_kv_kernel(mem_ref, g_ref, wk_ref, wv_ref, bd_ref, gk_ref, kc_ref, vct_ref):
    batch, _, mlen = vct_ref.shape
    mn = _rms_norm(mem_ref[...], g_ref[...]).astype(BF16)
    k = jnp.dot(mn, wk_ref[...], preferred_element_type=F32)
    kc_ref[...] = _group_rms_norm(k, bd_ref[...], gk_ref[...], HEAD_DIM).astype(BF16)
    v = jnp.dot(mn, wv_ref[...], preferred_element_type=F32)
    ones = jnp.ones((VT_ROWS - HEAD_DIM, mlen), BF16)
    for b in range(batch):
        vt = v[b * mlen:(b + 1) * mlen, :].T.astype(BF16)
        for h in range(MEM_HEADS):
            vct_ref[b, h * VT_ROWS:h * VT_ROWS + HEAD_DIM, :] = vt[h * HEAD_DIM:(h + 1) * HEAD_DIM, :]
            vct_ref[b, h * VT_ROWS + HEAD_DIM:(h + 1) * VT_ROWS, :] = ones


def _mem_kv(mem2d, g, w_k, w_v, bd128, gk, batch):
    m = mem2d.shape[0]
    return pl.pallas_call(
        _mem_kv_kernel,
        out_shape=(jax.ShapeDtypeStruct((m, w_k.shape[1]), BF16),
                   jax.ShapeDtypeStruct((batch, MEM_HEADS * VT_ROWS, m // batch), BF16)),
        name="mem_kv",
    )(mem2d, g, w_k, w_v, bd128, gk)


def _in_proj_kernel(x_ref, g_ref, w_lru_ref, w_qvc_ref, w_k_ref,
                    gq_ref, gk_ref, gc_ref, bd64_ref, bd128_ref, alibi_ref,
                    lrux_ref, lrug_ref, qt_ref, kaug_ref, vt_ref, kmean_ref, cqt_ref, *, seq):
    tm = x_ref.shape[0]
    n_lru = lrux_ref.shape[1]
    xn = _rms_norm(x_ref[...], g_ref[...]).astype(BF16)

    lrux_ref[...] = jnp.dot(xn, w_lru_ref[:, :n_lru], preferred_element_type=F32)
    lrug_ref[...] = jnp.dot(xn, w_lru_ref[:, n_lru:], preferred_element_type=F32)

    wq = qt_ref.shape[1]
    qvc = jnp.dot(xn, w_qvc_ref[...], preferred_element_type=F32)

    cq = _group_rms_norm(qvc[:, 2 * wq:], bd64_ref[...], gc_ref[...], HEAD_DIM) * (HEAD_DIM ** -0.5 * LOG2E)
    for r in range(tm // MOBA_BLOCK):
        cqt_ref[r] = cq[r * MOBA_BLOCK:(r + 1) * MOBA_BLOCK, :].T.astype(BF16)

    q = _group_rms_norm(qvc[:, :wq], bd64_ref[...], gq_ref[...], HEAD_DIM)
    v = qvc[:, wq:2 * wq]
    ones = jnp.ones((VT_ROWS - HEAD_DIM, MOBA_BLOCK), BF16)
    for r in range(tm // MOBA_BLOCK):
        rows = slice(r * MOBA_BLOCK, (r + 1) * MOBA_BLOCK)
        qt_ref[r] = q[rows, :].T
        vt = v[rows, :].T.astype(BF16)
        for h in range(MOBA_HEADS):
            vt_ref[r, h * VT_ROWS:h * VT_ROWS + HEAD_DIM, :] = vt[h * HEAD_DIM:(h + 1) * HEAD_DIM, :]
            vt_ref[r, h * VT_ROWS + HEAD_DIM:(h + 1) * VT_ROWS, :] = ones

    k = jnp.dot(xn, w_k_ref[...], preferred_element_type=F32)
    k = _group_rms_norm(k, bd128_ref[...], gk_ref[...], HEAD_DIM)
    kmean_ref[...] = jnp.zeros(kmean_ref.shape, F32)
    for r in range(tm // MOBA_BLOCK):
        rows = slice(r * MOBA_BLOCK, (r + 1) * MOBA_BLOCK)
        kmean_ref[0, r:r + 1, :] = jnp.sum(k[rows, :], axis=0, keepdims=True) * (1.0 / MOBA_BLOCK)

    shape = k.shape
    row = lax.broadcasted_iota(jnp.int32, shape, 0)
    lane = lax.broadcasted_iota(jnp.int32, shape, 1) % AUG_W
    t = (pl.program_id(0) * tm + row) % seq
    bias = t.astype(F32) * alibi_ref[...]
    b_hi = bias.astype(BF16).astype(F32)
    b_mid = (bias - b_hi).astype(BF16).astype(F32)
    b_lo = bias - b_hi - b_mid
    pieces = jnp.where(lane == AUG_ALIBI, b_hi, jnp.where(lane == AUG_ALIBI + 1, b_mid, b_lo))
    onehot = jnp.where(lane - AUG_MASK == t // MOBA_BLOCK, 1.0, 0.0)
    is_alibi = (lane >= AUG_ALIBI) & (lane < AUG_ALIBI + 3)
    kaug_ref[...] = (k + jnp.where(is_alibi, pieces, onehot)).astype(BF16)


def _in_proj(x1, g, w_lru, w_qvc, w_k, gq, gk, gc, bd64, bd128, alibi, seq):
    n, d = x1.shape
    tm = TOKEN_TILE
    rb = tm // MOBA_BLOCK
    nblk = n // MOBA_BLOCK
    n_lru = w_lru.shape[1] // 2
    wq = w_qvc.shape[1] // 3
    wk = w_k.shape[1]
    consts = (g, w_lru, w_qvc, w_k, gq, gk, gc, bd64, bd128, alibi)
    resident = sum(c.size * c.dtype.itemsize for c in consts)
    return pl.pallas_call(
        functools.partial(_in_proj_kernel, seq=seq),
        grid=(n // tm,),
        in_specs=[pl.BlockSpec((tm, d), lambda i: (i, 0))] + [_const_spec(c.shape) for c in consts],
        out_specs=[pl.BlockSpec((tm, n_lru), lambda i: (i, 0)),
                   pl.BlockSpec((tm, n_lru), lambda i: (i, 0)),
                   pl.BlockSpec((rb, wq, MOBA_BLOCK), lambda i: (i, 0, 0)),
                   pl.BlockSpec((tm, wk), lambda i: (i, 0)),
                   pl.BlockSpec((rb, MOBA_HEADS * VT_ROWS, MOBA_BLOCK), lambda i: (i, 0, 0)),
                   pl.BlockSpec((1, 8, wk), lambda i: (i, 0, 0)),
                   pl.BlockSpec((rb, wq, MOBA_BLOCK), lambda i: (i, 0, 0))],
        out_shape=(jax.ShapeDtypeStruct((n, n_lru), F32),
                   jax.ShapeDtypeStruct((n, n_lru), F32),
                   jax.ShapeDtypeStruct((nblk, wq, MOBA_BLOCK), F32),
                   jax.ShapeDtypeStruct((n, wk), BF16),
                   jax.ShapeDtypeStruct((nblk, MOBA_HEADS * VT_ROWS, MOBA_BLOCK), BF16),
                   jax.ShapeDtypeStruct((n // tm, 8, wk), F32),
                   jax.ShapeDtypeStruct((nblk, wq, MOBA_BLOCK), BF16)),
        compiler_params=pltpu.CompilerParams(dimension_semantics=("arbitrary",),
                                             vmem_limit_bytes=_vmem_limit(resident)),
        name="in_proj",
    )(x1, *consts)


def _gelu_tanh(x):
    c = math.sqrt(2.0 / math.pi)
    return 0.5 * x * (1.0 + jnp.tanh(c * (x + 0.044715 * (x * x * x))))


def _lru_kernel(x_ref, g_ref, cw_ref, cb_ref, wa_ref, ba_ref, wx_ref, bx_ref, lam_ref,
                o_ref, xpad_ref, a_ref, b_ref, hl_ref, carry_ref):
    tl, c = x_ref.shape
    half = V7X_MXU_DIM
    ti = pl.program_id(1)

    @pl.when(ti == 0)
    def _():
        xpad_ref[0:8, :] = jnp.zeros((8, c), F32)
        carry_ref[...] = jnp.zeros(carry_ref.shape, F32)

    xpad_ref[8:8 + tl, :] = x_ref[...]
    xb = jnp.broadcast_to(cb_ref[...], (tl, c))
    for j in range(CONV_W):
        off = 8 - (CONV_W - 1) + j
        xb = xb + cw_ref[j:j + 1, :] * xpad_ref[off:off + tl, :]
    xpad_ref[0:8, :] = x_ref[tl - 8:tl, :]

    xbb = xb.astype(BF16)

    def gate(w_ref, bias_ref):
        z = jnp.concatenate(
            [jnp.dot(xbb[:, :half], w_ref[0], preferred_element_type=F32),
             jnp.dot(xbb[:, half:], w_ref[1], preferred_element_type=F32)], axis=1)
        return jax.nn.sigmoid(z + bias_ref[...])

    r = gate(wa_ref, ba_ref)
    i = gate(wx_ref, bx_ref)
    neg_lam = -lam_ref[...]
    softplus = jnp.maximum(neg_lam, 0.0) + jnp.log1p(jnp.exp(-jnp.abs(neg_lam)))
    log_a = (-LRU_C) * r * softplus
    a = jnp.exp(log_a)
    mult = jnp.sqrt(1.0 - a * a)
    row = lax.broadcasted_iota(jnp.int32, (tl, c), 0)
    start_row = jnp.where(ti == 0, 0, -1)
    mult = jnp.where(row == start_row, 1.0, mult)
    bt = mult * i * xb

    groups = tl // SUBLANES
    sub = lax.broadcasted_iota(jnp.int32, (groups, SUBLANES, LANES), 1)
    for k in range(c // LANES):
        lanes = slice(k * LANES, (k + 1) * LANES)
        a3 = a[:, lanes].reshape(groups, SUBLANES, LANES)
        b3 = bt[:, lanes].reshape(groups, SUBLANES, LANES)
        d = 1
        while d < SUBLANES:
            has_prev = sub >= d
            a_prev = pltpu.roll(a3, d, axis=1)
            b_prev = pltpu.roll(b3, d, axis=1)
            b3 = b3 + jnp.where(has_prev, a3, 0.0) * b_prev
            a3 = jnp.where(has_prev, a3 * a_prev, a3)
            d *= 2
        a_ref[:, lanes] = a3.reshape(tl, LANES)
        b_ref[:, lanes] = b3.reshape(tl, LANES)

    h = carry_ref[...]
    for grp in range(groups):
        rows = slice(grp * SUBLANES, (grp + 1) * SUBLANES)
        h_last = jnp.broadcast_to(h[SUBLANES - 1:SUBLANES, :], (SUBLANES, c))
        h = a_ref[rows, :] * h_last + b_ref[rows, :]
        hl_ref[rows, :] = h
    carry_ref[...] = h

    o_ref[...] = (hl_ref[...] * _gelu_tanh(g_ref[...])).astype(o_ref.dtype)


def _lru(lru_x, lru_g, cw, cb, wa, ba, wx, bx, lam, batch, seq):
    n, c = lru_x.shape
    tl = LRU_TILE
    nt = seq // tl
    consts = (cw, cb, wa, ba, wx, bx, lam)
    tile = pl.BlockSpec((tl, c), lambda b, t: (b * nt + t, 0))
    return pl.pallas_call(
        _lru_kernel,
        grid=(batch, nt),
        in_specs=[tile, tile] + [_const_spec(w.shape) for w in consts],
        out_specs=tile,
        out_shape=jax.ShapeDtypeStruct((n, c), BF16),
        scratch_shapes=[pltpu.VMEM((tl + SUBLANES, c), F32)]
        + [pltpu.VMEM((tl, c), F32)] * 3
        + [pltpu.VMEM((SUBLANES, c), F32)],
        compiler_params=pltpu.CompilerParams(dimension_semantics=("arbitrary", "arbitrary")),
        name="lru",
    )(lru_x, lru_g, *consts)


def _moba_kernel(qt_ref, cqt_ref, kaug_ref, vt_ref, kmean_ref, kc_ref, vct_ref, o_ref, om_ref,
                 qaug_ref, qown_ref, qmem_ref, s_ref, p_ref, alpha_ref, m_ref, acc_ref, accm_ref,
                 ot_ref, otm_ref):
    qi = pl.program_id(1)
    nblk = kmean_ref.shape[1]
    blk = MOBA_BLOCK

    def build_query_operands(h):
        qt = qt_ref[0, 0, h * HEAD_DIM:(h + 1) * HEAD_DIM, :]
        qs = (qt * (HEAD_DIM ** -0.5 * LOG2E)).astype(BF16)
        sub = lax.broadcasted_iota(jnp.int32, (AUG_MASK - AUG_ALIBI, blk), 0)
        ones = jnp.where(sub < 3, 1.0, 0.0).astype(BF16)
        for ref in (qaug_ref, qown_ref):
            ref[h, 0:HEAD_DIM, :] = qs
            ref[h, AUG_ALIBI:AUG_MASK, :] = ones
        qown_ref[h, AUG_MASK:, :] = jnp.zeros((AUG_W - AUG_MASK, blk), BF16)
        qmem_ref[h, 0:HEAD_DIM, :] = cqt_ref[0, 0, h * HEAD_DIM:(h + 1) * HEAD_DIM, :]
        qmem_ref[h, HEAD_DIM:, :] = jnp.zeros((AUG_W - HEAD_DIM, blk), BF16)

    def build_block_mask(h):
        blk_id = lax.broadcasted_iota(jnp.int32, (nblk, blk), 0).astype(F32)
        qt = qt_ref[0, 0, h * HEAD_DIM:(h + 1) * HEAD_DIM, :]
        gate = jnp.dot(kmean_ref[0, :, h * AUG_W:h * AUG_W + HEAD_DIM], qt,
                       preferred_element_type=F32, precision=lax.Precision.HIGHEST)
        gate = jnp.where(blk_id < qi.astype(F32), gate, -jnp.inf)
        keep = blk_id < 0.0
        for _ in range(MOBA_TOPK):
            best = jnp.max(gate, axis=0, keepdims=True)
            first = jnp.min(jnp.where(gate == best, blk_id, float(nblk)), axis=0, keepdims=True)
            pick = (blk_id == first) & (best > -jnp.inf)
            keep = keep | pick
            gate = jnp.where(pick, -jnp.inf, gate)
        qaug_ref[h, AUG_MASK:AUG_MASK + nblk, :] = jnp.where(keep, 0.0, NEG_BIG).astype(BF16)
        qaug_ref[h, AUG_MASK + nblk:, :] = jnp.zeros((AUG_W - AUG_MASK - nblk, blk), BF16)

    kk = lax.broadcasted_iota(jnp.int32, (blk, blk), 0)
    qq = lax.broadcasted_iota(jnp.int32, (blk, blk), 1)
    causal = kk <= qq

    per_trip = PAST_UNROLL * MOBA_HEADS
    lead = MOBA_HEADS + MEM_HEADS

    def past_block(trip, u):
        return jnp.minimum(trip * PAST_UNROLL + u, nblk - 1)

    def stream_item(trip, n):
        if n < 0:
            idx = n + lead
            if idx < MOBA_HEADS:
                return "own", qi, idx, idx % SCORE_SLOTS
            return "mem", 0, idx - MOBA_HEADS, idx % SCORE_SLOTS
        if n >= per_trip:
            trip, n = trip + 1, n - per_trip
        return "past", past_block(trip, n // MOBA_HEADS), n % MOBA_HEADS, (n + lead) % SCORE_SLOTS

    def scores(kind, j, h, slot):
        lanes = slice(h * AUG_W, (h + 1) * AUG_W)
        if kind == "mem":
            keys, q_ref = kc_ref[0, :, lanes], qmem_ref
        else:
            keys, q_ref = kaug_ref[0, j, :, lanes], (qown_ref if kind == "own" else qaug_ref)
        s_ref[slot] = jnp.dot(keys, q_ref[h], preferred_element_type=F32)

    def probabilities(kind, j, h, slot):
        s = s_ref[slot]
        if kind == "past":
            m_old = m_ref[h]
            m_new = jnp.maximum(m_old, jnp.max(s, axis=0, keepdims=True))
            alpha_ref[slot % PROB_SLOTS] = jnp.exp2(m_old - m_new)
        else:
            if kind == "own":
                s = jnp.where(causal, s, NEG_BIG)
            m_new = jnp.max(s, axis=0, keepdims=True)
        p_ref[slot % PROB_SLOTS] = jnp.exp2(s - m_new).astype(BF16)
        if kind != "mem":
            m_ref[h] = m_new

    def accumulate(kind, j, h, slot):
        rows = slice(h * VT_ROWS, (h + 1) * VT_ROWS)
        p = p_ref[slot % PROB_SLOTS]
        if kind == "mem":
            accm_ref[h] = jnp.dot(vct_ref[0, rows, :], p, preferred_element_type=F32)
        else:
            pv = jnp.dot(vt_ref[0, j, rows, :], p, preferred_element_type=F32)
            acc_ref[h] = pv if kind == "own" else alpha_ref[slot % PROB_SLOTS] * acc_ref[h] + pv

    for h in range(MOBA_HEADS):
        build_query_operands(h)
        build_block_mask(h)
    for n in range(-lead, -lead + SCORE_LOOKAHEAD):
        scores(*stream_item(0, n))
    for n in range(-lead, -lead + PROB_LOOKAHEAD):
        probabilities(*stream_item(0, n))
    for n in range(-lead, 0):
        scores(*stream_item(0, n + SCORE_LOOKAHEAD))
        probabilities(*stream_item(0, n + PROB_LOOKAHEAD))
        accumulate(*stream_item(0, n))

    def past_trip(trip, carry):
        for n in range(per_trip):
            scores(*stream_item(trip, n + SCORE_LOOKAHEAD))
            probabilities(*stream_item(trip, n + PROB_LOOKAHEAD))
            accumulate(*stream_item(trip, n))
        return carry

    lax.fori_loop(0, (qi + PAST_UNROLL - 1) // PAST_UNROLL, past_trip, 0)

    for ref, t_ref, out_ref in ((acc_ref, ot_ref, o_ref), (accm_ref, otm_ref, om_ref)):
        for h in range(MOBA_HEADS):
            t_ref[h * HEAD_DIM:(h + 1) * HEAD_DIM, :] = (ref[h, 0:HEAD_DIM, :]
                                                         / ref[h, HEAD_DIM:HEAD_DIM + 1, :])
        out_ref[...] = t_ref[...].T.astype(out_ref.dtype)


def _moba(qt, cqt, kaug, vt, kmean, kc, vct, batch, seq):
    nblk = seq // MOBA_BLOCK
    width = MOBA_HEADS * HEAD_DIM
    aug = MOBA_HEADS * AUG_W
    vt_rows = MOBA_HEADS * VT_ROWS
    mlen = vct.shape[2]
    assert mlen == MOBA_BLOCK and MEM_HEADS == MOBA_HEADS, "memory items reuse the key-block buffers"
    qt = qt.reshape(batch, nblk, width, MOBA_BLOCK)
    cqt = cqt.reshape(batch, nblk, width, MOBA_BLOCK)
    kaug = kaug.reshape(batch, nblk, MOBA_BLOCK, aug)
    vt = vt.reshape(batch, nblk, vt_rows, MOBA_BLOCK)
    kc = kc.reshape(batch, mlen, aug)
    resident = 2 * (kaug.size // batch + vt.size // batch) * 2
    qblock = pl.BlockSpec((1, 1, width, MOBA_BLOCK), lambda b, i: (b, i, 0, 0))
    out_block = pl.BlockSpec((MOBA_BLOCK, width), lambda b, i: (b * nblk + i, 0))
    out_shape = jax.ShapeDtypeStruct((batch * seq, width), BF16)
    return pl.pallas_call(
        _moba_kernel,
        grid=(batch, nblk),
        in_specs=[qblock, qblock,
                  pl.BlockSpec((1, nblk, MOBA_BLOCK, aug), lambda b, i: (b, 0, 0, 0)),
                  pl.BlockSpec((1, nblk, vt_rows, MOBA_BLOCK), lambda b, i: (b, 0, 0, 0)),
                  pl.BlockSpec((1, nblk, aug), lambda b, i: (b, 0, 0)),
                  pl.BlockSpec((1, mlen, aug), lambda b, i: (b, 0, 0)),
                  pl.BlockSpec((1, vt_rows, mlen), lambda b, i: (b, 0, 0))],
        out_specs=(out_block, out_block),
        out_shape=(out_shape, out_shape),
        scratch_shapes=[pltpu.VMEM((MOBA_HEADS, AUG_W, MOBA_BLOCK), BF16),
                        pltpu.VMEM((MOBA_HEADS, AUG_W, MOBA_BLOCK), BF16),
                        pltpu.VMEM((MOBA_HEADS, AUG_W, MOBA_BLOCK), BF16),
                        pltpu.VMEM((SCORE_SLOTS, MOBA_BLOCK, MOBA_BLOCK), F32),
                        pltpu.VMEM((PROB_SLOTS, MOBA_BLOCK, MOBA_BLOCK), BF16),
                        pltpu.VMEM((PROB_SLOTS, 1, MOBA_BLOCK), F32),
                        pltpu.VMEM((MOBA_HEADS, 1, MOBA_BLOCK), F32),
                        pltpu.VMEM((MOBA_HEADS, VT_ROWS, MOBA_BLOCK), F32),
                        pltpu.VMEM((MOBA_HEADS, VT_ROWS, MOBA_BLOCK), F32),
                        pltpu.VMEM((width, MOBA_BLOCK), F32),
                        pltpu.VMEM((width, MOBA_BLOCK), F32)],
        compiler_params=pltpu.CompilerParams(dimension_semantics=("arbitrary", "arbitrary"),
                                             vmem_limit_bytes=_vmem_limit(resident)),
        name="moba",
    )(qt, cqt, kaug, vt, kmean, kc, vct)


def _block_diag_ones(group):
    idx = jnp.arange(V7X_MXU_DIM) // group
    return (idx[:, None] == idx[None, :]).astype(BF16)


def _pack_block_diag(w):
    nb, bw, _ = w.shape
    per = V7X_MXU_DIM // bw
    w = w.reshape(nb // per, per, bw, bw)
    eye = jnp.eye(per, dtype=w.dtype)
    return jnp.einsum("gpij,pq->gpiqj", w, eye).reshape(nb // per, V7X_MXU_DIM, V7X_MXU_DIM)


def _pad_heads(w, heads):
    lead = w.shape[:-1]
    w = w.reshape(lead + (heads, HEAD_DIM))
    w = jnp.pad(w, [(0, 0)] * len(lead) + [(0, 0), (0, AUG_W - HEAD_DIM)])
    return w.reshape(lead + (heads * AUG_W,))


def _layer(x, mem, ffn1_norm, ffn1_w_in, ffn1_w_out, mix_norm, mem_norm, w_in,
           lru_conv_w, lru_conv_b, lru_a_w, lru_a_b, lru_x_w, lru_x_b, lru_lambda,
           moba_q_norm, moba_k_norm, mem_w_kv, mem_q_norm, mem_k_norm, w_out,
           ffn2_norm, ffn2_w_in, ffn2_w_out):
    batch, seq, d = x.shape
    n = batch * seq
    n_lru = lru_lambda.shape[0]
    wq = MOBA_HEADS * HEAD_DIM
    wc = MEM_HEADS * HEAD_DIM
    row = lambda v: v.reshape(1, -1).astype(F32)

    bd64 = _block_diag_ones(HEAD_DIM)
    bd128 = _block_diag_ones(AUG_W)

    w_in_b = w_in.astype(BF16)
    o = 2 * n_lru
    w_lru = w_in_b[:, :o]
    w_k = _pad_heads(w_in_b[:, o + wq:o + 2 * wq], MOBA_HEADS)
    w_qvc = jnp.concatenate([w_in_b[:, o:o + wq], w_in_b[:, o + 2 * wq:]], axis=1)
    gq = row(jnp.tile(moba_q_norm, MOBA_HEADS))
    gk = row(_pad_heads(jnp.tile(moba_k_norm, MOBA_HEADS), MOBA_HEADS))
    gc = row(jnp.tile(mem_q_norm, MEM_HEADS))
    gck = row(_pad_heads(jnp.tile(mem_k_norm, MEM_HEADS), MEM_HEADS))
    w_kv_b = mem_w_kv.astype(BF16)
    w_ck = _pad_heads(w_kv_b[:, :wc], MEM_HEADS)
    w_cv = w_kv_b[:, wc:]
    slopes = 2.0 ** (-8.0 * jnp.arange(1, MOBA_HEADS + 1, dtype=F32) / MOBA_HEADS)
    lane = jnp.arange(AUG_W)
    is_alibi = (lane >= AUG_ALIBI) & (lane < AUG_ALIBI + 3)
    alibi = row(jnp.where(is_alibi[None, :], slopes[:, None] * LOG2E, 0.0))

    x2d = x.reshape(n, d)
    kc, vct = _mem_kv(mem.reshape(-1, d), row(mem_norm), w_ck, w_cv, bd128, gck, batch)

    x1 = _ffn(x2d, row(ffn1_norm), ffn1_w_in.astype(BF16), ffn1_w_out.astype(BF16))

    lru_x, lru_g, qt, kaug, vt, kmean, cqt = _in_proj(
        x1, row(mix_norm), w_lru, w_qvc, w_k, gq, gk, gc, bd64, bd128, alibi, seq)
    kmean = kmean[:, :TOKEN_TILE // MOBA_BLOCK, :].reshape(batch, seq // MOBA_BLOCK, -1)

    y_lru = _lru(lru_x, lru_g, lru_conv_w.astype(F32), row(lru_conv_b),
                 _pack_block_diag(lru_a_w).astype(BF16), row(lru_a_b),
                 _pack_block_diag(lru_x_w).astype(BF16), row(lru_x_b), row(lru_lambda), batch, seq)
    y_moba, y_mem = _moba(qt, cqt, kaug, vt, kmean, kc, vct, batch, seq)

    out = _out_ffn(x1, y_lru, y_moba, y_mem, w_out.astype(BF16), row(ffn2_norm),
                   ffn2_w_in.astype(BF16), ffn2_w_out.astype(BF16))
    return out.reshape(batch, seq, d)


def kernel(x, mem, ffn1_norm, ffn1_w_in, ffn1_w_out, mix_norm, mem_norm, w_in, lru_conv_w, lru_conv_b,
           lru_a_w, lru_a_b, lru_x_w, lru_x_b, lru_lambda, moba_q_norm, moba_k_norm, mem_w_kv,
           mem_q_norm, mem_k_norm, w_out, ffn2_norm, ffn2_w_in, ffn2_w_out):
    params = (ffn1_norm, ffn1_w_in, ffn1_w_out, mix_norm, mem_norm, w_in, lru_conv_w, lru_conv_b,
              lru_a_w, lru_a_b, lru_x_w, lru_x_b, lru_lambda, moba_q_norm, moba_k_norm, mem_w_kv,
              mem_q_norm, mem_k_norm, w_out, ffn2_norm, ffn2_w_in, ffn2_w_out)
    for layer in range(ffn1_norm.shape[0]):
        x = _layer(x, mem, *(p[layer] for p in params))
    return x
```

```python
import functools
import math

import jax
import jax.numpy as jnp
from jax import lax
from jax.experimental import pallas as pl
from jax.experimental.pallas import tpu as pltpu

F32 = jnp.float32
BF16 = jnp.bfloat16

HEAD_DIM = 64
CONV_W = 4
LRU_C = 8.0
MOBA_HEADS = 4
MOBA_BLOCK = 256
MOBA_TOPK = 3
MEM_HEADS = 4
NORM_EPS = 1e-6
LOG2E = 1.4426950408889634
NEG_BIG = -1e30

V7X_MXU_DIM = 256
LANES = 128
SUBLANES = 8
V7X_VMEM_BYTES = 64 * 1024 * 1024

AUG_W = 128
AUG_ALIBI = 64
AUG_MASK = 80
VT_ROWS = HEAD_DIM + 16
SCORE_LOOKAHEAD = 5
PAST_UNROLL = 4
SCORE_SLOTS = 8
assert MOBA_HEADS <= SCORE_LOOKAHEAD < SCORE_SLOTS and (PAST_UNROLL * MOBA_HEADS) % SCORE_SLOTS == 0
assert (MOBA_HEADS + MEM_HEADS) % SCORE_SLOTS == 0
PROB_SLOTS = 4
PROB_LOOKAHEAD = 1
assert PROB_LOOKAHEAD < SCORE_LOOKAHEAD

TOKEN_TILE = 512
LRU_TILE = 512
FFN_CHUNK = 256


def _vmem_limit(resident_bytes):
    return int(min(V7X_VMEM_BYTES - 8 * 1024 * 1024, resident_bytes + 24 * 1024 * 1024))


def _const_spec(shape):
    n = len(shape)
    return pl.BlockSpec(shape, lambda *_: (0,) * n, pipeline_mode=pl.Buffered(1))


def _rms_norm(x, g):
    ms = jnp.mean(x * x, axis=-1, keepdims=True)
    return x * lax.rsqrt(ms + NORM_EPS) * g


def _group_rms_norm(u, bd, g, group):
    sq = u * u
    hi = sq.astype(BF16)
    lo = (sq - hi.astype(F32)).astype(BF16)
    parts = []
    for c in range(u.shape[1] // V7X_MXU_DIM):
        sl = slice(c * V7X_MXU_DIM, (c + 1) * V7X_MXU_DIM)
        parts.append(jnp.dot(hi[:, sl], bd, preferred_element_type=F32)
                     + jnp.dot(lo[:, sl], bd, preferred_element_type=F32))
    ss = parts[0] if len(parts) == 1 else jnp.concatenate(parts, axis=1)
    return u * lax.rsqrt(ss * (1.0 / group) + NORM_EPS) * g


def _swiglu_half_step(x, g_ref, w_in_ref, w_out_ref, act_ref):
    d_ff = w_out_ref.shape[0]
    xn = _rms_norm(x, g_ref[...]).astype(BF16)
    for c in range(d_ff // FFN_CHUNK):
        lo = c * FFN_CHUNK
        a = jnp.dot(xn, w_in_ref[:, lo:lo + FFN_CHUNK].astype(BF16), preferred_element_type=F32)
        b = jnp.dot(xn, w_in_ref[:, d_ff + lo:d_ff + lo + FFN_CHUNK].astype(BF16),
                    preferred_element_type=F32)
        act_ref[:, lo:lo + FFN_CHUNK] = (a * jax.nn.sigmoid(a) * b).astype(BF16)
    y = jnp.dot(act_ref[...], w_out_ref[...].astype(BF16), preferred_element_type=F32)
    return x + 0.5 * y


def _ffn_kernel(x_ref, g_ref, w_in_ref, w_out_ref, o_ref, act_ref):
    o_ref[...] = _swiglu_half_step(x_ref[...], g_ref, w_in_ref, w_out_ref, act_ref)


def _ffn(x, g, w_in, w_out):
    n, d = x.shape
    d_ff = w_out.shape[0]
    tm = TOKEN_TILE
    resident = sum(w.size * w.dtype.itemsize for w in (w_in, w_out))
    return pl.pallas_call(
        _ffn_kernel,
        grid=(n // tm,),
        in_specs=[pl.BlockSpec((tm, d), lambda i: (i, 0)),
                  _const_spec((1, d)), _const_spec(w_in.shape), _const_spec(w_out.shape)],
        out_specs=pl.BlockSpec((tm, d), lambda i: (i, 0)),
        out_shape=jax.ShapeDtypeStruct((n, d), F32),
        scratch_shapes=[pltpu.VMEM((tm, d_ff), BF16)],
        compiler_params=pltpu.CompilerParams(dimension_semantics=("arbitrary",),
                                             vmem_limit_bytes=_vmem_limit(resident)),
        name="ffn1",
    )(x, g, w_in, w_out)


def _out_ffn_kernel(x_ref, ylru_ref, ymoba_ref, ymem_ref, wo_ref, g_ref, w_in_ref, w_out_ref,
                    o_ref, act_ref):
    n_lru = ylru_ref.shape[1]
    n_moba = ymoba_ref.shape[1]
    x2 = (x_ref[...]
          + jnp.dot(ylru_ref[...], wo_ref[0:n_lru, :], preferred_element_type=F32)
          + jnp.dot(ymoba_ref[...], wo_ref[n_lru:n_lru + n_moba, :], preferred_element_type=F32)
          + jnp.dot(ymem_ref[...], wo_ref[n_lru + n_moba:, :], preferred_element_type=F32))
    o_ref[...] = _swiglu_half_step(x2, g_ref, w_in_ref, w_out_ref, act_ref)


def _out_ffn(x1, y_lru, y_moba, y_mem, w_o, g, w_in, w_out):
    n, d = x1.shape
    d_ff = w_out.shape[0]
    tm = TOKEN_TILE
    resident = sum(w.size * w.dtype.itemsize for w in (w_in, w_out, w_o))

    def tile(w):
        return pl.BlockSpec((tm, w), lambda i: (i, 0))

    return pl.pallas_call(
        _out_ffn_kernel,
        grid=(n // tm,),
        in_specs=[tile(d), tile(y_lru.shape[1]), tile(y_moba.shape[1]), tile(y_mem.shape[1]),
                  _const_spec(w_o.shape), _const_spec((1, d)),
                  _const_spec(w_in.shape), _const_spec(w_out.shape)],
        out_specs=tile(d),
        out_shape=jax.ShapeDtypeStruct((n, d), F32),
        scratch_shapes=[pltpu.VMEM((tm, d_ff), BF16)],
        compiler_params=pltpu.CompilerParams(dimension_semantics=("arbitrary",),
                                             vmem_limit_bytes=_vmem_limit(resident)),
        name="out_ffn2",
    )(x1, y_lru, y_moba, y_mem, w_o, g, w_in, w_out)


def _mem_kv_kernel(mem_ref, g_ref, wk_ref, wv_ref, bd_ref, gk_ref, kc_ref, vct_ref):
    batch, _, mlen = vct_ref.shape
    mn = _rms_norm(mem_ref[...], g_ref[...]).astype(BF16)
    k = jnp.dot(mn, wk_ref[...], preferred_element_type=F32)
    kc_ref[...] = _group_rms_norm(k, bd_ref[...], gk_ref[...], HEAD_DIM).astype(BF16)
    v = jnp.dot(mn, wv_ref[...], preferred_element_type=F32)
    ones = jnp.ones((VT_ROWS - HEAD_DIM, mlen), BF16)
    for b in range(batch):
        vt = v[b * mlen:(b + 1) * mlen, :].T.astype(BF16)
        for h in range(MEM_HEADS):
            vct_ref[b, h * VT_ROWS:h * VT_ROWS + HEAD_DIM, :] = vt[h * HEAD_DIM:(h + 1) * HEAD_DIM, :]
            vct_ref[b, h * VT_ROWS + HEAD_DIM:(h + 1) * VT_ROWS, :] = ones


def _mem_kv(mem2d, g, w_k, w_v, bd128, gk, batch):
    m = mem2d.shape[0]
    return pl.pallas_call(
        _mem_kv_kernel,
        out_shape=(jax.ShapeDtypeStruct((m, w_k.shape[1]), BF16),
                   jax.ShapeDtypeStruct((batch, MEM_HEADS * VT_ROWS, m // batch), BF16)),
        name="mem_kv",
    )(mem2d, g, w_k, w_v, bd128, gk)


def _in_proj_kernel(x_ref, g_ref, w_lru_ref, w_qvc_ref, w_k_ref,
                    gq_ref, gk_ref, gc_ref, bd64_ref, bd128_ref, alibi_ref,
                    lrux_ref, lrug_ref, qt_ref, kaug_ref, vt_ref, kmean_ref, cqt_ref, *, seq):
    tm = x_ref.shape[0]
    n_lru = lrux_ref.shape[1]
    xn = _rms_norm(x_ref[...], g_ref[...]).astype(BF16)

    lrux_ref[...] = jnp.dot(xn, w_lru_ref[:, :n_lru], preferred_element_type=F32)
    lrug_ref[...] = jnp.dot(xn, w_lru_ref[:, n_lru:], preferred_element_type=F32)

    wq = qt_ref.shape[1]
    qvc = jnp.dot(xn, w_qvc_ref[...], preferred_element_type=F32)

    cq = _group_rms_norm(qvc[:, 2 * wq:], bd64_ref[...], gc_ref[...], HEAD_DIM) * (HEAD_DIM ** -0.5 * LOG2E)
    for r in range(tm // MOBA_BLOCK):
        cqt_ref[r] = cq[r * MOBA_BLOCK:(r + 1) * MOBA_BLOCK, :].T.astype(BF16)

    q = _group_rms_norm(qvc[:, :wq], bd64_ref[...], gq_ref[...], HEAD_DIM)
    v = qvc[:, wq:2 * wq]
    ones = jnp.ones((VT_ROWS - HEAD_DIM, MOBA_BLOCK), BF16)
    for r in range(tm // MOBA_BLOCK):
        rows = slice(r * MOBA_BLOCK, (r + 1) * MOBA_BLOCK)
        qt_ref[r] = q[rows, :].T
        vt = v[rows, :].T.astype(BF16)
        for h in range(MOBA_HEADS):
            vt_ref[r, h * VT_ROWS:h * VT_ROWS + HEAD_DIM, :] = vt[h * HEAD_DIM:(h + 1) * HEAD_DIM, :]
            vt_ref[r, h * VT_ROWS + HEAD_DIM:(h + 1) * VT_ROWS, :] = ones

    k = jnp.dot(xn, w_k_ref[...], preferred_element_type=F32)
    k = _group_rms_norm(k, bd128_ref[...], gk_ref[...], HEAD_DIM)
    kmean_ref[...] = jnp.zeros(kmean_ref.shape, F32)
    for r in range(tm // MOBA_BLOCK):
        rows = slice(r * MOBA_BLOCK, (r + 1) * MOBA_BLOCK)
        kmean_ref[0, r:r + 1, :] = jnp.sum(k[rows, :], axis=0, keepdims=True) * (1.0 / MOBA_BLOCK)

    shape = k.shape
    row = lax.broadcasted_iota(jnp.int32, shape, 0)
    lane = lax.broadcasted_iota(jnp.int32, shape, 1) % AUG_W
    t = (pl.program_id(0) * tm + row) % seq
    bias = t.astype(F32) * alibi_ref[...]
    b_hi = bias.astype(BF16).astype(F32)
    b_mid = (bias - b_hi).astype(BF16).astype(F32)
    b_lo = bias - b_hi - b_mid
    pieces = jnp.where(lane == AUG_ALIBI, b_hi, jnp.where(lane == AUG_ALIBI + 1, b_mid, b_lo))
    onehot = jnp.where(lane - AUG_MASK == t // MOBA_BLOCK, 1.0, 0.0)
    is_alibi = (lane >= AUG_ALIBI) & (lane < AUG_ALIBI + 3)
    kaug_ref[...] = (k + jnp.where(is_alibi, pieces, onehot)).astype(BF16)


def _in_proj(x1, g, w_lru, w_qvc, w_k, gq, gk, gc, bd64, bd128, alibi, seq):
    n, d = x1.shape
    tm = TOKEN_TILE
    rb = tm // MOBA_BLOCK
    nblk = n // MOBA_BLOCK
    n_lru = w_lru.shape[1] // 2
    wq = w_qvc.shape[1] // 3
    wk = w_k.shape[1]
    consts = (g, w_lru, w_qvc, w_k, gq, gk, gc, bd64, bd128, alibi)
    resident = sum(c.size * c.dtype.itemsize for c in consts)
    return pl.pallas_call(
        functools.partial(_in_proj_kernel, seq=seq),
        grid=(n // tm,),
        in_specs=[pl.BlockSpec((tm, d), lambda i: (i, 0))] + [_const_spec(c.shape) for c in consts],
        out_specs=[pl.BlockSpec((tm, n_lru), lambda i: (i, 0)),
                   pl.BlockSpec((tm, n_lru), lambda i: (i, 0)),
                   pl.BlockSpec((rb, wq, MOBA_BLOCK), lambda i: (i, 0, 0)),
                   pl.BlockSpec((tm, wk), lambda i: (i, 0)),
                   pl.BlockSpec((rb, MOBA_HEADS * VT_ROWS, MOBA_BLOCK), lambda i: (i, 0, 0)),
                   pl.BlockSpec((1, 8, wk), lambda i: (i, 0, 0)),
                   pl.BlockSpec((rb, wq, MOBA_BLOCK), lambda i: (i, 0, 0))],
        out_shape=(jax.ShapeDtypeStruct((n, n_lru), F32),
                   jax.ShapeDtypeStruct((n, n_lru), F32),
                   jax.ShapeDtypeStruct((nblk, wq, MOBA_BLOCK), F32),
                   jax.ShapeDtypeStruct((n, wk), BF16),
                   jax.ShapeDtypeStruct((nblk, MOBA_HEADS * VT_ROWS, MOBA_BLOCK), BF16),
                   jax.ShapeDtypeStruct((n // tm, 8, wk), F32),
                   jax.ShapeDtypeStruct((nblk, wq, MOBA_BLOCK), BF16)),
        compiler_params=pltpu.CompilerParams(dimension_semantics=("arbitrary",),
                                             vmem_limit_bytes=_vmem_limit(resident)),
        name="in_proj",
    )(x1, *consts)


def _gelu_tanh(x):
    c = math.sqrt(2.0 / math.pi)
    return 0.5 * x * (1.0 + jnp.tanh(c * (x + 0.044715 * (x * x * x))))


def _lru_kernel(x_ref, g_ref, cw_ref, cb_ref, wa_ref, ba_ref, wx_ref, bx_ref, lam_ref,
                o_ref, xpad_ref, a_ref, b_ref, hl_ref, carry_ref):
    tl, c = x_ref.shape
    half = V7X_MXU_DIM
    ti = pl.program_id(1)

    @pl.when(ti == 0)
    def _():
        xpad_ref[0:8, :] = jnp.zeros((8, c), F32)
        carry_ref[...] = jnp.zeros(carry_ref.shape, F32)

    xpad_ref[8:8 + tl, :] = x_ref[...]
    xb = jnp.broadcast_to(cb_ref[...], (tl, c))
    for j in range(CONV_W):
        off = 8 - (CONV_W - 1) + j
        xb = xb + cw_ref[j:j + 1, :] * xpad_ref[off:off + tl, :]
    xpad_ref[0:8, :] = x_ref[tl - 8:tl, :]

    xbb = xb.astype(BF16)

    def gate(w_ref, bias_ref):
        z = jnp.concatenate(
            [jnp.dot(xbb[:, :half], w_ref[0], preferred_element_type=F32),
             jnp.dot(xbb[:, half:], w_ref[1], preferred_element_type=F32)], axis=1)
        return jax.nn.sigmoid(z + bias_ref[...])

    r = gate(wa_ref, ba_ref)
    i = gate(wx_ref, bx_ref)
    neg_lam = -lam_ref[...]
    softplus = jnp.maximum(neg_lam, 0.0) + jnp.log1p(jnp.exp(-jnp.abs(neg_lam)))
    log_a = (-LRU_C) * r * softplus
    a = jnp.exp(log_a)
    mult = jnp.sqrt(1.0 - a * a)
    row = lax.broadcasted_iota(jnp.int32, (tl, c), 0)
    start_row = jnp.where(ti == 0, 0, -1)
    mult = jnp.where(row == start_row, 1.0, mult)
    bt = mult * i * xb

    groups = tl // SUBLANES
    sub = lax.broadcasted_iota(jnp.int32, (groups, SUBLANES, LANES), 1)
    for k in range(c // LANES):
        lanes = slice(k * LANES, (k + 1) * LANES)
        a3 = a[:, lanes].reshape(groups, SUBLANES, LANES)
        b3 = bt[:, lanes].reshape(groups, SUBLANES, LANES)
        d = 1
        while d < SUBLANES:
            has_prev = sub >= d
            a_prev = pltpu.roll(a3, d, axis=1)
            b_prev = pltpu.roll(b3, d, axis=1)
            b3 = b3 + jnp.where(has_prev, a3, 0.0) * b_prev
            a3 = jnp.where(has_prev, a3 * a_prev, a3)
            d *= 2
        a_ref[:, lanes] = a3.reshape(tl, LANES)
        b_ref[:, lanes] = b3.reshape(tl, LANES)

    h = carry_ref[...]
    for grp in range(groups):
        rows = slice(grp * SUBLANES, (grp + 1) * SUBLANES)
        h_last = jnp.broadcast_to(h[SUBLANES - 1:SUBLANES, :], (SUBLANES, c))
        h = a_ref[rows, :] * h_last + b_ref[rows, :]
        hl_ref[rows, :] = h
    carry_ref[...] = h

    o_ref[...] = (hl_ref[...] * _gelu_tanh(g_ref[...])).astype(o_ref.dtype)


def _lru(lru_x, lru_g, cw, cb, wa, ba, wx, bx, lam, batch, seq):
    n, c = lru_x.shape
    tl = LRU_TILE
    nt = seq // tl
    consts = (cw, cb, wa, ba, wx, bx, lam)
    tile = pl.BlockSpec((tl, c), lambda b, t: (b * nt + t, 0))
    return pl.pallas_call(
        _lru_kernel,
        grid=(batch, nt),
        in_specs=[tile, tile] + [_const_spec(w.shape) for w in consts],
        out_specs=tile,
        out_shape=jax.ShapeDtypeStruct((n, c), BF16),
        scratch_shapes=[pltpu.VMEM((tl + SUBLANES, c), F32)]
        + [pltpu.VMEM((tl, c), F32)] * 3
        + [pltpu.VMEM((SUBLANES, c), F32)],
        compiler_params=pltpu.CompilerParams(dimension_semantics=("arbitrary", "arbitrary")),
        name="lru",
    )(lru_x, lru_g, *consts)


def _moba_kernel(qt_ref, cqt_ref, kaug_ref, vt_ref, kmean_ref, kc_ref, vct_ref, o_ref, om_ref,
                 qaug_ref, qown_ref, qmem_ref, s_ref, p_ref, alpha_ref, m_ref, acc_ref, accm_ref,
                 ot_ref, otm_ref):
    qi = pl.program_id(1)
    nblk = kmean_ref.shape[1]
    blk = MOBA_BLOCK

    def build_query_operands(h):
        qt = qt_ref[0, 0, h * HEAD_DIM:(h + 1) * HEAD_DIM, :]
        qs = (qt * (HEAD_DIM ** -0.5 * LOG2E)).astype(BF16)
        sub = lax.broadcasted_iota(jnp.int32, (AUG_MASK - AUG_ALIBI, blk), 0)
        ones = jnp.where(sub < 3, 1.0, 0.0).astype(BF16)
        for ref in (qaug_ref, qown_ref):
            ref[h, 0:HEAD_DIM, :] = qs
            ref[h, AUG_ALIBI:AUG_MASK, :] = ones
        qown_ref[h, AUG_MASK:, :] = jnp.zeros((AUG_W - AUG_MASK, blk), BF16)
        qmem_ref[h, 0:HEAD_DIM, :] = cqt_ref[0, 0, h * HEAD_DIM:(h + 1) * HEAD_DIM, :]
        qmem_ref[h, HEAD_DIM:, :] = jnp.zeros((AUG_W - HEAD_DIM, blk), BF16)

    def build_block_mask(h):
        blk_id = lax.broadcasted_iota(jnp.int32, (nblk, blk), 0).astype(F32)
        qt = qt_ref[0, 0, h * HEAD_DIM:(h + 1) * HEAD_DIM, :]
        gate = jnp.dot(kmean_ref[0, :, h * AUG_W:h * AUG_W + HEAD_DIM], qt,
                       preferred_element_type=F32, precision=lax.Precision.HIGHEST)
        gate = jnp.where(blk_id < qi.astype(F32), gate, -jnp.inf)
        keep = blk_id < 0.0
        for _ in range(MOBA_TOPK):
            best = jnp.max(gate, axis=0, keepdims=True)
            first = jnp.min(jnp.where(gate == best, blk_id, float(nblk)), axis=0, keepdims=True)
            pick = (blk_id == first) & (best > -jnp.inf)
            keep = keep | pick
            gate = jnp.where(pick, -jnp.inf, gate)
        qaug_ref[h, AUG_MASK:AUG_MASK + nblk, :] = jnp.where(keep, 0.0, NEG_BIG).astype(BF16)
        qaug_ref[h, AUG_MASK + nblk:, :] = jnp.zeros((AUG_W - AUG_MASK - nblk, blk), BF16)

    kk = lax.broadcasted_iota(jnp.int32, (blk, blk), 0)
    qq = lax.broadcasted_iota(jnp.int32, (blk, blk), 1)
    causal = kk <= qq

    per_trip = PAST_UNROLL * MOBA_HEADS
    lead = MOBA_HEADS + MEM_HEADS

    def past_block(trip, u):
        return jnp.minimum(trip * PAST_UNROLL + u, nblk - 1)

    def stream_item(trip, n):
        if n < 0:
            idx = n + lead
            if idx < MOBA_HEADS:
                return "own", qi, idx, idx % SCORE_SLOTS
            return "mem", 0, idx - MOBA_HEADS, idx % SCORE_SLOTS
        if n >= per_trip:
            trip, n = trip + 1, n - per_trip
        return "past", past_block(trip, n // MOBA_HEADS), n % MOBA_HEADS, (n + lead) % SCORE_SLOTS

    def scores(kind, j, h, slot):
        lanes = slice(h * AUG_W, (h + 1) * AUG_W)
        if kind == "mem":
            keys, q_ref = kc_ref[0, :, lanes], qmem_ref
        else:
            keys, q_ref = kaug_ref[0, j, :, lanes], (qown_ref if kind == "own" else qaug_ref)
        s_ref[slot] = jnp.dot(keys, q_ref[h], preferred_element_type=F32)

    def probabilities(kind, j, h, slot):
        s = s_ref[slot]
        if kind == "past":
            m_old = m_ref[h]
            m_new = jnp.maximum(m_old, jnp.max(s, axis=0, keepdims=True))
            alpha_ref[slot % PROB_SLOTS] = jnp.exp2(m_old - m_new)
        else:
            if kind == "own":
                s = jnp.where(causal, s, NEG_BIG)
            m_new = jnp.max(s, axis=0, keepdims=True)
        p_ref[slot % PROB_SLOTS] = jnp.exp2(s - m_new).astype(BF16)
        if kind != "mem":
            m_ref[h] = m_new

    def accumulate(kind, j, h, slot):
        rows = slice(h * VT_ROWS, (h + 1) * VT_ROWS)
        p = p_ref[slot % PROB_SLOTS]
        if kind == "mem":
            accm_ref[h] = jnp.dot(vct_ref[0, rows, :], p, preferred_element_type=F32)
        else:
            pv = jnp.dot(vt_ref[0, j, rows, :], p, preferred_element_type=F32)
            acc_ref[h] = pv if kind == "own" else alpha_ref[slot % PROB_SLOTS] * acc_ref[h] + pv

    for h in range(MOBA_HEADS):
        build_query_operands(h)
        build_block_mask(h)
    for n in range(-lead, -lead + SCORE_LOOKAHEAD):
        scores(*stream_item(0, n))
    for n in range(-lead, -lead + PROB_LOOKAHEAD):
        probabilities(*stream_item(0, n))
    for n in range(-lead, 0):
        scores(*stream_item(0, n + SCORE_LOOKAHEAD))
        probabilities(*stream_item(0, n + PROB_LOOKAHEAD))
        accumulate(*stream_item(0, n))

    def past_trip(trip, carry):
        for n in range(per_trip):
            scores(*stream_item(trip, n + SCORE_LOOKAHEAD))
            probabilities(*stream_item(trip, n + PROB_LOOKAHEAD))
            accumulate(*stream_item(trip, n))
        return carry

    lax.fori_loop(0, (qi + PAST_UNROLL - 1) // PAST_UNROLL, past_trip, 0)

    for ref, t_ref, out_ref in ((acc_ref, ot_ref, o_ref), (accm_ref, otm_ref, om_ref)):
        for h in range(MOBA_HEADS):
            t_ref[h * HEAD_DIM:(h + 1) * HEAD_DIM, :] = (ref[h, 0:HEAD_DIM, :]
                                                         / ref[h, HEAD_DIM:HEAD_DIM + 1, :])
        out_ref[...] = t_ref[...].T.astype(out_ref.dtype)


def _moba(qt, cqt, kaug, vt, kmean, kc, vct, batch, seq):
    nblk = seq // MOBA_BLOCK
    width = MOBA_HEADS * HEAD_DIM
    aug = MOBA_HEADS * AUG_W
    vt_rows = MOBA_HEADS * VT_ROWS
    mlen = vct.shape[2]
    assert mlen == MOBA_BLOCK and MEM_HEADS == MOBA_HEADS, "memory items reuse the key-block buffers"
    qt = qt.reshape(batch, nblk, width, MOBA_BLOCK)
    cqt = cqt.reshape(batch, nblk, width, MOBA_BLOCK)
    kaug = kaug.reshape(batch, nblk, MOBA_BLOCK, aug)
    vt = vt.reshape(batch, nblk, vt_rows, MOBA_BLOCK)
    kc = kc.reshape(batch, mlen, aug)
    resident = 2 * (kaug.size // batch + vt.size // batch) * 2
    qblock = pl.BlockSpec((1, 1, width, MOBA_BLOCK), lambda b, i: (b, i, 0, 0))
    out_block = pl.BlockSpec((MOBA_BLOCK, width), lambda b, i: (b * nblk + i, 0))
    out_shape = jax.ShapeDtypeStruct((batch * seq, width), BF16)
    return pl.pallas_call(
        _moba_kernel,
        grid=(batch, nblk),
        in_specs=[qblock, qblock,
                  pl.BlockSpec((1, nblk, MOBA_BLOCK, aug), lambda b, i: (b, 0, 0, 0)),
                  pl.BlockSpec((1, nblk, vt_rows, MOBA_BLOCK), lambda b, i: (b, 0, 0, 0)),
                  pl.BlockSpec((1, nblk, aug), lambda b, i: (b, 0, 0)),
                  pl.BlockSpec((1, mlen, aug), lambda b, i: (b, 0, 0)),
                  pl.BlockSpec((1, vt_rows, mlen), lambda b, i: (b, 0, 0))],
        out_specs=(out_block, out_block),
        out_shape=(out_shape, out_shape),
        scratch_shapes=[pltpu.VMEM((MOBA_HEADS, AUG_W, MOBA_BLOCK), BF16),
                        pltpu.VMEM((MOBA_HEADS, AUG_W, MOBA_BLOCK), BF16),
                        pltpu.VMEM((MOBA_HEADS, AUG_W, MOBA_BLOCK), BF16),
                        pltpu.VMEM((SCORE_SLOTS, MOBA_BLOCK, MOBA_BLOCK), F32),
                        pltpu.VMEM((PROB_SLOTS, MOBA_BLOCK, MOBA_BLOCK), BF16),
                        pltpu.VMEM((PROB_SLOTS, 1, MOBA_BLOCK), F32),
                        pltpu.VMEM((MOBA_HEADS, 1, MOBA_BLOCK), F32),
                        pltpu.VMEM((MOBA_HEADS, VT_ROWS, MOBA_BLOCK), F32),
                        pltpu.VMEM((MOBA_HEADS, VT_ROWS, MOBA_BLOCK), F32),
                        pltpu.VMEM((width, MOBA_BLOCK), F32),
                        pltpu.VMEM((width, MOBA_BLOCK), F32)],
        compiler_params=pltpu.CompilerParams(dimension_semantics=("arbitrary", "arbitrary"),
                                             vmem_limit_bytes=_vmem_limit(resident)),
        name="moba",
    )(qt, cqt, kaug, vt, kmean, kc, vct)


def _block_diag_ones(group):
    idx = jnp.arange(V7X_MXU_DIM) // group
    return (idx[:, None] == idx[None, :]).astype(BF16)


def _pack_block_diag(w):
    nb, bw, _ = w.shape
    per = V7X_MXU_DIM // bw
    w = w.reshape(nb // per, per, bw, bw)
    eye = jnp.eye(per, dtype=w.dtype)
    return jnp.einsum("gpij,pq->gpiqj", w, eye).reshape(nb // per, V7X_MXU_DIM, V7X_MXU_DIM)


def _pad_heads(w, heads):
    lead = w.shape[:-1]
    w = w.reshape(lead + (heads, HEAD_DIM))
    w = jnp.pad(w, [(0, 0)] * len(lead) + [(0, 0), (0, AUG_W - HEAD_DIM)])
    return w.reshape(lead + (heads * AUG_W,))


def _layer(x, mem, ffn1_norm, ffn1_w_in, ffn1_w_out, mix_norm, mem_norm, w_in,
           lru_conv_w, lru_conv_b, lru_a_w, lru_a_b, lru_x_w, lru_x_b, lru_lambda,
           moba_q_norm, moba_k_norm, mem_w_kv, mem_q_norm, mem_k_norm, w_out,
           ffn2_norm, ffn2_w_in, ffn2_w_out):
    batch, seq, d = x.shape
    n = batch * seq
    n_lru = lru_lambda.shape[0]
    wq = MOBA_HEADS * HEAD_DIM
    wc = MEM_HEADS * HEAD_DIM
    row = lambda v: v.reshape(1, -1).astype(F32)

    bd64 = _block_diag_ones(HEAD_DIM)
    bd128 = _block_diag_ones(AUG_W)

    w_in_b = w_in.astype(BF16)
    o = 2 * n_lru
    w_lru = w_in_b[:, :o]
    w_k = _pad_heads(w_in_b[:, o + wq:o + 2 * wq], MOBA_HEADS)
    w_qvc = jnp.concatenate([w_in_b[:, o:o + wq], w_in_b[:, o + 2 * wq:]], axis=1)
    gq = row(jnp.tile(moba_q_norm, MOBA_HEADS))
    gk = row(_pad_heads(jnp.tile(moba_k_norm, MOBA_HEADS), MOBA_HEADS))
    gc = row(jnp.tile(mem_q_norm, MEM_HEADS))
    gck = row(_pad_heads(jnp.tile(mem_k_norm, MEM_HEADS), MEM_HEADS))
    w_kv_b = mem_w_kv.astype(BF16)
    w_ck = _pad_heads(w_kv_b[:, :wc], MEM_HEADS)
    w_cv = w_kv_b[:, wc:]
    slopes = 2.0 ** (-8.0 * jnp.arange(1, MOBA_HEADS + 1, dtype=F32) / MOBA_HEADS)
    lane = jnp.arange(AUG_W)
    is_alibi = (lane >= AUG_ALIBI) & (lane < AUG_ALIBI + 3)
    alibi = row(jnp.where(is_alibi[None, :], slopes[:, None] * LOG2E, 0.0))

    x2d = x.reshape(n, d)
    kc, vct = _mem_kv(mem.reshape(-1, d), row(mem_norm), w_ck, w_cv, bd128, gck, batch)

    x1 = _ffn(x2d, row(ffn1_norm), ffn1_w_in, ffn1_w_out)

    lru_x, lru_g, qt, kaug, vt, kmean, cqt = _in_proj(
        x1, row(mix_norm), w_lru, w_qvc, w_k, gq, gk, gc, bd64, bd128, alibi, seq)
    kmean = kmean[:, :TOKEN_TILE // MOBA_BLOCK, :].reshape(batch, seq // MOBA_BLOCK, -1)

    y_lru = _lru(lru_x, lru_g, lru_conv_w.astype(F32), row(lru_conv_b),
                 _pack_block_diag(lru_a_w).astype(BF16), row(lru_a_b),
                 _pack_block_diag(lru_x_w).astype(BF16), row(lru_x_b), row(lru_lambda), batch, seq)
    y_moba, y_mem = _moba(qt, cqt, kaug, vt, kmean, kc, vct, batch, seq)

    out = _out_ffn(x1, y_lru, y_moba, y_mem, w_out.astype(BF16), row(ffn2_norm),
                   ffn2_w_in, ffn2_w_out)
    return out.reshape(batch, seq, d)


def kernel(x, mem, ffn1_norm, ffn1_w_in, ffn1_w_out, mix_norm, mem_norm, w_in, lru_conv_w, lru_conv_b,
           lru_a_w, lru_a_b, lru_x_w, lru_x_b, lru_lambda, moba_q_norm, moba_k_norm, mem_w_kv,
           mem_q_norm, mem_k_norm, w_out, ffn2_norm, ffn2_w_in, ffn2_w_out):
    params = (ffn1_norm, ffn1_w_in, ffn1_w_out, mix_norm, mem_norm, w_in, lru_conv_w, lru_conv_b,
              lru_a_w, lru_a_b, lru_x_w, lru_x_b, lru_lambda, moba_q_norm, moba_k_norm, mem_w_kv,
              mem_q_norm, mem_k_norm, w_out, ffn2_norm, ffn2_w_in, ffn2_w_out)
    for layer in range(ffn1_norm.shape[0]):
        x = _layer(x, mem, *(p[layer] for p in params))
    return x
```

```python
import functools
import math

import jax
import jax.numpy as jnp
from jax import lax
from jax.experimental import pallas as pl
from jax.experimental.pallas import tpu as pltpu

F32 = jnp.float32
BF16 = jnp.bfloat16

HEAD_DIM = 64
CONV_W = 4
LRU_C = 8.0
MOBA_HEADS = 4
MOBA_BLOCK = 256
MOBA_TOPK = 3
MEM_HEADS = 4
NORM_EPS = 1e-6
LOG2E = 1.4426950408889634
NEG_BIG = -1e30

V7X_MXU_DIM = 256
LANES = 128
SUBLANES = 8
V7X_VMEM_BYTES = 64 * 1024 * 1024

AUG_W = 128
AUG_ALIBI = 64
AUG_MASK = 80
VT_ROWS = HEAD_DIM + 16
SCORE_LOOKAHEAD = 5
PAST_UNROLL = 4
SCORE_SLOTS = 8
assert MOBA_HEADS <= SCORE_LOOKAHEAD < SCORE_SLOTS and (PAST_UNROLL * MOBA_HEADS) % SCORE_SLOTS == 0
assert (MOBA_HEADS + MEM_HEADS) % SCORE_SLOTS == 0
PROB_SLOTS = 4
PROB_LOOKAHEAD = 1
assert PROB_LOOKAHEAD < SCORE_LOOKAHEAD

TOKEN_TILE = 512
LRU_TILE = 512
FFN_CHUNK = 256


def _vmem_limit(resident_bytes):
    return int(min(V7X_VMEM_BYTES - 8 * 1024 * 1024, resident_bytes + 24 * 1024 * 1024))


def _const_spec(shape):
    n = len(shape)
    return pl.BlockSpec(shape, lambda *_: (0,) * n, pipeline_mode=pl.Buffered(1))


def _sigmoid(x):
    return 0.5 * jnp.tanh(0.5 * x) + 0.5


def _rms_norm(x, g):
    ms = jnp.mean(x * x, axis=-1, keepdims=True)
    return x * lax.rsqrt(ms + NORM_EPS) * g


def _group_rms_norm(u, bd, g, group):
    sq = u * u
    hi = sq.astype(BF16)
    lo = (sq - hi.astype(F32)).astype(BF16)
    parts = []
    for c in range(u.shape[1] // V7X_MXU_DIM):
        sl = slice(c * V7X_MXU_DIM, (c + 1) * V7X_MXU_DIM)
        parts.append(jnp.dot(hi[:, sl], bd, preferred_element_type=F32)
                     + jnp.dot(lo[:, sl], bd, preferred_element_type=F32))
    ss = parts[0] if len(parts) == 1 else jnp.concatenate(parts, axis=1)
    return u * lax.rsqrt(ss * (1.0 / group) + NORM_EPS) * g


def _swiglu_half_step(x, g_ref, w_in_ref, w_out_ref, act_ref):
    d_ff = w_out_ref.shape[0]
    xn = _rms_norm(x, g_ref[...]).astype(BF16)
    for c in range(d_ff // FFN_CHUNK):
        lo = c * FFN_CHUNK
        a = jnp.dot(xn, w_in_ref[:, lo:lo + FFN_CHUNK].astype(BF16), preferred_element_type=F32)
        b = jnp.dot(xn, w_in_ref[:, d_ff + lo:d_ff + lo + FFN_CHUNK].astype(BF16),
                    preferred_element_type=F32)
        act_ref[:, lo:lo + FFN_CHUNK] = (a * _sigmoid(a) * b).astype(BF16)
    y = jnp.dot(act_ref[...], w_out_ref[...].astype(BF16), preferred_element_type=F32)
    return x + 0.5 * y


def _ffn_kernel(x_ref, g_ref, w_in_ref, w_out_ref, o_ref, act_ref):
    o_ref[...] = _swiglu_half_step(x_ref[...], g_ref, w_in_ref, w_out_ref, act_ref)


def _ffn(x, g, w_in, w_out):
    n, d = x.shape
    d_ff = w_out.shape[0]
    tm = TOKEN_TILE
    resident = sum(w.size * w.dtype.itemsize for w in (w_in, w_out))
    return pl.pallas_call(
        _ffn_kernel,
        grid=(n // tm,),
        in_specs=[pl.BlockSpec((tm, d), lambda i: (i, 0)),
                  _const_spec((1, d)), _const_spec(w_in.shape), _const_spec(w_out.shape)],
        out_specs=pl.BlockSpec((tm, d), lambda i: (i, 0)),
        out_shape=jax.ShapeDtypeStruct((n, d), F32),
        scratch_shapes=[pltpu.VMEM((tm, d_ff), BF16)],
        compiler_params=pltpu.CompilerParams(dimension_semantics=("arbitrary",),
                                             vmem_limit_bytes=_vmem_limit(resident)),
        name="ffn1",
    )(x, g, w_in, w_out)


def _out_ffn_kernel(x_ref, ylru_ref, ymoba_ref, ymem_ref, wo_ref, g_ref, w_in_ref, w_out_ref,
                    o_ref, act_ref):
    n_lru = ylru_ref.shape[1]
    n_moba = ymoba_ref.shape[1]
    x2 = (x_ref[...]
          + jnp.dot(ylru_ref[...], wo_ref[0:n_lru, :], preferred_element_type=F32)
          + jnp.dot(ymoba_ref[...], wo_ref[n_lru:n_lru + n_moba, :], preferred_element_type=F32)
          + jnp.dot(ymem_ref[...], wo_ref[n_lru + n_moba:, :], preferred_element_type=F32))
    o_ref[...] = _swiglu_half_step(x2, g_ref, w_in_ref, w_out_ref, act_ref)


def _out_ffn(x1, y_lru, y_moba, y_mem, w_o, g, w_in, w_out):
    n, d = x1.shape
    d_ff = w_out.shape[0]
    tm = TOKEN_TILE
    resident = sum(w.size * w.dtype.itemsize for w in (w_in, w_out, w_o))

    def tile(w):
        return pl.BlockSpec((tm, w), lambda i: (i, 0))

    return pl.pallas_call(
        _out_ffn_kernel,
        grid=(n // tm,),
        in_specs=[tile(d), tile(y_lru.shape[1]), tile(y_moba.shape[1]), tile(y_mem.shape[1]),
                  _const_spec(w_o.shape), _const_spec((1, d)),
                  _const_spec(w_in.shape), _const_spec(w_out.shape)],
        out_specs=tile(d),
        out_shape=jax.ShapeDtypeStruct((n, d), F32),
        scratch_shapes=[pltpu.VMEM((tm, d_ff), BF16)],
        compiler_params=pltpu.CompilerParams(dimension_semantics=("arbitrary",),
                                             vmem_limit_bytes=_vmem_limit(resident)),
        name="out_ffn2",
    )(x1, y_lru, y_moba, y_mem, w_o, g, w_in, w_out)


def _mem_kv_kernel(mem_ref, g_ref, wk_ref, wv_ref, bd_ref, gk_ref, kc_ref, vct_ref):
    batch, _, mlen = vct_ref.shape
    mn = _rms_norm(mem_ref[...], g_ref[...]).astype(BF16)
    k = jnp.dot(mn, wk_ref[...], preferred_element_type=F32)
    kc_ref[...] = _group_rms_norm(k, bd_ref[...], gk_ref[...], HEAD_DIM).astype(BF16)
    v = jnp.dot(mn, wv_ref[...], preferred_element_type=F32)
    ones = jnp.ones((VT_ROWS - HEAD_DIM, mlen), BF16)
    for b in range(batch):
        vt = v[b * mlen:(b + 1) * mlen, :].T.astype(BF16)
        for h in range(MEM_HEADS):
            vct_ref[b, h * VT_ROWS:h * VT_ROWS + HEAD_DIM, :] = vt[h * HEAD_DIM:(h + 1) * HEAD_DIM, :]
            vct_ref[b, h * VT_ROWS + HEAD_DIM:(h + 1) * VT_ROWS, :] = ones


def _mem_kv(mem2d, g, w_k, w_v, bd128, gk, batch):
    m = mem2d.shape[0]
    return pl.pallas_call(
        _mem_kv_kernel,
        out_shape=(jax.ShapeDtypeStruct((m, w_k.shape[1]), BF16),
                   jax.ShapeDtypeStruct((batch, MEM_HEADS * VT_ROWS, m // batch), BF16)),
        name="mem_kv",
    )(mem2d, g, w_k, w_v, bd128, gk)


def _in_proj_kernel(x_ref, g_ref, w_lru_ref, w_qvc_ref, w_k_ref,
                    gq_ref, gk_ref, gc_ref, bd64_ref, bd128_ref, alibi_ref,
                    lrux_ref, lrug_ref, qt_ref, kaug_ref, vt_ref, kmean_ref, cqt_ref, *, seq):
    tm = x_ref.shape[0]
    n_lru = lrux_ref.shape[1]
    xn = _rms_norm(x_ref[...], g_ref[...]).astype(BF16)

    lrux_ref[...] = jnp.dot(xn, w_lru_ref[:, :n_lru], preferred_element_type=F32)
    lrug_ref[...] = jnp.dot(xn, w_lru_ref[:, n_lru:], preferred_element_type=F32)

    wq = qt_ref.shape[1]
    qvc = jnp.dot(xn, w_qvc_ref[...], preferred_element_type=F32)

    cq = _group_rms_norm(qvc[:, 2 * wq:], bd64_ref[...], gc_ref[...], HEAD_DIM) * (HEAD_DIM ** -0.5 * LOG2E)
    for r in range(tm // MOBA_BLOCK):
        cqt_ref[r] = cq[r * MOBA_BLOCK:(r + 1) * MOBA_BLOCK, :].T.astype(BF16)

    q = _group_rms_norm(qvc[:, :wq], bd64_ref[...], gq_ref[...], HEAD_DIM)
    v = qvc[:, wq:2 * wq]
    ones = jnp.ones((VT_ROWS - HEAD_DIM, MOBA_BLOCK), BF16)
    for r in range(tm // MOBA_BLOCK):
        rows = slice(r * MOBA_BLOCK, (r + 1) * MOBA_BLOCK)
        qt_ref[r] = q[rows, :].T
        vt = v[rows, :].T.astype(BF16)
        for h in range(MOBA_HEADS):
            vt_ref[r, h * VT_ROWS:h * VT_ROWS + HEAD_DIM, :] = vt[h * HEAD_DIM:(h + 1) * HEAD_DIM, :]
            vt_ref[r, h * VT_ROWS + HEAD_DIM:(h + 1) * VT_ROWS, :] = ones

    k = jnp.dot(xn, w_k_ref[...], preferred_element_type=F32)
    k = _group_rms_norm(k, bd128_ref[...], gk_ref[...], HEAD_DIM)
    kmean_ref[...] = jnp.zeros(kmean_ref.shape, F32)
    for r in range(tm // MOBA_BLOCK):
        rows = slice(r * MOBA_BLOCK, (r + 1) * MOBA_BLOCK)
        kmean_ref[0, r:r + 1, :] = jnp.sum(k[rows, :], axis=0, keepdims=True) * (1.0 / MOBA_BLOCK)

    shape = k.shape
    row = lax.broadcasted_iota(jnp.int32, shape, 0)
    lane = lax.broadcasted_iota(jnp.int32, shape, 1) % AUG_W
    t = (pl.program_id(0) * tm + row) % seq
    bias = t.astype(F32) * alibi_ref[...]
    b_hi = bias.astype(BF16).astype(F32)
    b_mid = (bias - b_hi).astype(BF16).astype(F32)
    b_lo = bias - b_hi - b_mid
    pieces = jnp.where(lane == AUG_ALIBI, b_hi, jnp.where(lane == AUG_ALIBI + 1, b_mid, b_lo))
    onehot = jnp.where(lane - AUG_MASK == t // MOBA_BLOCK, 1.0, 0.0)
    is_alibi = (lane >= AUG_ALIBI) & (lane < AUG_ALIBI + 3)
    kaug_ref[...] = (k + jnp.where(is_alibi, pieces, onehot)).astype(BF16)


def _in_proj(x1, g, w_lru, w_qvc, w_k, gq, gk, gc, bd64, bd128, alibi, seq):
    n, d = x1.shape
    tm = TOKEN_TILE
    rb = tm // MOBA_BLOCK
    nblk = n // MOBA_BLOCK
    n_lru = w_lru.shape[1] // 2
    wq = w_qvc.shape[1] // 3
    wk = w_k.shape[1]
    consts = (g, w_lru, w_qvc, w_k, gq, gk, gc, bd64, bd128, alibi)
    resident = sum(c.size * c.dtype.itemsize for c in consts)
    return pl.pallas_call(
        functools.partial(_in_proj_kernel, seq=seq),
        grid=(n // tm,),
        in_specs=[pl.BlockSpec((tm, d), lambda i: (i, 0))] + [_const_spec(c.shape) for c in consts],
        out_specs=[pl.BlockSpec((tm, n_lru), lambda i: (i, 0)),
                   pl.BlockSpec((tm, n_lru), lambda i: (i, 0)),
                   pl.BlockSpec((rb, wq, MOBA_BLOCK), lambda i: (i, 0, 0)),
                   pl.BlockSpec((tm, wk), lambda i: (i, 0)),
                   pl.BlockSpec((rb, MOBA_HEADS * VT_ROWS, MOBA_BLOCK), lambda i: (i, 0, 0)),
                   pl.BlockSpec((1, 8, wk), lambda i: (i, 0, 0)),
                   pl.BlockSpec((rb, wq, MOBA_BLOCK), lambda i: (i, 0, 0))],
        out_shape=(jax.ShapeDtypeStruct((n, n_lru), F32),
                   jax.ShapeDtypeStruct((n, n_lru), F32),
                   jax.ShapeDtypeStruct((nblk, wq, MOBA_BLOCK), F32),
                   jax.ShapeDtypeStruct((n, wk), BF16),
                   jax.ShapeDtypeStruct((nblk, MOBA_HEADS * VT_ROWS, MOBA_BLOCK), BF16),
                   jax.ShapeDtypeStruct((n // tm, 8, wk), F32),
                   jax.ShapeDtypeStruct((nblk, wq, MOBA_BLOCK), BF16)),
        compiler_params=pltpu.CompilerParams(dimension_semantics=("arbitrary",),
                                             vmem_limit_bytes=_vmem_limit(resident)),
        name="in_proj",
    )(x1, *consts)


def _gelu_tanh(x):
    c = math.sqrt(2.0 / math.pi)
    return 0.5 * x * (1.0 + jnp.tanh(c * (x + 0.044715 * (x * x * x))))


def _lru_kernel(x_ref, g_ref, cw_ref, cb_ref, wa_ref, ba_ref, wx_ref, bx_ref, lam_ref,
                o_ref, xpad_ref, a_ref, b_ref, hl_ref, carry_ref):
    tl, c = x_ref.shape
    half = V7X_MXU_DIM
    ti = pl.program_id(1)

    @pl.when(ti == 0)
    def _():
        xpad_ref[0:8, :] = jnp.zeros((8, c), F32)
        carry_ref[...] = jnp.zeros(carry_ref.shape, F32)

    xpad_ref[8:8 + tl, :] = x_ref[...]
    n_grp = tl // SUBLANES
    sub1 = lax.broadcasted_iota(jnp.int32, (n_grp, SUBLANES, LANES), 1)
    parts = []
    for k in range(c // LANES):
        lanes = slice(k * LANES, (k + 1) * LANES)
        xg = xpad_ref[:, lanes].reshape(n_grp + 1, SUBLANES, LANES)
        acc = cb_ref[:, lanes] + cw_ref[CONV_W - 1:CONV_W, lanes] * xg[1:]
        for d in range(1, CONV_W):
            rolled = pltpu.roll(xg, d, axis=1)
            delayed = jnp.where(sub1 >= d, rolled[1:], rolled[:-1])
            acc = acc + cw_ref[CONV_W - 1 - d:CONV_W - d, lanes] * delayed
        parts.append(acc.reshape(tl, LANES))
    xb = jnp.concatenate(parts, axis=1)
    xpad_ref[0:8, :] = x_ref[tl - 8:tl, :]

    xbb = xb.astype(BF16)

    def gate(w_ref, bias_ref):
        z = jnp.concatenate(
            [jnp.dot(xbb[:, :half], w_ref[0], preferred_element_type=F32),
             jnp.dot(xbb[:, half:], w_ref[1], preferred_element_type=F32)], axis=1)
        return _sigmoid(z + bias_ref[...])

    r = gate(wa_ref, ba_ref)
    i = gate(wx_ref, bx_ref)
    neg_lam = -lam_ref[...]
    softplus = jnp.maximum(neg_lam, 0.0) + jnp.log1p(jnp.exp(-jnp.abs(neg_lam)))
    log_a = (-LRU_C) * r * softplus
    a = jnp.exp(log_a)
    one_minus_a2 = 1.0 - a * a
    mult = jnp.where(one_minus_a2 > 0.0, one_minus_a2 * lax.rsqrt(one_minus_a2), 0.0)
    row = lax.broadcasted_iota(jnp.int32, (tl, c), 0)
    start_row = jnp.where(ti == 0, 0, -1)
    mult = jnp.where(row == start_row, 1.0, mult)
    bt = mult * i * xb

    groups = tl // SUBLANES
    sub = lax.broadcasted_iota(jnp.int32, (groups, SUBLANES, LANES), 1)
    for k in range(c // LANES):
        lanes = slice(k * LANES, (k + 1) * LANES)
        a3 = a[:, lanes].reshape(groups, SUBLANES, LANES)
        b3 = bt[:, lanes].reshape(groups, SUBLANES, LANES)
        d = 1
        while d < SUBLANES:
            has_prev = sub >= d
            a_prev = pltpu.roll(a3, d, axis=1)
            b_prev = pltpu.roll(b3, d, axis=1)
            b3 = b3 + jnp.where(has_prev, a3, 0.0) * b_prev
            a3 = jnp.where(has_prev, a3 * a_prev, a3)
            d *= 2
        a_ref[:, lanes] = a3.reshape(tl, LANES)
        b_ref[:, lanes] = b3.reshape(tl, LANES)

    h = carry_ref[...]
    for grp in range(groups):
        rows = slice(grp * SUBLANES, (grp + 1) * SUBLANES)
        h_last = jnp.broadcast_to(h[SUBLANES - 1:SUBLANES, :], (SUBLANES, c))
        h = a_ref[rows, :] * h_last + b_ref[rows, :]
        hl_ref[rows, :] = h
    carry_ref[...] = h

    o_ref[...] = (hl_ref[...] * _gelu_tanh(g_ref[...])).astype(o_ref.dtype)


def _lru(lru_x, lru_g, cw, cb, wa, ba, wx, bx, lam, batch, seq):
    n, c = lru_x.shape
    tl = LRU_TILE
    nt = seq // tl
    consts = (cw, cb, wa, ba, wx, bx, lam)
    tile = pl.BlockSpec((tl, c), lambda b, t: (b * nt + t, 0))
    return pl.pallas_call(
        _lru_kernel,
        grid=(batch, nt),
        in_specs=[tile, tile] + [_const_spec(w.shape) for w in consts],
        out_specs=tile,
        out_shape=jax.ShapeDtypeStruct((n, c), BF16),
        scratch_shapes=[pltpu.VMEM((tl + SUBLANES, c), F32)]
        + [pltpu.VMEM((tl, c), F32)] * 3
        + [pltpu.VMEM((SUBLANES, c), F32)],
        compiler_params=pltpu.CompilerParams(dimension_semantics=("arbitrary", "arbitrary")),
        name="lru",
    )(lru_x, lru_g, *consts)


def _moba_kernel(qt_ref, cqt_ref, kaug_ref, vt_ref, kmean_ref, kc_ref, vct_ref, o_ref, om_ref,
                 qaug_ref, qown_ref, qmem_ref, s_ref, p_ref, alpha_ref, m_ref, acc_ref, accm_ref,
                 ot_ref, otm_ref):
    qi = pl.program_id(1)
    nblk = kmean_ref.shape[1]
    blk = MOBA_BLOCK

    def build_query_operands(h):
        qt = qt_ref[0, 0, h * HEAD_DIM:(h + 1) * HEAD_DIM, :]
        qs = (qt * (HEAD_DIM ** -0.5 * LOG2E)).astype(BF16)
        sub = lax.broadcasted_iota(jnp.int32, (AUG_MASK - AUG_ALIBI, blk), 0)
        ones = jnp.where(sub < 3, 1.0, 0.0).astype(BF16)
        for ref in (qaug_ref, qown_ref):
            ref[h, 0:HEAD_DIM, :] = qs
            ref[h, AUG_ALIBI:AUG_MASK, :] = ones
        qown_ref[h, AUG_MASK:, :] = jnp.zeros((AUG_W - AUG_MASK, blk), BF16)
        qmem_ref[h, 0:HEAD_DIM, :] = cqt_ref[0, 0, h * HEAD_DIM:(h + 1) * HEAD_DIM, :]
        qmem_ref[h, HEAD_DIM:, :] = jnp.zeros((AUG_W - HEAD_DIM, blk), BF16)

    def build_block_mask(h):
        blk_id = lax.broadcasted_iota(jnp.int32, (nblk, blk), 0).astype(F32)
        qt = qt_ref[0, 0, h * HEAD_DIM:(h + 1) * HEAD_DIM, :]
        gate = jnp.dot(kmean_ref[0, :, h * AUG_W:h * AUG_W + HEAD_DIM], qt,
                       preferred_element_type=F32, precision=lax.Precision.HIGHEST)
        gate = jnp.where(blk_id < qi.astype(F32), gate, -jnp.inf)
        keep = blk_id < 0.0
        for _ in range(MOBA_TOPK):
            best = jnp.max(gate, axis=0, keepdims=True)
            first = jnp.min(jnp.where(gate == best, blk_id, float(nblk)), axis=0, keepdims=True)
            pick = (blk_id == first) & (best > -jnp.inf)
            keep = keep | pick
            gate = jnp.where(pick, -jnp.inf, gate)
        qaug_ref[h, AUG_MASK:AUG_MASK + nblk, :] = jnp.where(keep, 0.0, NEG_BIG).astype(BF16)
        qaug_ref[h, AUG_MASK + nblk:, :] = jnp.zeros((AUG_W - AUG_MASK - nblk, blk), BF16)

    kk = lax.broadcasted_iota(jnp.int32, (blk, blk), 0)
    qq = lax.broadcasted_iota(jnp.int32, (blk, blk), 1)
    causal = kk <= qq

    per_trip = PAST_UNROLL * MOBA_HEADS
    lead = MOBA_HEADS + MEM_HEADS

    def past_block(trip, u):
        return jnp.minimum(trip * PAST_UNROLL + u, nblk - 1)

    def stream_item(trip, n):
        if n < 0:
            idx = n + lead
            if idx < MOBA_HEADS:
                return "own", qi, idx, idx % SCORE_SLOTS
            return "mem", 0, idx - MOBA_HEADS, idx % SCORE_SLOTS
        if n >= per_trip:
            trip, n = trip + 1, n - per_trip
        return "past", past_block(trip, n // MOBA_HEADS), n % MOBA_HEADS, (n + lead) % SCORE_SLOTS

    def scores(kind, j, h, slot):
        lanes = slice(h * AUG_W, (h + 1) * AUG_W)
        if kind == "mem":
            keys, q_ref = kc_ref[0, :, lanes], qmem_ref
        else:
            keys, q_ref = kaug_ref[0, j, :, lanes], (qown_ref if kind == "own" else qaug_ref)
        s_ref[slot] = jnp.dot(keys, q_ref[h], preferred_element_type=F32)

    def probabilities(kind, j, h, slot):
        s = s_ref[slot]
        if kind == "past":
            m_old = m_ref[h]
            m_new = jnp.maximum(m_old, jnp.max(s, axis=0, keepdims=True))
            alpha_ref[slot % PROB_SLOTS] = jnp.exp2(m_old - m_new)
        else:
            if kind == "own":
                s = jnp.where(causal, s, NEG_BIG)
            m_new = jnp.max(s, axis=0, keepdims=True)
        p_ref[slot % PROB_SLOTS] = jnp.exp2(s - m_new).astype(BF16)
        if kind != "mem":
            m_ref[h] = m_new

    def accumulate(kind, j, h, slot):
        rows = slice(h * VT_ROWS, (h + 1) * VT_ROWS)
        p = p_ref[slot % PROB_SLOTS]
        if kind == "mem":
            accm_ref[h] = jnp.dot(vct_ref[0, rows, :], p, preferred_element_type=F32)
        else:
            pv = jnp.dot(vt_ref[0, j, rows, :], p, preferred_element_type=F32)
            acc_ref[h] = pv if kind == "own" else alpha_ref[slot % PROB_SLOTS] * acc_ref[h] + pv

    for h in range(MOBA_HEADS):
        build_query_operands(h)
        build_block_mask(h)
    for n in range(-lead, -lead + SCORE_LOOKAHEAD):
        scores(*stream_item(0, n))
    for n in range(-lead, -lead + PROB_LOOKAHEAD):
        probabilities(*stream_item(0, n))
    for n in range(-lead, 0):
        scores(*stream_item(0, n + SCORE_LOOKAHEAD))
        probabilities(*stream_item(0, n + PROB_LOOKAHEAD))
        accumulate(*stream_item(0, n))

    def past_trip(trip, carry):
        for n in range(per_trip):
            scores(*stream_item(trip, n + SCORE_LOOKAHEAD))
            probabilities(*stream_item(trip, n + PROB_LOOKAHEAD))
            accumulate(*stream_item(trip, n))
        return carry

    lax.fori_loop(0, (qi + PAST_UNROLL - 1) // PAST_UNROLL, past_trip, 0)

    for ref, t_ref, out_ref in ((acc_ref, ot_ref, o_ref), (accm_ref, otm_ref, om_ref)):
        for h in range(MOBA_HEADS):
            t_ref[h * HEAD_DIM:(h + 1) * HEAD_DIM, :] = (ref[h, 0:HEAD_DIM, :]
                                                         / ref[h, HEAD_DIM:HEAD_DIM + 1, :])
        out_ref[...] = t_ref[...].T.astype(out_ref.dtype)


def _moba(qt, cqt, kaug, vt, kmean, kc, vct, batch, seq):
    nblk = seq // MOBA_BLOCK
    width = MOBA_HEADS * HEAD_DIM
    aug = MOBA_HEADS * AUG_W
    vt_rows = MOBA_HEADS * VT_ROWS
    mlen = vct.shape[2]
    assert mlen == MOBA_BLOCK and MEM_HEADS == MOBA_HEADS, "memory items reuse the key-block buffers"
    qt = qt.reshape(batch, nblk, width, MOBA_BLOCK)
    cqt = cqt.reshape(batch, nblk, width, MOBA_BLOCK)
    kaug = kaug.reshape(batch, nblk, MOBA_BLOCK, aug)
    vt = vt.reshape(batch, nblk, vt_rows, MOBA_BLOCK)
    kc = kc.reshape(batch, mlen, aug)
    resident = 2 * (kaug.size // batch + vt.size // batch) * 2
    qblock = pl.BlockSpec((1, 1, width, MOBA_BLOCK), lambda b, i: (b, i, 0, 0))
    out_block = pl.BlockSpec((MOBA_BLOCK, width), lambda b, i: (b * nblk + i, 0))
    out_shape = jax.ShapeDtypeStruct((batch * seq, width), BF16)
    return pl.pallas_call(
        _moba_kernel,
        grid=(batch, nblk),
        in_specs=[qblock, qblock,
                  pl.BlockSpec((1, nblk, MOBA_BLOCK, aug), lambda b, i: (b, 0, 0, 0)),
                  pl.BlockSpec((1, nblk, vt_rows, MOBA_BLOCK), lambda b, i: (b, 0, 0, 0)),
                  pl.BlockSpec((1, nblk, aug), lambda b, i: (b, 0, 0)),
                  pl.BlockSpec((1, mlen, aug), lambda b, i: (b, 0, 0)),
                  pl.BlockSpec((1, vt_rows, mlen), lambda b, i: (b, 0, 0))],
        out_specs=(out_block, out_block),
        out_shape=(out_shape, out_shape),
        scratch_shapes=[pltpu.VMEM((MOBA_HEADS, AUG_W, MOBA_BLOCK), BF16),
                        pltpu.VMEM((MOBA_HEADS, AUG_W, MOBA_BLOCK), BF16),
                        pltpu.VMEM((MOBA_HEADS, AUG_W, MOBA_BLOCK), BF16),
                        pltpu.VMEM((SCORE_SLOTS, MOBA_BLOCK, MOBA_BLOCK), F32),
                        pltpu.VMEM((PROB_SLOTS, MOBA_BLOCK, MOBA_BLOCK), BF16),
                        pltpu.VMEM((PROB_SLOTS, 1, MOBA_BLOCK), F32),
                        pltpu.VMEM((MOBA_HEADS, 1, MOBA_BLOCK), F32),
                        pltpu.VMEM((MOBA_HEADS, VT_ROWS, MOBA_BLOCK), F32),
                        pltpu.VMEM((MOBA_HEADS, VT_ROWS, MOBA_BLOCK), F32),
                        pltpu.VMEM((width, MOBA_BLOCK), F32),
                        pltpu.VMEM((width, MOBA_BLOCK), F32)],
        compiler_params=pltpu.CompilerParams(dimension_semantics=("arbitrary", "arbitrary"),
                                             vmem_limit_bytes=_vmem_limit(resident)),
        name="moba",
    )(qt, cqt, kaug, vt, kmean, kc, vct)


def _block_diag_ones(group):
    idx = jnp.arange(V7X_MXU_DIM) // group
    return (idx[:, None] == idx[None, :]).astype(BF16)


def _pack_block_diag(w):
    nb, bw, _ = w.shape
    per = V7X_MXU_DIM // bw
    w = w.reshape(nb // per, per, bw, bw)
    eye = jnp.eye(per, dtype=w.dtype)
    return jnp.einsum("gpij,pq->gpiqj", w, eye).reshape(nb // per, V7X_MXU_DIM, V7X_MXU_DIM)


def _pad_heads(w, heads):
    lead = w.shape[:-1]
    w = w.reshape(lead + (heads, HEAD_DIM))
    w = jnp.pad(w, [(0, 0)] * len(lead) + [(0, 0), (0, AUG_W - HEAD_DIM)])
    return w.reshape(lead + (heads * AUG_W,))


def _layer(x, mem, ffn1_norm, ffn1_w_in, ffn1_w_out, mix_norm, mem_norm, w_in,
           lru_conv_w, lru_conv_b, lru_a_w, lru_a_b, lru_x_w, lru_x_b, lru_lambda,
           moba_q_norm, moba_k_norm, mem_w_kv, mem_q_norm, mem_k_norm, w_out,
           ffn2_norm, ffn2_w_in, ffn2_w_out):
    batch, seq, d = x.shape
    n = batch * seq
    n_lru = lru_lambda.shape[0]
    wq = MOBA_HEADS * HEAD_DIM
    wc = MEM_HEADS * HEAD_DIM
    row = lambda v: v.reshape(1, -1).astype(F32)

    bd64 = _block_diag_ones(HEAD_DIM)
    bd128 = _block_diag_ones(AUG_W)

    w_in_b = w_in.astype(BF16)
    o = 2 * n_lru
    w_lru = w_in_b[:, :o]
    w_k = _pad_heads(w_in_b[:, o + wq:o + 2 * wq], MOBA_HEADS)
    w_qvc = jnp.concatenate([w_in_b[:, o:o + wq], w_in_b[:, o + 2 * wq:]], axis=1)
    gq = row(jnp.tile(moba_q_norm, MOBA_HEADS))
    gk = row(_pad_heads(jnp.tile(moba_k_norm, MOBA_HEADS), MOBA_HEADS))
    gc = row(jnp.tile(mem_q_norm, MEM_HEADS))
    gck = row(_pad_heads(jnp.tile(mem_k_norm, MEM_HEADS), MEM_HEADS))
    w_kv_b = mem_w_kv.astype(BF16)
    w_ck = _pad_heads(w_kv_b[:, :wc], MEM_HEADS)
    w_cv = w_kv_b[:, wc:]
    slopes = 2.0 ** (-8.0 * jnp.arange(1, MOBA_HEADS + 1, dtype=F32) / MOBA_HEADS)
    lane = jnp.arange(AUG_W)
    is_alibi = (lane >= AUG_ALIBI) & (lane < AUG_ALIBI + 3)
    alibi = row(jnp.where(is_alibi[None, :], slopes[:, None] * LOG2E, 0.0))

    x2d = x.reshape(n, d)
    kc, vct = _mem_kv(mem.reshape(-1, d), row(mem_norm), w_ck, w_cv, bd128, gck, batch)

    x1 = _ffn(x2d, row(ffn1_norm), ffn1_w_in, ffn1_w_out)

    lru_x, lru_g, qt, kaug, vt, kmean, cqt = _in_proj(
        x1, row(mix_norm), w_lru, w_qvc, w_k, gq, gk, gc, bd64, bd128, alibi, seq)
    kmean = kmean[:, :TOKEN_TILE // MOBA_BLOCK, :].reshape(batch, seq // MOBA_BLOCK, -1)

    y_lru = _lru(lru_x, lru_g, lru_conv_w.astype(F32), row(lru_conv_b),
                 _pack_block_diag(lru_a_w).astype(BF16), row(lru_a_b),
                 _pack_block_diag(lru_x_w).astype(BF16), row(lru_x_b), row(lru_lambda), batch, seq)
    y_moba, y_mem = _moba(qt, cqt, kaug, vt, kmean, kc, vct, batch, seq)

    out = _out_ffn(x1, y_lru, y_moba, y_mem, w_out.astype(BF16), row(ffn2_norm),
                   ffn2_w_in, ffn2_w_out)
    return out.reshape(batch, seq, d)


def kernel(x, mem, ffn1_norm, ffn1_w_in, ffn1_w_out, mix_norm, mem_norm, w_in, lru_conv_w, lru_conv_b,
           lru_a_w, lru_a_b, lru_x_w, lru_x_b, lru_lambda, moba_q_norm, moba_k_norm, mem_w_kv,
           mem_q_norm, mem_k_norm, w_out, ffn2_norm, ffn2_w_in, ffn2_w_out):
    params = (ffn1_norm, ffn1_w_in, ffn1_w_out, mix_norm, mem_norm, w_in, lru_conv_w, lru_conv_b,
              lru_a_w, lru_a_b, lru_x_w, lru_x_b, lru_lambda, moba_q_norm, moba_k_norm, mem_w_kv,
              mem_q_norm, mem_k_norm, w_out, ffn2_norm, ffn2_w_in, ffn2_w_out)
    for layer in range(ffn1_norm.shape[0]):
        x = _layer(x, mem, *(p[layer] for p in params))
    return x
```

```python
import functools
import math

import jax
import jax.numpy as jnp
from jax import lax
from jax.experimental import pallas as pl
from jax.experimental.pallas import tpu as pltpu

F32 = jnp.float32
BF16 = jnp.bfloat16

HEAD_DIM = 64
CONV_W = 4
LRU_C = 8.0
MOBA_HEADS = 4
MOBA_BLOCK = 256
MOBA_TOPK = 3
MEM_HEADS = 4
NORM_EPS = 1e-6
LOG2E = 1.4426950408889634
NEG_BIG = -1e30

V7X_MXU_DIM = 256
LANES = 128
SUBLANES = 8
V7X_VMEM_BYTES = 64 * 1024 * 1024

AUG_W = 128
AUG_ALIBI = 64
AUG_MASK = 80
assert AUG_W & (AUG_W - 1) == 0 and MOBA_BLOCK & (MOBA_BLOCK - 1) == 0
VT_ROWS = HEAD_DIM + 16
SCORE_LOOKAHEAD = 5
PAST_UNROLL = 4
SCORE_SLOTS = 8
assert MOBA_HEADS <= SCORE_LOOKAHEAD < SCORE_SLOTS and (PAST_UNROLL * MOBA_HEADS) % SCORE_SLOTS == 0
assert (MOBA_HEADS + MEM_HEADS) % SCORE_SLOTS == 0
PROB_SLOTS = 4
PROB_LOOKAHEAD = 1
assert PROB_LOOKAHEAD < SCORE_LOOKAHEAD

TOKEN_TILE = 512
LRU_TILE = 512
FFN_CHUNK = 256


def _vmem_limit(resident_bytes):
    return int(min(V7X_VMEM_BYTES - 8 * 1024 * 1024, resident_bytes + 24 * 1024 * 1024))


def _const_spec(shape):
    n = len(shape)
    return pl.BlockSpec(shape, lambda *_: (0,) * n, pipeline_mode=pl.Buffered(1))


def _sigmoid(x):
    return 0.5 * jnp.tanh(0.5 * x) + 0.5


def _rms_norm(x, g):
    ms = jnp.mean(x * x, axis=-1, keepdims=True)
    return x * lax.rsqrt(ms + NORM_EPS) * g


def _group_rms_norm(u, bd, g, group):
    sq = u * u
    hi = sq.astype(BF16)
    lo = (sq - hi.astype(F32)).astype(BF16)
    parts = []
    for c in range(u.shape[1] // V7X_MXU_DIM):
        sl = slice(c * V7X_MXU_DIM, (c + 1) * V7X_MXU_DIM)
        parts.append(jnp.dot(hi[:, sl], bd, preferred_element_type=F32)
                     + jnp.dot(lo[:, sl], bd, preferred_element_type=F32))
    ss = parts[0] if len(parts) == 1 else jnp.concatenate(parts, axis=1)
    return u * lax.rsqrt(ss * (1.0 / group) + NORM_EPS) * g


def _swiglu_half_step(x, g_ref, w_in_ref, w_out_ref, act_ref):
    d_ff = w_out_ref.shape[0]
    xn = _rms_norm(x, g_ref[...]).astype(BF16)
    for c in range(d_ff // FFN_CHUNK):
        lo = c * FFN_CHUNK
        a = jnp.dot(xn, w_in_ref[:, lo:lo + FFN_CHUNK].astype(BF16), preferred_element_type=F32)
        b = jnp.dot(xn, w_in_ref[:, d_ff + lo:d_ff + lo + FFN_CHUNK].astype(BF16),
                    preferred_element_type=F32)
        act_ref[:, lo:lo + FFN_CHUNK] = (a * _sigmoid(a) * b).astype(BF16)
    y = jnp.dot(act_ref[...], w_out_ref[...].astype(BF16), preferred_element_type=F32)
    return x + 0.5 * y


def _ffn_kernel(x_ref, g_ref, w_in_ref, w_out_ref, o_ref, act_ref):
    o_ref[...] = _swiglu_half_step(x_ref[...], g_ref, w_in_ref, w_out_ref, act_ref)


def _ffn(x, g, w_in, w_out):
    n, d = x.shape
    d_ff = w_out.shape[0]
    tm = TOKEN_TILE
    resident = sum(w.size * w.dtype.itemsize for w in (w_in, w_out))
    return pl.pallas_call(
        _ffn_kernel,
        grid=(n // tm,),
        in_specs=[pl.BlockSpec((tm, d), lambda i: (i, 0)),
                  _const_spec((1, d)), _const_spec(w_in.shape), _const_spec(w_out.shape)],
        out_specs=pl.BlockSpec((tm, d), lambda i: (i, 0)),
        out_shape=jax.ShapeDtypeStruct((n, d), F32),
        scratch_shapes=[pltpu.VMEM((tm, d_ff), BF16)],
        compiler_params=pltpu.CompilerParams(dimension_semantics=("arbitrary",),
                                             vmem_limit_bytes=_vmem_limit(resident)),
        name="ffn1",
    )(x, g, w_in, w_out)


def _out_ffn_kernel(x_ref, ylru_ref, ymoba_ref, ymem_ref, wo_ref, g_ref, w_in_ref, w_out_ref,
                    o_ref, act_ref):
    n_lru = ylru_ref.shape[1]
    n_moba = ymoba_ref.shape[1]
    x2 = (x_ref[...]
          + jnp.dot(ylru_ref[...], wo_ref[0:n_lru, :], preferred_element_type=F32)
          + jnp.dot(ymoba_ref[...], wo_ref[n_lru:n_lru + n_moba, :], preferred_element_type=F32)
          + jnp.dot(ymem_ref[...], wo_ref[n_lru + n_moba:, :], preferred_element_type=F32))
    o_ref[...] = _swiglu_half_step(x2, g_ref, w_in_ref, w_out_ref, act_ref)


def _out_ffn(x1, y_lru, y_moba, y_mem, w_o, g, w_in, w_out):
    n, d = x1.shape
    d_ff = w_out.shape[0]
    tm = TOKEN_TILE
    resident = sum(w.size * w.dtype.itemsize for w in (w_in, w_out, w_o))

    def tile(w):
        return pl.BlockSpec((tm, w), lambda i: (i, 0))

    return pl.pallas_call(
        _out_ffn_kernel,
        grid=(n // tm,),
        in_specs=[tile(d), tile(y_lru.shape[1]), tile(y_moba.shape[1]), tile(y_mem.shape[1]),
                  _const_spec(w_o.shape), _const_spec((1, d)),
                  _const_spec(w_in.shape), _const_spec(w_out.shape)],
        out_specs=tile(d),
        out_shape=jax.ShapeDtypeStruct((n, d), F32),
        scratch_shapes=[pltpu.VMEM((tm, d_ff), BF16)],
        compiler_params=pltpu.CompilerParams(dimension_semantics=("arbitrary",),
                                             vmem_limit_bytes=_vmem_limit(resident)),
        name="out_ffn2",
    )(x1, y_lru, y_moba, y_mem, w_o, g, w_in, w_out)


def _mem_kv_kernel(mem_ref, g_ref, wk_ref, wv_ref, bd_ref, gk_ref, kc_ref, vct_ref):
    batch, _, mlen = vct_ref.shape
    mn = _rms_norm(mem_ref[...], g_ref[...]).astype(BF16)
    k = jnp.dot(mn, wk_ref[...], preferred_element_type=F32)
    kc_ref[...] = _group_rms_norm(k, bd_ref[...], gk_ref[...], HEAD_DIM).astype(BF16)
    v = jnp.dot(mn, wv_ref[...], preferred_element_type=F32)
    ones = jnp.ones((VT_ROWS - HEAD_DIM, mlen), BF16)
    for b in range(batch):
        vt = v[b * mlen:(b + 1) * mlen, :].T.astype(BF16)
        for h in range(MEM_HEADS):
            vct_ref[b, h * VT_ROWS:h * VT_ROWS + HEAD_DIM, :] = vt[h * HEAD_DIM:(h + 1) * HEAD_DIM, :]
            vct_ref[b, h * VT_ROWS + HEAD_DIM:(h + 1) * VT_ROWS, :] = ones


def _mem_kv(mem2d, g, w_k, w_v, bd128, gk, batch):
    m = mem2d.shape[0]
    return pl.pallas_call(
        _mem_kv_kernel,
        out_shape=(jax.ShapeDtypeStruct((m, w_k.shape[1]), BF16),
                   jax.ShapeDtypeStruct((batch, MEM_HEADS * VT_ROWS, m // batch), BF16)),
        name="mem_kv",
    )(mem2d, g, w_k, w_v, bd128, gk)


def _in_proj_kernel(x_ref, g_ref, w_lru_ref, w_qvc_ref, w_k_ref,
                    gq_ref, gk_ref, gc_ref, bd64_ref, bd128_ref, alibi_ref,
                    lrux_ref, lrug_ref, qt_ref, kaug_ref, vt_ref, kmean_ref, cqt_ref, *, seq):
    tm = x_ref.shape[0]
    n_lru = lrux_ref.shape[1]
    xn = _rms_norm(x_ref[...], g_ref[...]).astype(BF16)

    lrux_ref[...] = jnp.dot(xn, w_lru_ref[:, :n_lru], preferred_element_type=F32)
    lrug_ref[...] = jnp.dot(xn, w_lru_ref[:, n_lru:], preferred_element_type=F32)

    wq = qt_ref.shape[1]
    qvc = jnp.dot(xn, w_qvc_ref[...], preferred_element_type=F32)

    cq = _group_rms_norm(qvc[:, 2 * wq:], bd64_ref[...], gc_ref[...], HEAD_DIM) * (HEAD_DIM ** -0.5 * LOG2E)
    for r in range(tm // MOBA_BLOCK):
        cqt_ref[r] = cq[r * MOBA_BLOCK:(r + 1) * MOBA_BLOCK, :].T.astype(BF16)

    q = _group_rms_norm(qvc[:, :wq], bd64_ref[...], gq_ref[...], HEAD_DIM)
    v = qvc[:, wq:2 * wq]
    ones = jnp.ones((VT_ROWS - HEAD_DIM, MOBA_BLOCK), BF16)
    for r in range(tm // MOBA_BLOCK):
        rows = slice(r * MOBA_BLOCK, (r + 1) * MOBA_BLOCK)
        qt_ref[r] = q[rows, :].T
        vt = v[rows, :].T.astype(BF16)
        for h in range(MOBA_HEADS):
            vt_ref[r, h * VT_ROWS:h * VT_ROWS + HEAD_DIM, :] = vt[h * HEAD_DIM:(h + 1) * HEAD_DIM, :]
            vt_ref[r, h * VT_ROWS + HEAD_DIM:(h + 1) * VT_ROWS, :] = ones

    k = jnp.dot(xn, w_k_ref[...], preferred_element_type=F32)
    k = _group_rms_norm(k, bd128_ref[...], gk_ref[...], HEAD_DIM)
    kmean_ref[...] = jnp.zeros(kmean_ref.shape, F32)
    for r in range(tm // MOBA_BLOCK):
        rows = slice(r * MOBA_BLOCK, (r + 1) * MOBA_BLOCK)
        kmean_ref[0, r:r + 1, :] = jnp.sum(k[rows, :], axis=0, keepdims=True) * (1.0 / MOBA_BLOCK)

    shape = k.shape
    row = lax.broadcasted_iota(jnp.int32, shape, 0)
    lane = lax.broadcasted_iota(jnp.int32, shape, 1) & (AUG_W - 1)
    t = (pl.program_id(0) % (seq // tm)) * tm + row
    bias = t.astype(F32) * alibi_ref[...]
    b_hi = bias.astype(BF16).astype(F32)
    b_mid = (bias - b_hi).astype(BF16).astype(F32)
    b_lo = bias - b_hi - b_mid
    pieces = jnp.where(lane == AUG_ALIBI, b_hi, jnp.where(lane == AUG_ALIBI + 1, b_mid, b_lo))
    onehot = jnp.where(lane - AUG_MASK == lax.shift_right_logical(t, MOBA_BLOCK.bit_length() - 1), 1.0, 0.0)
    is_alibi = (lane >= AUG_ALIBI) & (lane < AUG_ALIBI + 3)
    kaug_ref[...] = (k + jnp.where(is_alibi, pieces, onehot)).astype(BF16)


def _in_proj(x1, g, w_lru, w_qvc, w_k, gq, gk, gc, bd64, bd128, alibi, seq):
    n, d = x1.shape
    tm = TOKEN_TILE
    rb = tm // MOBA_BLOCK
    nblk = n // MOBA_BLOCK
    n_lru = w_lru.shape[1] // 2
    wq = w_qvc.shape[1] // 3
    wk = w_k.shape[1]
    consts = (g, w_lru, w_qvc, w_k, gq, gk, gc, bd64, bd128, alibi)
    resident = sum(c.size * c.dtype.itemsize for c in consts)
    return pl.pallas_call(
        functools.partial(_in_proj_kernel, seq=seq),
        grid=(n // tm,),
        in_specs=[pl.BlockSpec((tm, d), lambda i: (i, 0))] + [_const_spec(c.shape) for c in consts],
        out_specs=[pl.BlockSpec((tm, n_lru), lambda i: (i, 0)),
                   pl.BlockSpec((tm, n_lru), lambda i: (i, 0)),
                   pl.BlockSpec((rb, wq, MOBA_BLOCK), lambda i: (i, 0, 0)),
                   pl.BlockSpec((tm, wk), lambda i: (i, 0)),
                   pl.BlockSpec((rb, MOBA_HEADS * VT_ROWS, MOBA_BLOCK), lambda i: (i, 0, 0)),
                   pl.BlockSpec((1, 8, wk), lambda i: (i, 0, 0)),
                   pl.BlockSpec((rb, wq, MOBA_BLOCK), lambda i: (i, 0, 0))],
        out_shape=(jax.ShapeDtypeStruct((n, n_lru), F32),
                   jax.ShapeDtypeStruct((n, n_lru), F32),
                   jax.ShapeDtypeStruct((nblk, wq, MOBA_BLOCK), F32),
                   jax.ShapeDtypeStruct((n, wk), BF16),
                   jax.ShapeDtypeStruct((nblk, MOBA_HEADS * VT_ROWS, MOBA_BLOCK), BF16),
                   jax.ShapeDtypeStruct((n // tm, 8, wk), F32),
                   jax.ShapeDtypeStruct((nblk, wq, MOBA_BLOCK), BF16)),
        compiler_params=pltpu.CompilerParams(dimension_semantics=("arbitrary",),
                                             vmem_limit_bytes=_vmem_limit(resident)),
        name="in_proj",
    )(x1, *consts)


def _gelu_tanh(x):
    c = math.sqrt(2.0 / math.pi)
    return 0.5 * x * (1.0 + jnp.tanh(c * (x + 0.044715 * (x * x * x))))


def _lru_kernel(x_ref, g_ref, cw_ref, cb_ref, wa_ref, ba_ref, wx_ref, bx_ref, lam_ref,
                o_ref, xpad_ref, a_ref, b_ref, hl_ref, carry_ref):
    tl, c = x_ref.shape
    half = V7X_MXU_DIM
    ti = pl.program_id(1)

    @pl.when(ti == 0)
    def _():
        xpad_ref[0:8, :] = jnp.zeros((8, c), F32)
        carry_ref[...] = jnp.zeros(carry_ref.shape, F32)

    xpad_ref[8:8 + tl, :] = x_ref[...]
    n_grp = tl // SUBLANES
    sub1 = lax.broadcasted_iota(jnp.int32, (n_grp, SUBLANES, LANES), 1)
    parts = []
    for k in range(c // LANES):
        lanes = slice(k * LANES, (k + 1) * LANES)
        xg = xpad_ref[:, lanes].reshape(n_grp + 1, SUBLANES, LANES)
        acc = cb_ref[:, lanes] + cw_ref[CONV_W - 1:CONV_W, lanes] * xg[1:]
        for d in range(1, CONV_W):
            rolled = pltpu.roll(xg, d, axis=1)
            delayed = jnp.where(sub1 >= d, rolled[1:], rolled[:-1])
            acc = acc + cw_ref[CONV_W - 1 - d:CONV_W - d, lanes] * delayed
        parts.append(acc.reshape(tl, LANES))
    xb = jnp.concatenate(parts, axis=1)
    xpad_ref[0:8, :] = x_ref[tl - 8:tl, :]

    xbb = xb.astype(BF16)

    def gate_tanh(w_ref, bias_ref):
        z_half = jnp.concatenate(
            [jnp.dot(xbb[:, :half], w_ref[0], preferred_element_type=F32),
             jnp.dot(xbb[:, half:], w_ref[1], preferred_element_type=F32)], axis=1)
        return jnp.tanh(z_half + bias_ref[...])

    neg_lam = -lam_ref[...]
    softplus = jnp.maximum(neg_lam, 0.0) + jnp.log1p(jnp.exp(-jnp.abs(neg_lam)))
    half_rate = (-0.5 * LRU_C) * softplus
    log_a = half_rate * gate_tanh(wa_ref, ba_ref) + half_rate
    i = 0.5 * gate_tanh(wx_ref, bx_ref) + 0.5
    a = jnp.exp(log_a)
    one_minus_a2 = 1.0 - a * a
    mult = jnp.where(one_minus_a2 > 0.0, one_minus_a2 * lax.rsqrt(one_minus_a2), 0.0)
    row = lax.broadcasted_iota(jnp.int32, (tl, c), 0)
    start_row = jnp.where(ti == 0, 0, -1)
    mult = jnp.where(row == start_row, 1.0, mult)
    bt = mult * i * xb

    groups = tl // SUBLANES
    sub = lax.broadcasted_iota(jnp.int32, (groups, SUBLANES, LANES), 1)
    for k in range(c // LANES):
        lanes = slice(k * LANES, (k + 1) * LANES)
        a3 = a[:, lanes].reshape(groups, SUBLANES, LANES)
        b3 = bt[:, lanes].reshape(groups, SUBLANES, LANES)
        d = 1
        while d < SUBLANES:
            has_prev = sub >= d
            a_prev = pltpu.roll(a3, d, axis=1)
            b_prev = pltpu.roll(b3, d, axis=1)
            b3 = b3 + jnp.where(has_prev, a3, 0.0) * b_prev
            a3 = jnp.where(has_prev, a3 * a_prev, a3)
            d *= 2
        a_ref[:, lanes] = a3.reshape(tl, LANES)
        b_ref[:, lanes] = b3.reshape(tl, LANES)

    h = carry_ref[...]
    for grp in range(groups):
        rows = slice(grp * SUBLANES, (grp + 1) * SUBLANES)
        h_last = jnp.broadcast_to(h[SUBLANES - 1:SUBLANES, :], (SUBLANES, c))
        h = a_ref[rows, :] * h_last + b_ref[rows, :]
        hl_ref[rows, :] = h
    carry_ref[...] = h

    o_ref[...] = (hl_ref[...] * _gelu_tanh(g_ref[...])).astype(o_ref.dtype)


def _lru(lru_x, lru_g, cw, cb, wa, ba, wx, bx, lam, batch, seq):
    n, c = lru_x.shape
    tl = LRU_TILE
    nt = seq // tl
    consts = (cw, cb, wa, ba, wx, bx, lam)
    tile = pl.BlockSpec((tl, c), lambda b, t: (b * nt + t, 0))
    return pl.pallas_call(
        _lru_kernel,
        grid=(batch, nt),
        in_specs=[tile, tile] + [_const_spec(w.shape) for w in consts],
        out_specs=tile,
        out_shape=jax.ShapeDtypeStruct((n, c), BF16),
        scratch_shapes=[pltpu.VMEM((tl + SUBLANES, c), F32)]
        + [pltpu.VMEM((tl, c), F32)] * 3
        + [pltpu.VMEM((SUBLANES, c), F32)],
        compiler_params=pltpu.CompilerParams(dimension_semantics=("arbitrary", "arbitrary")),
        name="lru",
    )(lru_x, lru_g, *consts)


def _moba_kernel(qt_ref, cqt_ref, kaug_ref, vt_ref, kmean_ref, kc_ref, vct_ref, o_ref, om_ref,
                 qaug_ref, qown_ref, qmem_ref, s_ref, p_ref, alpha_ref, m_ref, acc_ref, accm_ref,
                 ot_ref, otm_ref):
    qi = pl.program_id(1)
    nblk = kmean_ref.shape[1]
    blk = MOBA_BLOCK

    def build_query_operands(h):
        qt = qt_ref[0, 0, h * HEAD_DIM:(h + 1) * HEAD_DIM, :]
        qs = (qt * (HEAD_DIM ** -0.5 * LOG2E)).astype(BF16)
        sub = lax.broadcasted_iota(jnp.int32, (AUG_MASK - AUG_ALIBI, blk), 0)
        ones = jnp.where(sub < 3, 1.0, 0.0).astype(BF16)
        for ref in (qaug_ref, qown_ref):
            ref[h, 0:HEAD_DIM, :] = qs
            ref[h, AUG_ALIBI:AUG_MASK, :] = ones
        qown_ref[h, AUG_MASK:, :] = jnp.zeros((AUG_W - AUG_MASK, blk), BF16)
        qmem_ref[h, 0:HEAD_DIM, :] = cqt_ref[0, 0, h * HEAD_DIM:(h + 1) * HEAD_DIM, :]
        qmem_ref[h, HEAD_DIM:, :] = jnp.zeros((AUG_W - HEAD_DIM, blk), BF16)

    def build_block_mask(h):
        blk_id = lax.broadcasted_iota(jnp.int32, (nblk, blk), 0).astype(F32)
        qt = qt_ref[0, 0, h * HEAD_DIM:(h + 1) * HEAD_DIM, :]
        gate = jnp.dot(kmean_ref[0, :, h * AUG_W:h * AUG_W + HEAD_DIM], qt,
                       preferred_element_type=F32, precision=lax.Precision.HIGHEST)
        gate = jnp.where(blk_id < qi.astype(F32), gate, -jnp.inf)
        keep = blk_id < 0.0
        for _ in range(MOBA_TOPK):
            best = jnp.max(gate, axis=0, keepdims=True)
            first = jnp.min(jnp.where(gate == best, blk_id, float(nblk)), axis=0, keepdims=True)
            pick = (blk_id == first) & (best > -jnp.inf)
            keep = keep | pick
            gate = jnp.where(pick, -jnp.inf, gate)
        qaug_ref[h, AUG_MASK:AUG_MASK + nblk, :] = jnp.where(keep, 0.0, NEG_BIG).astype(BF16)
        qaug_ref[h, AUG_MASK + nblk:, :] = jnp.zeros((AUG_W - AUG_MASK - nblk, blk), BF16)

    kk = lax.broadcasted_iota(jnp.int32, (blk, blk), 0)
    qq = lax.broadcasted_iota(jnp.int32, (blk, blk), 1)
    causal = kk <= qq

    per_trip = PAST_UNROLL * MOBA_HEADS
    lead = MOBA_HEADS + MEM_HEADS

    def past_block(trip, u):
        return jnp.minimum(trip * PAST_UNROLL + u, nblk - 1)

    def stream_item(trip, n):
        if n < 0:
            idx = n + lead
            if idx < MOBA_HEADS:
                return "own", qi, idx, idx % SCORE_SLOTS
            return "mem", 0, idx - MOBA_HEADS, idx % SCORE_SLOTS
        if n >= per_trip:
            trip, n = trip + 1, n - per_trip
        return "past", past_block(trip, n // MOBA_HEADS), n % MOBA_HEADS, (n + lead) % SCORE_SLOTS

    def scores(kind, j, h, slot):
        lanes = slice(h * AUG_W, (h + 1) * AUG_W)
        if kind == "mem":
            keys, q_ref = kc_ref[0, :, lanes], qmem_ref
        else:
            keys, q_ref = kaug_ref[0, j, :, lanes], (qown_ref if kind == "own" else qaug_ref)
        s_ref[slot] = jnp.dot(keys, q_ref[h], preferred_element_type=F32)

    def probabilities(kind, j, h, slot):
        s = s_ref[slot]
        if kind == "past":
            m_old = m_ref[h]
            m_new = jnp.maximum(m_old, jnp.max(s, axis=0, keepdims=True))
            alpha_ref[slot % PROB_SLOTS] = jnp.exp2(m_old - m_new)
        else:
            if kind == "own":
                s = jnp.where(causal, s, NEG_BIG)
            m_new = jnp.max(s, axis=0, keepdims=True)
        p_ref[slot % PROB_SLOTS] = jnp.exp2(s - m_new).astype(BF16)
        if kind != "mem":
            m_ref[h] = m_new

    def accumulate(kind, j, h, slot):
        rows = slice(h * VT_ROWS, (h + 1) * VT_ROWS)
        p = p_ref[slot % PROB_SLOTS]
        if kind == "mem":
            accm_ref[h] = jnp.dot(vct_ref[0, rows, :], p, preferred_element_type=F32)
        else:
            pv = jnp.dot(vt_ref[0, j, rows, :], p, preferred_element_type=F32)
            acc_ref[h] = pv if kind == "own" else alpha_ref[slot % PROB_SLOTS] * acc_ref[h] + pv

    for h in range(MOBA_HEADS):
        build_query_operands(h)
        build_block_mask(h)
    for n in range(-lead, -lead + SCORE_LOOKAHEAD):
        scores(*stream_item(0, n))
    for n in range(-lead, -lead + PROB_LOOKAHEAD):
        probabilities(*stream_item(0, n))
    for n in range(-lead, 0):
        scores(*stream_item(0, n + SCORE_LOOKAHEAD))
        probabilities(*stream_item(0, n + PROB_LOOKAHEAD))
        accumulate(*stream_item(0, n))

    def past_trip(trip, carry):
        for n in range(per_trip):
            scores(*stream_item(trip, n + SCORE_LOOKAHEAD))
            probabilities(*stream_item(trip, n + PROB_LOOKAHEAD))
            accumulate(*stream_item(trip, n))
        return carry

    lax.fori_loop(0, (qi + PAST_UNROLL - 1) // PAST_UNROLL, past_trip, 0)

    for ref, t_ref, out_ref in ((acc_ref, ot_ref, o_ref), (accm_ref, otm_ref, om_ref)):
        for h in range(MOBA_HEADS):
            t_ref[h * HEAD_DIM:(h + 1) * HEAD_DIM, :] = (ref[h, 0:HEAD_DIM, :]
                                                         / ref[h, HEAD_DIM:HEAD_DIM + 1, :])
        out_ref[...] = t_ref[...].T.astype(out_ref.dtype)


def _moba(qt, cqt, kaug, vt, kmean, kc, vct, batch, seq):
    nblk = seq // MOBA_BLOCK
    width = MOBA_HEADS * HEAD_DIM
    aug = MOBA_HEADS * AUG_W
    vt_rows = MOBA_HEADS * VT_ROWS
    mlen = vct.shape[2]
    assert mlen == MOBA_BLOCK and MEM_HEADS == MOBA_HEADS, "memory items reuse the key-block buffers"
    qt = qt.reshape(batch, nblk, width, MOBA_BLOCK)
    cqt = cqt.reshape(batch, nblk, width, MOBA_BLOCK)
    kaug = kaug.reshape(batch, nblk, MOBA_BLOCK, aug)
    vt = vt.reshape(batch, nblk, vt_rows, MOBA_BLOCK)
    kc = kc.reshape(batch, mlen, aug)
    resident = 2 * (kaug.size // batch + vt.size // batch) * 2
    qblock = pl.BlockSpec((1, 1, width, MOBA_BLOCK), lambda b, i: (b, i, 0, 0))
    out_block = pl.BlockSpec((MOBA_BLOCK, width), lambda b, i: (b * nblk + i, 0))
    out_shape = jax.ShapeDtypeStruct((batch * seq, width), BF16)
    return pl.pallas_call(
        _moba_kernel,
        grid=(batch, nblk),
        in_specs=[qblock, qblock,
                  pl.BlockSpec((1, nblk, MOBA_BLOCK, aug), lambda b, i: (b, 0, 0, 0)),
                  pl.BlockSpec((1, nblk, vt_rows, MOBA_BLOCK), lambda b, i: (b, 0, 0, 0)),
                  pl.BlockSpec((1, nblk, aug), lambda b, i: (b, 0, 0)),
                  pl.BlockSpec((1, mlen, aug), lambda b, i: (b, 0, 0)),
                  pl.BlockSpec((1, vt_rows, mlen), lambda b, i: (b, 0, 0))],
        out_specs=(out_block, out_block),
        out_shape=(out_shape, out_shape),
        scratch_shapes=[pltpu.VMEM((MOBA_HEADS, AUG_W, MOBA_BLOCK), BF16),
                        pltpu.VMEM((MOBA_HEADS, AUG_W, MOBA_BLOCK), BF16),
                        pltpu.VMEM((MOBA_HEADS, AUG_W, MOBA_BLOCK), BF16),
                        pltpu.VMEM((SCORE_SLOTS, MOBA_BLOCK, MOBA_BLOCK), F32),
                        pltpu.VMEM((PROB_SLOTS, MOBA_BLOCK, MOBA_BLOCK), BF16),
                        pltpu.VMEM((PROB_SLOTS, 1, MOBA_BLOCK), F32),
                        pltpu.VMEM((MOBA_HEADS, 1, MOBA_BLOCK), F32),
                        pltpu.VMEM((MOBA_HEADS, VT_ROWS, MOBA_BLOCK), F32),
                        pltpu.VMEM((MOBA_HEADS, VT_ROWS, MOBA_BLOCK), F32),
                        pltpu.VMEM((width, MOBA_BLOCK), F32),
                        pltpu.VMEM((width, MOBA_BLOCK), F32)],
        compiler_params=pltpu.CompilerParams(dimension_semantics=("arbitrary", "arbitrary"),
                                             vmem_limit_bytes=_vmem_limit(resident)),
        name="moba",
    )(qt, cqt, kaug, vt, kmean, kc, vct)


def _block_diag_ones(group):
    idx = jnp.arange(V7X_MXU_DIM) // group
    return (idx[:, None] == idx[None, :]).astype(BF16)


def _pack_block_diag(w):
    nb, bw, _ = w.shape
    per = V7X_MXU_DIM // bw
    w = w.reshape(nb // per, per, bw, bw)
    eye = jnp.eye(per, dtype=w.dtype)
    return jnp.einsum("gpij,pq->gpiqj", w, eye).reshape(nb // per, V7X_MXU_DIM, V7X_MXU_DIM)


def _pad_heads(w, heads):
    lead = w.shape[:-1]
    w = w.reshape(lead + (heads, HEAD_DIM))
    w = jnp.pad(w, [(0, 0)] * len(lead) + [(0, 0), (0, AUG_W - HEAD_DIM)])
    return w.reshape(lead + (heads * AUG_W,))


def _layer(x, mem, ffn1_norm, ffn1_w_in, ffn1_w_out, mix_norm, mem_norm, w_in,
           lru_conv_w, lru_conv_b, lru_a_w, lru_a_b, lru_x_w, lru_x_b, lru_lambda,
           moba_q_norm, moba_k_norm, mem_w_kv, mem_q_norm, mem_k_norm, w_out,
           ffn2_norm, ffn2_w_in, ffn2_w_out):
    batch, seq, d = x.shape
    n = batch * seq
    n_lru = lru_lambda.shape[0]
    wq = MOBA_HEADS * HEAD_DIM
    wc = MEM_HEADS * HEAD_DIM
    row = lambda v: v.reshape(1, -1).astype(F32)

    bd64 = _block_diag_ones(HEAD_DIM)
    bd128 = _block_diag_ones(AUG_W)

    w_in_b = w_in.astype(BF16)
    o = 2 * n_lru
    w_lru = w_in_b[:, :o]
    w_k = _pad_heads(w_in_b[:, o + wq:o + 2 * wq], MOBA_HEADS)
    w_qvc = jnp.concatenate([w_in_b[:, o:o + wq], w_in_b[:, o + 2 * wq:]], axis=1)
    gq = row(jnp.tile(moba_q_norm, MOBA_HEADS))
    gk = row(_pad_heads(jnp.tile(moba_k_norm, MOBA_HEADS), MOBA_HEADS))
    gc = row(jnp.tile(mem_q_norm, MEM_HEADS))
    gck = row(_pad_heads(jnp.tile(mem_k_norm, MEM_HEADS), MEM_HEADS))
    w_kv_b = mem_w_kv.astype(BF16)
    w_ck = _pad_heads(w_kv_b[:, :wc], MEM_HEADS)
    w_cv = w_kv_b[:, wc:]
    slopes = 2.0 ** (-8.0 * jnp.arange(1, MOBA_HEADS + 1, dtype=F32) / MOBA_HEADS)
    lane = jnp.arange(AUG_W)
    is_alibi = (lane >= AUG_ALIBI) & (lane < AUG_ALIBI + 3)
    alibi = row(jnp.where(is_alibi[None, :], slopes[:, None] * LOG2E, 0.0))

    x2d = x.reshape(n, d)
    kc, vct = _mem_kv(mem.reshape(-1, d), row(mem_norm), w_ck, w_cv, bd128, gck, batch)

    x1 = _ffn(x2d, row(ffn1_norm), ffn1_w_in, ffn1_w_out)

    lru_x, lru_g, qt, kaug, vt, kmean, cqt = _in_proj(
        x1, row(mix_norm), w_lru, w_qvc, w_k, gq, gk, gc, bd64, bd128, alibi, seq)
    kmean = kmean[:, :TOKEN_TILE // MOBA_BLOCK, :].reshape(batch, seq // MOBA_BLOCK, -1)

    y_lru = _lru(lru_x, lru_g, lru_conv_w.astype(F32), row(lru_conv_b),
                 _pack_block_diag(0.5 * lru_a_w).astype(BF16), row(0.5 * lru_a_b),
                 _pack_block_diag(0.5 * lru_x_w).astype(BF16), row(0.5 * lru_x_b),
                 row(lru_lambda), batch, seq)
    y_moba, y_mem = _moba(qt, cqt, kaug, vt, kmean, kc, vct, batch, seq)

    out = _out_ffn(x1, y_lru, y_moba, y_mem, w_out.astype(BF16), row(ffn2_norm),
                   ffn2_w_in, ffn2_w_out)
    return out.reshape(batch, seq, d)


def kernel(x, mem, ffn1_norm, ffn1_w_in, ffn1_w_out, mix_norm, mem_norm, w_in, lru_conv_w, lru_conv_b,
           lru_a_w, lru_a_b, lru_x_w, lru_x_b, lru_lambda, moba_q_norm, moba_k_norm, mem_w_kv,
           mem_q_norm, mem_k_norm, w_out, ffn2_norm, ffn2_w_in, ffn2_w_out):
    params = (ffn1_norm, ffn1_w_in, ffn1_w_out, mix_norm, mem_norm, w_in, lru_conv_w, lru_conv_b,
              lru_a_w, lru_a_b, lru_x_w, lru_x_b, lru_lambda, moba_q_norm, moba_k_norm, mem_w_kv,
              mem_q_norm, mem_k_norm, w_out, ffn2_norm, ffn2_w_in, ffn2_w_out)
    for layer in range(ffn1_norm.shape[0]):
        x = _layer(x, mem, *(p[layer] for p in params))
    return x
```

```python
import functools
import math

import jax
import jax.numpy as jnp
from jax import lax
from jax.experimental import pallas as pl
from jax.experimental.pallas import tpu as pltpu

F32 = jnp.float32
BF16 = jnp.bfloat16

HEAD_DIM = 64
CONV_W = 4
LRU_C = 8.0
MOBA_HEADS = 4
MOBA_BLOCK = 256
MOBA_TOPK = 3
MEM_HEADS = 4
NORM_EPS = 1e-6
LOG2E = 1.4426950408889634
NEG_BIG = -1e30

V7X_MXU_DIM = 256
LANES = 128
SUBLANES = 8
V7X_VMEM_BYTES = 64 * 1024 * 1024

AUG_W = 128
AUG_ALIBI = 64
AUG_MASK = 80
assert AUG_W & (AUG_W - 1) == 0 and MOBA_BLOCK & (MOBA_BLOCK - 1) == 0
VT_ROWS = HEAD_DIM + 16
SCORE_LOOKAHEAD = 5
PAST_UNROLL = 4
SCORE_SLOTS = 8
assert MOBA_HEADS <= SCORE_LOOKAHEAD < SCORE_SLOTS and (PAST_UNROLL * MOBA_HEADS) % SCORE_SLOTS == 0
assert (MOBA_HEADS + MEM_HEADS) % SCORE_SLOTS == 0
PROB_SLOTS = 4
PROB_LOOKAHEAD = 1
assert PROB_LOOKAHEAD < SCORE_LOOKAHEAD

TOKEN_TILE = 512
IN_PROJ_TILE = 1024
LRU_TILE = 512
FFN_CHUNK = 256


def _vmem_limit(resident_bytes):
    return int(min(V7X_VMEM_BYTES - 8 * 1024 * 1024, resident_bytes + 24 * 1024 * 1024))


def _const_spec(shape):
    n = len(shape)
    return pl.BlockSpec(shape, lambda *_: (0,) * n, pipeline_mode=pl.Buffered(1))


def _sigmoid(x):
    return 0.5 * jnp.tanh(0.5 * x) + 0.5


def _rms_norm(x, g):
    ms = jnp.mean(x * x, axis=-1, keepdims=True)
    return x * lax.rsqrt(ms + NORM_EPS) * g


def _group_rms_norm(u, bd, g, group):
    sq = u * u
    hi = sq.astype(BF16)
    lo = (sq - hi.astype(F32)).astype(BF16)
    parts = []
    for c in range(u.shape[1] // V7X_MXU_DIM):
        sl = slice(c * V7X_MXU_DIM, (c + 1) * V7X_MXU_DIM)
        parts.append(jnp.dot(hi[:, sl], bd, preferred_element_type=F32)
                     + jnp.dot(lo[:, sl], bd, preferred_element_type=F32))
    ss = parts[0] if len(parts) == 1 else jnp.concatenate(parts, axis=1)
    return u * lax.rsqrt(ss * (1.0 / group) + NORM_EPS) * g


def _swiglu_half_step(x, g_ref, w_in_ref, w_out_ref, act_ref):
    d_ff = w_out_ref.shape[0]
    xn = _rms_norm(x, g_ref[...]).astype(BF16)
    for c in range(d_ff // FFN_CHUNK):
        lo = c * FFN_CHUNK
        a = jnp.dot(xn, w_in_ref[:, lo:lo + FFN_CHUNK].astype(BF16), preferred_element_type=F32)
        b = jnp.dot(xn, w_in_ref[:, d_ff + lo:d_ff + lo + FFN_CHUNK].astype(BF16),
                    preferred_element_type=F32)
        act_ref[:, lo:lo + FFN_CHUNK] = (a * _sigmoid(a) * b).astype(BF16)
    y = jnp.dot(act_ref[...], w_out_ref[...].astype(BF16), preferred_element_type=F32)
    return x + 0.5 * y


def _ffn_kernel(x_ref, g_ref, w_in_ref, w_out_ref, o_ref, act_ref):
    o_ref[...] = _swiglu_half_step(x_ref[...], g_ref, w_in_ref, w_out_ref, act_ref)


def _ffn(x, g, w_in, w_out):
    n, d = x.shape
    d_ff = w_out.shape[0]
    tm = TOKEN_TILE
    resident = sum(w.size * w.dtype.itemsize for w in (w_in, w_out))
    return pl.pallas_call(
        _ffn_kernel,
        grid=(n // tm,),
        in_specs=[pl.BlockSpec((tm, d), lambda i: (i, 0)),
                  _const_spec((1, d)), _const_spec(w_in.shape), _const_spec(w_out.shape)],
        out_specs=pl.BlockSpec((tm, d), lambda i: (i, 0)),
        out_shape=jax.ShapeDtypeStruct((n, d), F32),
        scratch_shapes=[pltpu.VMEM((tm, d_ff), BF16)],
        compiler_params=pltpu.CompilerParams(dimension_semantics=("arbitrary",),
                                             vmem_limit_bytes=_vmem_limit(resident)),
        name="ffn1",
    )(x, g, w_in, w_out)


def _out_ffn_kernel(x_ref, ylru_ref, ymoba_ref, ymem_ref, wo_ref, g_ref, w_in_ref, w_out_ref,
                    o_ref, act_ref):
    n_lru = ylru_ref.shape[1]
    n_moba = ymoba_ref.shape[1]
    x2 = (x_ref[...]
          + jnp.dot(ylru_ref[...], wo_ref[0:n_lru, :], preferred_element_type=F32)
          + jnp.dot(ymoba_ref[...], wo_ref[n_lru:n_lru + n_moba, :], preferred_element_type=F32)
          + jnp.dot(ymem_ref[...], wo_ref[n_lru + n_moba:, :], preferred_element_type=F32))
    o_ref[...] = _swiglu_half_step(x2, g_ref, w_in_ref, w_out_ref, act_ref)


def _out_ffn(x1, y_lru, y_moba, y_mem, w_o, g, w_in, w_out):
    n, d = x1.shape
    d_ff = w_out.shape[0]
    tm = TOKEN_TILE
    resident = sum(w.size * w.dtype.itemsize for w in (w_in, w_out, w_o))

    def tile(w):
        return pl.BlockSpec((tm, w), lambda i: (i, 0))

    return pl.pallas_call(
        _out_ffn_kernel,
        grid=(n // tm,),
        in_specs=[tile(d), tile(y_lru.shape[1]), tile(y_moba.shape[1]), tile(y_mem.shape[1]),
                  _const_spec(w_o.shape), _const_spec((1, d)),
                  _const_spec(w_in.shape), _const_spec(w_out.shape)],
        out_specs=tile(d),
        out_shape=jax.ShapeDtypeStruct((n, d), F32),
        scratch_shapes=[pltpu.VMEM((tm, d_ff), BF16)],
        compiler_params=pltpu.CompilerParams(dimension_semantics=("arbitrary",),
                                             vmem_limit_bytes=_vmem_limit(resident)),
        name="out_ffn2",
    )(x1, y_lru, y_moba, y_mem, w_o, g, w_in, w_out)


def _mem_kv_kernel(mem_ref, g_ref, wk_ref, wv_ref, bd_ref, gk_ref, kc_ref, vct_ref):
    batch, _, mlen = vct_ref.shape
    mn = _rms_norm(mem_ref[...], g_ref[...]).astype(BF16)
    k = jnp.dot(mn, wk_ref[...], preferred_element_type=F32)
    kc_ref[...] = _group_rms_norm(k, bd_ref[...], gk_ref[...], HEAD_DIM).astype(BF16)
    v = jnp.dot(mn, wv_ref[...], preferred_element_type=F32)
    ones = jnp.ones((VT_ROWS - HEAD_DIM, mlen), BF16)
    for b in range(batch):
        vt = v[b * mlen:(b + 1) * mlen, :].T.astype(BF16)
        for h in range(MEM_HEADS):
            vct_ref[b, h * VT_ROWS:h * VT_ROWS + HEAD_DIM, :] = vt[h * HEAD_DIM:(h + 1) * HEAD_DIM, :]
            vct_ref[b, h * VT_ROWS + HEAD_DIM:(h + 1) * VT_ROWS, :] = ones


def _mem_kv(mem2d, g, w_k, w_v, bd128, gk, batch):
    m = mem2d.shape[0]
    return pl.pallas_call(
        _mem_kv_kernel,
        out_shape=(jax.ShapeDtypeStruct((m, w_k.shape[1]), BF16),
                   jax.ShapeDtypeStruct((batch, MEM_HEADS * VT_ROWS, m // batch), BF16)),
        name="mem_kv",
    )(mem2d, g, w_k, w_v, bd128, gk)


def _in_proj_kernel(x_ref, g_ref, w_lru_ref, w_qvc_ref, w_k_ref,
                    gq_ref, gk_ref, gc_ref, bd64_ref, bd128_ref, alibi_ref,
                    lrux_ref, lrug_ref, qt_ref, kaug_ref, vt_ref, kmean_ref, cqt_ref, *, seq):
    tm = x_ref.shape[0]
    n_lru = lrux_ref.shape[1]
    xn = _rms_norm(x_ref[...], g_ref[...]).astype(BF16)

    lrux_ref[...] = jnp.dot(xn, w_lru_ref[:, :n_lru], preferred_element_type=F32)
    lrug_ref[...] = jnp.dot(xn, w_lru_ref[:, n_lru:], preferred_element_type=F32)

    wq = qt_ref.shape[1]
    qvc = jnp.dot(xn, w_qvc_ref[...], preferred_element_type=F32)

    cq = _group_rms_norm(qvc[:, 2 * wq:], bd64_ref[...], gc_ref[...], HEAD_DIM) * (HEAD_DIM ** -0.5 * LOG2E)
    for r in range(tm // MOBA_BLOCK):
        cqt_ref[r] = cq[r * MOBA_BLOCK:(r + 1) * MOBA_BLOCK, :].T.astype(BF16)

    q = _group_rms_norm(qvc[:, :wq], bd64_ref[...], gq_ref[...], HEAD_DIM)
    v = qvc[:, wq:2 * wq]
    ones = jnp.ones((VT_ROWS - HEAD_DIM, MOBA_BLOCK), BF16)
    for r in range(tm // MOBA_BLOCK):
        rows = slice(r * MOBA_BLOCK, (r + 1) * MOBA_BLOCK)
        qt_ref[r] = q[rows, :].T
        vt = v[rows, :].T.astype(BF16)
        for h in range(MOBA_HEADS):
            vt_ref[r, h * VT_ROWS:h * VT_ROWS + HEAD_DIM, :] = vt[h * HEAD_DIM:(h + 1) * HEAD_DIM, :]
            vt_ref[r, h * VT_ROWS + HEAD_DIM:(h + 1) * VT_ROWS, :] = ones

    k = jnp.dot(xn, w_k_ref[...], preferred_element_type=F32)
    k = _group_rms_norm(k, bd128_ref[...], gk_ref[...], HEAD_DIM)
    kmean_ref[...] = jnp.zeros(kmean_ref.shape, F32)
    for r in range(tm // MOBA_BLOCK):
        rows = slice(r * MOBA_BLOCK, (r + 1) * MOBA_BLOCK)
        kmean_ref[0, r:r + 1, :] = jnp.sum(k[rows, :], axis=0, keepdims=True) * (1.0 / MOBA_BLOCK)

    shape = k.shape
    row = lax.broadcasted_iota(jnp.int32, shape, 0)
    lane = lax.broadcasted_iota(jnp.int32, shape, 1) & (AUG_W - 1)
    t = (pl.program_id(0) % (seq // tm)) * tm + row
    bias = t.astype(F32) * alibi_ref[...]
    b_hi = bias.astype(BF16).astype(F32)
    b_mid = (bias - b_hi).astype(BF16).astype(F32)
    b_lo = bias - b_hi - b_mid
    pieces = jnp.where(lane == AUG_ALIBI, b_hi, jnp.where(lane == AUG_ALIBI + 1, b_mid, b_lo))
    onehot = jnp.where(lane - AUG_MASK == lax.shift_right_logical(t, MOBA_BLOCK.bit_length() - 1), 1.0, 0.0)
    is_alibi = (lane >= AUG_ALIBI) & (lane < AUG_ALIBI + 3)
    kaug_ref[...] = (k + jnp.where(is_alibi, pieces, onehot)).astype(BF16)


def _in_proj(x1, g, w_lru, w_qvc, w_k, gq, gk, gc, bd64, bd128, alibi, seq):
    n, d = x1.shape
    tm = IN_PROJ_TILE
    rb = tm // MOBA_BLOCK
    nblk = n // MOBA_BLOCK
    n_lru = w_lru.shape[1] // 2
    wq = w_qvc.shape[1] // 3
    wk = w_k.shape[1]
    consts = (g, w_lru, w_qvc, w_k, gq, gk, gc, bd64, bd128, alibi)
    resident = sum(c.size * c.dtype.itemsize for c in consts)
    return pl.pallas_call(
        functools.partial(_in_proj_kernel, seq=seq),
        grid=(n // tm,),
        in_specs=[pl.BlockSpec((tm, d), lambda i: (i, 0))] + [_const_spec(c.shape) for c in consts],
        out_specs=[pl.BlockSpec((tm, n_lru), lambda i: (i, 0)),
                   pl.BlockSpec((tm, n_lru), lambda i: (i, 0)),
                   pl.BlockSpec((rb, wq, MOBA_BLOCK), lambda i: (i, 0, 0)),
                   pl.BlockSpec((tm, wk), lambda i: (i, 0)),
                   pl.BlockSpec((rb, MOBA_HEADS * VT_ROWS, MOBA_BLOCK), lambda i: (i, 0, 0)),
                   pl.BlockSpec((1, 8, wk), lambda i: (i, 0, 0)),
                   pl.BlockSpec((rb, wq, MOBA_BLOCK), lambda i: (i, 0, 0))],
        out_shape=(jax.ShapeDtypeStruct((n, n_lru), F32),
                   jax.ShapeDtypeStruct((n, n_lru), F32),
                   jax.ShapeDtypeStruct((nblk, wq, MOBA_BLOCK), F32),
                   jax.ShapeDtypeStruct((n, wk), BF16),
                   jax.ShapeDtypeStruct((nblk, MOBA_HEADS * VT_ROWS, MOBA_BLOCK), BF16),
                   jax.ShapeDtypeStruct((n // tm, 8, wk), F32),
                   jax.ShapeDtypeStruct((nblk, wq, MOBA_BLOCK), BF16)),
        compiler_params=pltpu.CompilerParams(dimension_semantics=("arbitrary",),
                                             vmem_limit_bytes=_vmem_limit(resident)),
        name="in_proj",
    )(x1, *consts)


def _gelu_tanh(x):
    c = math.sqrt(2.0 / math.pi)
    return 0.5 * x * (1.0 + jnp.tanh(c * (x + 0.044715 * (x * x * x))))


def _lru_kernel(x_ref, g_ref, cw_ref, cb_ref, wa_ref, ba_ref, wx_ref, bx_ref, lam_ref,
                o_ref, xpad_ref, a_ref, b_ref, hl_ref, carry_ref):
    tl, c = x_ref.shape
    half = V7X_MXU_DIM
    ti = pl.program_id(1)

    @pl.when(ti == 0)
    def _():
        xpad_ref[0:8, :] = jnp.zeros((8, c), F32)
        carry_ref[...] = jnp.zeros(carry_ref.shape, F32)

    xpad_ref[8:8 + tl, :] = x_ref[...]
    n_grp = tl // SUBLANES
    sub1 = lax.broadcasted_iota(jnp.int32, (n_grp, SUBLANES, LANES), 1)
    parts = []
    for k in range(c // LANES):
        lanes = slice(k * LANES, (k + 1) * LANES)
        xg = xpad_ref[:, lanes].reshape(n_grp + 1, SUBLANES, LANES)
        acc = cb_ref[:, lanes] + cw_ref[CONV_W - 1:CONV_W, lanes] * xg[1:]
        for d in range(1, CONV_W):
            rolled = pltpu.roll(xg, d, axis=1)
            delayed = jnp.where(sub1 >= d, rolled[1:], rolled[:-1])
            acc = acc + cw_ref[CONV_W - 1 - d:CONV_W - d, lanes] * delayed
        parts.append(acc.reshape(tl, LANES))
    xb = jnp.concatenate(parts, axis=1)
    xpad_ref[0:8, :] = x_ref[tl - 8:tl, :]

    xbb = xb.astype(BF16)

    def gate_tanh(w_ref, bias_ref):
        z_half = jnp.concatenate(
            [jnp.dot(xbb[:, :half], w_ref[0], preferred_element_type=F32),
             jnp.dot(xbb[:, half:], w_ref[1], preferred_element_type=F32)], axis=1)
        return jnp.tanh(z_half + bias_ref[...])

    neg_lam = -lam_ref[...]
    softplus = jnp.maximum(neg_lam, 0.0) + jnp.log1p(jnp.exp(-jnp.abs(neg_lam)))
    half_rate = (-0.5 * LRU_C) * softplus
    log_a = half_rate * gate_tanh(wa_ref, ba_ref) + half_rate
    i = 0.5 * gate_tanh(wx_ref, bx_ref) + 0.5
    a = jnp.exp(log_a)
    one_minus_a2 = 1.0 - a * a
    mult = jnp.where(one_minus_a2 > 0.0, one_minus_a2 * lax.rsqrt(one_minus_a2), 0.0)
    row = lax.broadcasted_iota(jnp.int32, (tl, c), 0)
    start_row = jnp.where(ti == 0, 0, -1)
    mult = jnp.where(row == start_row, 1.0, mult)
    bt = mult * i * xb

    groups = tl // SUBLANES
    sub = lax.broadcasted_iota(jnp.int32, (groups, SUBLANES, LANES), 1)
    for k in range(c // LANES):
        lanes = slice(k * LANES, (k + 1) * LANES)
        a3 = a[:, lanes].reshape(groups, SUBLANES, LANES)
        b3 = bt[:, lanes].reshape(groups, SUBLANES, LANES)
        d = 1
        while d < SUBLANES:
            has_prev = sub >= d
            a_prev = pltpu.roll(a3, d, axis=1)
            b_prev = pltpu.roll(b3, d, axis=1)
            b3 = b3 + jnp.where(has_prev, a3, 0.0) * b_prev
            a3 = jnp.where(has_prev, a3 * a_prev, a3)
            d *= 2
        a_ref[:, lanes] = a3.reshape(tl, LANES)
        b_ref[:, lanes] = b3.reshape(tl, LANES)

    h = carry_ref[...]
    for grp in range(groups):
        rows = slice(grp * SUBLANES, (grp + 1) * SUBLANES)
        h_last = jnp.broadcast_to(h[SUBLANES - 1:SUBLANES, :], (SUBLANES, c))
        h = a_ref[rows, :] * h_last + b_ref[rows, :]
        hl_ref[rows, :] = h
    carry_ref[...] = h

    o_ref[...] = (hl_ref[...] * _gelu_tanh(g_ref[...])).astype(o_ref.dtype)


def _lru(lru_x, lru_g, cw, cb, wa, ba, wx, bx, lam, batch, seq):
    n, c = lru_x.shape
    tl = LRU_TILE
    nt = seq // tl
    consts = (cw, cb, wa, ba, wx, bx, lam)
    tile = pl.BlockSpec((tl, c), lambda b, t: (b * nt + t, 0))
    return pl.pallas_call(
        _lru_kernel,
        grid=(batch, nt),
        in_specs=[tile, tile] + [_const_spec(w.shape) for w in consts],
        out_specs=tile,
        out_shape=jax.ShapeDtypeStruct((n, c), BF16),
        scratch_shapes=[pltpu.VMEM((tl + SUBLANES, c), F32)]
        + [pltpu.VMEM((tl, c), F32)] * 3
        + [pltpu.VMEM((SUBLANES, c), F32)],
        compiler_params=pltpu.CompilerParams(dimension_semantics=("arbitrary", "arbitrary")),
        name="lru",
    )(lru_x, lru_g, *consts)


def _moba_kernel(qta_ref, qtb_ref, cqta_ref, cqtb_ref, kaug_ref, vt_ref, kmean_ref, kc_ref, vct_ref,
                 oa_ref, ob_ref, oma_ref, omb_ref,
                 qaug_ref, qown_ref, qmem_ref, s_ref, p_ref, alpha_ref, m_ref, acc_ref, accm_ref,
                 ot_ref, otm_ref):
    step = pl.program_id(1)
    nblk = kmean_ref.shape[1]
    blk = MOBA_BLOCK
    q_blocks = (step, nblk - 1 - step)
    qt_refs = (qta_ref, qtb_ref)
    cqt_refs = (cqta_ref, cqtb_ref)

    def build_query_operands(side, h):
        sh = side * MOBA_HEADS + h
        qt = qt_refs[side][0, 0, h * HEAD_DIM:(h + 1) * HEAD_DIM, :]
        qs = (qt * (HEAD_DIM ** -0.5 * LOG2E)).astype(BF16)
        sub = lax.broadcasted_iota(jnp.int32, (AUG_MASK - AUG_ALIBI, blk), 0)
        ones = jnp.where(sub < 3, 1.0, 0.0).astype(BF16)
        for ref in (qaug_ref, qown_ref):
            ref[sh, 0:HEAD_DIM, :] = qs
            ref[sh, AUG_ALIBI:AUG_MASK, :] = ones
        qown_ref[sh, AUG_MASK:, :] = jnp.zeros((AUG_W - AUG_MASK, blk), BF16)
        qmem_ref[sh, 0:HEAD_DIM, :] = cqt_refs[side][0, 0, h * HEAD_DIM:(h + 1) * HEAD_DIM, :]
        qmem_ref[sh, HEAD_DIM:, :] = jnp.zeros((AUG_W - HEAD_DIM, blk), BF16)

    def build_block_mask(side, h):
        sh = side * MOBA_HEADS + h
        blk_id = lax.broadcasted_iota(jnp.int32, (nblk, blk), 0).astype(F32)
        qt = qt_refs[side][0, 0, h * HEAD_DIM:(h + 1) * HEAD_DIM, :]
        gate = jnp.dot(kmean_ref[0, :, h * AUG_W:h * AUG_W + HEAD_DIM], qt,
                       preferred_element_type=F32, precision=lax.Precision.HIGHEST)
        gate = jnp.where(blk_id < q_blocks[side].astype(F32), gate, -jnp.inf)
        keep = blk_id < 0.0
        for _ in range(MOBA_TOPK):
            best = jnp.max(gate, axis=0, keepdims=True)
            first = jnp.min(jnp.where(gate == best, blk_id, float(nblk)), axis=0, keepdims=True)
            pick = (blk_id == first) & (best > -jnp.inf)
            keep = keep | pick
            gate = jnp.where(pick, -jnp.inf, gate)
        qaug_ref[sh, AUG_MASK:AUG_MASK + nblk, :] = jnp.where(keep, 0.0, NEG_BIG).astype(BF16)
        qaug_ref[sh, AUG_MASK + nblk:, :] = jnp.zeros((AUG_W - AUG_MASK - nblk, blk), BF16)

    kk = lax.broadcasted_iota(jnp.int32, (blk, blk), 0)
    qq = lax.broadcasted_iota(jnp.int32, (blk, blk), 1)
    causal = kk <= qq

    per_trip = PAST_UNROLL * MOBA_HEADS
    lead_groups = (("own", 0), ("own", 1), ("mem", 0), ("mem", 1))
    lead = len(lead_groups) * MOBA_HEADS
    n_trips = (nblk - 1 + PAST_UNROLL - 1) // PAST_UNROLL

    def stream_item(trip, n):
        if n < 0:
            idx = n + lead
            kind, side = lead_groups[idx // MOBA_HEADS]
            return kind, side, q_blocks[side], idx % MOBA_HEADS, idx % SCORE_SLOTS
        if n >= per_trip:
            trip, n = trip + 1, n - per_trip
        pos = trip * PAST_UNROLL + n // MOBA_HEADS
        side = (pos >= step).astype(jnp.int32)
        block = jnp.minimum(pos - side * step, nblk - 1)
        return "past", side, block, n % MOBA_HEADS, (n + lead) % SCORE_SLOTS

    def scores(kind, side, j, h, slot):
        lanes = slice(h * AUG_W, (h + 1) * AUG_W)
        sh = side * MOBA_HEADS + h
        if kind == "mem":
            keys, q_ref = kc_ref[0, :, lanes], qmem_ref
        else:
            keys, q_ref = kaug_ref[0, j, :, lanes], (qown_ref if kind == "own" else qaug_ref)
        s_ref[slot] = jnp.dot(keys, q_ref[sh], preferred_element_type=F32)

    def probabilities(kind, side, j, h, slot):
        s = s_ref[slot]
        sh = side * MOBA_HEADS + h
        if kind == "past":
            m_old = m_ref[sh]
            m_new = jnp.maximum(m_old, jnp.max(s, axis=0, keepdims=True))
            alpha_ref[slot % PROB_SLOTS] = jnp.exp2(m_old - m_new)
        else:
            if kind == "own":
                s = jnp.where(causal, s, NEG_BIG)
            m_new = jnp.max(s, axis=0, keepdims=True)
        p_ref[slot % PROB_SLOTS] = jnp.exp2(s - m_new).astype(BF16)
        if kind != "mem":
            m_ref[sh] = m_new

    def accumulate(kind, side, j, h, slot):
        rows = slice(h * VT_ROWS, (h + 1) * VT_ROWS)
        sh = side * MOBA_HEADS + h
        p = p_ref[slot % PROB_SLOTS]
        if kind == "mem":
            accm_ref[sh] = jnp.dot(vct_ref[0, rows, :], p, preferred_element_type=F32)
        else:
            pv = jnp.dot(vt_ref[0, j, rows, :], p, preferred_element_type=F32)
            acc_ref[sh] = pv if kind == "own" else alpha_ref[slot % PROB_SLOTS] * acc_ref[sh] + pv

    for h in range(MOBA_HEADS):
        for side in range(2):
            build_query_operands(side, h)
            build_block_mask(side, h)
    for n in range(-lead, -lead + SCORE_LOOKAHEAD):
        scores(*stream_item(0, n))
    for n in range(-lead, -lead + PROB_LOOKAHEAD):
        probabilities(*stream_item(0, n))
    for n in range(-lead, 0):
        scores(*stream_item(0, n + SCORE_LOOKAHEAD))
        probabilities(*stream_item(0, n + PROB_LOOKAHEAD))
        accumulate(*stream_item(0, n))

    def past_trip(trip, carry):
        for n in range(per_trip):
            scores(*stream_item(trip, n + SCORE_LOOKAHEAD))
            probabilities(*stream_item(trip, n + PROB_LOOKAHEAD))
            accumulate(*stream_item(trip, n))
        return carry

    lax.fori_loop(0, n_trips, past_trip, 0)

    for side, o_ref, om_ref in ((0, oa_ref, oma_ref), (1, ob_ref, omb_ref)):
        for ref, t_ref, out_ref in ((acc_ref, ot_ref, o_ref), (accm_ref, otm_ref, om_ref)):
            for h in range(MOBA_HEADS):
                sh = side * MOBA_HEADS + h
                t_ref[side, h * HEAD_DIM:(h + 1) * HEAD_DIM, :] = (ref[sh, 0:HEAD_DIM, :]
                                                                   / ref[sh, HEAD_DIM:HEAD_DIM + 1, :])
            out_ref[...] = t_ref[side].T.astype(out_ref.dtype)


def _moba(qt, cqt, kaug, vt, kmean, kc, vct, batch, seq):
    nblk = seq // MOBA_BLOCK
    width = MOBA_HEADS * HEAD_DIM
    aug = MOBA_HEADS * AUG_W
    vt_rows = MOBA_HEADS * VT_ROWS
    mlen = vct.shape[2]
    assert mlen == MOBA_BLOCK and MEM_HEADS == MOBA_HEADS, "memory items reuse the key-block buffers"
    qt = qt.reshape(batch, nblk, width, MOBA_BLOCK)
    cqt = cqt.reshape(batch, nblk, width, MOBA_BLOCK)
    kaug = kaug.reshape(batch, nblk, MOBA_BLOCK, aug)
    vt = vt.reshape(batch, nblk, vt_rows, MOBA_BLOCK)
    kc = kc.reshape(batch, mlen, aug)
    resident = 2 * (kaug.size // batch + vt.size // batch) * 2
    assert nblk % 2 == 0
    half = nblk // 2
    sides = 2 * MOBA_HEADS
    qblock_a = pl.BlockSpec((1, 1, width, MOBA_BLOCK), lambda b, s: (b, s, 0, 0))
    qblock_b = pl.BlockSpec((1, 1, width, MOBA_BLOCK), lambda b, s: (b, nblk - 1 - s, 0, 0))
    out_a = pl.BlockSpec((MOBA_BLOCK, width), lambda b, s: (b * half + s, 0))
    out_b = pl.BlockSpec((MOBA_BLOCK, width), lambda b, s: (b * half + half - 1 - s, 0))
    out_shape = jax.ShapeDtypeStruct((batch * half * MOBA_BLOCK, width), BF16)
    y_lo, y_hi, m_lo, m_hi = pl.pallas_call(
        _moba_kernel,
        grid=(batch, half),
        in_specs=[qblock_a, qblock_b, qblock_a, qblock_b,
                  pl.BlockSpec((1, nblk, MOBA_BLOCK, aug), lambda b, s: (b, 0, 0, 0)),
                  pl.BlockSpec((1, nblk, vt_rows, MOBA_BLOCK), lambda b, s: (b, 0, 0, 0)),
                  pl.BlockSpec((1, nblk, aug), lambda b, s: (b, 0, 0)),
                  pl.BlockSpec((1, mlen, aug), lambda b, s: (b, 0, 0)),
                  pl.BlockSpec((1, vt_rows, mlen), lambda b, s: (b, 0, 0))],
        out_specs=(out_a, out_b, out_a, out_b),
        out_shape=(out_shape,) * 4,
        scratch_shapes=[pltpu.VMEM((sides, AUG_W, MOBA_BLOCK), BF16),
                        pltpu.VMEM((sides, AUG_W, MOBA_BLOCK), BF16),
                        pltpu.VMEM((sides, AUG_W, MOBA_BLOCK), BF16),
                        pltpu.VMEM((SCORE_SLOTS, MOBA_BLOCK, MOBA_BLOCK), F32),
                        pltpu.VMEM((PROB_SLOTS, MOBA_BLOCK, MOBA_BLOCK), BF16),
                        pltpu.VMEM((PROB_SLOTS, 1, MOBA_BLOCK), F32),
                        pltpu.VMEM((sides, 1, MOBA_BLOCK), F32),
                        pltpu.VMEM((sides, VT_ROWS, MOBA_BLOCK), F32),
                        pltpu.VMEM((sides, VT_ROWS, MOBA_BLOCK), F32),
                        pltpu.VMEM((2, width, MOBA_BLOCK), F32),
                        pltpu.VMEM((2, width, MOBA_BLOCK), F32)],
        compiler_params=pltpu.CompilerParams(dimension_semantics=("arbitrary", "arbitrary"),
                                             vmem_limit_bytes=_vmem_limit(resident)),
        name="moba",
    )(qt, qt, cqt, cqt, kaug, vt, kmean, kc, vct)

    def join(lo, hi):
        shape = (batch, half * MOBA_BLOCK, width)
        return jnp.concatenate([lo.reshape(shape), hi.reshape(shape)], axis=1).reshape(batch * seq, width)

    return join(y_lo, y_hi), join(m_lo, m_hi)


def _block_diag_ones(group):
    idx = jnp.arange(V7X_MXU_DIM) // group
    return (idx[:, None] == idx[None, :]).astype(BF16)


def _pack_block_diag(w):
    nb, bw, _ = w.shape
    per = V7X_MXU_DIM // bw
    w = w.reshape(nb // per, per, bw, bw)
    eye = jnp.eye(per, dtype=w.dtype)
    return jnp.einsum("gpij,pq->gpiqj", w, eye).reshape(nb // per, V7X_MXU_DIM, V7X_MXU_DIM)


def _pad_heads(w, heads):
    lead = w.shape[:-1]
    w = w.reshape(lead + (heads, HEAD_DIM))
    w = jnp.pad(w, [(0, 0)] * len(lead) + [(0, 0), (0, AUG_W - HEAD_DIM)])
    return w.reshape(lead + (heads * AUG_W,))


def _layer(x, mem, ffn1_norm, ffn1_w_in, ffn1_w_out, mix_norm, mem_norm, w_in,
           lru_conv_w, lru_conv_b, lru_a_w, lru_a_b, lru_x_w, lru_x_b, lru_lambda,
           moba_q_norm, moba_k_norm, mem_w_kv, mem_q_norm, mem_k_norm, w_out,
           ffn2_norm, ffn2_w_in, ffn2_w_out):
    batch, seq, d = x.shape
    n = batch * seq
    n_lru = lru_lambda.shape[0]
    wq = MOBA_HEADS * HEAD_DIM
    wc = MEM_HEADS * HEAD_DIM
    row = lambda v: v.reshape(1, -1).astype(F32)

    bd64 = _block_diag_ones(HEAD_DIM)
    bd128 = _block_diag_ones(AUG_W)

    w_in_b = w_in.astype(BF16)
    o = 2 * n_lru
    w_lru = w_in_b[:, :o]
    w_k = _pad_heads(w_in_b[:, o + wq:o + 2 * wq], MOBA_HEADS)
    w_qvc = jnp.concatenate([w_in_b[:, o:o + wq], w_in_b[:, o + 2 * wq:]], axis=1)
    gq = row(jnp.tile(moba_q_norm, MOBA_HEADS))
    gk = row(_pad_heads(jnp.tile(moba_k_norm, MOBA_HEADS), MOBA_HEADS))
    gc = row(jnp.tile(mem_q_norm, MEM_HEADS))
    gck = row(_pad_heads(jnp.tile(mem_k_norm, MEM_HEADS), MEM_HEADS))
    w_kv_b = mem_w_kv.astype(BF16)
    w_ck = _pad_heads(w_kv_b[:, :wc], MEM_HEADS)
    w_cv = w_kv_b[:, wc:]
    slopes = 2.0 ** (-8.0 * jnp.arange(1, MOBA_HEADS + 1, dtype=F32) / MOBA_HEADS)
    lane = jnp.arange(AUG_W)
    is_alibi = (lane >= AUG_ALIBI) & (lane < AUG_ALIBI + 3)
    alibi = row(jnp.where(is_alibi[None, :], slopes[:, None] * LOG2E, 0.0))

    x2d = x.reshape(n, d)
    kc, vct = _mem_kv(mem.reshape(-1, d), row(mem_norm), w_ck, w_cv, bd128, gck, batch)

    x1 = _ffn(x2d, row(ffn1_norm), ffn1_w_in, ffn1_w_out)

    lru_x, lru_g, qt, kaug, vt, kmean, cqt = _in_proj(
        x1, row(mix_norm), w_lru, w_qvc, w_k, gq, gk, gc, bd64, bd128, alibi, seq)
    kmean = kmean[:, :IN_PROJ_TILE // MOBA_BLOCK, :].reshape(batch, seq // MOBA_BLOCK, -1)

    y_lru = _lru(lru_x, lru_g, lru_conv_w.astype(F32), row(lru_conv_b),
                 _pack_block_diag(0.5 * lru_a_w).astype(BF16), row(0.5 * lru_a_b),
                 _pack_block_diag(0.5 * lru_x_w).astype(BF16), row(0.5 * lru_x_b),
                 row(lru_lambda), batch, seq)
    y_moba, y_mem = _moba(qt, cqt, kaug, vt, kmean, kc, vct, batch, seq)

    out = _out_ffn(x1, y_lru, y_moba, y_mem, w_out.astype(BF16), row(ffn2_norm),
                   ffn2_w_in, ffn2_w_out)
    return out.reshape(batch, seq, d)


def kernel(x, mem, ffn1_norm, ffn1_w_in, ffn1_w_out, mix_norm, mem_norm, w_in, lru_conv_w, lru_conv_b,
           lru_a_w, lru_a_b, lru_x_w, lru_x_b, lru_lambda, moba_q_norm, moba_k_norm, mem_w_kv,
           mem_q_norm, mem_k_norm, w_out, ffn2_norm, ffn2_w_in, ffn2_w_out):
    params = (ffn1_norm, ffn1_w_in, ffn1_w_out, mix_norm, mem_norm, w_in, lru_conv_w, lru_conv_b,
              lru_a_w, lru_a_b, lru_x_w, lru_x_b, lru_lambda, moba_q_norm, moba_k_norm, mem_w_kv,
              mem_q_norm, mem_k_norm, w_out, ffn2_norm, ffn2_w_in, ffn2_w_out)
    for layer in range(ffn1_norm.shape[0]):
        x = _layer(x, mem, *(p[layer] for p in params))
    return x
```

```python
import functools
import math

import jax
import jax.numpy as jnp
from jax import lax
from jax.experimental import pallas as pl
from jax.experimental.pallas import tpu as pltpu

F32 = jnp.float32
BF16 = jnp.bfloat16

HEAD_DIM = 64
CONV_W = 4
LRU_C = 8.0
MOBA_HEADS = 4
MOBA_BLOCK = 256
MOBA_TOPK = 3
MEM_HEADS = 4
NORM_EPS = 1e-6
LOG2E = 1.4426950408889634
NEG_BIG = -1e30

V7X_MXU_DIM = 256
LANES = 128
SUBLANES = 8
V7X_VMEM_BYTES = 64 * 1024 * 1024

AUG_W = 128
AUG_ALIBI = 64
AUG_MASK = 80
assert AUG_W & (AUG_W - 1) == 0 and MOBA_BLOCK & (MOBA_BLOCK - 1) == 0
VT_ROWS = HEAD_DIM + 16
SCORE_LOOKAHEAD = 5
PAST_UNROLL = 4
SCORE_SLOTS = 8
assert MOBA_HEADS <= SCORE_LOOKAHEAD < SCORE_SLOTS and (PAST_UNROLL * MOBA_HEADS) % SCORE_SLOTS == 0
assert (MOBA_HEADS + MEM_HEADS) % SCORE_SLOTS == 0
PROB_SLOTS = 4
PROB_LOOKAHEAD = 1
assert PROB_LOOKAHEAD < SCORE_LOOKAHEAD

TOKEN_TILE = 512
IN_PROJ_TILE = 1024
LRU_TILE = 512
FFN_CHUNK = 256


def _vmem_limit(resident_bytes):
    return int(min(V7X_VMEM_BYTES - 8 * 1024 * 1024, resident_bytes + 24 * 1024 * 1024))


def _const_spec(shape):
    n = len(shape)
    return pl.BlockSpec(shape, lambda *_: (0,) * n, pipeline_mode=pl.Buffered(1))


def _sigmoid(x):
    return 0.5 * jnp.tanh(0.5 * x) + 0.5


def _rms_norm(x, g):
    ms = jnp.mean(x * x, axis=-1, keepdims=True)
    return x * lax.rsqrt(ms + NORM_EPS) * g


def _group_rms_norm(u, bd, g, group):
    sq = u * u
    hi = sq.astype(BF16)
    lo = (sq - hi.astype(F32)).astype(BF16)
    parts = []
    for c in range(u.shape[1] // V7X_MXU_DIM):
        sl = slice(c * V7X_MXU_DIM, (c + 1) * V7X_MXU_DIM)
        parts.append(jnp.dot(hi[:, sl], bd, preferred_element_type=F32)
                     + jnp.dot(lo[:, sl], bd, preferred_element_type=F32))
    ss = parts[0] if len(parts) == 1 else jnp.concatenate(parts, axis=1)
    return u * lax.rsqrt(ss * (1.0 / group) + NORM_EPS) * g


def _swiglu_half_step(x, g_ref, w_in_ref, w_out_ref, act_ref):
    d_ff = w_out_ref.shape[0]
    xn = _rms_norm(x, g_ref[...]).astype(BF16)
    for c in range(d_ff // FFN_CHUNK):
        lo = c * FFN_CHUNK
        a = jnp.dot(xn, w_in_ref[:, lo:lo + FFN_CHUNK].astype(BF16), preferred_element_type=F32)
        b = jnp.dot(xn, w_in_ref[:, d_ff + lo:d_ff + lo + FFN_CHUNK].astype(BF16),
                    preferred_element_type=F32)
        act_ref[:, lo:lo + FFN_CHUNK] = (a * _sigmoid(a) * b).astype(BF16)
    y = jnp.dot(act_ref[...], w_out_ref[...].astype(BF16), preferred_element_type=F32)
    return x + 0.5 * y


def _ffn_kernel(x_ref, g_ref, w_in_ref, w_out_ref, o_ref, act_ref):
    o_ref[...] = _swiglu_half_step(x_ref[...], g_ref, w_in_ref, w_out_ref, act_ref)


def _ffn(x, g, w_in, w_out):
    n, d = x.shape
    d_ff = w_out.shape[0]
    tm = TOKEN_TILE
    resident = sum(w.size * w.dtype.itemsize for w in (w_in, w_out))
    return pl.pallas_call(
        _ffn_kernel,
        grid=(n // tm,),
        in_specs=[pl.BlockSpec((tm, d), lambda i: (i, 0)),
                  _const_spec((1, d)), _const_spec(w_in.shape), _const_spec(w_out.shape)],
        out_specs=pl.BlockSpec((tm, d), lambda i: (i, 0)),
        out_shape=jax.ShapeDtypeStruct((n, d), F32),
        scratch_shapes=[pltpu.VMEM((tm, d_ff), BF16)],
        compiler_params=pltpu.CompilerParams(dimension_semantics=("arbitrary",),
                                             vmem_limit_bytes=_vmem_limit(resident)),
        name="ffn1",
    )(x, g, w_in, w_out)


def _out_ffn_kernel(x_ref, ylru_ref, ymoba_lo_ref, ymoba_hi_ref, ymem_lo_ref, ymem_hi_ref,
                    wo_ref, g_ref, w_in_ref, w_out_ref, o_ref, act_ref, *, tiles_per_seq):
    n_lru = ylru_ref.shape[1]
    n_moba = ymoba_lo_ref.shape[1]
    first_half = pl.program_id(0) % tiles_per_seq < tiles_per_seq // 2
    y_moba = jnp.where(first_half, ymoba_lo_ref[...], ymoba_hi_ref[...])
    y_mem = jnp.where(first_half, ymem_lo_ref[...], ymem_hi_ref[...])
    x2 = (x_ref[...]
          + jnp.dot(ylru_ref[...], wo_ref[0:n_lru, :], preferred_element_type=F32)
          + jnp.dot(y_moba, wo_ref[n_lru:n_lru + n_moba, :], preferred_element_type=F32)
          + jnp.dot(y_mem, wo_ref[n_lru + n_moba:, :], preferred_element_type=F32))
    o_ref[...] = _swiglu_half_step(x2, g_ref, w_in_ref, w_out_ref, act_ref)


def _out_ffn(x1, y_lru, y_moba, y_mem, w_o, g, w_in, w_out, seq):
    n, d = x1.shape
    d_ff = w_out.shape[0]
    tm = TOKEN_TILE
    tps = seq // tm
    half = tps // 2
    assert tps % 2 == 0
    resident = sum(w.size * w.dtype.itemsize for w in (w_in, w_out, w_o))

    def tile(w):
        return pl.BlockSpec((tm, w), lambda i: (i, 0))

    def lo_tile(w):
        return pl.BlockSpec((tm, w), lambda i: ((i // tps) * half + jnp.minimum(i % tps, half - 1), 0))

    def hi_tile(w):
        return pl.BlockSpec((tm, w), lambda i: ((i // tps) * half + jnp.maximum(i % tps - half, 0), 0))

    wa = y_moba[0].shape[1]
    wm = y_mem[0].shape[1]
    return pl.pallas_call(
        functools.partial(_out_ffn_kernel, tiles_per_seq=tps),
        grid=(n // tm,),
        in_specs=[tile(d), tile(y_lru.shape[1]), lo_tile(wa), hi_tile(wa), lo_tile(wm), hi_tile(wm),
                  _const_spec(w_o.shape), _const_spec((1, d)),
                  _const_spec(w_in.shape), _const_spec(w_out.shape)],
        out_specs=tile(d),
        out_shape=jax.ShapeDtypeStruct((n, d), F32),
        scratch_shapes=[pltpu.VMEM((tm, d_ff), BF16)],
        compiler_params=pltpu.CompilerParams(dimension_semantics=("arbitrary",),
                                             vmem_limit_bytes=_vmem_limit(resident)),
        name="out_ffn2",
    )(x1, y_lru, *y_moba, *y_mem, w_o, g, w_in, w_out)


def _mem_kv_kernel(mem_ref, g_ref, wk_ref, wv_ref, bd_ref, gk_ref, kc_ref, vct_ref):
    batch, _, mlen = vct_ref.shape
    mn = _rms_norm(mem_ref[...], g_ref[...]).astype(BF16)
    k = jnp.dot(mn, wk_ref[...], preferred_element_type=F32)
    kc_ref[...] = _group_rms_norm(k, bd_ref[...], gk_ref[...], HEAD_DIM).astype(BF16)
    v = jnp.dot(mn, wv_ref[...], preferred_element_type=F32)
    ones = jnp.ones((VT_ROWS - HEAD_DIM, mlen), BF16)
    for b in range(batch):
        vt = v[b * mlen:(b + 1) * mlen, :].T.astype(BF16)
        for h in range(MEM_HEADS):
            vct_ref[b, h * VT_ROWS:h * VT_ROWS + HEAD_DIM, :] = vt[h * HEAD_DIM:(h + 1) * HEAD_DIM, :]
            vct_ref[b, h * VT_ROWS + HEAD_DIM:(h + 1) * VT_ROWS, :] = ones


def _mem_kv(mem2d, g, w_k, w_v, bd128, gk, batch):
    m = mem2d.shape[0]
    return pl.pallas_call(
        _mem_kv_kernel,
        out_shape=(jax.ShapeDtypeStruct((m, w_k.shape[1]), BF16),
                   jax.ShapeDtypeStruct((batch, MEM_HEADS * VT_ROWS, m // batch), BF16)),
        name="mem_kv",
    )(mem2d, g, w_k, w_v, bd128, gk)


def _in_proj_kernel(x_ref, g_ref, w_lru_ref, w_qvc_ref, w_k_ref,
                    gq_ref, gk_ref, gc_ref, bd64_ref, bd128_ref, alibi_ref,
                    lrux_ref, lrug_ref, qt_ref, kaug_ref, vt_ref, kmean_ref, cqt_ref, *, seq):
    tm = x_ref.shape[0]
    n_lru = lrux_ref.shape[1]
    xn = _rms_norm(x_ref[...], g_ref[...]).astype(BF16)

    lrux_ref[...] = jnp.dot(xn, w_lru_ref[:, :n_lru], preferred_element_type=F32)
    lrug_ref[...] = jnp.dot(xn, w_lru_ref[:, n_lru:], preferred_element_type=F32)

    wq = qt_ref.shape[1]
    qvc = jnp.dot(xn, w_qvc_ref[...], preferred_element_type=F32)

    cq = _group_rms_norm(qvc[:, 2 * wq:], bd64_ref[...], gc_ref[...], HEAD_DIM) * (HEAD_DIM ** -0.5 * LOG2E)
    for r in range(tm // MOBA_BLOCK):
        cqt_ref[r] = cq[r * MOBA_BLOCK:(r + 1) * MOBA_BLOCK, :].T.astype(BF16)

    q = _group_rms_norm(qvc[:, :wq], bd64_ref[...], gq_ref[...], HEAD_DIM)
    v = qvc[:, wq:2 * wq]
    ones = jnp.ones((VT_ROWS - HEAD_DIM, MOBA_BLOCK), BF16)
    for r in range(tm // MOBA_BLOCK):
        rows = slice(r * MOBA_BLOCK, (r + 1) * MOBA_BLOCK)
        qt_ref[r] = q[rows, :].T
        vt = v[rows, :].T.astype(BF16)
        for h in range(MOBA_HEADS):
            vt_ref[r, h * VT_ROWS:h * VT_ROWS + HEAD_DIM, :] = vt[h * HEAD_DIM:(h + 1) * HEAD_DIM, :]
            vt_ref[r, h * VT_ROWS + HEAD_DIM:(h + 1) * VT_ROWS, :] = ones

    k = jnp.dot(xn, w_k_ref[...], preferred_element_type=F32)
    k = _group_rms_norm(k, bd128_ref[...], gk_ref[...], HEAD_DIM)
    kmean_ref[...] = jnp.zeros(kmean_ref.shape, F32)
    for r in range(tm // MOBA_BLOCK):
        rows = slice(r * MOBA_BLOCK, (r + 1) * MOBA_BLOCK)
        kmean_ref[0, r:r + 1, :] = jnp.sum(k[rows, :], axis=0, keepdims=True) * (1.0 / MOBA_BLOCK)

    shape = k.shape
    row = lax.broadcasted_iota(jnp.int32, shape, 0)
    lane = lax.broadcasted_iota(jnp.int32, shape, 1) & (AUG_W - 1)
    t = (pl.program_id(0) % (seq // tm)) * tm + row
    bias = t.astype(F32) * alibi_ref[...]
    b_hi = bias.astype(BF16).astype(F32)
    b_mid = (bias - b_hi).astype(BF16).astype(F32)
    b_lo = bias - b_hi - b_mid
    pieces = jnp.where(lane == AUG_ALIBI, b_hi, jnp.where(lane == AUG_ALIBI + 1, b_mid, b_lo))
    onehot = jnp.where(lane - AUG_MASK == lax.shift_right_logical(t, MOBA_BLOCK.bit_length() - 1), 1.0, 0.0)
    is_alibi = (lane >= AUG_ALIBI) & (lane < AUG_ALIBI + 3)
    kaug_ref[...] = (k + jnp.where(is_alibi, pieces, onehot)).astype(BF16)


def _in_proj(x1, g, w_lru, w_qvc, w_k, gq, gk, gc, bd64, bd128, alibi, seq):
    n, d = x1.shape
    tm = IN_PROJ_TILE
    rb = tm // MOBA_BLOCK
    nblk = n // MOBA_BLOCK
    n_lru = w_lru.shape[1] // 2
    wq = w_qvc.shape[1] // 3
    wk = w_k.shape[1]
    consts = (g, w_lru, w_qvc, w_k, gq, gk, gc, bd64, bd128, alibi)
    resident = sum(c.size * c.dtype.itemsize for c in consts)
    return pl.pallas_call(
        functools.partial(_in_proj_kernel, seq=seq),
        grid=(n // tm,),
        in_specs=[pl.BlockSpec((tm, d), lambda i: (i, 0))] + [_const_spec(c.shape) for c in consts],
        out_specs=[pl.BlockSpec((tm, n_lru), lambda i: (i, 0)),
                   pl.BlockSpec((tm, n_lru), lambda i: (i, 0)),
                   pl.BlockSpec((rb, wq, MOBA_BLOCK), lambda i: (i, 0, 0)),
                   pl.BlockSpec((tm, wk), lambda i: (i, 0)),
                   pl.BlockSpec((rb, MOBA_HEADS * VT_ROWS, MOBA_BLOCK), lambda i: (i, 0, 0)),
                   pl.BlockSpec((1, 8, wk), lambda i: (i, 0, 0)),
                   pl.BlockSpec((rb, wq, MOBA_BLOCK), lambda i: (i, 0, 0))],
        out_shape=(jax.ShapeDtypeStruct((n, n_lru), F32),
                   jax.ShapeDtypeStruct((n, n_lru), F32),
                   jax.ShapeDtypeStruct((nblk, wq, MOBA_BLOCK), F32),
                   jax.ShapeDtypeStruct((n, wk), BF16),
                   jax.ShapeDtypeStruct((nblk, MOBA_HEADS * VT_ROWS, MOBA_BLOCK), BF16),
                   jax.ShapeDtypeStruct((n // tm, 8, wk), F32),
                   jax.ShapeDtypeStruct((nblk, wq, MOBA_BLOCK), BF16)),
        compiler_params=pltpu.CompilerParams(dimension_semantics=("arbitrary",),
                                             vmem_limit_bytes=_vmem_limit(resident)),
        name="in_proj",
    )(x1, *consts)


def _gelu_tanh(x):
    c = math.sqrt(2.0 / math.pi)
    return 0.5 * x * (1.0 + jnp.tanh(c * (x + 0.044715 * (x * x * x))))


def _lru_kernel(x_ref, g_ref, cw_ref, cb_ref, wa_ref, ba_ref, wx_ref, bx_ref, lam_ref,
                o_ref, xpad_ref, a_ref, b_ref, hl_ref, carry_ref):
    tl, c = x_ref.shape
    half = V7X_MXU_DIM
    ti = pl.program_id(1)

    @pl.when(ti == 0)
    def _():
        xpad_ref[0:8, :] = jnp.zeros((8, c), F32)
        carry_ref[...] = jnp.zeros(carry_ref.shape, F32)

    xpad_ref[8:8 + tl, :] = x_ref[...]
    n_grp = tl // SUBLANES
    sub1 = lax.broadcasted_iota(jnp.int32, (n_grp, SUBLANES, LANES), 1)
    parts = []
    for k in range(c // LANES):
        lanes = slice(k * LANES, (k + 1) * LANES)
        xg = xpad_ref[:, lanes].reshape(n_grp + 1, SUBLANES, LANES)
        acc = cb_ref[:, lanes] + cw_ref[CONV_W - 1:CONV_W, lanes] * xg[1:]
        for d in range(1, CONV_W):
            rolled = pltpu.roll(xg, d, axis=1)
            delayed = jnp.where(sub1 >= d, rolled[1:], rolled[:-1])
            acc = acc + cw_ref[CONV_W - 1 - d:CONV_W - d, lanes] * delayed
        parts.append(acc.reshape(tl, LANES))
    xb = jnp.concatenate(parts, axis=1)
    xpad_ref[0:8, :] = x_ref[tl - 8:tl, :]

    xbb = xb.astype(BF16)

    def gate_tanh(w_ref, bias_ref):
        z_half = jnp.concatenate(
            [jnp.dot(xbb[:, :half], w_ref[0], preferred_element_type=F32),
             jnp.dot(xbb[:, half:], w_ref[1], preferred_element_type=F32)], axis=1)
        return jnp.tanh(z_half + bias_ref[...])

    neg_lam = -lam_ref[...]
    softplus = jnp.maximum(neg_lam, 0.0) + jnp.log1p(jnp.exp(-jnp.abs(neg_lam)))
    half_rate = (-0.5 * LRU_C) * softplus
    log_a = half_rate * gate_tanh(wa_ref, ba_ref) + half_rate
    i = 0.5 * gate_tanh(wx_ref, bx_ref) + 0.5
    a = jnp.exp(log_a)
    one_minus_a2 = 1.0 - a * a
    mult = jnp.where(one_minus_a2 > 0.0, one_minus_a2 * lax.rsqrt(one_minus_a2), 0.0)
    row = lax.broadcasted_iota(jnp.int32, (tl, c), 0)
    start_row = jnp.where(ti == 0, 0, -1)
    mult = jnp.where(row == start_row, 1.0, mult)
    bt = mult * i * xb

    groups = tl // SUBLANES
    sub = lax.broadcasted_iota(jnp.int32, (groups, SUBLANES, LANES), 1)
    for k in range(c // LANES):
        lanes = slice(k * LANES, (k + 1) * LANES)
        a3 = a[:, lanes].reshape(groups, SUBLANES, LANES)
        b3 = bt[:, lanes].reshape(groups, SUBLANES, LANES)
        d = 1
        while d < SUBLANES:
            has_prev = sub >= d
            a_prev = pltpu.roll(a3, d, axis=1)
            b_prev = pltpu.roll(b3, d, axis=1)
            b3 = b3 + jnp.where(has_prev, a3, 0.0) * b_prev
            a3 = jnp.where(has_prev, a3 * a_prev, a3)
            d *= 2
        a_ref[:, lanes] = a3.reshape(tl, LANES)
        b_ref[:, lanes] = b3.reshape(tl, LANES)

    h = carry_ref[...]
    for grp in range(groups):
        rows = slice(grp * SUBLANES, (grp + 1) * SUBLANES)
        h_last = jnp.broadcast_to(h[SUBLANES - 1:SUBLANES, :], (SUBLANES, c))
        h = a_ref[rows, :] * h_last + b_ref[rows, :]
        hl_ref[rows, :] = h
    carry_ref[...] = h

    o_ref[...] = (hl_ref[...] * _gelu_tanh(g_ref[...])).astype(o_ref.dtype)


def _lru(lru_x, lru_g, cw, cb, wa, ba, wx, bx, lam, batch, seq):
    n, c = lru_x.shape
    tl = LRU_TILE
    nt = seq // tl
    consts = (cw, cb, wa, ba, wx, bx, lam)
    tile = pl.BlockSpec((tl, c), lambda b, t: (b * nt + t, 0))
    return pl.pallas_call(
        _lru_kernel,
        grid=(batch, nt),
        in_specs=[tile, tile] + [_const_spec(w.shape) for w in consts],
        out_specs=tile,
        out_shape=jax.ShapeDtypeStruct((n, c), BF16),
        scratch_shapes=[pltpu.VMEM((tl + SUBLANES, c), F32)]
        + [pltpu.VMEM((tl, c), F32)] * 3
        + [pltpu.VMEM((SUBLANES, c), F32)],
        compiler_params=pltpu.CompilerParams(dimension_semantics=("arbitrary", "arbitrary")),
        name="lru",
    )(lru_x, lru_g, *consts)


def _moba_kernel(qta_ref, qtb_ref, cqta_ref, cqtb_ref, kaug_ref, vt_ref, kmean_ref, kc_ref, vct_ref,
                 oa_ref, ob_ref, oma_ref, omb_ref,
                 qaug_ref, qown_ref, qmem_ref, s_ref, p_ref, alpha_ref, m_ref, acc_ref, accm_ref,
                 ot_ref, otm_ref):
    step = pl.program_id(1)
    nblk = kmean_ref.shape[1]
    blk = MOBA_BLOCK
    q_blocks = (step, nblk - 1 - step)
    qt_refs = (qta_ref, qtb_ref)
    cqt_refs = (cqta_ref, cqtb_ref)

    def build_query_operands(side, h):
        sh = side * MOBA_HEADS + h
        qt = qt_refs[side][0, 0, h * HEAD_DIM:(h + 1) * HEAD_DIM, :]
        qs = (qt * (HEAD_DIM ** -0.5 * LOG2E)).astype(BF16)
        sub = lax.broadcasted_iota(jnp.int32, (AUG_MASK - AUG_ALIBI, blk), 0)
        ones = jnp.where(sub < 3, 1.0, 0.0).astype(BF16)
        for ref in (qaug_ref, qown_ref):
            ref[sh, 0:HEAD_DIM, :] = qs
            ref[sh, AUG_ALIBI:AUG_MASK, :] = ones
        qown_ref[sh, AUG_MASK:, :] = jnp.zeros((AUG_W - AUG_MASK, blk), BF16)
        qmem_ref[sh, 0:HEAD_DIM, :] = cqt_refs[side][0, 0, h * HEAD_DIM:(h + 1) * HEAD_DIM, :]
        qmem_ref[sh, HEAD_DIM:, :] = jnp.zeros((AUG_W - HEAD_DIM, blk), BF16)

    def build_block_mask(side, h):
        sh = side * MOBA_HEADS + h
        blk_id = lax.broadcasted_iota(jnp.int32, (nblk, blk), 0).astype(F32)
        qt = qt_refs[side][0, 0, h * HEAD_DIM:(h + 1) * HEAD_DIM, :]
        gate = jnp.dot(kmean_ref[0, :, h * AUG_W:h * AUG_W + HEAD_DIM], qt,
                       preferred_element_type=F32, precision=lax.Precision.HIGHEST)
        gate = jnp.where(blk_id < q_blocks[side].astype(F32), gate, -jnp.inf)
        keep = blk_id < 0.0
        for _ in range(MOBA_TOPK):
            best = jnp.max(gate, axis=0, keepdims=True)
            first = jnp.min(jnp.where(gate == best, blk_id, float(nblk)), axis=0, keepdims=True)
            pick = (blk_id == first) & (best > -jnp.inf)
            keep = keep | pick
            gate = jnp.where(pick, -jnp.inf, gate)
        qaug_ref[sh, AUG_MASK:AUG_MASK + nblk, :] = jnp.where(keep, 0.0, NEG_BIG).astype(BF16)
        qaug_ref[sh, AUG_MASK + nblk:, :] = jnp.zeros((AUG_W - AUG_MASK - nblk, blk), BF16)

    kk = lax.broadcasted_iota(jnp.int32, (blk, blk), 0)
    qq = lax.broadcasted_iota(jnp.int32, (blk, blk), 1)
    causal = kk <= qq

    per_trip = PAST_UNROLL * MOBA_HEADS
    lead_groups = (("own", 0), ("own", 1), ("mem", 0), ("mem", 1))
    lead = len(lead_groups) * MOBA_HEADS
    n_trips = (nblk - 1 + PAST_UNROLL - 1) // PAST_UNROLL

    def stream_item(trip, n):
        if n < 0:
            idx = n + lead
            kind, side = lead_groups[idx // MOBA_HEADS]
            return kind, side, q_blocks[side], idx % MOBA_HEADS, idx % SCORE_SLOTS
        if n >= per_trip:
            trip, n = trip + 1, n - per_trip
        pos = trip * PAST_UNROLL + n // MOBA_HEADS
        side = (pos >= step).astype(jnp.int32)
        block = jnp.minimum(pos - side * step, nblk - 1)
        return "past", side, block, n % MOBA_HEADS, (n + lead) % SCORE_SLOTS

    def scores(kind, side, j, h, slot):
        lanes = slice(h * AUG_W, (h + 1) * AUG_W)
        sh = side * MOBA_HEADS + h
        if kind == "mem":
            keys, q_ref = kc_ref[0, :, lanes], qmem_ref
        else:
            keys, q_ref = kaug_ref[0, j, :, lanes], (qown_ref if kind == "own" else qaug_ref)
        s_ref[slot] = jnp.dot(keys, q_ref[sh], preferred_element_type=F32)

    def probabilities(kind, side, j, h, slot):
        s = s_ref[slot]
        sh = side * MOBA_HEADS + h
        if kind == "past":
            m_old = m_ref[sh]
            m_new = jnp.maximum(m_old, jnp.max(s, axis=0, keepdims=True))
            alpha_ref[slot % PROB_SLOTS] = jnp.exp2(m_old - m_new)
        else:
            if kind == "own":
                s = jnp.where(causal, s, NEG_BIG)
            m_new = jnp.max(s, axis=0, keepdims=True)
        p_ref[slot % PROB_SLOTS] = jnp.exp2(s - m_new).astype(BF16)
        if kind != "mem":
            m_ref[sh] = m_new

    def accumulate(kind, side, j, h, slot):
        rows = slice(h * VT_ROWS, (h + 1) * VT_ROWS)
        sh = side * MOBA_HEADS + h
        p = p_ref[slot % PROB_SLOTS]
        if kind == "mem":
            accm_ref[sh] = jnp.dot(vct_ref[0, rows, :], p, preferred_element_type=F32)
        else:
            pv = jnp.dot(vt_ref[0, j, rows, :], p, preferred_element_type=F32)
            acc_ref[sh] = pv if kind == "own" else alpha_ref[slot % PROB_SLOTS] * acc_ref[sh] + pv

    for h in range(MOBA_HEADS):
        for side in range(2):
            build_query_operands(side, h)
            build_block_mask(side, h)
    for n in range(-lead, -lead + SCORE_LOOKAHEAD):
        scores(*stream_item(0, n))
    for n in range(-lead, -lead + PROB_LOOKAHEAD):
        probabilities(*stream_item(0, n))
    for n in range(-lead, 0):
        scores(*stream_item(0, n + SCORE_LOOKAHEAD))
        probabilities(*stream_item(0, n + PROB_LOOKAHEAD))
        accumulate(*stream_item(0, n))

    def past_trip(trip, carry):
        for n in range(per_trip):
            scores(*stream_item(trip, n + SCORE_LOOKAHEAD))
            probabilities(*stream_item(trip, n + PROB_LOOKAHEAD))
            accumulate(*stream_item(trip, n))
        return carry

    lax.fori_loop(0, n_trips, past_trip, 0)

    for side, o_ref, om_ref in ((0, oa_ref, oma_ref), (1, ob_ref, omb_ref)):
        for ref, t_ref, out_ref in ((acc_ref, ot_ref, o_ref), (accm_ref, otm_ref, om_ref)):
            for h in range(MOBA_HEADS):
                sh = side * MOBA_HEADS + h
                t_ref[side, h * HEAD_DIM:(h + 1) * HEAD_DIM, :] = (ref[sh, 0:HEAD_DIM, :]
                                                                   / ref[sh, HEAD_DIM:HEAD_DIM + 1, :])
            out_ref[...] = t_ref[side].T.astype(out_ref.dtype)


def _moba(qt, cqt, kaug, vt, kmean, kc, vct, batch, seq):
    nblk = seq // MOBA_BLOCK
    width = MOBA_HEADS * HEAD_DIM
    aug = MOBA_HEADS * AUG_W
    vt_rows = MOBA_HEADS * VT_ROWS
    mlen = vct.shape[2]
    assert mlen == MOBA_BLOCK and MEM_HEADS == MOBA_HEADS, "memory items reuse the key-block buffers"
    qt = qt.reshape(batch, nblk, width, MOBA_BLOCK)
    cqt = cqt.reshape(batch, nblk, width, MOBA_BLOCK)
    kaug = kaug.reshape(batch, nblk, MOBA_BLOCK, aug)
    vt = vt.reshape(batch, nblk, vt_rows, MOBA_BLOCK)
    kc = kc.reshape(batch, mlen, aug)
    resident = 2 * (kaug.size // batch + vt.size // batch) * 2
    assert nblk % 2 == 0
    half = nblk // 2
    sides = 2 * MOBA_HEADS
    qblock_a = pl.BlockSpec((1, 1, width, MOBA_BLOCK), lambda b, s: (b, s, 0, 0))
    qblock_b = pl.BlockSpec((1, 1, width, MOBA_BLOCK), lambda b, s: (b, nblk - 1 - s, 0, 0))
    out_a = pl.BlockSpec((MOBA_BLOCK, width), lambda b, s: (b * half + s, 0))
    out_b = pl.BlockSpec((MOBA_BLOCK, width), lambda b, s: (b * half + half - 1 - s, 0))
    out_shape = jax.ShapeDtypeStruct((batch * half * MOBA_BLOCK, width), BF16)
    y_lo, y_hi, m_lo, m_hi = pl.pallas_call(
        _moba_kernel,
        grid=(batch, half),
        in_specs=[qblock_a, qblock_b, qblock_a, qblock_b,
                  pl.BlockSpec((1, nblk, MOBA_BLOCK, aug), lambda b, s: (b, 0, 0, 0)),
                  pl.BlockSpec((1, nblk, vt_rows, MOBA_BLOCK), lambda b, s: (b, 0, 0, 0)),
                  pl.BlockSpec((1, nblk, aug), lambda b, s: (b, 0, 0)),
                  pl.BlockSpec((1, mlen, aug), lambda b, s: (b, 0, 0)),
                  pl.BlockSpec((1, vt_rows, mlen), lambda b, s: (b, 0, 0))],
        out_specs=(out_a, out_b, out_a, out_b),
        out_shape=(out_shape,) * 4,
        scratch_shapes=[pltpu.VMEM((sides, AUG_W, MOBA_BLOCK), BF16),
                        pltpu.VMEM((sides, AUG_W, MOBA_BLOCK), BF16),
                        pltpu.VMEM((sides, AUG_W, MOBA_BLOCK), BF16),
                        pltpu.VMEM((SCORE_SLOTS, MOBA_BLOCK, MOBA_BLOCK), F32),
                        pltpu.VMEM((PROB_SLOTS, MOBA_BLOCK, MOBA_BLOCK), BF16),
                        pltpu.VMEM((PROB_SLOTS, 1, MOBA_BLOCK), F32),
                        pltpu.VMEM((sides, 1, MOBA_BLOCK), F32),
                        pltpu.VMEM((sides, VT_ROWS, MOBA_BLOCK), F32),
                        pltpu.VMEM((sides, VT_ROWS, MOBA_BLOCK), F32),
                        pltpu.VMEM((2, width, MOBA_BLOCK), F32),
                        pltpu.VMEM((2, width, MOBA_BLOCK), F32)],
        compiler_params=pltpu.CompilerParams(dimension_semantics=("arbitrary", "arbitrary"),
                                             vmem_limit_bytes=_vmem_limit(resident)),
        name="moba",
    )(qt, qt, cqt, cqt, kaug, vt, kmean, kc, vct)

    return (y_lo, y_hi), (m_lo, m_hi)


def _block_diag_ones(group):
    idx = jnp.arange(V7X_MXU_DIM) // group
    return (idx[:, None] == idx[None, :]).astype(BF16)


def _pack_block_diag(w):
    nb, bw, _ = w.shape
    per = V7X_MXU_DIM // bw
    w = w.reshape(nb // per, per, bw, bw)
    eye = jnp.eye(per, dtype=w.dtype)
    return jnp.einsum("gpij,pq->gpiqj", w, eye).reshape(nb // per, V7X_MXU_DIM, V7X_MXU_DIM)


def _pad_heads(w, heads):
    lead = w.shape[:-1]
    w = w.reshape(lead + (heads, HEAD_DIM))
    w = jnp.pad(w, [(0, 0)] * len(lead) + [(0, 0), (0, AUG_W - HEAD_DIM)])
    return w.reshape(lead + (heads * AUG_W,))


def _layer(x, mem, ffn1_norm, ffn1_w_in, ffn1_w_out, mix_norm, mem_norm, w_in,
           lru_conv_w, lru_conv_b, lru_a_w, lru_a_b, lru_x_w, lru_x_b, lru_lambda,
           moba_q_norm, moba_k_norm, mem_w_kv, mem_q_norm, mem_k_norm, w_out,
           ffn2_norm, ffn2_w_in, ffn2_w_out):
    batch, seq, d = x.shape
    n = batch * seq
    n_lru = lru_lambda.shape[0]
    wq = MOBA_HEADS * HEAD_DIM
    wc = MEM_HEADS * HEAD_DIM
    row = lambda v: v.reshape(1, -1).astype(F32)

    bd64 = _block_diag_ones(HEAD_DIM)
    bd128 = _block_diag_ones(AUG_W)

    w_in_b = w_in.astype(BF16)
    o = 2 * n_lru
    w_lru = w_in_b[:, :o]
    w_k = _pad_heads(w_in_b[:, o + wq:o + 2 * wq], MOBA_HEADS)
    w_qvc = jnp.concatenate([w_in_b[:, o:o + wq], w_in_b[:, o + 2 * wq:]], axis=1)
    gq = row(jnp.tile(moba_q_norm, MOBA_HEADS))
    gk = row(_pad_heads(jnp.tile(moba_k_norm, MOBA_HEADS), MOBA_HEADS))
    gc = row(jnp.tile(mem_q_norm, MEM_HEADS))
    gck = row(_pad_heads(jnp.tile(mem_k_norm, MEM_HEADS), MEM_HEADS))
    w_kv_b = mem_w_kv.astype(BF16)
    w_ck = _pad_heads(w_kv_b[:, :wc], MEM_HEADS)
    w_cv = w_kv_b[:, wc:]
    slopes = 2.0 ** (-8.0 * jnp.arange(1, MOBA_HEADS + 1, dtype=F32) / MOBA_HEADS)
    lane = jnp.arange(AUG_W)
    is_alibi = (lane >= AUG_ALIBI) & (lane < AUG_ALIBI + 3)
    alibi = row(jnp.where(is_alibi[None, :], slopes[:, None] * LOG2E, 0.0))

    x2d = x.reshape(n, d)
    kc, vct = _mem_kv(mem.reshape(-1, d), row(mem_norm), w_ck, w_cv, bd128, gck, batch)

    x1 = _ffn(x2d, row(ffn1_norm), ffn1_w_in, ffn1_w_out)

    lru_x, lru_g, qt, kaug, vt, kmean, cqt = _in_proj(
        x1, row(mix_norm), w_lru, w_qvc, w_k, gq, gk, gc, bd64, bd128, alibi, seq)
    kmean = kmean[:, :IN_PROJ_TILE // MOBA_BLOCK, :].reshape(batch, seq // MOBA_BLOCK, -1)

    y_lru = _lru(lru_x, lru_g, lru_conv_w.astype(F32), row(lru_conv_b),
                 _pack_block_diag(0.5 * lru_a_w).astype(BF16), row(0.5 * lru_a_b),
                 _pack_block_diag(0.5 * lru_x_w).astype(BF16), row(0.5 * lru_x_b),
                 row(lru_lambda), batch, seq)
    y_moba, y_mem = _moba(qt, cqt, kaug, vt, kmean, kc, vct, batch, seq)

    out = _out_ffn(x1, y_lru, y_moba, y_mem, w_out.astype(BF16), row(ffn2_norm),
                   ffn2_w_in, ffn2_w_out, seq)
    return out.reshape(batch, seq, d)


def kernel(x, mem, ffn1_norm, ffn1_w_in, ffn1_w_out, mix_norm, mem_norm, w_in, lru_conv_w, lru_conv_b,
           lru_a_w, lru_a_b, lru_x_w, lru_x_b, lru_lambda, moba_q_norm, moba_k_norm, mem_w_kv,
           mem_q_norm, mem_k_norm, w_out, ffn2_norm, ffn2_w_in, ffn2_w_out):
    params = (ffn1_norm, ffn1_w_in, ffn1_w_out, mix_norm, mem_norm, w_in, lru_conv_w, lru_conv_b,
              lru_a_w, lru_a_b, lru_x_w, lru_x_b, lru_lambda, moba_q_norm, moba_k_norm, mem_w_kv,
              mem_q_norm, mem_k_norm, w_out, ffn2_norm, ffn2_w_in, ffn2_w_out)
    for layer in range(ffn1_norm.shape[0]):
        x = _layer(x, mem, *(p[layer] for p in params))
    return x
```

```python
import functools
import math

import jax
import jax.numpy as jnp
from jax import lax
from jax.experimental import pallas as pl
from jax.experimental.pallas import tpu as pltpu

F32 = jnp.float32
BF16 = jnp.bfloat16

HEAD_DIM = 64
CONV_W = 4
LRU_C = 8.0
MOBA_HEADS = 4
MOBA_BLOCK = 256
MOBA_TOPK = 3
MEM_HEADS = 4
NORM_EPS = 1e-6
LOG2E = 1.4426950408889634
NEG_BIG = -1e30

V7X_MXU_DIM = 256
LANES = 128
SUBLANES = 8
V7X_VMEM_BYTES = 64 * 1024 * 1024

AUG_W = 128
AUG_ALIBI = 64
AUG_MASK = 80
assert AUG_W & (AUG_W - 1) == 0 and MOBA_BLOCK & (MOBA_BLOCK - 1) == 0
VT_ROWS = HEAD_DIM + 16
SCORE_LOOKAHEAD = 5
PAST_UNROLL = 16
SCORE_SLOTS = 8
assert MOBA_HEADS <= SCORE_LOOKAHEAD < SCORE_SLOTS and (PAST_UNROLL * MOBA_HEADS) % SCORE_SLOTS == 0
assert (MOBA_HEADS + MEM_HEADS) % SCORE_SLOTS == 0
PROB_SLOTS = 4
PROB_LOOKAHEAD = 1
assert PROB_LOOKAHEAD < SCORE_LOOKAHEAD

TOKEN_TILE = 512
IN_PROJ_TILE = 1024
LRU_TILE = 512
FFN_CHUNK = 256


def _vmem_limit(resident_bytes):
    return int(min(V7X_VMEM_BYTES - 8 * 1024 * 1024, resident_bytes + 24 * 1024 * 1024))


def _const_spec(shape):
    n = len(shape)
    return pl.BlockSpec(shape, lambda *_: (0,) * n, pipeline_mode=pl.Buffered(1))


def _sigmoid(x):
    return 0.5 * jnp.tanh(0.5 * x) + 0.5


def _rms_norm(x, g):
    ms = jnp.mean(x * x, axis=-1, keepdims=True)
    return x * lax.rsqrt(ms + NORM_EPS) * g


def _group_rms_norm(u, bd, g, group):
    sq = u * u
    hi = sq.astype(BF16)
    lo = (sq - hi.astype(F32)).astype(BF16)
    parts = []
    for c in range(u.shape[1] // V7X_MXU_DIM):
        sl = slice(c * V7X_MXU_DIM, (c + 1) * V7X_MXU_DIM)
        parts.append(jnp.dot(hi[:, sl], bd, preferred_element_type=F32)
                     + jnp.dot(lo[:, sl], bd, preferred_element_type=F32))
    ss = parts[0] if len(parts) == 1 else jnp.concatenate(parts, axis=1)
    return u * lax.rsqrt(ss * (1.0 / group) + NORM_EPS) * g


def _swiglu_half_step(x, g_ref, w_in_ref, w_out_ref, act_ref):
    d_ff = w_out_ref.shape[0]
    xn = _rms_norm(x, g_ref[...]).astype(BF16)
    for c in range(d_ff // FFN_CHUNK):
        lo = c * FFN_CHUNK
        a = jnp.dot(xn, w_in_ref[:, lo:lo + FFN_CHUNK].astype(BF16), preferred_element_type=F32)
        b = jnp.dot(xn, w_in_ref[:, d_ff + lo:d_ff + lo + FFN_CHUNK].astype(BF16),
                    preferred_element_type=F32)
        act_ref[:, lo:lo + FFN_CHUNK] = (a * _sigmoid(a) * b).astype(BF16)
    y = jnp.dot(act_ref[...], w_out_ref[...].astype(BF16), preferred_element_type=F32)
    return x + 0.5 * y


def _ffn_kernel(x_ref, g_ref, w_in_ref, w_out_ref, o_ref, act_ref):
    o_ref[...] = _swiglu_half_step(x_ref[...], g_ref, w_in_ref, w_out_ref, act_ref)


def _ffn(x, g, w_in, w_out):
    n, d = x.shape
    d_ff = w_out.shape[0]
    tm = TOKEN_TILE
    resident = sum(w.size * w.dtype.itemsize for w in (w_in, w_out))
    return pl.pallas_call(
        _ffn_kernel,
        grid=(n // tm,),
        in_specs=[pl.BlockSpec((tm, d), lambda i: (i, 0)),
                  _const_spec((1, d)), _const_spec(w_in.shape), _const_spec(w_out.shape)],
        out_specs=pl.BlockSpec((tm, d), lambda i: (i, 0)),
        out_shape=jax.ShapeDtypeStruct((n, d), F32),
        scratch_shapes=[pltpu.VMEM((tm, d_ff), BF16)],
        compiler_params=pltpu.CompilerParams(dimension_semantics=("arbitrary",),
                                             vmem_limit_bytes=_vmem_limit(resident)),
        name="ffn1",
    )(x, g, w_in, w_out)


def _out_ffn_kernel(x_ref, ylru_ref, ymoba_lo_ref, ymoba_hi_ref, ymem_lo_ref, ymem_hi_ref,
                    wo_ref, g_ref, w_in_ref, w_out_ref, o_ref, act_ref, *, tiles_per_seq):
    n_lru = ylru_ref.shape[1]
    n_moba = ymoba_lo_ref.shape[1]
    first_half = pl.program_id(0) % tiles_per_seq < tiles_per_seq // 2
    y_moba = jnp.where(first_half, ymoba_lo_ref[...], ymoba_hi_ref[...])
    y_mem = jnp.where(first_half, ymem_lo_ref[...], ymem_hi_ref[...])
    x2 = (x_ref[...]
          + jnp.dot(ylru_ref[...], wo_ref[0:n_lru, :], preferred_element_type=F32)
          + jnp.dot(y_moba, wo_ref[n_lru:n_lru + n_moba, :], preferred_element_type=F32)
          + jnp.dot(y_mem, wo_ref[n_lru + n_moba:, :], preferred_element_type=F32))
    o_ref[...] = _swiglu_half_step(x2, g_ref, w_in_ref, w_out_ref, act_ref)


def _out_ffn(x1, y_lru, y_moba, y_mem, w_o, g, w_in, w_out, seq):
    n, d = x1.shape
    d_ff = w_out.shape[0]
    tm = TOKEN_TILE
    tps = seq // tm
    half = tps // 2
    assert tps % 2 == 0
    resident = sum(w.size * w.dtype.itemsize for w in (w_in, w_out, w_o))

    def tile(w):
        return pl.BlockSpec((tm, w), lambda i: (i, 0))

    def lo_tile(w):
        return pl.BlockSpec((tm, w), lambda i: ((i // tps) * half + jnp.minimum(i % tps, half - 1), 0))

    def hi_tile(w):
        return pl.BlockSpec((tm, w), lambda i: ((i // tps) * half + jnp.maximum(i % tps - half, 0), 0))

    wa = y_moba[0].shape[1]
    wm = y_mem[0].shape[1]
    return pl.pallas_call(
        functools.partial(_out_ffn_kernel, tiles_per_seq=tps),
        grid=(n // tm,),
        in_specs=[tile(d), tile(y_lru.shape[1]), lo_tile(wa), hi_tile(wa), lo_tile(wm), hi_tile(wm),
                  _const_spec(w_o.shape), _const_spec((1, d)),
                  _const_spec(w_in.shape), _const_spec(w_out.shape)],
        out_specs=tile(d),
        out_shape=jax.ShapeDtypeStruct((n, d), F32),
        scratch_shapes=[pltpu.VMEM((tm, d_ff), BF16)],
        compiler_params=pltpu.CompilerParams(dimension_semantics=("arbitrary",),
                                             vmem_limit_bytes=_vmem_limit(resident)),
        name="out_ffn2",
    )(x1, y_lru, *y_moba, *y_mem, w_o, g, w_in, w_out)


def _mem_kv_kernel(mem_ref, g_ref, wk_ref, wv_ref, bd_ref, gk_ref, kc_ref, vct_ref):
    batch, _, mlen = vct_ref.shape
    mn = _rms_norm(mem_ref[...], g_ref[...]).astype(BF16)
    k = jnp.dot(mn, wk_ref[...], preferred_element_type=F32)
    kc_ref[...] = _group_rms_norm(k, bd_ref[...], gk_ref[...], HEAD_DIM).astype(BF16)
    v = jnp.dot(mn, wv_ref[...], preferred_element_type=F32)
    ones = jnp.ones((VT_ROWS - HEAD_DIM, mlen), BF16)
    for b in range(batch):
        vt = v[b * mlen:(b + 1) * mlen, :].T.astype(BF16)
        for h in range(MEM_HEADS):
            vct_ref[b, h * VT_ROWS:h * VT_ROWS + HEAD_DIM, :] = vt[h * HEAD_DIM:(h + 1) * HEAD_DIM, :]
            vct_ref[b, h * VT_ROWS + HEAD_DIM:(h + 1) * VT_ROWS, :] = ones


def _mem_kv(mem2d, g, w_k, w_v, bd128, gk, batch):
    m = mem2d.shape[0]
    return pl.pallas_call(
        _mem_kv_kernel,
        out_shape=(jax.ShapeDtypeStruct((m, w_k.shape[1]), BF16),
                   jax.ShapeDtypeStruct((batch, MEM_HEADS * VT_ROWS, m // batch), BF16)),
        name="mem_kv",
    )(mem2d, g, w_k, w_v, bd128, gk)


def _in_proj_kernel(x_ref, g_ref, w_lru_ref, w_qvc_ref, w_k_ref,
                    gq_ref, gk_ref, gc_ref, bd64_ref, bd128_ref, alibi_ref,
                    lrux_ref, lrug_ref, qt_ref, kaug_ref, vt_ref, kmean_ref, cqt_ref, *, seq):
    tm = x_ref.shape[0]
    n_lru = lrux_ref.shape[1]
    xn = _rms_norm(x_ref[...], g_ref[...]).astype(BF16)

    lrux_ref[...] = jnp.dot(xn, w_lru_ref[:, :n_lru], preferred_element_type=F32)
    lrug_ref[...] = jnp.dot(xn, w_lru_ref[:, n_lru:], preferred_element_type=F32)

    wq = qt_ref.shape[1]
    qvc = jnp.dot(xn, w_qvc_ref[...], preferred_element_type=F32)

    cq = _group_rms_norm(qvc[:, 2 * wq:], bd64_ref[...], gc_ref[...], HEAD_DIM) * (HEAD_DIM ** -0.5 * LOG2E)
    for r in range(tm // MOBA_BLOCK):
        cqt_ref[r] = cq[r * MOBA_BLOCK:(r + 1) * MOBA_BLOCK, :].T.astype(BF16)

    q = _group_rms_norm(qvc[:, :wq], bd64_ref[...], gq_ref[...], HEAD_DIM)
    v = qvc[:, wq:2 * wq]
    ones = jnp.ones((VT_ROWS - HEAD_DIM, MOBA_BLOCK), BF16)
    for r in range(tm // MOBA_BLOCK):
        rows = slice(r * MOBA_BLOCK, (r + 1) * MOBA_BLOCK)
        qt_ref[r] = q[rows, :].T
        vt = v[rows, :].T.astype(BF16)
        for h in range(MOBA_HEADS):
            vt_ref[r, h * VT_ROWS:h * VT_ROWS + HEAD_DIM, :] = vt[h * HEAD_DIM:(h + 1) * HEAD_DIM, :]
            vt_ref[r, h * VT_ROWS + HEAD_DIM:(h + 1) * VT_ROWS, :] = ones

    k = jnp.dot(xn, w_k_ref[...], preferred_element_type=F32)
    k = _group_rms_norm(k, bd128_ref[...], gk_ref[...], HEAD_DIM)
    kmean_ref[...] = jnp.zeros(kmean_ref.shape, F32)
    for r in range(tm // MOBA_BLOCK):
        rows = slice(r * MOBA_BLOCK, (r + 1) * MOBA_BLOCK)
        kmean_ref[0, r:r + 1, :] = jnp.sum(k[rows, :], axis=0, keepdims=True) * (1.0 / MOBA_BLOCK)

    shape = k.shape
    row = lax.broadcasted_iota(jnp.int32, shape, 0)
    lane = lax.broadcasted_iota(jnp.int32, shape, 1) & (AUG_W - 1)
    t = (pl.program_id(0) % (seq // tm)) * tm + row
    bias = t.astype(F32) * alibi_ref[...]
    b_hi = bias.astype(BF16).astype(F32)
    b_mid = (bias - b_hi).astype(BF16).astype(F32)
    b_lo = bias - b_hi - b_mid
    pieces = jnp.where(lane == AUG_ALIBI, b_hi, jnp.where(lane == AUG_ALIBI + 1, b_mid, b_lo))
    onehot = jnp.where(lane - AUG_MASK == lax.shift_right_logical(t, MOBA_BLOCK.bit_length() - 1), 1.0, 0.0)
    is_alibi = (lane >= AUG_ALIBI) & (lane < AUG_ALIBI + 3)
    kaug_ref[...] = (k + jnp.where(is_alibi, pieces, onehot)).astype(BF16)


def _in_proj(x1, g, w_lru, w_qvc, w_k, gq, gk, gc, bd64, bd128, alibi, seq):
    n, d = x1.shape
    tm = IN_PROJ_TILE
    rb = tm // MOBA_BLOCK
    nblk = n // MOBA_BLOCK
    n_lru = w_lru.shape[1] // 2
    wq = w_qvc.shape[1] // 3
    wk = w_k.shape[1]
    consts = (g, w_lru, w_qvc, w_k, gq, gk, gc, bd64, bd128, alibi)
    resident = sum(c.size * c.dtype.itemsize for c in consts)
    return pl.pallas_call(
        functools.partial(_in_proj_kernel, seq=seq),
        grid=(n // tm,),
        in_specs=[pl.BlockSpec((tm, d), lambda i: (i, 0))] + [_const_spec(c.shape) for c in consts],
        out_specs=[pl.BlockSpec((tm, n_lru), lambda i: (i, 0)),
                   pl.BlockSpec((tm, n_lru), lambda i: (i, 0)),
                   pl.BlockSpec((rb, wq, MOBA_BLOCK), lambda i: (i, 0, 0)),
                   pl.BlockSpec((tm, wk), lambda i: (i, 0)),
                   pl.BlockSpec((rb, MOBA_HEADS * VT_ROWS, MOBA_BLOCK), lambda i: (i, 0, 0)),
                   pl.BlockSpec((1, 8, wk), lambda i: (i, 0, 0)),
                   pl.BlockSpec((rb, wq, MOBA_BLOCK), lambda i: (i, 0, 0))],
        out_shape=(jax.ShapeDtypeStruct((n, n_lru), F32),
                   jax.ShapeDtypeStruct((n, n_lru), F32),
                   jax.ShapeDtypeStruct((nblk, wq, MOBA_BLOCK), F32),
                   jax.ShapeDtypeStruct((n, wk), BF16),
                   jax.ShapeDtypeStruct((nblk, MOBA_HEADS * VT_ROWS, MOBA_BLOCK), BF16),
                   jax.ShapeDtypeStruct((n // tm, 8, wk), F32),
                   jax.ShapeDtypeStruct((nblk, wq, MOBA_BLOCK), BF16)),
        compiler_params=pltpu.CompilerParams(dimension_semantics=("arbitrary",),
                                             vmem_limit_bytes=_vmem_limit(resident)),
        name="in_proj",
    )(x1, *consts)


def _gelu_tanh(x):
    c = math.sqrt(2.0 / math.pi)
    return 0.5 * x * (1.0 + jnp.tanh(c * (x + 0.044715 * (x * x * x))))


def _lru_kernel(x_ref, g_ref, cw_ref, cb_ref, wa_ref, ba_ref, wx_ref, bx_ref, lam_ref,
                o_ref, xpad_ref, a_ref, b_ref, hl_ref, carry_ref):
    tl, c = x_ref.shape
    half = V7X_MXU_DIM
    ti = pl.program_id(1)

    @pl.when(ti == 0)
    def _():
        xpad_ref[0:8, :] = jnp.zeros((8, c), F32)
        carry_ref[...] = jnp.zeros(carry_ref.shape, F32)

    xpad_ref[8:8 + tl, :] = x_ref[...]
    n_grp = tl // SUBLANES
    sub1 = lax.broadcasted_iota(jnp.int32, (n_grp, SUBLANES, LANES), 1)
    parts = []
    for k in range(c // LANES):
        lanes = slice(k * LANES, (k + 1) * LANES)
        xg = xpad_ref[:, lanes].reshape(n_grp + 1, SUBLANES, LANES)
        acc = cb_ref[:, lanes] + cw_ref[CONV_W - 1:CONV_W, lanes] * xg[1:]
        for d in range(1, CONV_W):
            rolled = pltpu.roll(xg, d, axis=1)
            delayed = jnp.where(sub1 >= d, rolled[1:], rolled[:-1])
            acc = acc + cw_ref[CONV_W - 1 - d:CONV_W - d, lanes] * delayed
        parts.append(acc.reshape(tl, LANES))
    xb = jnp.concatenate(parts, axis=1)
    xpad_ref[0:8, :] = x_ref[tl - 8:tl, :]

    xbb = xb.astype(BF16)

    def gate_tanh(w_ref, bias_ref):
        z_half = jnp.concatenate(
            [jnp.dot(xbb[:, :half], w_ref[0], preferred_element_type=F32),
             jnp.dot(xbb[:, half:], w_ref[1], preferred_element_type=F32)], axis=1)
        return jnp.tanh(z_half + bias_ref[...])

    neg_lam = -lam_ref[...]
    softplus = jnp.maximum(neg_lam, 0.0) + jnp.log1p(jnp.exp(-jnp.abs(neg_lam)))
    half_rate = (-0.5 * LRU_C) * softplus
    log_a = half_rate * gate_tanh(wa_ref, ba_ref) + half_rate
    i = 0.5 * gate_tanh(wx_ref, bx_ref) + 0.5
    a = jnp.exp(log_a)
    one_minus_a2 = 1.0 - a * a
    mult = jnp.where(one_minus_a2 > 0.0, one_minus_a2 * lax.rsqrt(one_minus_a2), 0.0)
    row = lax.broadcasted_iota(jnp.int32, (tl, c), 0)
    start_row = jnp.where(ti == 0, 0, -1)
    mult = jnp.where(row == start_row, 1.0, mult)
    bt = mult * i * xb

    groups = tl // SUBLANES
    sub = lax.broadcasted_iota(jnp.int32, (groups, SUBLANES, LANES), 1)
    for k in range(c // LANES):
        lanes = slice(k * LANES, (k + 1) * LANES)
        a3 = a[:, lanes].reshape(groups, SUBLANES, LANES)
        b3 = bt[:, lanes].reshape(groups, SUBLANES, LANES)
        d = 1
        while d < SUBLANES:
            has_prev = sub >= d
            a_prev = pltpu.roll(a3, d, axis=1)
            b_prev = pltpu.roll(b3, d, axis=1)
            b3 = b3 + jnp.where(has_prev, a3, 0.0) * b_prev
            a3 = jnp.where(has_prev, a3 * a_prev, a3)
            d *= 2
        a_ref[:, lanes] = a3.reshape(tl, LANES)
        b_ref[:, lanes] = b3.reshape(tl, LANES)

    h = carry_ref[...]
    for grp in range(groups):
        rows = slice(grp * SUBLANES, (grp + 1) * SUBLANES)
        h_last = jnp.broadcast_to(h[SUBLANES - 1:SUBLANES, :], (SUBLANES, c))
        h = a_ref[rows, :] * h_last + b_ref[rows, :]
        hl_ref[rows, :] = h
    carry_ref[...] = h

    o_ref[...] = (hl_ref[...] * _gelu_tanh(g_ref[...])).astype(o_ref.dtype)


def _lru(lru_x, lru_g, cw, cb, wa, ba, wx, bx, lam, batch, seq):
    n, c = lru_x.shape
    tl = LRU_TILE
    nt = seq // tl
    consts = (cw, cb, wa, ba, wx, bx, lam)
    tile = pl.BlockSpec((tl, c), lambda b, t: (b * nt + t, 0))
    return pl.pallas_call(
        _lru_kernel,
        grid=(batch, nt),
        in_specs=[tile, tile] + [_const_spec(w.shape) for w in consts],
        out_specs=tile,
        out_shape=jax.ShapeDtypeStruct((n, c), BF16),
        scratch_shapes=[pltpu.VMEM((tl + SUBLANES, c), F32)]
        + [pltpu.VMEM((tl, c), F32)] * 3
        + [pltpu.VMEM((SUBLANES, c), F32)],
        compiler_params=pltpu.CompilerParams(dimension_semantics=("arbitrary", "arbitrary")),
        name="lru",
    )(lru_x, lru_g, *consts)


def _moba_kernel(qta_ref, qtb_ref, cqta_ref, cqtb_ref, kaug_ref, vt_ref, kmean_ref, kc_ref, vct_ref,
                 oa_ref, ob_ref, oma_ref, omb_ref,
                 qaug_ref, qown_ref, qmem_ref, s_ref, p_ref, alpha_ref, m_ref, acc_ref, accm_ref,
                 ot_ref, otm_ref):
    step = pl.program_id(1)
    nblk = kmean_ref.shape[1]
    blk = MOBA_BLOCK
    q_blocks = (step, nblk - 1 - step)
    qt_refs = (qta_ref, qtb_ref)
    cqt_refs = (cqta_ref, cqtb_ref)

    def build_query_operands(side, h):
        sh = side * MOBA_HEADS + h
        qt = qt_refs[side][0, 0, h * HEAD_DIM:(h + 1) * HEAD_DIM, :]
        qs = (qt * (HEAD_DIM ** -0.5 * LOG2E)).astype(BF16)
        sub = lax.broadcasted_iota(jnp.int32, (AUG_MASK - AUG_ALIBI, blk), 0)
        ones = jnp.where(sub < 3, 1.0, 0.0).astype(BF16)
        for ref in (qaug_ref, qown_ref):
            ref[sh, 0:HEAD_DIM, :] = qs
            ref[sh, AUG_ALIBI:AUG_MASK, :] = ones
        qown_ref[sh, AUG_MASK:, :] = jnp.zeros((AUG_W - AUG_MASK, blk), BF16)
        qmem_ref[sh, 0:HEAD_DIM, :] = cqt_refs[side][0, 0, h * HEAD_DIM:(h + 1) * HEAD_DIM, :]
        qmem_ref[sh, HEAD_DIM:, :] = jnp.zeros((AUG_W - HEAD_DIM, blk), BF16)

    def build_block_mask(side, h):
        sh = side * MOBA_HEADS + h
        blk_id = lax.broadcasted_iota(jnp.int32, (nblk, blk), 0).astype(F32)
        qt = qt_refs[side][0, 0, h * HEAD_DIM:(h + 1) * HEAD_DIM, :]
        gate = jnp.dot(kmean_ref[0, :, h * AUG_W:h * AUG_W + HEAD_DIM], qt,
                       preferred_element_type=F32, precision=lax.Precision.HIGHEST)
        gate = jnp.where(blk_id < q_blocks[side].astype(F32), gate, -jnp.inf)
        keep = blk_id < 0.0
        for _ in range(MOBA_TOPK):
            best = jnp.max(gate, axis=0, keepdims=True)
            first = jnp.min(jnp.where(gate == best, blk_id, float(nblk)), axis=0, keepdims=True)
            pick = (blk_id == first) & (best > -jnp.inf)
            keep = keep | pick
            gate = jnp.where(pick, -jnp.inf, gate)
        qaug_ref[sh, AUG_MASK:AUG_MASK + nblk, :] = jnp.where(keep, 0.0, NEG_BIG).astype(BF16)
        qaug_ref[sh, AUG_MASK + nblk:, :] = jnp.zeros((AUG_W - AUG_MASK - nblk, blk), BF16)

    kk = lax.broadcasted_iota(jnp.int32, (blk, blk), 0)
    qq = lax.broadcasted_iota(jnp.int32, (blk, blk), 1)
    causal = kk <= qq

    per_trip = PAST_UNROLL * MOBA_HEADS
    lead_groups = (("own", 0), ("own", 1), ("mem", 0), ("mem", 1))
    lead = len(lead_groups) * MOBA_HEADS
    n_trips = (nblk - 1 + PAST_UNROLL - 1) // PAST_UNROLL

    def stream_item(trip, n):
        if n < 0:
            idx = n + lead
            kind, side = lead_groups[idx // MOBA_HEADS]
            return kind, side, q_blocks[side], idx % MOBA_HEADS, idx % SCORE_SLOTS
        if n >= per_trip:
            trip, n = trip + 1, n - per_trip
        pos = trip * PAST_UNROLL + n // MOBA_HEADS
        side = (pos >= step).astype(jnp.int32)
        block = jnp.minimum(pos - side * step, nblk - 1)
        return "past", side, block, n % MOBA_HEADS, (n + lead) % SCORE_SLOTS

    def scores(kind, side, j, h, slot):
        lanes = slice(h * AUG_W, (h + 1) * AUG_W)
        sh = side * MOBA_HEADS + h
        if kind == "mem":
            keys, q_ref = kc_ref[0, :, lanes], qmem_ref
        else:
            keys, q_ref = kaug_ref[0, j, :, lanes], (qown_ref if kind == "own" else qaug_ref)
        s_ref[slot] = jnp.dot(keys, q_ref[sh], preferred_element_type=F32)

    def probabilities(kind, side, j, h, slot):
        s = s_ref[slot]
        sh = side * MOBA_HEADS + h
        if kind == "past":
            m_old = m_ref[sh]
            m_new = jnp.maximum(m_old, jnp.max(s, axis=0, keepdims=True))
            alpha_ref[slot % PROB_SLOTS] = jnp.exp2(m_old - m_new)
        else:
            if kind == "own":
                s = jnp.where(causal, s, NEG_BIG)
            m_new = jnp.max(s, axis=0, keepdims=True)
        p_ref[slot % PROB_SLOTS] = jnp.exp2(s - m_new).astype(BF16)
        if kind != "mem":
            m_ref[sh] = m_new

    def accumulate(kind, side, j, h, slot):
        rows = slice(h * VT_ROWS, (h + 1) * VT_ROWS)
        sh = side * MOBA_HEADS + h
        p = p_ref[slot % PROB_SLOTS]
        if kind == "mem":
            accm_ref[sh] = jnp.dot(vct_ref[0, rows, :], p, preferred_element_type=F32)
        else:
            pv = jnp.dot(vt_ref[0, j, rows, :], p, preferred_element_type=F32)
            acc_ref[sh] = pv if kind == "own" else alpha_ref[slot % PROB_SLOTS] * acc_ref[sh] + pv

    for h in range(MOBA_HEADS):
        for side in range(2):
            build_query_operands(side, h)
            build_block_mask(side, h)
    for n in range(-lead, -lead + SCORE_LOOKAHEAD):
        scores(*stream_item(0, n))
    for n in range(-lead, -lead + PROB_LOOKAHEAD):
        probabilities(*stream_item(0, n))
    for n in range(-lead, 0):
        scores(*stream_item(0, n + SCORE_LOOKAHEAD))
        probabilities(*stream_item(0, n + PROB_LOOKAHEAD))
        accumulate(*stream_item(0, n))

    def past_trip(trip, carry):
        for n in range(per_trip):
            scores(*stream_item(trip, n + SCORE_LOOKAHEAD))
            probabilities(*stream_item(trip, n + PROB_LOOKAHEAD))
            accumulate(*stream_item(trip, n))
        return carry

    lax.fori_loop(0, n_trips, past_trip, 0)

    for side, o_ref, om_ref in ((0, oa_ref, oma_ref), (1, ob_ref, omb_ref)):
        for ref, t_ref, out_ref in ((acc_ref, ot_ref, o_ref), (accm_ref, otm_ref, om_ref)):
            for h in range(MOBA_HEADS):
                sh = side * MOBA_HEADS + h
                t_ref[side, h * HEAD_DIM:(h + 1) * HEAD_DIM, :] = (ref[sh, 0:HEAD_DIM, :]
                                                                   / ref[sh, HEAD_DIM:HEAD_DIM + 1, :])
            out_ref[...] = t_ref[side].T.astype(out_ref.dtype)


def _moba(qt, cqt, kaug, vt, kmean, kc, vct, batch, seq):
    nblk = seq // MOBA_BLOCK
    width = MOBA_HEADS * HEAD_DIM
    aug = MOBA_HEADS * AUG_W
    vt_rows = MOBA_HEADS * VT_ROWS
    mlen = vct.shape[2]
    assert mlen == MOBA_BLOCK and MEM_HEADS == MOBA_HEADS, "memory items reuse the key-block buffers"
    qt = qt.reshape(batch, nblk, width, MOBA_BLOCK)
    cqt = cqt.reshape(batch, nblk, width, MOBA_BLOCK)
    kaug = kaug.reshape(batch, nblk, MOBA_BLOCK, aug)
    vt = vt.reshape(batch, nblk, vt_rows, MOBA_BLOCK)
    kc = kc.reshape(batch, mlen, aug)
    resident = 2 * (kaug.size // batch + vt.size // batch) * 2
    assert nblk % 2 == 0
    half = nblk // 2
    sides = 2 * MOBA_HEADS
    qblock_a = pl.BlockSpec((1, 1, width, MOBA_BLOCK), lambda b, s: (b, s, 0, 0))
    qblock_b = pl.BlockSpec((1, 1, width, MOBA_BLOCK), lambda b, s: (b, nblk - 1 - s, 0, 0))
    out_a = pl.BlockSpec((MOBA_BLOCK, width), lambda b, s: (b * half + s, 0))
    out_b = pl.BlockSpec((MOBA_BLOCK, width), lambda b, s: (b * half + half - 1 - s, 0))
    out_shape = jax.ShapeDtypeStruct((batch * half * MOBA_BLOCK, width), BF16)
    y_lo, y_hi, m_lo, m_hi = pl.pallas_call(
        _moba_kernel,
        grid=(batch, half),
        in_specs=[qblock_a, qblock_b, qblock_a, qblock_b,
                  pl.BlockSpec((1, nblk, MOBA_BLOCK, aug), lambda b, s: (b, 0, 0, 0)),
                  pl.BlockSpec((1, nblk, vt_rows, MOBA_BLOCK), lambda b, s: (b, 0, 0, 0)),
                  pl.BlockSpec((1, nblk, aug), lambda b, s: (b, 0, 0)),
                  pl.BlockSpec((1, mlen, aug), lambda b, s: (b, 0, 0)),
                  pl.BlockSpec((1, vt_rows, mlen), lambda b, s: (b, 0, 0))],
        out_specs=(out_a, out_b, out_a, out_b),
        out_shape=(out_shape,) * 4,
        scratch_shapes=[pltpu.VMEM((sides, AUG_W, MOBA_BLOCK), BF16),
                        pltpu.VMEM((sides, AUG_W, MOBA_BLOCK), BF16),
                        pltpu.VMEM((sides, AUG_W, MOBA_BLOCK), BF16),
                        pltpu.VMEM((SCORE_SLOTS, MOBA_BLOCK, MOBA_BLOCK), F32),
                        pltpu.VMEM((PROB_SLOTS, MOBA_BLOCK, MOBA_BLOCK), BF16),
                        pltpu.VMEM((PROB_SLOTS, 1, MOBA_BLOCK), F32),
                        pltpu.VMEM((sides, 1, MOBA_BLOCK), F32),
                        pltpu.VMEM((sides, VT_ROWS, MOBA_BLOCK), F32),
                        pltpu.VMEM((sides, VT_ROWS, MOBA_BLOCK), F32),
                        pltpu.VMEM((2, width, MOBA_BLOCK), F32),
                        pltpu.VMEM((2, width, MOBA_BLOCK), F32)],
        compiler_params=pltpu.CompilerParams(dimension_semantics=("arbitrary", "arbitrary"),
                                             vmem_limit_bytes=_vmem_limit(resident)),
        name="moba",
    )(qt, qt, cqt, cqt, kaug, vt, kmean, kc, vct)

    return (y_lo, y_hi), (m_lo, m_hi)


def _block_diag_ones(group):
    idx = jnp.arange(V7X_MXU_DIM) // group
    return (idx[:, None] == idx[None, :]).astype(BF16)


def _pack_block_diag(w):
    nb, bw, _ = w.shape
    per = V7X_MXU_DIM // bw
    w = w.reshape(nb // per, per, bw, bw)
    eye = jnp.eye(per, dtype=w.dtype)
    return jnp.einsum("gpij,pq->gpiqj", w, eye).reshape(nb // per, V7X_MXU_DIM, V7X_MXU_DIM)


def _pad_heads(w, heads):
    lead = w.shape[:-1]
    w = w.reshape(lead + (heads, HEAD_DIM))
    w = jnp.pad(w, [(0, 0)] * len(lead) + [(0, 0), (0, AUG_W - HEAD_DIM)])
    return w.reshape(lead + (heads * AUG_W,))


def _layer(x, mem, ffn1_norm, ffn1_w_in, ffn1_w_out, mix_norm, mem_norm, w_in,
           lru_conv_w, lru_conv_b, lru_a_w, lru_a_b, lru_x_w, lru_x_b, lru_lambda,
           moba_q_norm, moba_k_norm, mem_w_kv, mem_q_norm, mem_k_norm, w_out,
           ffn2_norm, ffn2_w_in, ffn2_w_out):
    batch, seq, d = x.shape
    n = batch * seq
    n_lru = lru_lambda.shape[0]
    wq = MOBA_HEADS * HEAD_DIM
    wc = MEM_HEADS * HEAD_DIM
    row = lambda v: v.reshape(1, -1).astype(F32)

    bd64 = _block_diag_ones(HEAD_DIM)
    bd128 = _block_diag_ones(AUG_W)

    w_in_b = w_in.astype(BF16)
    o = 2 * n_lru
    w_lru = w_in_b[:, :o]
    w_k = _pad_heads(w_in_b[:, o + wq:o + 2 * wq], MOBA_HEADS)
    w_qvc = jnp.concatenate([w_in_b[:, o:o + wq], w_in_b[:, o + 2 * wq:]], axis=1)
    gq = row(jnp.tile(moba_q_norm, MOBA_HEADS))
    gk = row(_pad_heads(jnp.tile(moba_k_norm, MOBA_HEADS), MOBA_HEADS))
    gc = row(jnp.tile(mem_q_norm, MEM_HEADS))
    gck = row(_pad_heads(jnp.tile(mem_k_norm, MEM_HEADS), MEM_HEADS))
    w_kv_b = mem_w_kv.astype(BF16)
    w_ck = _pad_heads(w_kv_b[:, :wc], MEM_HEADS)
    w_cv = w_kv_b[:, wc:]
    slopes = 2.0 ** (-8.0 * jnp.arange(1, MOBA_HEADS + 1, dtype=F32) / MOBA_HEADS)
    lane = jnp.arange(AUG_W)
    is_alibi = (lane >= AUG_ALIBI) & (lane < AUG_ALIBI + 3)
    alibi = row(jnp.where(is_alibi[None, :], slopes[:, None] * LOG2E, 0.0))

    x2d = x.reshape(n, d)
    kc, vct = _mem_kv(mem.reshape(-1, d), row(mem_norm), w_ck, w_cv, bd128, gck, batch)

    x1 = _ffn(x2d, row(ffn1_norm), ffn1_w_in, ffn1_w_out)

    lru_x, lru_g, qt, kaug, vt, kmean, cqt = _in_proj(
        x1, row(mix_norm), w_lru, w_qvc, w_k, gq, gk, gc, bd64, bd128, alibi, seq)
    kmean = kmean[:, :IN_PROJ_TILE // MOBA_BLOCK, :].reshape(batch, seq // MOBA_BLOCK, -1)

    y_lru = _lru(lru_x, lru_g, lru_conv_w.astype(F32), row(lru_conv_b),
                 _pack_block_diag(0.5 * lru_a_w).astype(BF16), row(0.5 * lru_a_b),
                 _pack_block_diag(0.5 * lru_x_w).astype(BF16), row(0.5 * lru_x_b),
                 row(lru_lambda), batch, seq)
    y_moba, y_mem = _moba(qt, cqt, kaug, vt, kmean, kc, vct, batch, seq)

    out = _out_ffn(x1, y_lru, y_moba, y_mem, w_out.astype(BF16), row(ffn2_norm),
                   ffn2_w_in, ffn2_w_out, seq)
    return out.reshape(batch, seq, d)


def kernel(x, mem, ffn1_norm, ffn1_w_in, ffn1_w_out, mix_norm, mem_norm, w_in, lru_conv_w, lru_conv_b,
           lru_a_w, lru_a_b, lru_x_w, lru_x_b, lru_lambda, moba_q_norm, moba_k_norm, mem_w_kv,
           mem_q_norm, mem_k_norm, w_out, ffn2_norm, ffn2_w_in, ffn2_w_out):
    params = (ffn1_norm, ffn1_w_in, ffn1_w_out, mix_norm, mem_norm, w_in, lru_conv_w, lru_conv_b,
              lru_a_w, lru_a_b, lru_x_w, lru_x_b, lru_lambda, moba_q_norm, moba_k_norm, mem_w_kv,
              mem_q_norm, mem_k_norm, w_out, ffn2_norm, ffn2_w_in, ffn2_w_out)
    for layer in range(ffn1_norm.shape[0]):
        x = _layer(x, mem, *(p[layer] for p in params))
    return x
```

```python
import functools
import math

import jax
import jax.numpy as jnp
from jax import lax
from jax.experimental import pallas as pl
from jax.experimental.pallas import tpu as pltpu

F32 = jnp.float32
BF16 = jnp.bfloat16

HEAD_DIM = 64
CONV_W = 4
LRU_C = 8.0
MOBA_HEADS = 4
MOBA_BLOCK = 256
MOBA_TOPK = 3
MEM_HEADS = 4
NORM_EPS = 1e-6
LOG2E = 1.4426950408889634
NEG_BIG = -1e30

V7X_MXU_DIM = 256
LANES = 128
SUBLANES = 8
V7X_VMEM_BYTES = 64 * 1024 * 1024

AUG_W = 128
AUG_ALIBI = 64
AUG_MASK = 80
assert AUG_W & (AUG_W - 1) == 0 and MOBA_BLOCK & (MOBA_BLOCK - 1) == 0
VT_ROWS = HEAD_DIM + 16
SCORE_LOOKAHEAD = 5
PAST_UNROLL = 32
SCORE_SLOTS = 8
assert MOBA_HEADS <= SCORE_LOOKAHEAD < SCORE_SLOTS and (PAST_UNROLL * MOBA_HEADS) % SCORE_SLOTS == 0
assert (MOBA_HEADS + MEM_HEADS) % SCORE_SLOTS == 0
PROB_SLOTS = 4
PROB_LOOKAHEAD = 1
assert PROB_LOOKAHEAD < SCORE_LOOKAHEAD

TOKEN_TILE = 512
IN_PROJ_TILE = 1024
LRU_TILE = 512
FFN_CHUNK = 256


def _vmem_limit(resident_bytes):
    return int(min(V7X_VMEM_BYTES - 8 * 1024 * 1024, resident_bytes + 24 * 1024 * 1024))


def _const_spec(shape):
    n = len(shape)
    return pl.BlockSpec(shape, lambda *_: (0,) * n, pipeline_mode=pl.Buffered(1))


def _sigmoid(x):
    return 0.5 * jnp.tanh(0.5 * x) + 0.5


def _rms_norm(x, g):
    ms = jnp.mean(x * x, axis=-1, keepdims=True)
    return x * lax.rsqrt(ms + NORM_EPS) * g


def _group_rms_norm(u, bd, g, group):
    sq = u * u
    hi = sq.astype(BF16)
    lo = (sq - hi.astype(F32)).astype(BF16)
    parts = []
    for c in range(u.shape[1] // V7X_MXU_DIM):
        sl = slice(c * V7X_MXU_DIM, (c + 1) * V7X_MXU_DIM)
        parts.append(jnp.dot(hi[:, sl], bd, preferred_element_type=F32)
                     + jnp.dot(lo[:, sl], bd, preferred_element_type=F32))
    ss = parts[0] if len(parts) == 1 else jnp.concatenate(parts, axis=1)
    return u * lax.rsqrt(ss * (1.0 / group) + NORM_EPS) * g


def _swiglu_half_step(x, g_ref, w_in_ref, w_out_ref, act_ref):
    d_ff = w_out_ref.shape[0]
    xn = _rms_norm(x, g_ref[...]).astype(BF16)
    for c in range(d_ff // FFN_CHUNK):
        lo = c * FFN_CHUNK
        a = jnp.dot(xn, w_in_ref[:, lo:lo + FFN_CHUNK].astype(BF16), preferred_element_type=F32)
        b = jnp.dot(xn, w_in_ref[:, d_ff + lo:d_ff + lo + FFN_CHUNK].astype(BF16),
                    preferred_element_type=F32)
        act_ref[:, lo:lo + FFN_CHUNK] = (a * _sigmoid(a) * b).astype(BF16)
    y = jnp.dot(act_ref[...], w_out_ref[...].astype(BF16), preferred_element_type=F32)
    return x + 0.5 * y


def _ffn_kernel(x_ref, g_ref, w_in_ref, w_out_ref, o_ref, act_ref):
    o_ref[...] = _swiglu_half_step(x_ref[...], g_ref, w_in_ref, w_out_ref, act_ref)


def _ffn(x, g, w_in, w_out):
    n, d = x.shape
    d_ff = w_out.shape[0]
    tm = TOKEN_TILE
    resident = sum(w.size * w.dtype.itemsize for w in (w_in, w_out))
    return pl.pallas_call(
        _ffn_kernel,
        grid=(n // tm,),
        in_specs=[pl.BlockSpec((tm, d), lambda i: (i, 0)),
                  _const_spec((1, d)), _const_spec(w_in.shape), _const_spec(w_out.shape)],
        out_specs=pl.BlockSpec((tm, d), lambda i: (i, 0)),
        out_shape=jax.ShapeDtypeStruct((n, d), F32),
        scratch_shapes=[pltpu.VMEM((tm, d_ff), BF16)],
        compiler_params=pltpu.CompilerParams(dimension_semantics=("arbitrary",),
                                             vmem_limit_bytes=_vmem_limit(resident)),
        name="ffn1",
    )(x, g, w_in, w_out)


def _out_ffn_kernel(x_ref, ylru_ref, ymoba_lo_ref, ymoba_hi_ref, ymem_lo_ref, ymem_hi_ref,
                    wo_ref, g_ref, w_in_ref, w_out_ref, o_ref, act_ref, *, tiles_per_seq):
    n_lru = ylru_ref.shape[1]
    n_moba = ymoba_lo_ref.shape[1]
    first_half = pl.program_id(0) % tiles_per_seq < tiles_per_seq // 2
    y_moba = jnp.where(first_half, ymoba_lo_ref[...], ymoba_hi_ref[...])
    y_mem = jnp.where(first_half, ymem_lo_ref[...], ymem_hi_ref[...])
    x2 = (x_ref[...]
          + jnp.dot(ylru_ref[...], wo_ref[0:n_lru, :], preferred_element_type=F32)
          + jnp.dot(y_moba, wo_ref[n_lru:n_lru + n_moba, :], preferred_element_type=F32)
          + jnp.dot(y_mem, wo_ref[n_lru + n_moba:, :], preferred_element_type=F32))
    o_ref[...] = _swiglu_half_step(x2, g_ref, w_in_ref, w_out_ref, act_ref)


def _out_ffn(x1, y_lru, y_moba, y_mem, w_o, g, w_in, w_out, seq):
    n, d = x1.shape
    d_ff = w_out.shape[0]
    tm = TOKEN_TILE
    tps = seq // tm
    half = tps // 2
    assert tps % 2 == 0
    resident = sum(w.size * w.dtype.itemsize for w in (w_in, w_out, w_o))

    def tile(w):
        return pl.BlockSpec((tm, w), lambda i: (i, 0))

    def lo_tile(w):
        return pl.BlockSpec((tm, w), lambda i: ((i // tps) * half + jnp.minimum(i % tps, half - 1), 0))

    def hi_tile(w):
        return pl.BlockSpec((tm, w), lambda i: ((i // tps) * half + jnp.maximum(i % tps - half, 0), 0))

    wa = y_moba[0].shape[1]
    wm = y_mem[0].shape[1]
    return pl.pallas_call(
        functools.partial(_out_ffn_kernel, tiles_per_seq=tps),
        grid=(n // tm,),
        in_specs=[tile(d), tile(y_lru.shape[1]), lo_tile(wa), hi_tile(wa), lo_tile(wm), hi_tile(wm),
                  _const_spec(w_o.shape), _const_spec((1, d)),
                  _const_spec(w_in.shape), _const_spec(w_out.shape)],
        out_specs=tile(d),
        out_shape=jax.ShapeDtypeStruct((n, d), F32),
        scratch_shapes=[pltpu.VMEM((tm, d_ff), BF16)],
        compiler_params=pltpu.CompilerParams(dimension_semantics=("arbitrary",),
                                             vmem_limit_bytes=_vmem_limit(resident)),
        name="out_ffn2",
    )(x1, y_lru, *y_moba, *y_mem, w_o, g, w_in, w_out)


def _mem_kv_kernel(mem_ref, g_ref, wk_ref, wv_ref, bd_ref, gk_ref, kc_ref, vct_ref):
    batch, _, mlen = vct_ref.shape
    mn = _rms_norm(mem_ref[...], g_ref[...]).astype(BF16)
    k = jnp.dot(mn, wk_ref[...], preferred_element_type=F32)
    kc_ref[...] = _group_rms_norm(k, bd_ref[...], gk_ref[...], HEAD_DIM).astype(BF16)
    v = jnp.dot(mn, wv_ref[...], preferred_element_type=F32)
    ones = jnp.ones((VT_ROWS - HEAD_DIM, mlen), BF16)
    for b in range(batch):
        vt = v[b * mlen:(b + 1) * mlen, :].T.astype(BF16)
        for h in range(MEM_HEADS):
            vct_ref[b, h * VT_ROWS:h * VT_ROWS + HEAD_DIM, :] = vt[h * HEAD_DIM:(h + 1) * HEAD_DIM, :]
            vct_ref[b, h * VT_ROWS + HEAD_DIM:(h + 1) * VT_ROWS, :] = ones


def _mem_kv(mem2d, g, w_k, w_v, bd128, gk, batch):
    m = mem2d.shape[0]
    return pl.pallas_call(
        _mem_kv_kernel,
        out_shape=(jax.ShapeDtypeStruct((m, w_k.shape[1]), BF16),
                   jax.ShapeDtypeStruct((batch, MEM_HEADS * VT_ROWS, m // batch), BF16)),
        name="mem_kv",
    )(mem2d, g, w_k, w_v, bd128, gk)


def _in_proj_kernel(x_ref, g_ref, w_lru_ref, w_qvc_ref, w_k_ref,
                    gq_ref, gk_ref, gc_ref, bd64_ref, bd128_ref, alibi_ref,
                    lrux_ref, lrug_ref, qt_ref, kaug_ref, vt_ref, kmean_ref, cqt_ref, *, seq):
    tm = x_ref.shape[0]
    n_lru = lrux_ref.shape[1]
    xn = _rms_norm(x_ref[...], g_ref[...]).astype(BF16)

    lrux_ref[...] = jnp.dot(xn, w_lru_ref[:, :n_lru], preferred_element_type=F32)
    lrug_ref[...] = jnp.dot(xn, w_lru_ref[:, n_lru:], preferred_element_type=F32)

    wq = qt_ref.shape[1]
    qvc = jnp.dot(xn, w_qvc_ref[...], preferred_element_type=F32)

    cq = _group_rms_norm(qvc[:, 2 * wq:], bd64_ref[...], gc_ref[...], HEAD_DIM) * (HEAD_DIM ** -0.5 * LOG2E)
    for r in range(tm // MOBA_BLOCK):
        cqt_ref[r] = cq[r * MOBA_BLOCK:(r + 1) * MOBA_BLOCK, :].T.astype(BF16)

    q = _group_rms_norm(qvc[:, :wq], bd64_ref[...], gq_ref[...], HEAD_DIM)
    v = qvc[:, wq:2 * wq]
    ones = jnp.ones((VT_ROWS - HEAD_DIM, MOBA_BLOCK), BF16)
    for r in range(tm // MOBA_BLOCK):
        rows = slice(r * MOBA_BLOCK, (r + 1) * MOBA_BLOCK)
        qt_ref[r] = q[rows, :].T
        vt = v[rows, :].T.astype(BF16)
        for h in range(MOBA_HEADS):
            vt_ref[r, h * VT_ROWS:h * VT_ROWS + HEAD_DIM, :] = vt[h * HEAD_DIM:(h + 1) * HEAD_DIM, :]
            vt_ref[r, h * VT_ROWS + HEAD_DIM:(h + 1) * VT_ROWS, :] = ones

    k = jnp.dot(xn, w_k_ref[...], preferred_element_type=F32)
    k = _group_rms_norm(k, bd128_ref[...], gk_ref[...], HEAD_DIM)
    kmean_ref[...] = jnp.zeros(kmean_ref.shape, F32)
    for r in range(tm // MOBA_BLOCK):
        rows = slice(r * MOBA_BLOCK, (r + 1) * MOBA_BLOCK)
        kmean_ref[0, r:r + 1, :] = jnp.sum(k[rows, :], axis=0, keepdims=True) * (1.0 / MOBA_BLOCK)

    shape = k.shape
    row = lax.broadcasted_iota(jnp.int32, shape, 0)
    lane = lax.broadcasted_iota(jnp.int32, shape, 1) & (AUG_W - 1)
    t = (pl.program_id(0) % (seq // tm)) * tm + row
    bias = t.astype(F32) * alibi_ref[...]
    b_hi = bias.astype(BF16).astype(F32)
    b_mid = (bias - b_hi).astype(BF16).astype(F32)
    b_lo = bias - b_hi - b_mid
    pieces = jnp.where(lane == AUG_ALIBI, b_hi, jnp.where(lane == AUG_ALIBI + 1, b_mid, b_lo))
    onehot = jnp.where(lane - AUG_MASK == lax.shift_right_logical(t, MOBA_BLOCK.bit_length() - 1), 1.0, 0.0)
    is_alibi = (lane >= AUG_ALIBI) & (lane < AUG_ALIBI + 3)
    kaug_ref[...] = (k + jnp.where(is_alibi, pieces, onehot)).astype(BF16)


def _in_proj(x1, g, w_lru, w_qvc, w_k, gq, gk, gc, bd64, bd128, alibi, seq):
    n, d = x1.shape
    tm = IN_PROJ_TILE
    rb = tm // MOBA_BLOCK
    nblk = n // MOBA_BLOCK
    n_lru = w_lru.shape[1] // 2
    wq = w_qvc.shape[1] // 3
    wk = w_k.shape[1]
    consts = (g, w_lru, w_qvc, w_k, gq, gk, gc, bd64, bd128, alibi)
    resident = sum(c.size * c.dtype.itemsize for c in consts)
    return pl.pallas_call(
        functools.partial(_in_proj_kernel, seq=seq),
        grid=(n // tm,),
        in_specs=[pl.BlockSpec((tm, d), lambda i: (i, 0))] + [_const_spec(c.shape) for c in consts],
        out_specs=[pl.BlockSpec((tm, n_lru), lambda i: (i, 0)),
                   pl.BlockSpec((tm, n_lru), lambda i: (i, 0)),
                   pl.BlockSpec((rb, wq, MOBA_BLOCK), lambda i: (i, 0, 0)),
                   pl.BlockSpec((tm, wk), lambda i: (i, 0)),
                   pl.BlockSpec((rb, MOBA_HEADS * VT_ROWS, MOBA_BLOCK), lambda i: (i, 0, 0)),
                   pl.BlockSpec((1, 8, wk), lambda i: (i, 0, 0)),
                   pl.BlockSpec((rb, wq, MOBA_BLOCK), lambda i: (i, 0, 0))],
        out_shape=(jax.ShapeDtypeStruct((n, n_lru), F32),
                   jax.ShapeDtypeStruct((n, n_lru), F32),
                   jax.ShapeDtypeStruct((nblk, wq, MOBA_BLOCK), F32),
                   jax.ShapeDtypeStruct((n, wk), BF16),
                   jax.ShapeDtypeStruct((nblk, MOBA_HEADS * VT_ROWS, MOBA_BLOCK), BF16),
                   jax.ShapeDtypeStruct((n // tm, 8, wk), F32),
                   jax.ShapeDtypeStruct((nblk, wq, MOBA_BLOCK), BF16)),
        compiler_params=pltpu.CompilerParams(dimension_semantics=("arbitrary",),
                                             vmem_limit_bytes=_vmem_limit(resident)),
        name="in_proj",
    )(x1, *consts)


def _gelu_tanh(x):
    c = math.sqrt(2.0 / math.pi)
    return 0.5 * x * (1.0 + jnp.tanh(c * (x + 0.044715 * (x * x * x))))


def _lru_kernel(x_ref, g_ref, cw_ref, cb_ref, wa_ref, ba_ref, wx_ref, bx_ref, lam_ref,
                o_ref, xpad_ref, a_ref, b_ref, hl_ref, carry_ref):
    tl, c = x_ref.shape
    half = V7X_MXU_DIM
    ti = pl.program_id(1)

    @pl.when(ti == 0)
    def _():
        xpad_ref[0:8, :] = jnp.zeros((8, c), F32)
        carry_ref[...] = jnp.zeros(carry_ref.shape, F32)

    xpad_ref[8:8 + tl, :] = x_ref[...]
    n_grp = tl // SUBLANES
    sub1 = lax.broadcasted_iota(jnp.int32, (n_grp, SUBLANES, LANES), 1)
    parts = []
    for k in range(c // LANES):
        lanes = slice(k * LANES, (k + 1) * LANES)
        xg = xpad_ref[:, lanes].reshape(n_grp + 1, SUBLANES, LANES)
        acc = cb_ref[:, lanes] + cw_ref[CONV_W - 1:CONV_W, lanes] * xg[1:]
        for d in range(1, CONV_W):
            rolled = pltpu.roll(xg, d, axis=1)
            delayed = jnp.where(sub1 >= d, rolled[1:], rolled[:-1])
            acc = acc + cw_ref[CONV_W - 1 - d:CONV_W - d, lanes] * delayed
        parts.append(acc.reshape(tl, LANES))
    xb = jnp.concatenate(parts, axis=1)
    xpad_ref[0:8, :] = x_ref[tl - 8:tl, :]

    xbb = xb.astype(BF16)

    def gate_tanh(w_ref, bias_ref):
        z_half = jnp.concatenate(
            [jnp.dot(xbb[:, :half], w_ref[0], preferred_element_type=F32),
             jnp.dot(xbb[:, half:], w_ref[1], preferred_element_type=F32)], axis=1)
        return jnp.tanh(z_half + bias_ref[...])

    neg_lam = -lam_ref[...]
    softplus = jnp.maximum(neg_lam, 0.0) + jnp.log1p(jnp.exp(-jnp.abs(neg_lam)))
    half_rate = (-0.5 * LRU_C) * softplus
    log_a = half_rate * gate_tanh(wa_ref, ba_ref) + half_rate
    i = 0.5 * gate_tanh(wx_ref, bx_ref) + 0.5
    a = jnp.exp(log_a)
    one_minus_a2 = 1.0 - a * a
    mult = jnp.where(one_minus_a2 > 0.0, one_minus_a2 * lax.rsqrt(one_minus_a2), 0.0)
    row = lax.broadcasted_iota(jnp.int32, (tl, c), 0)
    start_row = jnp.where(ti == 0, 0, -1)
    mult = jnp.where(row == start_row, 1.0, mult)
    bt = mult * i * xb

    groups = tl // SUBLANES
    sub = lax.broadcasted_iota(jnp.int32, (groups, SUBLANES, LANES), 1)
    for k in range(c // LANES):
        lanes = slice(k * LANES, (k + 1) * LANES)
        a3 = a[:, lanes].reshape(groups, SUBLANES, LANES)
        b3 = bt[:, lanes].reshape(groups, SUBLANES, LANES)
        d = 1
        while d < SUBLANES:
            has_prev = sub >= d
            a_prev = pltpu.roll(a3, d, axis=1)
            b_prev = pltpu.roll(b3, d, axis=1)
            b3 = b3 + jnp.where(has_prev, a3, 0.0) * b_prev
            a3 = jnp.where(has_prev, a3 * a_prev, a3)
            d *= 2
        a_ref[:, lanes] = a3.reshape(tl, LANES)
        b_ref[:, lanes] = b3.reshape(tl, LANES)

    h = carry_ref[...]
    for grp in range(groups):
        rows = slice(grp * SUBLANES, (grp + 1) * SUBLANES)
        h_last = jnp.broadcast_to(h[SUBLANES - 1:SUBLANES, :], (SUBLANES, c))
        h = a_ref[rows, :] * h_last + b_ref[rows, :]
        hl_ref[rows, :] = h
    carry_ref[...] = h

    o_ref[...] = (hl_ref[...] * _gelu_tanh(g_ref[...])).astype(o_ref.dtype)


def _lru(lru_x, lru_g, cw, cb, wa, ba, wx, bx, lam, batch, seq):
    n, c = lru_x.shape
    tl = LRU_TILE
    nt = seq // tl
    consts = (cw, cb, wa, ba, wx, bx, lam)
    tile = pl.BlockSpec((tl, c), lambda b, t: (b * nt + t, 0))
    return pl.pallas_call(
        _lru_kernel,
        grid=(batch, nt),
        in_specs=[tile, tile] + [_const_spec(w.shape) for w in consts],
        out_specs=tile,
        out_shape=jax.ShapeDtypeStruct((n, c), BF16),
        scratch_shapes=[pltpu.VMEM((tl + SUBLANES, c), F32)]
        + [pltpu.VMEM((tl, c), F32)] * 3
        + [pltpu.VMEM((SUBLANES, c), F32)],
        compiler_params=pltpu.CompilerParams(dimension_semantics=("arbitrary", "arbitrary")),
        name="lru",
    )(lru_x, lru_g, *consts)


def _moba_kernel(qta_ref, qtb_ref, cqta_ref, cqtb_ref, kaug_ref, vt_ref, kmean_ref, kc_ref, vct_ref,
                 oa_ref, ob_ref, oma_ref, omb_ref,
                 qaug_ref, qown_ref, qmem_ref, s_ref, p_ref, alpha_ref, m_ref, acc_ref, accm_ref,
                 ot_ref, otm_ref):
    step = pl.program_id(1)
    nblk = kmean_ref.shape[1]
    blk = MOBA_BLOCK
    q_blocks = (step, nblk - 1 - step)
    qt_refs = (qta_ref, qtb_ref)
    cqt_refs = (cqta_ref, cqtb_ref)

    def build_query_operands(side, h):
        sh = side * MOBA_HEADS + h
        qt = qt_refs[side][0, 0, h * HEAD_DIM:(h + 1) * HEAD_DIM, :]
        qs = (qt * (HEAD_DIM ** -0.5 * LOG2E)).astype(BF16)
        sub = lax.broadcasted_iota(jnp.int32, (AUG_MASK - AUG_ALIBI, blk), 0)
        ones = jnp.where(sub < 3, 1.0, 0.0).astype(BF16)
        for ref in (qaug_ref, qown_ref):
            ref[sh, 0:HEAD_DIM, :] = qs
            ref[sh, AUG_ALIBI:AUG_MASK, :] = ones
        qown_ref[sh, AUG_MASK:, :] = jnp.zeros((AUG_W - AUG_MASK, blk), BF16)
        qmem_ref[sh, 0:HEAD_DIM, :] = cqt_refs[side][0, 0, h * HEAD_DIM:(h + 1) * HEAD_DIM, :]
        qmem_ref[sh, HEAD_DIM:, :] = jnp.zeros((AUG_W - HEAD_DIM, blk), BF16)

    def build_block_mask(side, h):
        sh = side * MOBA_HEADS + h
        blk_id = lax.broadcasted_iota(jnp.int32, (nblk, blk), 0).astype(F32)
        qt = qt_refs[side][0, 0, h * HEAD_DIM:(h + 1) * HEAD_DIM, :]
        gate = jnp.dot(kmean_ref[0, :, h * AUG_W:h * AUG_W + HEAD_DIM], qt,
                       preferred_element_type=F32, precision=lax.Precision.HIGHEST)
        gate = jnp.where(blk_id < q_blocks[side].astype(F32), gate, -jnp.inf)
        keep = blk_id < 0.0
        for _ in range(MOBA_TOPK):
            best = jnp.max(gate, axis=0, keepdims=True)
            first = jnp.min(jnp.where(gate == best, blk_id, float(nblk)), axis=0, keepdims=True)
            pick = (blk_id == first) & (best > -jnp.inf)
            keep = keep | pick
            gate = jnp.where(pick, -jnp.inf, gate)
        qaug_ref[sh, AUG_MASK:AUG_MASK + nblk, :] = jnp.where(keep, 0.0, NEG_BIG).astype(BF16)
        qaug_ref[sh, AUG_MASK + nblk:, :] = jnp.zeros((AUG_W - AUG_MASK - nblk, blk), BF16)

    kk = lax.broadcasted_iota(jnp.int32, (blk, blk), 0)
    qq = lax.broadcasted_iota(jnp.int32, (blk, blk), 1)
    causal = kk <= qq

    lead_groups = (("own", 0), ("own", 1), ("mem", 0), ("mem", 1))
    lead = len(lead_groups) * MOBA_HEADS
    n_trips = (nblk - 1 + PAST_UNROLL - 1) // PAST_UNROLL
    single_pass = n_trips == 1
    per_trip = (nblk - 1 if single_pass else PAST_UNROLL) * MOBA_HEADS

    def stream_item(trip, n):
        if n < 0:
            idx = n + lead
            kind, side = lead_groups[idx // MOBA_HEADS]
            return kind, side, q_blocks[side], idx % MOBA_HEADS, idx % SCORE_SLOTS
        if n >= per_trip:
            trip, n = trip + 1, n - per_trip
        pos = trip * PAST_UNROLL + n // MOBA_HEADS
        side = (pos >= step).astype(jnp.int32)
        block = jnp.minimum(pos - side * step, nblk - 1)
        return "past", side, block, n % MOBA_HEADS, (n + lead) % SCORE_SLOTS

    def scores(kind, side, j, h, slot):
        lanes = slice(h * AUG_W, (h + 1) * AUG_W)
        sh = side * MOBA_HEADS + h
        if kind == "mem":
            keys, q_ref = kc_ref[0, :, lanes], qmem_ref
        else:
            keys, q_ref = kaug_ref[0, j, :, lanes], (qown_ref if kind == "own" else qaug_ref)
        s_ref[slot] = jnp.dot(keys, q_ref[sh], preferred_element_type=F32)

    def probabilities(kind, side, j, h, slot):
        s = s_ref[slot]
        sh = side * MOBA_HEADS + h
        if kind == "past":
            m_old = m_ref[sh]
            m_new = jnp.maximum(m_old, jnp.max(s, axis=0, keepdims=True))
            alpha_ref[slot % PROB_SLOTS] = jnp.exp2(m_old - m_new)
        else:
            if kind == "own":
                s = jnp.where(causal, s, NEG_BIG)
            m_new = jnp.max(s, axis=0, keepdims=True)
        p_ref[slot % PROB_SLOTS] = jnp.exp2(s - m_new).astype(BF16)
        if kind != "mem":
            m_ref[sh] = m_new

    def accumulate(kind, side, j, h, slot):
        rows = slice(h * VT_ROWS, (h + 1) * VT_ROWS)
        sh = side * MOBA_HEADS + h
        p = p_ref[slot % PROB_SLOTS]
        if kind == "mem":
            accm_ref[sh] = jnp.dot(vct_ref[0, rows, :], p, preferred_element_type=F32)
        else:
            pv = jnp.dot(vt_ref[0, j, rows, :], p, preferred_element_type=F32)
            acc_ref[sh] = pv if kind == "own" else alpha_ref[slot % PROB_SLOTS] * acc_ref[sh] + pv

    for h in range(MOBA_HEADS):
        for side in range(2):
            build_query_operands(side, h)
            build_block_mask(side, h)
    def ahead(stage, trip, n):
        if not (single_pass and n >= per_trip):
            stage(*stream_item(trip, n))

    for n in range(-lead, -lead + SCORE_LOOKAHEAD):
        scores(*stream_item(0, n))
    for n in range(-lead, -lead + PROB_LOOKAHEAD):
        probabilities(*stream_item(0, n))
    for n in range(-lead, 0):
        ahead(scores, 0, n + SCORE_LOOKAHEAD)
        ahead(probabilities, 0, n + PROB_LOOKAHEAD)
        accumulate(*stream_item(0, n))

    def past_trip(trip, carry):
        for n in range(per_trip):
            ahead(scores, trip, n + SCORE_LOOKAHEAD)
            ahead(probabilities, trip, n + PROB_LOOKAHEAD)
            accumulate(*stream_item(trip, n))
        return carry

    if single_pass:
        past_trip(0, 0)
    else:
        lax.fori_loop(0, n_trips, past_trip, 0)

    for side, o_ref, om_ref in ((0, oa_ref, oma_ref), (1, ob_ref, omb_ref)):
        for ref, t_ref, out_ref in ((acc_ref, ot_ref, o_ref), (accm_ref, otm_ref, om_ref)):
            for h in range(MOBA_HEADS):
                sh = side * MOBA_HEADS + h
                t_ref[side, h * HEAD_DIM:(h + 1) * HEAD_DIM, :] = (ref[sh, 0:HEAD_DIM, :]
                                                                   / ref[sh, HEAD_DIM:HEAD_DIM + 1, :])
            out_ref[...] = t_ref[side].T.astype(out_ref.dtype)


def _moba(qt, cqt, kaug, vt, kmean, kc, vct, batch, seq):
    nblk = seq // MOBA_BLOCK
    width = MOBA_HEADS * HEAD_DIM
    aug = MOBA_HEADS * AUG_W
    vt_rows = MOBA_HEADS * VT_ROWS
    mlen = vct.shape[2]
    assert mlen == MOBA_BLOCK and MEM_HEADS == MOBA_HEADS, "memory items reuse the key-block buffers"
    qt = qt.reshape(batch, nblk, width, MOBA_BLOCK)
    cqt = cqt.reshape(batch, nblk, width, MOBA_BLOCK)
    kaug = kaug.reshape(batch, nblk, MOBA_BLOCK, aug)
    vt = vt.reshape(batch, nblk, vt_rows, MOBA_BLOCK)
    kc = kc.reshape(batch, mlen, aug)
    resident = 2 * (kaug.size // batch + vt.size // batch) * 2
    assert nblk % 2 == 0
    half = nblk // 2
    sides = 2 * MOBA_HEADS
    qblock_a = pl.BlockSpec((1, 1, width, MOBA_BLOCK), lambda b, s: (b, s, 0, 0))
    qblock_b = pl.BlockSpec((1, 1, width, MOBA_BLOCK), lambda b, s: (b, nblk - 1 - s, 0, 0))
    out_a = pl.BlockSpec((MOBA_BLOCK, width), lambda b, s: (b * half + s, 0))
    out_b = pl.BlockSpec((MOBA_BLOCK, width), lambda b, s: (b * half + half - 1 - s, 0))
    out_shape = jax.ShapeDtypeStruct((batch * half * MOBA_BLOCK, width), BF16)
    y_lo, y_hi, m_lo, m_hi = pl.pallas_call(
        _moba_kernel,
        grid=(batch, half),
        in_specs=[qblock_a, qblock_b, qblock_a, qblock_b,
                  pl.BlockSpec((1, nblk, MOBA_BLOCK, aug), lambda b, s: (b, 0, 0, 0)),
                  pl.BlockSpec((1, nblk, vt_rows, MOBA_BLOCK), lambda b, s: (b, 0, 0, 0)),
                  pl.BlockSpec((1, nblk, aug), lambda b, s: (b, 0, 0)),
                  pl.BlockSpec((1, mlen, aug), lambda b, s: (b, 0, 0)),
                  pl.BlockSpec((1, vt_rows, mlen), lambda b, s: (b, 0, 0))],
        out_specs=(out_a, out_b, out_a, out_b),
        out_shape=(out_shape,) * 4,
        scratch_shapes=[pltpu.VMEM((sides, AUG_W, MOBA_BLOCK), BF16),
                        pltpu.VMEM((sides, AUG_W, MOBA_BLOCK), BF16),
                        pltpu.VMEM((sides, AUG_W, MOBA_BLOCK), BF16),
                        pltpu.VMEM((SCORE_SLOTS, MOBA_BLOCK, MOBA_BLOCK), F32),
                        pltpu.VMEM((PROB_SLOTS, MOBA_BLOCK, MOBA_BLOCK), BF16),
                        pltpu.VMEM((PROB_SLOTS, 1, MOBA_BLOCK), F32),
                        pltpu.VMEM((sides, 1, MOBA_BLOCK), F32),
                        pltpu.VMEM((sides, VT_ROWS, MOBA_BLOCK), F32),
                        pltpu.VMEM((sides, VT_ROWS, MOBA_BLOCK), F32),
                        pltpu.VMEM((2, width, MOBA_BLOCK), F32),
                        pltpu.VMEM((2, width, MOBA_BLOCK), F32)],
        compiler_params=pltpu.CompilerParams(dimension_semantics=("arbitrary", "arbitrary"),
                                             vmem_limit_bytes=_vmem_limit(resident)),
        name="moba",
    )(qt, qt, cqt, cqt, kaug, vt, kmean, kc, vct)

    return (y_lo, y_hi), (m_lo, m_hi)


def _block_diag_ones(group):
    idx = jnp.arange(V7X_MXU_DIM) // group
    return (idx[:, None] == idx[None, :]).astype(BF16)


def _pack_block_diag(w):
    nb, bw, _ = w.shape
    per = V7X_MXU_DIM // bw
    w = w.reshape(nb // per, per, bw, bw)
    eye = jnp.eye(per, dtype=w.dtype)
    return jnp.einsum("gpij,pq->gpiqj", w, eye).reshape(nb // per, V7X_MXU_DIM, V7X_MXU_DIM)


def _pad_heads(w, heads):
    lead = w.shape[:-1]
    w = w.reshape(lead + (heads, HEAD_DIM))
    w = jnp.pad(w, [(0, 0)] * len(lead) + [(0, 0), (0, AUG_W - HEAD_DIM)])
    return w.reshape(lead + (heads * AUG_W,))


def _layer(x, mem, ffn1_norm, ffn1_w_in, ffn1_w_out, mix_norm, mem_norm, w_in,
           lru_conv_w, lru_conv_b, lru_a_w, lru_a_b, lru_x_w, lru_x_b, lru_lambda,
           moba_q_norm, moba_k_norm, mem_w_kv, mem_q_norm, mem_k_norm, w_out,
           ffn2_norm, ffn2_w_in, ffn2_w_out):
    batch, seq, d = x.shape
    n = batch * seq
    n_lru = lru_lambda.shape[0]
    wq = MOBA_HEADS * HEAD_DIM
    wc = MEM_HEADS * HEAD_DIM
    row = lambda v: v.reshape(1, -1).astype(F32)

    bd64 = _block_diag_ones(HEAD_DIM)
    bd128 = _block_diag_ones(AUG_W)

    w_in_b = w_in.astype(BF16)
    o = 2 * n_lru
    w_lru = w_in_b[:, :o]
    w_k = _pad_heads(w_in_b[:, o + wq:o + 2 * wq], MOBA_HEADS)
    w_qvc = jnp.concatenate([w_in_b[:, o:o + wq], w_in_b[:, o + 2 * wq:]], axis=1)
    gq = row(jnp.tile(moba_q_norm, MOBA_HEADS))
    gk = row(_pad_heads(jnp.tile(moba_k_norm, MOBA_HEADS), MOBA_HEADS))
    gc = row(jnp.tile(mem_q_norm, MEM_HEADS))
    gck = row(_pad_heads(jnp.tile(mem_k_norm, MEM_HEADS), MEM_HEADS))
    w_kv_b = mem_w_kv.astype(BF16)
    w_ck = _pad_heads(w_kv_b[:, :wc], MEM_HEADS)
    w_cv = w_kv_b[:, wc:]
    slopes = 2.0 ** (-8.0 * jnp.arange(1, MOBA_HEADS + 1, dtype=F32) / MOBA_HEADS)
    lane = jnp.arange(AUG_W)
    is_alibi = (lane >= AUG_ALIBI) & (lane < AUG_ALIBI + 3)
    alibi = row(jnp.where(is_alibi[None, :], slopes[:, None] * LOG2E, 0.0))

    x2d = x.reshape(n, d)
    kc, vct = _mem_kv(mem.reshape(-1, d), row(mem_norm), w_ck, w_cv, bd128, gck, batch)

    x1 = _ffn(x2d, row(ffn1_norm), ffn1_w_in, ffn1_w_out)

    lru_x, lru_g, qt, kaug, vt, kmean, cqt = _in_proj(
        x1, row(mix_norm), w_lru, w_qvc, w_k, gq, gk, gc, bd64, bd128, alibi, seq)
    kmean = kmean[:, :IN_PROJ_TILE // MOBA_BLOCK, :].reshape(batch, seq // MOBA_BLOCK, -1)

    y_lru = _lru(lru_x, lru_g, lru_conv_w.astype(F32), row(lru_conv_b),
                 _pack_block_diag(0.5 * lru_a_w).astype(BF16), row(0.5 * lru_a_b),
                 _pack_block_diag(0.5 * lru_x_w).astype(BF16), row(0.5 * lru_x_b),
                 row(lru_lambda), batch, seq)
    y_moba, y_mem = _moba(qt, cqt, kaug, vt, kmean, kc, vct, batch, seq)

    out = _out_ffn(x1, y_lru, y_moba, y_mem, w_out.astype(BF16), row(ffn2_norm),
                   ffn2_w_in, ffn2_w_out, seq)
    return out.reshape(batch, seq, d)


def kernel(x, mem, ffn1_norm, ffn1_w_in, ffn1_w_out, mix_norm, mem_norm, w_in, lru_conv_w, lru_conv_b,
           lru_a_w, lru_a_b, lru_x_w, lru_x_b, lru_lambda, moba_q_norm, moba_k_norm, mem_w_kv,
           mem_q_norm, mem_k_norm, w_out, ffn2_norm, ffn2_w_in, ffn2_w_out):
    params = (ffn1_norm, ffn1_w_in, ffn1_w_out, mix_norm, mem_norm, w_in, lru_conv_w, lru_conv_b,
              lru_a_w, lru_a_b, lru_x_w, lru_x_b, lru_lambda, moba_q_norm, moba_k_norm, mem_w_kv,
              mem_q_norm, mem_k_norm, w_out, ffn2_norm, ffn2_w_in, ffn2_w_out)
    for layer in range(ffn1_norm.shape[0]):
        x = _layer(x, mem, *(p[layer] for p in params))
    return x
```

```python
import functools
import math

import jax
import jax.numpy as jnp
from jax import lax
from jax.experimental import pallas as pl
from jax.experimental.pallas import tpu as pltpu

F32 = jnp.float32
BF16 = jnp.bfloat16

HEAD_DIM = 64
CONV_W = 4
LRU_C = 8.0
MOBA_HEADS = 4
MOBA_BLOCK = 256
MOBA_TOPK = 3
MEM_HEADS = 4
NORM_EPS = 1e-6
LOG2E = 1.4426950408889634
NEG_BIG = -1e30

V7X_MXU_DIM = 256
LANES = 128
SUBLANES = 8
V7X_VMEM_BYTES = 64 * 1024 * 1024

AUG_W = 128
AUG_ALIBI = 64
AUG_MASK = 80
assert AUG_W & (AUG_W - 1) == 0 and MOBA_BLOCK & (MOBA_BLOCK - 1) == 0
VT_ROWS = HEAD_DIM + 16
SCORE_LOOKAHEAD = 5
PAST_UNROLL = 32
SCORE_SLOTS = 8
assert MOBA_HEADS <= SCORE_LOOKAHEAD < SCORE_SLOTS and (PAST_UNROLL * MOBA_HEADS) % SCORE_SLOTS == 0
assert (MOBA_HEADS + MEM_HEADS) % SCORE_SLOTS == 0
PROB_SLOTS = 4
PROB_LOOKAHEAD = 1
assert PROB_LOOKAHEAD < SCORE_LOOKAHEAD

TOKEN_TILE = 512
IN_PROJ_TILE = 1024
LRU_TILE = 1024
FFN_CHUNK = 256


def _vmem_limit(resident_bytes):
    return int(min(V7X_VMEM_BYTES - 8 * 1024 * 1024, resident_bytes + 24 * 1024 * 1024))


def _const_spec(shape):
    n = len(shape)
    return pl.BlockSpec(shape, lambda *_: (0,) * n, pipeline_mode=pl.Buffered(1))


def _sigmoid(x):
    return 0.5 * jnp.tanh(0.5 * x) + 0.5


def _rms_norm(x, g):
    ms = jnp.mean(x * x, axis=-1, keepdims=True)
    return x * lax.rsqrt(ms + NORM_EPS) * g


def _group_rms_norm(u, bd, g, group):
    sq = u * u
    hi = sq.astype(BF16)
    lo = (sq - hi.astype(F32)).astype(BF16)
    parts = []
    for c in range(u.shape[1] // V7X_MXU_DIM):
        sl = slice(c * V7X_MXU_DIM, (c + 1) * V7X_MXU_DIM)
        parts.append(jnp.dot(hi[:, sl], bd, preferred_element_type=F32)
                     + jnp.dot(lo[:, sl], bd, preferred_element_type=F32))
    ss = parts[0] if len(parts) == 1 else jnp.concatenate(parts, axis=1)
    return u * lax.rsqrt(ss * (1.0 / group) + NORM_EPS) * g


def _swiglu_half_step(x, g_ref, w_in_ref, w_out_ref, act_ref):
    d_ff = w_out_ref.shape[0]
    xn = _rms_norm(x, g_ref[...]).astype(BF16)
    for c in range(d_ff // FFN_CHUNK):
        lo = c * FFN_CHUNK
        a = jnp.dot(xn, w_in_ref[:, lo:lo + FFN_CHUNK].astype(BF16), preferred_element_type=F32)
        b = jnp.dot(xn, w_in_ref[:, d_ff + lo:d_ff + lo + FFN_CHUNK].astype(BF16),
                    preferred_element_type=F32)
        act_ref[:, lo:lo + FFN_CHUNK] = (a * _sigmoid(a) * b).astype(BF16)
    y = jnp.dot(act_ref[...], w_out_ref[...].astype(BF16), preferred_element_type=F32)
    return x + 0.5 * y


def _ffn_kernel(x_ref, g_ref, w_in_ref, w_out_ref, o_ref, act_ref):
    o_ref[...] = _swiglu_half_step(x_ref[...], g_ref, w_in_ref, w_out_ref, act_ref)


def _ffn(x, g, w_in, w_out):
    n, d = x.shape
    d_ff = w_out.shape[0]
    tm = TOKEN_TILE
    resident = sum(w.size * w.dtype.itemsize for w in (w_in, w_out))
    return pl.pallas_call(
        _ffn_kernel,
        grid=(n // tm,),
        in_specs=[pl.BlockSpec((tm, d), lambda i: (i, 0)),
                  _const_spec((1, d)), _const_spec(w_in.shape), _const_spec(w_out.shape)],
        out_specs=pl.BlockSpec((tm, d), lambda i: (i, 0)),
        out_shape=jax.ShapeDtypeStruct((n, d), F32),
        scratch_shapes=[pltpu.VMEM((tm, d_ff), BF16)],
        compiler_params=pltpu.CompilerParams(dimension_semantics=("arbitrary",),
                                             vmem_limit_bytes=_vmem_limit(resident)),
        name="ffn1",
    )(x, g, w_in, w_out)


def _out_ffn_kernel(x_ref, ylru_ref, ymoba_lo_ref, ymoba_hi_ref, ymem_lo_ref, ymem_hi_ref,
                    wo_ref, g_ref, w_in_ref, w_out_ref, o_ref, act_ref, *, tiles_per_seq):
    n_lru = ylru_ref.shape[1]
    n_moba = ymoba_lo_ref.shape[1]
    first_half = pl.program_id(0) % tiles_per_seq < tiles_per_seq // 2
    y_moba = jnp.where(first_half, ymoba_lo_ref[...], ymoba_hi_ref[...])
    y_mem = jnp.where(first_half, ymem_lo_ref[...], ymem_hi_ref[...])
    x2 = (x_ref[...]
          + jnp.dot(ylru_ref[...], wo_ref[0:n_lru, :], preferred_element_type=F32)
          + jnp.dot(y_moba, wo_ref[n_lru:n_lru + n_moba, :], preferred_element_type=F32)
          + jnp.dot(y_mem, wo_ref[n_lru + n_moba:, :], preferred_element_type=F32))
    o_ref[...] = _swiglu_half_step(x2, g_ref, w_in_ref, w_out_ref, act_ref)


def _out_ffn(x1, y_lru, y_moba, y_mem, w_o, g, w_in, w_out, seq):
    n, d = x1.shape
    d_ff = w_out.shape[0]
    tm = TOKEN_TILE
    tps = seq // tm
    half = tps // 2
    assert tps % 2 == 0
    resident = sum(w.size * w.dtype.itemsize for w in (w_in, w_out, w_o))

    def tile(w):
        return pl.BlockSpec((tm, w), lambda i: (i, 0))

    def lo_tile(w):
        return pl.BlockSpec((tm, w), lambda i: ((i // tps) * half + jnp.minimum(i % tps, half - 1), 0))

    def hi_tile(w):
        return pl.BlockSpec((tm, w), lambda i: ((i // tps) * half + jnp.maximum(i % tps - half, 0), 0))

    wa = y_moba[0].shape[1]
    wm = y_mem[0].shape[1]
    return pl.pallas_call(
        functools.partial(_out_ffn_kernel, tiles_per_seq=tps),
        grid=(n // tm,),
        in_specs=[tile(d), tile(y_lru.shape[1]), lo_tile(wa), hi_tile(wa), lo_tile(wm), hi_tile(wm),
                  _const_spec(w_o.shape), _const_spec((1, d)),
                  _const_spec(w_in.shape), _const_spec(w_out.shape)],
        out_specs=tile(d),
        out_shape=jax.ShapeDtypeStruct((n, d), F32),
        scratch_shapes=[pltpu.VMEM((tm, d_ff), BF16)],
        compiler_params=pltpu.CompilerParams(dimension_semantics=("arbitrary",),
                                             vmem_limit_bytes=_vmem_limit(resident)),
        name="out_ffn2",
    )(x1, y_lru, *y_moba, *y_mem, w_o, g, w_in, w_out)


def _mem_kv_kernel(mem_ref, g_ref, wk_ref, wv_ref, bd_ref, gk_ref, kc_ref, vct_ref):
    batch, _, mlen = vct_ref.shape
    mn = _rms_norm(mem_ref[...], g_ref[...]).astype(BF16)
    k = jnp.dot(mn, wk_ref[...], preferred_element_type=F32)
    kc_ref[...] = _group_rms_norm(k, bd_ref[...], gk_ref[...], HEAD_DIM).astype(BF16)
    v = jnp.dot(mn, wv_ref[...], preferred_element_type=F32)
    ones = jnp.ones((VT_ROWS - HEAD_DIM, mlen), BF16)
    for b in range(batch):
        vt = v[b * mlen:(b + 1) * mlen, :].T.astype(BF16)
        for h in range(MEM_HEADS):
            vct_ref[b, h * VT_ROWS:h * VT_ROWS + HEAD_DIM, :] = vt[h * HEAD_DIM:(h + 1) * HEAD_DIM, :]
            vct_ref[b, h * VT_ROWS + HEAD_DIM:(h + 1) * VT_ROWS, :] = ones


def _mem_kv(mem2d, g, w_k, w_v, bd128, gk, batch):
    m = mem2d.shape[0]
    return pl.pallas_call(
        _mem_kv_kernel,
        out_shape=(jax.ShapeDtypeStruct((m, w_k.shape[1]), BF16),
                   jax.ShapeDtypeStruct((batch, MEM_HEADS * VT_ROWS, m // batch), BF16)),
        name="mem_kv",
    )(mem2d, g, w_k, w_v, bd128, gk)


def _in_proj_kernel(x_ref, g_ref, w_lru_ref, w_qvc_ref, w_k_ref,
                    gq_ref, gk_ref, gc_ref, bd64_ref, bd128_ref, alibi_ref,
                    lrux_ref, lrug_ref, qt_ref, kaug_ref, vt_ref, kmean_ref, cqt_ref, *, seq):
    tm = x_ref.shape[0]
    n_lru = lrux_ref.shape[1]
    xn = _rms_norm(x_ref[...], g_ref[...]).astype(BF16)

    lrux_ref[...] = jnp.dot(xn, w_lru_ref[:, :n_lru], preferred_element_type=F32)
    lrug_ref[...] = jnp.dot(xn, w_lru_ref[:, n_lru:], preferred_element_type=F32)

    wq = qt_ref.shape[1]
    qvc = jnp.dot(xn, w_qvc_ref[...], preferred_element_type=F32)

    cq = _group_rms_norm(qvc[:, 2 * wq:], bd64_ref[...], gc_ref[...], HEAD_DIM) * (HEAD_DIM ** -0.5 * LOG2E)
    for r in range(tm // MOBA_BLOCK):
        cqt_ref[r] = cq[r * MOBA_BLOCK:(r + 1) * MOBA_BLOCK, :].T.astype(BF16)

    q = _group_rms_norm(qvc[:, :wq], bd64_ref[...], gq_ref[...], HEAD_DIM)
    v = qvc[:, wq:2 * wq]
    ones = jnp.ones((VT_ROWS - HEAD_DIM, MOBA_BLOCK), BF16)
    for r in range(tm // MOBA_BLOCK):
        rows = slice(r * MOBA_BLOCK, (r + 1) * MOBA_BLOCK)
        qt_ref[r] = q[rows, :].T
        vt = v[rows, :].T.astype(BF16)
        for h in range(MOBA_HEADS):
            vt_ref[r, h * VT_ROWS:h * VT_ROWS + HEAD_DIM, :] = vt[h * HEAD_DIM:(h + 1) * HEAD_DIM, :]
            vt_ref[r, h * VT_ROWS + HEAD_DIM:(h + 1) * VT_ROWS, :] = ones

    k = jnp.dot(xn, w_k_ref[...], preferred_element_type=F32)
    k = _group_rms_norm(k, bd128_ref[...], gk_ref[...], HEAD_DIM)
    kmean_ref[...] = jnp.zeros(kmean_ref.shape, F32)
    for r in range(tm // MOBA_BLOCK):
        rows = slice(r * MOBA_BLOCK, (r + 1) * MOBA_BLOCK)
        kmean_ref[0, r:r + 1, :] = jnp.sum(k[rows, :], axis=0, keepdims=True) * (1.0 / MOBA_BLOCK)

    shape = k.shape
    row = lax.broadcasted_iota(jnp.int32, shape, 0)
    lane = lax.broadcasted_iota(jnp.int32, shape, 1) & (AUG_W - 1)
    t = (pl.program_id(0) % (seq // tm)) * tm + row
    bias = t.astype(F32) * alibi_ref[...]
    b_hi = bias.astype(BF16).astype(F32)
    b_mid = (bias - b_hi).astype(BF16).astype(F32)
    b_lo = bias - b_hi - b_mid
    pieces = jnp.where(lane == AUG_ALIBI, b_hi, jnp.where(lane == AUG_ALIBI + 1, b_mid, b_lo))
    onehot = jnp.where(lane - AUG_MASK == lax.shift_right_logical(t, MOBA_BLOCK.bit_length() - 1), 1.0, 0.0)
    is_alibi = (lane >= AUG_ALIBI) & (lane < AUG_ALIBI + 3)
    kaug_ref[...] = (k + jnp.where(is_alibi, pieces, onehot)).astype(BF16)


def _in_proj(x1, g, w_lru, w_qvc, w_k, gq, gk, gc, bd64, bd128, alibi, seq):
    n, d = x1.shape
    tm = IN_PROJ_TILE
    rb = tm // MOBA_BLOCK
    nblk = n // MOBA_BLOCK
    n_lru = w_lru.shape[1] // 2
    wq = w_qvc.shape[1] // 3
    wk = w_k.shape[1]
    consts = (g, w_lru, w_qvc, w_k, gq, gk, gc, bd64, bd128, alibi)
    resident = sum(c.size * c.dtype.itemsize for c in consts)
    return pl.pallas_call(
        functools.partial(_in_proj_kernel, seq=seq),
        grid=(n // tm,),
        in_specs=[pl.BlockSpec((tm, d), lambda i: (i, 0))] + [_const_spec(c.shape) for c in consts],
        out_specs=[pl.BlockSpec((tm, n_lru), lambda i: (i, 0)),
                   pl.BlockSpec((tm, n_lru), lambda i: (i, 0)),
                   pl.BlockSpec((rb, wq, MOBA_BLOCK), lambda i: (i, 0, 0)),
                   pl.BlockSpec((tm, wk), lambda i: (i, 0)),
                   pl.BlockSpec((rb, MOBA_HEADS * VT_ROWS, MOBA_BLOCK), lambda i: (i, 0, 0)),
                   pl.BlockSpec((1, 8, wk), lambda i: (i, 0, 0)),
                   pl.BlockSpec((rb, wq, MOBA_BLOCK), lambda i: (i, 0, 0))],
        out_shape=(jax.ShapeDtypeStruct((n, n_lru), F32),
                   jax.ShapeDtypeStruct((n, n_lru), F32),
                   jax.ShapeDtypeStruct((nblk, wq, MOBA_BLOCK), F32),
                   jax.ShapeDtypeStruct((n, wk), BF16),
                   jax.ShapeDtypeStruct((nblk, MOBA_HEADS * VT_ROWS, MOBA_BLOCK), BF16),
                   jax.ShapeDtypeStruct((n // tm, 8, wk), F32),
                   jax.ShapeDtypeStruct((nblk, wq, MOBA_BLOCK), BF16)),
        compiler_params=pltpu.CompilerParams(dimension_semantics=("arbitrary",),
                                             vmem_limit_bytes=_vmem_limit(resident)),
        name="in_proj",
    )(x1, *consts)


def _gelu_tanh(x):
    c = math.sqrt(2.0 / math.pi)
    return 0.5 * x * (1.0 + jnp.tanh(c * (x + 0.044715 * (x * x * x))))


def _lru_kernel(x_ref, g_ref, cw_ref, cb_ref, wa_ref, ba_ref, wx_ref, bx_ref, lam_ref,
                o_ref, xpad_ref, a_ref, b_ref, hl_ref, carry_ref):
    tl, c = x_ref.shape
    half = V7X_MXU_DIM
    ti = pl.program_id(1)

    @pl.when(ti == 0)
    def _():
        xpad_ref[0:8, :] = jnp.zeros((8, c), F32)
        carry_ref[...] = jnp.zeros(carry_ref.shape, F32)

    xpad_ref[8:8 + tl, :] = x_ref[...]
    n_grp = tl // SUBLANES
    sub1 = lax.broadcasted_iota(jnp.int32, (n_grp, SUBLANES, LANES), 1)
    parts = []
    for k in range(c // LANES):
        lanes = slice(k * LANES, (k + 1) * LANES)
        xg = xpad_ref[:, lanes].reshape(n_grp + 1, SUBLANES, LANES)
        acc = cb_ref[:, lanes] + cw_ref[CONV_W - 1:CONV_W, lanes] * xg[1:]
        for d in range(1, CONV_W):
            rolled = pltpu.roll(xg, d, axis=1)
            delayed = jnp.where(sub1 >= d, rolled[1:], rolled[:-1])
            acc = acc + cw_ref[CONV_W - 1 - d:CONV_W - d, lanes] * delayed
        parts.append(acc.reshape(tl, LANES))
    xb = jnp.concatenate(parts, axis=1)
    xpad_ref[0:8, :] = x_ref[tl - 8:tl, :]

    xbb = xb.astype(BF16)

    def gate_tanh(w_ref, bias_ref):
        z_half = jnp.concatenate(
            [jnp.dot(xbb[:, :half], w_ref[0], preferred_element_type=F32),
             jnp.dot(xbb[:, half:], w_ref[1], preferred_element_type=F32)], axis=1)
        return jnp.tanh(z_half + bias_ref[...])

    neg_lam = -lam_ref[...]
    softplus = jnp.maximum(neg_lam, 0.0) + jnp.log1p(jnp.exp(-jnp.abs(neg_lam)))
    half_rate = (-0.5 * LRU_C) * softplus
    log_a = half_rate * gate_tanh(wa_ref, ba_ref) + half_rate
    i = 0.5 * gate_tanh(wx_ref, bx_ref) + 0.5
    a = jnp.exp(log_a)
    one_minus_a2 = 1.0 - a * a
    mult = jnp.where(one_minus_a2 > 0.0, one_minus_a2 * lax.rsqrt(one_minus_a2), 0.0)
    row = lax.broadcasted_iota(jnp.int32, (tl, c), 0)
    start_row = jnp.where(ti == 0, 0, -1)
    mult = jnp.where(row == start_row, 1.0, mult)
    bt = mult * i * xb

    groups = tl // SUBLANES
    sub = lax.broadcasted_iota(jnp.int32, (groups, SUBLANES, LANES), 1)
    for k in range(c // LANES):
        lanes = slice(k * LANES, (k + 1) * LANES)
        a3 = a[:, lanes].reshape(groups, SUBLANES, LANES)
        b3 = bt[:, lanes].reshape(groups, SUBLANES, LANES)
        d = 1
        while d < SUBLANES:
            has_prev = sub >= d
            a_prev = pltpu.roll(a3, d, axis=1)
            b_prev = pltpu.roll(b3, d, axis=1)
            b3 = b3 + jnp.where(has_prev, a3, 0.0) * b_prev
            a3 = jnp.where(has_prev, a3 * a_prev, a3)
            d *= 2
        a_ref[:, lanes] = a3.reshape(tl, LANES)
        b_ref[:, lanes] = b3.reshape(tl, LANES)

    h = carry_ref[...]
    for grp in range(groups):
        rows = slice(grp * SUBLANES, (grp + 1) * SUBLANES)
        h_last = jnp.broadcast_to(h[SUBLANES - 1:SUBLANES, :], (SUBLANES, c))
        h = a_ref[rows, :] * h_last + b_ref[rows, :]
        hl_ref[rows, :] = h
    carry_ref[...] = h

    o_ref[...] = (hl_ref[...] * _gelu_tanh(g_ref[...])).astype(o_ref.dtype)


def _lru(lru_x, lru_g, cw, cb, wa, ba, wx, bx, lam, batch, seq):
    n, c = lru_x.shape
    tl = LRU_TILE
    nt = seq // tl
    consts = (cw, cb, wa, ba, wx, bx, lam)
    tile = pl.BlockSpec((tl, c), lambda b, t: (b * nt + t, 0))
    return pl.pallas_call(
        _lru_kernel,
        grid=(batch, nt),
        in_specs=[tile, tile] + [_const_spec(w.shape) for w in consts],
        out_specs=tile,
        out_shape=jax.ShapeDtypeStruct((n, c), BF16),
        scratch_shapes=[pltpu.VMEM((tl + SUBLANES, c), F32)]
        + [pltpu.VMEM((tl, c), F32)] * 3
        + [pltpu.VMEM((SUBLANES, c), F32)],
        compiler_params=pltpu.CompilerParams(dimension_semantics=("arbitrary", "arbitrary")),
        name="lru",
    )(lru_x, lru_g, *consts)


def _moba_kernel(qta_ref, qtb_ref, cqta_ref, cqtb_ref, kaug_ref, vt_ref, kmean_ref, kc_ref, vct_ref,
                 oa_ref, ob_ref, oma_ref, omb_ref,
                 qaug_ref, qown_ref, qmem_ref, s_ref, p_ref, alpha_ref, m_ref, acc_ref, accm_ref,
                 ot_ref, otm_ref):
    step = pl.program_id(1)
    nblk = kmean_ref.shape[1]
    blk = MOBA_BLOCK
    q_blocks = (step, nblk - 1 - step)
    qt_refs = (qta_ref, qtb_ref)
    cqt_refs = (cqta_ref, cqtb_ref)

    def build_query_operands(side, h):
        sh = side * MOBA_HEADS + h
        qt = qt_refs[side][0, 0, h * HEAD_DIM:(h + 1) * HEAD_DIM, :]
        qs = (qt * (HEAD_DIM ** -0.5 * LOG2E)).astype(BF16)
        sub = lax.broadcasted_iota(jnp.int32, (AUG_MASK - AUG_ALIBI, blk), 0)
        ones = jnp.where(sub < 3, 1.0, 0.0).astype(BF16)
        for ref in (qaug_ref, qown_ref):
            ref[sh, 0:HEAD_DIM, :] = qs
            ref[sh, AUG_ALIBI:AUG_MASK, :] = ones
        qown_ref[sh, AUG_MASK:, :] = jnp.zeros((AUG_W - AUG_MASK, blk), BF16)
        qmem_ref[sh, 0:HEAD_DIM, :] = cqt_refs[side][0, 0, h * HEAD_DIM:(h + 1) * HEAD_DIM, :]
        qmem_ref[sh, HEAD_DIM:, :] = jnp.zeros((AUG_W - HEAD_DIM, blk), BF16)

    def build_block_mask(side, h):
        sh = side * MOBA_HEADS + h
        blk_id = lax.broadcasted_iota(jnp.int32, (nblk, blk), 0).astype(F32)
        qt = qt_refs[side][0, 0, h * HEAD_DIM:(h + 1) * HEAD_DIM, :]
        gate = jnp.dot(kmean_ref[0, :, h * AUG_W:h * AUG_W + HEAD_DIM], qt,
                       preferred_element_type=F32, precision=lax.Precision.HIGHEST)
        gate = jnp.where(blk_id < q_blocks[side].astype(F32), gate, -jnp.inf)
        keep = blk_id < 0.0
        for _ in range(MOBA_TOPK):
            best = jnp.max(gate, axis=0, keepdims=True)
            first = jnp.min(jnp.where(gate == best, blk_id, float(nblk)), axis=0, keepdims=True)
            pick = (blk_id == first) & (best > -jnp.inf)
            keep = keep | pick
            gate = jnp.where(pick, -jnp.inf, gate)
        qaug_ref[sh, AUG_MASK:AUG_MASK + nblk, :] = jnp.where(keep, 0.0, NEG_BIG).astype(BF16)
        qaug_ref[sh, AUG_MASK + nblk:, :] = jnp.zeros((AUG_W - AUG_MASK - nblk, blk), BF16)

    kk = lax.broadcasted_iota(jnp.int32, (blk, blk), 0)
    qq = lax.broadcasted_iota(jnp.int32, (blk, blk), 1)
    causal = kk <= qq

    lead_groups = (("own", 0), ("own", 1), ("mem", 0), ("mem", 1))
    lead = len(lead_groups) * MOBA_HEADS
    n_trips = (nblk - 1 + PAST_UNROLL - 1) // PAST_UNROLL
    single_pass = n_trips == 1
    per_trip = (nblk - 1 if single_pass else PAST_UNROLL) * MOBA_HEADS

    def stream_item(trip, n):
        if n < 0:
            idx = n + lead
            kind, side = lead_groups[idx // MOBA_HEADS]
            return kind, side, q_blocks[side], idx % MOBA_HEADS, idx % SCORE_SLOTS
        if n >= per_trip:
            trip, n = trip + 1, n - per_trip
        pos = trip * PAST_UNROLL + n // MOBA_HEADS
        side = (pos >= step).astype(jnp.int32)
        block = jnp.minimum(pos - side * step, nblk - 1)
        return "past", side, block, n % MOBA_HEADS, (n + lead) % SCORE_SLOTS

    def scores(kind, side, j, h, slot):
        lanes = slice(h * AUG_W, (h + 1) * AUG_W)
        sh = side * MOBA_HEADS + h
        if kind == "mem":
            keys, q_ref = kc_ref[0, :, lanes], qmem_ref
        else:
            keys, q_ref = kaug_ref[0, j, :, lanes], (qown_ref if kind == "own" else qaug_ref)
        s_ref[slot] = jnp.dot(keys, q_ref[sh], preferred_element_type=F32)

    def probabilities(kind, side, j, h, slot):
        s = s_ref[slot]
        sh = side * MOBA_HEADS + h
        if kind == "past":
            m_old = m_ref[sh]
            m_new = jnp.maximum(m_old, jnp.max(s, axis=0, keepdims=True))
            alpha_ref[slot % PROB_SLOTS] = jnp.exp2(m_old - m_new)
        else:
            if kind == "own":
                s = jnp.where(causal, s, NEG_BIG)
            m_new = jnp.max(s, axis=0, keepdims=True)
        p_ref[slot % PROB_SLOTS] = jnp.exp2(s - m_new).astype(BF16)
        if kind != "mem":
            m_ref[sh] = m_new

    def accumulate(kind, side, j, h, slot):
        rows = slice(h * VT_ROWS, (h + 1) * VT_ROWS)
        sh = side * MOBA_HEADS + h
        p = p_ref[slot % PROB_SLOTS]
        if kind == "mem":
            accm_ref[sh] = jnp.dot(vct_ref[0, rows, :], p, preferred_element_type=F32)
        else:
            pv = jnp.dot(vt_ref[0, j, rows, :], p, preferred_element_type=F32)
            acc_ref[sh] = pv if kind == "own" else alpha_ref[slot % PROB_SLOTS] * acc_ref[sh] + pv

    for h in range(MOBA_HEADS):
        for side in range(2):
            build_query_operands(side, h)
            build_block_mask(side, h)
    def ahead(stage, trip, n):
        if not (single_pass and n >= per_trip):
            stage(*stream_item(trip, n))

    for n in range(-lead, -lead + SCORE_LOOKAHEAD):
        scores(*stream_item(0, n))
    for n in range(-lead, -lead + PROB_LOOKAHEAD):
        probabilities(*stream_item(0, n))
    for n in range(-lead, 0):
        ahead(scores, 0, n + SCORE_LOOKAHEAD)
        ahead(probabilities, 0, n + PROB_LOOKAHEAD)
        accumulate(*stream_item(0, n))

    def past_trip(trip, carry):
        for n in range(per_trip):
            ahead(scores, trip, n + SCORE_LOOKAHEAD)
            ahead(probabilities, trip, n + PROB_LOOKAHEAD)
            accumulate(*stream_item(trip, n))
        return carry

    if single_pass:
        past_trip(0, 0)
    else:
        lax.fori_loop(0, n_trips, past_trip, 0)

    for side, o_ref, om_ref in ((0, oa_ref, oma_ref), (1, ob_ref, omb_ref)):
        for ref, t_ref, out_ref in ((acc_ref, ot_ref, o_ref), (accm_ref, otm_ref, om_ref)):
            for h in range(MOBA_HEADS):
                sh = side * MOBA_HEADS + h
                t_ref[side, h * HEAD_DIM:(h + 1) * HEAD_DIM, :] = (ref[sh, 0:HEAD_DIM, :]
                                                                   / ref[sh, HEAD_DIM:HEAD_DIM + 1, :])
            out_ref[...] = t_ref[side].T.astype(out_ref.dtype)


def _moba(qt, cqt, kaug, vt, kmean, kc, vct, batch, seq):
    nblk = seq // MOBA_BLOCK
    width = MOBA_HEADS * HEAD_DIM
    aug = MOBA_HEADS * AUG_W
    vt_rows = MOBA_HEADS * VT_ROWS
    mlen = vct.shape[2]
    assert mlen == MOBA_BLOCK and MEM_HEADS == MOBA_HEADS, "memory items reuse the key-block buffers"
    qt = qt.reshape(batch, nblk, width, MOBA_BLOCK)
    cqt = cqt.reshape(batch, nblk, width, MOBA_BLOCK)
    kaug = kaug.reshape(batch, nblk, MOBA_BLOCK, aug)
    vt = vt.reshape(batch, nblk, vt_rows, MOBA_BLOCK)
    kc = kc.reshape(batch, mlen, aug)
    resident = 2 * (kaug.size // batch + vt.size // batch) * 2
    assert nblk % 2 == 0
    half = nblk // 2
    sides = 2 * MOBA_HEADS
    qblock_a = pl.BlockSpec((1, 1, width, MOBA_BLOCK), lambda b, s: (b, s, 0, 0))
    qblock_b = pl.BlockSpec((1, 1, width, MOBA_BLOCK), lambda b, s: (b, nblk - 1 - s, 0, 0))
    out_a = pl.BlockSpec((MOBA_BLOCK, width), lambda b, s: (b * half + s, 0))
    out_b = pl.BlockSpec((MOBA_BLOCK, width), lambda b, s: (b * half + half - 1 - s, 0))
    out_shape = jax.ShapeDtypeStruct((batch * half * MOBA_BLOCK, width), BF16)
    y_lo, y_hi, m_lo, m_hi = pl.pallas_call(
        _moba_kernel,
        grid=(batch, half),
        in_specs=[qblock_a, qblock_b, qblock_a, qblock_b,
                  pl.BlockSpec((1, nblk, MOBA_BLOCK, aug), lambda b, s: (b, 0, 0, 0)),
                  pl.BlockSpec((1, nblk, vt_rows, MOBA_BLOCK), lambda b, s: (b, 0, 0, 0)),
                  pl.BlockSpec((1, nblk, aug), lambda b, s: (b, 0, 0)),
                  pl.BlockSpec((1, mlen, aug), lambda b, s: (b, 0, 0)),
                  pl.BlockSpec((1, vt_rows, mlen), lambda b, s: (b, 0, 0))],
        out_specs=(out_a, out_b, out_a, out_b),
        out_shape=(out_shape,) * 4,
        scratch_shapes=[pltpu.VMEM((sides, AUG_W, MOBA_BLOCK), BF16),
                        pltpu.VMEM((sides, AUG_W, MOBA_BLOCK), BF16),
                        pltpu.VMEM((sides, AUG_W, MOBA_BLOCK), BF16),
                        pltpu.VMEM((SCORE_SLOTS, MOBA_BLOCK, MOBA_BLOCK), F32),
                        pltpu.VMEM((PROB_SLOTS, MOBA_BLOCK, MOBA_BLOCK), BF16),
                        pltpu.VMEM((PROB_SLOTS, 1, MOBA_BLOCK), F32),
                        pltpu.VMEM((sides, 1, MOBA_BLOCK), F32),
                        pltpu.VMEM((sides, VT_ROWS, MOBA_BLOCK), F32),
                        pltpu.VMEM((sides, VT_ROWS, MOBA_BLOCK), F32),
                        pltpu.VMEM((2, width, MOBA_BLOCK), F32),
                        pltpu.VMEM((2, width, MOBA_BLOCK), F32)],
        compiler_params=pltpu.CompilerParams(dimension_semantics=("arbitrary", "arbitrary"),
                                             vmem_limit_bytes=_vmem_limit(resident)),
        name="moba",
    )(qt, qt, cqt, cqt, kaug, vt, kmean, kc, vct)

    return (y_lo, y_hi), (m_lo, m_hi)


def _block_diag_ones(group):
    idx = jnp.arange(V7X_MXU_DIM) // group
    return (idx[:, None] == idx[None, :]).astype(BF16)


def _pack_block_diag(w):
    nb, bw, _ = w.shape
    per = V7X_MXU_DIM // bw
    w = w.reshape(nb // per, per, bw, bw)
    eye = jnp.eye(per, dtype=w.dtype)
    return jnp.einsum("gpij,pq->gpiqj", w, eye).reshape(nb // per, V7X_MXU_DIM, V7X_MXU_DIM)


def _pad_heads(w, heads):
    lead = w.shape[:-1]
    w = w.reshape(lead + (heads, HEAD_DIM))
    w = jnp.pad(w, [(0, 0)] * len(lead) + [(0, 0), (0, AUG_W - HEAD_DIM)])
    return w.reshape(lead + (heads * AUG_W,))


def _layer(x, mem, ffn1_norm, ffn1_w_in, ffn1_w_out, mix_norm, mem_norm, w_in,
           lru_conv_w, lru_conv_b, lru_a_w, lru_a_b, lru_x_w, lru_x_b, lru_lambda,
           moba_q_norm, moba_k_norm, mem_w_kv, mem_q_norm, mem_k_norm, w_out,
           ffn2_norm, ffn2_w_in, ffn2_w_out):
    batch, seq, d = x.shape
    n = batch * seq
    n_lru = lru_lambda.shape[0]
    wq = MOBA_HEADS * HEAD_DIM
    wc = MEM_HEADS * HEAD_DIM
    row = lambda v: v.reshape(1, -1).astype(F32)

    bd64 = _block_diag_ones(HEAD_DIM)
    bd128 = _block_diag_ones(AUG_W)

    w_in_b = w_in.astype(BF16)
    o = 2 * n_lru
    w_lru = w_in_b[:, :o]
    w_k = _pad_heads(w_in_b[:, o + wq:o + 2 * wq], MOBA_HEADS)
    w_qvc = jnp.concatenate([w_in_b[:, o:o + wq], w_in_b[:, o + 2 * wq:]], axis=1)
    gq = row(jnp.tile(moba_q_norm, MOBA_HEADS))
    gk = row(_pad_heads(jnp.tile(moba_k_norm, MOBA_HEADS), MOBA_HEADS))
    gc = row(jnp.tile(mem_q_norm, MEM_HEADS))
    gck = row(_pad_heads(jnp.tile(mem_k_norm, MEM_HEADS), MEM_HEADS))
    w_kv_b = mem_w_kv.astype(BF16)
    w_ck = _pad_heads(w_kv_b[:, :wc], MEM_HEADS)
    w_cv = w_kv_b[:, wc:]
    slopes = 2.0 ** (-8.0 * jnp.arange(1, MOBA_HEADS + 1, dtype=F32) / MOBA_HEADS)
    lane = jnp.arange(AUG_W)
    is_alibi = (lane >= AUG_ALIBI) & (lane < AUG_ALIBI + 3)
    alibi = row(jnp.where(is_alibi[None, :], slopes[:, None] * LOG2E, 0.0))

    x2d = x.reshape(n, d)
    kc, vct = _mem_kv(mem.reshape(-1, d), row(mem_norm), w_ck, w_cv, bd128, gck, batch)

    x1 = _ffn(x2d, row(ffn1_norm), ffn1_w_in, ffn1_w_out)

    lru_x, lru_g, qt, kaug, vt, kmean, cqt = _in_proj(
        x1, row(mix_norm), w_lru, w_qvc, w_k, gq, gk, gc, bd64, bd128, alibi, seq)
    kmean = kmean[:, :IN_PROJ_TILE // MOBA_BLOCK, :].reshape(batch, seq // MOBA_BLOCK, -1)

    y_lru = _lru(lru_x, lru_g, lru_conv_w.astype(F32), row(lru_conv_b),
                 _pack_block_diag(0.5 * lru_a_w).astype(BF16), row(0.5 * lru_a_b),
                 _pack_block_diag(0.5 * lru_x_w).astype(BF16), row(0.5 * lru_x_b),
                 row(lru_lambda), batch, seq)
    y_moba, y_mem = _moba(qt, cqt, kaug, vt, kmean, kc, vct, batch, seq)

    out = _out_ffn(x1, y_lru, y_moba, y_mem, w_out.astype(BF16), row(ffn2_norm),
                   ffn2_w_in, ffn2_w_out, seq)
    return out.reshape(batch, seq, d)


def kernel(x, mem, ffn1_norm, ffn1_w_in, ffn1_w_out, mix_norm, mem_norm, w_in, lru_conv_w, lru_conv_b,
           lru_a_w, lru_a_b, lru_x_w, lru_x_b, lru_lambda, moba_q_norm, moba_k_norm, mem_w_kv,
           mem_q_norm, mem_k_norm, w_out, ffn2_norm, ffn2_w_in, ffn2_w_out):
    params = (ffn1_norm, ffn1_w_in, ffn1_w_out, mix_norm, mem_norm, w_in, lru_conv_w, lru_conv_b,
              lru_a_w, lru_a_b, lru_x_w, lru_x_b, lru_lambda, moba_q_norm, moba_k_norm, mem_w_kv,
              mem_q_norm, mem_k_norm, w_out, ffn2_norm, ffn2_w_in, ffn2_w_out)
    for layer in range(ffn1_norm.shape[0]):
        x = _layer(x, mem, *(p[layer] for p in params))
    return x
```

```python
import functools
import math

import jax
import jax.numpy as jnp
from jax import lax
from jax.experimental import pallas as pl
from jax.experimental.pallas import tpu as pltpu

F32 = jnp.float32
BF16 = jnp.bfloat16

HEAD_DIM = 64
CONV_W = 4
LRU_C = 8.0
MOBA_HEADS = 4
MOBA_BLOCK = 256
MOBA_TOPK = 3
MEM_HEADS = 4
NORM_EPS = 1e-6
LOG2E = 1.4426950408889634
NEG_BIG = -1e30

V7X_MXU_DIM = 256
LANES = 128
SUBLANES = 8
V7X_VMEM_BYTES = 64 * 1024 * 1024

AUG_W = 128
AUG_ALIBI = 64
AUG_MASK = 80
assert AUG_W & (AUG_W - 1) == 0 and MOBA_BLOCK & (MOBA_BLOCK - 1) == 0
VT_ROWS = HEAD_DIM + 16
SCORE_LOOKAHEAD = 5
PAST_UNROLL = 32
SCORE_SLOTS = 8
assert MOBA_HEADS <= SCORE_LOOKAHEAD < SCORE_SLOTS and (PAST_UNROLL * MOBA_HEADS) % SCORE_SLOTS == 0
assert (MOBA_HEADS + MEM_HEADS) % SCORE_SLOTS == 0
PROB_SLOTS = 4
PROB_LOOKAHEAD = 1
assert PROB_LOOKAHEAD < SCORE_LOOKAHEAD

TOKEN_TILE = 512
IN_PROJ_TILE = 1024
LRU_TILE = 1024
FFN_CHUNK = 256


def _vmem_limit(resident_bytes):
    return int(min(V7X_VMEM_BYTES - 8 * 1024 * 1024, resident_bytes + 24 * 1024 * 1024))


def _const_spec(shape):
    n = len(shape)
    return pl.BlockSpec(shape, lambda *_: (0,) * n, pipeline_mode=pl.Buffered(1))


def _sigmoid(x):
    return 0.5 * jnp.tanh(0.5 * x) + 0.5


def _rms_norm(x, g):
    ms = jnp.mean(x * x, axis=-1, keepdims=True)
    return x * lax.rsqrt(ms + NORM_EPS) * g


def _group_rms_norm(u, bd, g, group):
    sq = u * u
    hi = sq.astype(BF16)
    lo = (sq - hi.astype(F32)).astype(BF16)
    parts = []
    for c in range(u.shape[1] // V7X_MXU_DIM):
        sl = slice(c * V7X_MXU_DIM, (c + 1) * V7X_MXU_DIM)
        parts.append(jnp.dot(hi[:, sl], bd, preferred_element_type=F32)
                     + jnp.dot(lo[:, sl], bd, preferred_element_type=F32))
    ss = parts[0] if len(parts) == 1 else jnp.concatenate(parts, axis=1)
    return u * lax.rsqrt(ss * (1.0 / group) + NORM_EPS) * g


def _swiglu_half_step(x, g_ref, w_in_ref, w_out_ref, act_ref):
    d_ff = w_out_ref.shape[0]
    xn = _rms_norm(x, g_ref[...]).astype(BF16)
    for c in range(d_ff // FFN_CHUNK):
        lo = c * FFN_CHUNK
        a = jnp.dot(xn, w_in_ref[:, lo:lo + FFN_CHUNK].astype(BF16), preferred_element_type=F32)
        b = jnp.dot(xn, w_in_ref[:, d_ff + lo:d_ff + lo + FFN_CHUNK].astype(BF16),
                    preferred_element_type=F32)
        act_ref[:, lo:lo + FFN_CHUNK] = (a * _sigmoid(a) * b).astype(BF16)
    y = jnp.dot(act_ref[...], w_out_ref[...].astype(BF16), preferred_element_type=F32)
    return x + 0.5 * y


def _ffn_kernel(x_ref, g_ref, w_in_ref, w_out_ref, o_ref, act_ref):
    o_ref[...] = _swiglu_half_step(x_ref[...], g_ref, w_in_ref, w_out_ref, act_ref)


def _ffn(x, g, w_in, w_out):
    n, d = x.shape
    d_ff = w_out.shape[0]
    tm = TOKEN_TILE
    resident = sum(w.size * w.dtype.itemsize for w in (w_in, w_out))
    return pl.pallas_call(
        _ffn_kernel,
        grid=(n // tm,),
        in_specs=[pl.BlockSpec((tm, d), lambda i: (i, 0)),
                  _const_spec((1, d)), _const_spec(w_in.shape), _const_spec(w_out.shape)],
        out_specs=pl.BlockSpec((tm, d), lambda i: (i, 0)),
        out_shape=jax.ShapeDtypeStruct((n, d), F32),
        scratch_shapes=[pltpu.VMEM((tm, d_ff), BF16)],
        compiler_params=pltpu.CompilerParams(dimension_semantics=("arbitrary",),
                                             vmem_limit_bytes=_vmem_limit(resident)),
        name="ffn1",
    )(x, g, w_in, w_out)


def _out_ffn_kernel(x_ref, ylru_ref, ymoba_lo_ref, ymoba_hi_ref, ymem_lo_ref, ymem_hi_ref,
                    wo_ref, g_ref, w_in_ref, w_out_ref, o_ref, act_ref, *, tiles_per_seq):
    n_lru = ylru_ref.shape[1]
    n_moba = ymoba_lo_ref.shape[1]
    first_half = pl.program_id(0) % tiles_per_seq < tiles_per_seq // 2
    y_moba = jnp.where(first_half, ymoba_lo_ref[...], ymoba_hi_ref[...])
    y_mem = jnp.where(first_half, ymem_lo_ref[...], ymem_hi_ref[...])
    x2 = (x_ref[...]
          + jnp.dot(ylru_ref[...], wo_ref[0:n_lru, :], preferred_element_type=F32)
          + jnp.dot(y_moba, wo_ref[n_lru:n_lru + n_moba, :], preferred_element_type=F32)
          + jnp.dot(y_mem, wo_ref[n_lru + n_moba:, :], preferred_element_type=F32))
    o_ref[...] = _swiglu_half_step(x2, g_ref, w_in_ref, w_out_ref, act_ref)


def _out_ffn(x1, y_lru, y_moba, y_mem, w_o, g, w_in, w_out, seq):
    n, d = x1.shape
    d_ff = w_out.shape[0]
    tm = TOKEN_TILE
    tps = seq // tm
    half = tps // 2
    assert tps % 2 == 0
    resident = sum(w.size * w.dtype.itemsize for w in (w_in, w_out, w_o))

    def tile(w):
        return pl.BlockSpec((tm, w), lambda i: (i, 0))

    def lo_tile(w):
        return pl.BlockSpec((tm, w), lambda i: ((i // tps) * half + jnp.minimum(i % tps, half - 1), 0))

    def hi_tile(w):
        return pl.BlockSpec((tm, w), lambda i: ((i // tps) * half + jnp.maximum(i % tps - half, 0), 0))

    wa = y_moba[0].shape[1]
    wm = y_mem[0].shape[1]
    return pl.pallas_call(
        functools.partial(_out_ffn_kernel, tiles_per_seq=tps),
        grid=(n // tm,),
        in_specs=[tile(d), tile(y_lru.shape[1]), lo_tile(wa), hi_tile(wa), lo_tile(wm), hi_tile(wm),
                  _const_spec(w_o.shape), _const_spec((1, d)),
                  _const_spec(w_in.shape), _const_spec(w_out.shape)],
        out_specs=tile(d),
        out_shape=jax.ShapeDtypeStruct((n, d), F32),
        scratch_shapes=[pltpu.VMEM((tm, d_ff), BF16)],
        compiler_params=pltpu.CompilerParams(dimension_semantics=("arbitrary",),
                                             vmem_limit_bytes=_vmem_limit(resident)),
        name="out_ffn2",
    )(x1, y_lru, *y_moba, *y_mem, w_o, g, w_in, w_out)


def _mem_kv_kernel(mem_ref, g_ref, wk_ref, wv_ref, bd_ref, gk_ref, kc_ref, vct_ref):
    batch, _, mlen = vct_ref.shape
    mn = _rms_norm(mem_ref[...], g_ref[...]).astype(BF16)
    k = jnp.dot(mn, wk_ref[...], preferred_element_type=F32)
    kc_ref[...] = _group_rms_norm(k, bd_ref[...], gk_ref[...], HEAD_DIM).astype(BF16)
    v = jnp.dot(mn, wv_ref[...], preferred_element_type=F32)
    ones = jnp.ones((VT_ROWS - HEAD_DIM, mlen), BF16)
    for b in range(batch):
        vt = v[b * mlen:(b + 1) * mlen, :].T.astype(BF16)
        for h in range(MEM_HEADS):
            vct_ref[b, h * VT_ROWS:h * VT_ROWS + HEAD_DIM, :] = vt[h * HEAD_DIM:(h + 1) * HEAD_DIM, :]
            vct_ref[b, h * VT_ROWS + HEAD_DIM:(h + 1) * VT_ROWS, :] = ones


def _mem_kv(mem2d, g, w_k, w_v, bd128, gk, batch):
    m = mem2d.shape[0]
    return pl.pallas_call(
        _mem_kv_kernel,
        out_shape=(jax.ShapeDtypeStruct((m, w_k.shape[1]), BF16),
                   jax.ShapeDtypeStruct((batch, MEM_HEADS * VT_ROWS, m // batch), BF16)),
        name="mem_kv",
    )(mem2d, g, w_k, w_v, bd128, gk)


def _in_proj_kernel(x_ref, g_ref, w_lru_ref, w_qvc_ref, w_k_ref,
                    gq_ref, gk_ref, gc_ref, bd64_ref, bd128_ref, alibi_ref,
                    lrux_ref, lrug_ref, qt_ref, kaug_ref, vt_ref, kmean_ref, cqt_ref, *, seq):
    tm = x_ref.shape[0]
    n_lru = lrux_ref.shape[1]
    xn = _rms_norm(x_ref[...], g_ref[...]).astype(BF16)

    wq = qt_ref.shape[1]
    qvc = jnp.dot(xn, w_qvc_ref[...], preferred_element_type=F32)

    cq = _group_rms_norm(qvc[:, 2 * wq:], bd64_ref[...], gc_ref[...], HEAD_DIM) * (HEAD_DIM ** -0.5 * LOG2E)
    for r in range(tm // MOBA_BLOCK):
        cqt_ref[r] = cq[r * MOBA_BLOCK:(r + 1) * MOBA_BLOCK, :].T.astype(BF16)

    q = _group_rms_norm(qvc[:, :wq], bd64_ref[...], gq_ref[...], HEAD_DIM)
    v = qvc[:, wq:2 * wq]
    ones = jnp.ones((VT_ROWS - HEAD_DIM, MOBA_BLOCK), BF16)
    for r in range(tm // MOBA_BLOCK):
        rows = slice(r * MOBA_BLOCK, (r + 1) * MOBA_BLOCK)
        qt_ref[r] = q[rows, :].T
        vt = v[rows, :].T.astype(BF16)
        for h in range(MOBA_HEADS):
            vt_ref[r, h * VT_ROWS:h * VT_ROWS + HEAD_DIM, :] = vt[h * HEAD_DIM:(h + 1) * HEAD_DIM, :]
            vt_ref[r, h * VT_ROWS + HEAD_DIM:(h + 1) * VT_ROWS, :] = ones

    k = jnp.dot(xn, w_k_ref[...], preferred_element_type=F32)
    k = _group_rms_norm(k, bd128_ref[...], gk_ref[...], HEAD_DIM)
    kmean_ref[...] = jnp.zeros(kmean_ref.shape, F32)
    for r in range(tm // MOBA_BLOCK):
        rows = slice(r * MOBA_BLOCK, (r + 1) * MOBA_BLOCK)
        kmean_ref[0, r:r + 1, :] = jnp.sum(k[rows, :], axis=0, keepdims=True) * (1.0 / MOBA_BLOCK)

    shape = k.shape
    row = lax.broadcasted_iota(jnp.int32, shape, 0)
    lane = lax.broadcasted_iota(jnp.int32, shape, 1) & (AUG_W - 1)
    t = (pl.program_id(0) % (seq // tm)) * tm + row
    bias = t.astype(F32) * alibi_ref[...]
    b_hi = bias.astype(BF16).astype(F32)
    b_mid = (bias - b_hi).astype(BF16).astype(F32)
    b_lo = bias - b_hi - b_mid
    pieces = jnp.where(lane == AUG_ALIBI, b_hi, jnp.where(lane == AUG_ALIBI + 1, b_mid, b_lo))
    onehot = jnp.where(lane - AUG_MASK == lax.shift_right_logical(t, MOBA_BLOCK.bit_length() - 1), 1.0, 0.0)
    is_alibi = (lane >= AUG_ALIBI) & (lane < AUG_ALIBI + 3)
    kaug_ref[...] = (k + jnp.where(is_alibi, pieces, onehot)).astype(BF16)

    lrux_ref[...] = jnp.dot(xn, w_lru_ref[:, :n_lru], preferred_element_type=F32)
    lrug_ref[...] = jnp.dot(xn, w_lru_ref[:, n_lru:], preferred_element_type=F32)


def _in_proj(x1, g, w_lru, w_qvc, w_k, gq, gk, gc, bd64, bd128, alibi, seq):
    n, d = x1.shape
    tm = IN_PROJ_TILE
    rb = tm // MOBA_BLOCK
    nblk = n // MOBA_BLOCK
    n_lru = w_lru.shape[1] // 2
    wq = w_qvc.shape[1] // 3
    wk = w_k.shape[1]
    consts = (g, w_lru, w_qvc, w_k, gq, gk, gc, bd64, bd128, alibi)
    resident = sum(c.size * c.dtype.itemsize for c in consts)
    return pl.pallas_call(
        functools.partial(_in_proj_kernel, seq=seq),
        grid=(n // tm,),
        in_specs=[pl.BlockSpec((tm, d), lambda i: (i, 0))] + [_const_spec(c.shape) for c in consts],
        out_specs=[pl.BlockSpec((tm, n_lru), lambda i: (i, 0)),
                   pl.BlockSpec((tm, n_lru), lambda i: (i, 0)),
                   pl.BlockSpec((rb, wq, MOBA_BLOCK), lambda i: (i, 0, 0)),
                   pl.BlockSpec((tm, wk), lambda i: (i, 0)),
                   pl.BlockSpec((rb, MOBA_HEADS * VT_ROWS, MOBA_BLOCK), lambda i: (i, 0, 0)),
                   pl.BlockSpec((1, 8, wk), lambda i: (i, 0, 0)),
                   pl.BlockSpec((rb, wq, MOBA_BLOCK), lambda i: (i, 0, 0))],
        out_shape=(jax.ShapeDtypeStruct((n, n_lru), F32),
                   jax.ShapeDtypeStruct((n, n_lru), F32),
                   jax.ShapeDtypeStruct((nblk, wq, MOBA_BLOCK), F32),
                   jax.ShapeDtypeStruct((n, wk), BF16),
                   jax.ShapeDtypeStruct((nblk, MOBA_HEADS * VT_ROWS, MOBA_BLOCK), BF16),
                   jax.ShapeDtypeStruct((n // tm, 8, wk), F32),
                   jax.ShapeDtypeStruct((nblk, wq, MOBA_BLOCK), BF16)),
        compiler_params=pltpu.CompilerParams(dimension_semantics=("arbitrary",),
                                             vmem_limit_bytes=_vmem_limit(resident)),
        name="in_proj",
    )(x1, *consts)


def _gelu_tanh(x):
    c = math.sqrt(2.0 / math.pi)
    return 0.5 * x * (1.0 + jnp.tanh(c * (x + 0.044715 * (x * x * x))))


def _lru_kernel(x_ref, g_ref, cw_ref, cb_ref, wa_ref, ba_ref, wx_ref, bx_ref, lam_ref,
                o_ref, xpad_ref, a_ref, b_ref, hl_ref, carry_ref):
    tl, c = x_ref.shape
    half = V7X_MXU_DIM
    ti = pl.program_id(1)

    @pl.when(ti == 0)
    def _():
        xpad_ref[0:8, :] = jnp.zeros((8, c), F32)
        carry_ref[...] = jnp.zeros(carry_ref.shape, F32)

    xpad_ref[8:8 + tl, :] = x_ref[...]
    n_grp = tl // SUBLANES
    sub1 = lax.broadcasted_iota(jnp.int32, (n_grp, SUBLANES, LANES), 1)
    parts = []
    for k in range(c // LANES):
        lanes = slice(k * LANES, (k + 1) * LANES)
        xg = xpad_ref[:, lanes].reshape(n_grp + 1, SUBLANES, LANES)
        acc = cb_ref[:, lanes] + cw_ref[CONV_W - 1:CONV_W, lanes] * xg[1:]
        for d in range(1, CONV_W):
            rolled = pltpu.roll(xg, d, axis=1)
            delayed = jnp.where(sub1 >= d, rolled[1:], rolled[:-1])
            acc = acc + cw_ref[CONV_W - 1 - d:CONV_W - d, lanes] * delayed
        parts.append(acc.reshape(tl, LANES))
    xb = jnp.concatenate(parts, axis=1)
    xpad_ref[0:8, :] = x_ref[tl - 8:tl, :]

    xbb = xb.astype(BF16)

    def gate_tanh(w_ref, bias_ref):
        z_half = jnp.concatenate(
            [jnp.dot(xbb[:, :half], w_ref[0], preferred_element_type=F32),
             jnp.dot(xbb[:, half:], w_ref[1], preferred_element_type=F32)], axis=1)
        return jnp.tanh(z_half + bias_ref[...])

    neg_lam = -lam_ref[...]
    softplus = jnp.maximum(neg_lam, 0.0) + jnp.log1p(jnp.exp(-jnp.abs(neg_lam)))
    half_rate = (-0.5 * LRU_C) * softplus
    log_a = half_rate * gate_tanh(wa_ref, ba_ref) + half_rate
    i = 0.5 * gate_tanh(wx_ref, bx_ref) + 0.5
    a = jnp.exp(log_a)
    one_minus_a2 = 1.0 - a * a
    mult = jnp.where(one_minus_a2 > 0.0, one_minus_a2 * lax.rsqrt(one_minus_a2), 0.0)
    row = lax.broadcasted_iota(jnp.int32, (tl, c), 0)
    start_row = jnp.where(ti == 0, 0, -1)
    mult = jnp.where(row == start_row, 1.0, mult)
    bt = mult * i * xb

    groups = tl // SUBLANES
    sub = lax.broadcasted_iota(jnp.int32, (groups, SUBLANES, LANES), 1)
    for k in range(c // LANES):
        lanes = slice(k * LANES, (k + 1) * LANES)
        a3 = a[:, lanes].reshape(groups, SUBLANES, LANES)
        b3 = bt[:, lanes].reshape(groups, SUBLANES, LANES)
        d = 1
        while d < SUBLANES:
            has_prev = sub >= d
            a_prev = pltpu.roll(a3, d, axis=1)
            b_prev = pltpu.roll(b3, d, axis=1)
            b3 = b3 + jnp.where(has_prev, a3, 0.0) * b_prev
            a3 = jnp.where(has_prev, a3 * a_prev, a3)
            d *= 2
        a_ref[:, lanes] = a3.reshape(tl, LANES)
        b_ref[:, lanes] = b3.reshape(tl, LANES)

    h = carry_ref[...]
    for grp in range(groups):
        rows = slice(grp * SUBLANES, (grp + 1) * SUBLANES)
        h_last = jnp.broadcast_to(h[SUBLANES - 1:SUBLANES, :], (SUBLANES, c))
        h = a_ref[rows, :] * h_last + b_ref[rows, :]
        hl_ref[rows, :] = h
    carry_ref[...] = h

    o_ref[...] = (hl_ref[...] * _gelu_tanh(g_ref[...])).astype(o_ref.dtype)


def _lru(lru_x, lru_g, cw, cb, wa, ba, wx, bx, lam, batch, seq):
    n, c = lru_x.shape
    tl = LRU_TILE
    nt = seq // tl
    consts = (cw, cb, wa, ba, wx, bx, lam)
    tile = pl.BlockSpec((tl, c), lambda b, t: (b * nt + t, 0))
    return pl.pallas_call(
        _lru_kernel,
        grid=(batch, nt),
        in_specs=[tile, tile] + [_const_spec(w.shape) for w in consts],
        out_specs=tile,
        out_shape=jax.ShapeDtypeStruct((n, c), BF16),
        scratch_shapes=[pltpu.VMEM((tl + SUBLANES, c), F32)]
        + [pltpu.VMEM((tl, c), F32)] * 3
        + [pltpu.VMEM((SUBLANES, c), F32)],
        compiler_params=pltpu.CompilerParams(dimension_semantics=("arbitrary", "arbitrary")),
        name="lru",
    )(lru_x, lru_g, *consts)


def _moba_kernel(qta_ref, qtb_ref, cqta_ref, cqtb_ref, kaug_ref, vt_ref, kmean_ref, kc_ref, vct_ref,
                 oa_ref, ob_ref, oma_ref, omb_ref,
                 qaug_ref, qown_ref, qmem_ref, s_ref, p_ref, alpha_ref, m_ref, acc_ref, accm_ref,
                 ot_ref, otm_ref):
    step = pl.program_id(1)
    nblk = kmean_ref.shape[1]
    blk = MOBA_BLOCK
    q_blocks = (step, nblk - 1 - step)
    qt_refs = (qta_ref, qtb_ref)
    cqt_refs = (cqta_ref, cqtb_ref)

    def build_query_operands(side, h):
        sh = side * MOBA_HEADS + h
        qt = qt_refs[side][0, 0, h * HEAD_DIM:(h + 1) * HEAD_DIM, :]
        qs = (qt * (HEAD_DIM ** -0.5 * LOG2E)).astype(BF16)
        sub = lax.broadcasted_iota(jnp.int32, (AUG_MASK - AUG_ALIBI, blk), 0)
        ones = jnp.where(sub < 3, 1.0, 0.0).astype(BF16)
        for ref in (qaug_ref, qown_ref):
            ref[sh, 0:HEAD_DIM, :] = qs
            ref[sh, AUG_ALIBI:AUG_MASK, :] = ones
        qown_ref[sh, AUG_MASK:, :] = jnp.zeros((AUG_W - AUG_MASK, blk), BF16)
        qmem_ref[sh, 0:HEAD_DIM, :] = cqt_refs[side][0, 0, h * HEAD_DIM:(h + 1) * HEAD_DIM, :]
        qmem_ref[sh, HEAD_DIM:, :] = jnp.zeros((AUG_W - HEAD_DIM, blk), BF16)

    def build_block_mask(side, h):
        sh = side * MOBA_HEADS + h
        blk_id = lax.broadcasted_iota(jnp.int32, (nblk, blk), 0).astype(F32)
        qt = qt_refs[side][0, 0, h * HEAD_DIM:(h + 1) * HEAD_DIM, :]
        gate = jnp.dot(kmean_ref[0, :, h * AUG_W:h * AUG_W + HEAD_DIM], qt,
                       preferred_element_type=F32, precision=lax.Precision.HIGHEST)
        gate = jnp.where(blk_id < q_blocks[side].astype(F32), gate, -jnp.inf)
        keep = blk_id < 0.0
        for _ in range(MOBA_TOPK):
            best = jnp.max(gate, axis=0, keepdims=True)
            first = jnp.min(jnp.where(gate == best, blk_id, float(nblk)), axis=0, keepdims=True)
            pick = (blk_id == first) & (best > -jnp.inf)
            keep = keep | pick
            gate = jnp.where(pick, -jnp.inf, gate)
        qaug_ref[sh, AUG_MASK:AUG_MASK + nblk, :] = jnp.where(keep, 0.0, NEG_BIG).astype(BF16)
        qaug_ref[sh, AUG_MASK + nblk:, :] = jnp.zeros((AUG_W - AUG_MASK - nblk, blk), BF16)

    kk = lax.broadcasted_iota(jnp.int32, (blk, blk), 0)
    qq = lax.broadcasted_iota(jnp.int32, (blk, blk), 1)
    causal = kk <= qq

    lead_groups = (("mem", 0), ("mem", 1), ("own", 0), ("own", 1))
    lead = len(lead_groups) * MOBA_HEADS
    n_trips = (nblk - 1 + PAST_UNROLL - 1) // PAST_UNROLL
    single_pass = n_trips == 1
    per_trip = (nblk - 1 if single_pass else PAST_UNROLL) * MOBA_HEADS

    def stream_item(trip, n):
        if n < 0:
            idx = n + lead
            kind, side = lead_groups[idx // MOBA_HEADS]
            return kind, side, q_blocks[side], idx % MOBA_HEADS, idx % SCORE_SLOTS
        if n >= per_trip:
            trip, n = trip + 1, n - per_trip
        pos = trip * PAST_UNROLL + n // MOBA_HEADS
        side = (pos >= step).astype(jnp.int32)
        block = jnp.minimum(pos - side * step, nblk - 1)
        return "past", side, block, n % MOBA_HEADS, (n + lead) % SCORE_SLOTS

    def scores(kind, side, j, h, slot):
        lanes = slice(h * AUG_W, (h + 1) * AUG_W)
        sh = side * MOBA_HEADS + h
        if kind == "mem":
            keys, q_ref = kc_ref[0, :, lanes], qmem_ref
        else:
            keys, q_ref = kaug_ref[0, j, :, lanes], (qown_ref if kind == "own" else qaug_ref)
        s_ref[slot] = jnp.dot(keys, q_ref[sh], preferred_element_type=F32)

    def probabilities(kind, side, j, h, slot):
        s = s_ref[slot]
        sh = side * MOBA_HEADS + h
        if kind == "past":
            m_old = m_ref[sh]
            m_new = jnp.maximum(m_old, jnp.max(s, axis=0, keepdims=True))
            alpha_ref[slot % PROB_SLOTS] = jnp.exp2(m_old - m_new)
        else:
            if kind == "own":
                s = jnp.where(causal, s, NEG_BIG)
            m_new = jnp.max(s, axis=0, keepdims=True)
        p_ref[slot % PROB_SLOTS] = jnp.exp2(s - m_new).astype(BF16)
        if kind != "mem":
            m_ref[sh] = m_new

    def accumulate(kind, side, j, h, slot):
        rows = slice(h * VT_ROWS, (h + 1) * VT_ROWS)
        sh = side * MOBA_HEADS + h
        p = p_ref[slot % PROB_SLOTS]
        if kind == "mem":
            accm_ref[sh] = jnp.dot(vct_ref[0, rows, :], p, preferred_element_type=F32)
        else:
            pv = jnp.dot(vt_ref[0, j, rows, :], p, preferred_element_type=F32)
            acc_ref[sh] = pv if kind == "own" else alpha_ref[slot % PROB_SLOTS] * acc_ref[sh] + pv

    for h in range(MOBA_HEADS):
        for side in range(2):
            build_query_operands(side, h)
            build_block_mask(side, h)
    def ahead(stage, trip, n):
        if not (single_pass and n >= per_trip):
            stage(*stream_item(trip, n))

    for n in range(-lead, -lead + SCORE_LOOKAHEAD):
        scores(*stream_item(0, n))
    for n in range(-lead, -lead + PROB_LOOKAHEAD):
        probabilities(*stream_item(0, n))
    for n in range(-lead, 0):
        ahead(scores, 0, n + SCORE_LOOKAHEAD)
        ahead(probabilities, 0, n + PROB_LOOKAHEAD)
        accumulate(*stream_item(0, n))

    def past_trip(trip, carry):
        for n in range(per_trip):
            ahead(scores, trip, n + SCORE_LOOKAHEAD)
            ahead(probabilities, trip, n + PROB_LOOKAHEAD)
            accumulate(*stream_item(trip, n))
        return carry

    if single_pass:
        past_trip(0, 0)
    else:
        lax.fori_loop(0, n_trips, past_trip, 0)

    for side, o_ref, om_ref in ((0, oa_ref, oma_ref), (1, ob_ref, omb_ref)):
        for ref, t_ref, out_ref in ((acc_ref, ot_ref, o_ref), (accm_ref, otm_ref, om_ref)):
            for h in range(MOBA_HEADS):
                sh = side * MOBA_HEADS + h
                t_ref[side, h * HEAD_DIM:(h + 1) * HEAD_DIM, :] = (ref[sh, 0:HEAD_DIM, :]
                                                                   / ref[sh, HEAD_DIM:HEAD_DIM + 1, :])
            out_ref[...] = t_ref[side].T.astype(out_ref.dtype)


def _moba(qt, cqt, kaug, vt, kmean, kc, vct, batch, seq):
    nblk = seq // MOBA_BLOCK
    width = MOBA_HEADS * HEAD_DIM
    aug = MOBA_HEADS * AUG_W
    vt_rows = MOBA_HEADS * VT_ROWS
    mlen = vct.shape[2]
    assert mlen == MOBA_BLOCK and MEM_HEADS == MOBA_HEADS, "memory items reuse the key-block buffers"
    qt = qt.reshape(batch, nblk, width, MOBA_BLOCK)
    cqt = cqt.reshape(batch, nblk, width, MOBA_BLOCK)
    kaug = kaug.reshape(batch, nblk, MOBA_BLOCK, aug)
    vt = vt.reshape(batch, nblk, vt_rows, MOBA_BLOCK)
    kc = kc.reshape(batch, mlen, aug)
    resident = 2 * (kaug.size // batch + vt.size // batch) * 2
    assert nblk % 2 == 0
    half = nblk // 2
    sides = 2 * MOBA_HEADS
    qblock_a = pl.BlockSpec((1, 1, width, MOBA_BLOCK), lambda b, s: (b, s, 0, 0))
    qblock_b = pl.BlockSpec((1, 1, width, MOBA_BLOCK), lambda b, s: (b, nblk - 1 - s, 0, 0))
    out_a = pl.BlockSpec((MOBA_BLOCK, width), lambda b, s: (b * half + s, 0))
    out_b = pl.BlockSpec((MOBA_BLOCK, width), lambda b, s: (b * half + half - 1 - s, 0))
    out_shape = jax.ShapeDtypeStruct((batch * half * MOBA_BLOCK, width), BF16)
    y_lo, y_hi, m_lo, m_hi = pl.pallas_call(
        _moba_kernel,
        grid=(batch, half),
        in_specs=[qblock_a, qblock_b, qblock_a, qblock_b,
                  pl.BlockSpec((1, nblk, MOBA_BLOCK, aug), lambda b, s: (b, 0, 0, 0)),
                  pl.BlockSpec((1, nblk, vt_rows, MOBA_BLOCK), lambda b, s: (b, 0, 0, 0)),
                  pl.BlockSpec((1, nblk, aug), lambda b, s: (b, 0, 0)),
                  pl.BlockSpec((1, mlen, aug), lambda b, s: (b, 0, 0)),
                  pl.BlockSpec((1, vt_rows, mlen), lambda b, s: (b, 0, 0))],
        out_specs=(out_a, out_b, out_a, out_b),
        out_shape=(out_shape,) * 4,
        scratch_shapes=[pltpu.VMEM((sides, AUG_W, MOBA_BLOCK), BF16),
                        pltpu.VMEM((sides, AUG_W, MOBA_BLOCK), BF16),
                        pltpu.VMEM((sides, AUG_W, MOBA_BLOCK), BF16),
                        pltpu.VMEM((SCORE_SLOTS, MOBA_BLOCK, MOBA_BLOCK), F32),
                        pltpu.VMEM((PROB_SLOTS, MOBA_BLOCK, MOBA_BLOCK), BF16),
                        pltpu.VMEM((PROB_SLOTS, 1, MOBA_BLOCK), F32),
                        pltpu.VMEM((sides, 1, MOBA_BLOCK), F32),
                        pltpu.VMEM((sides, VT_ROWS, MOBA_BLOCK), F32),
                        pltpu.VMEM((sides, VT_ROWS, MOBA_BLOCK), F32),
                        pltpu.VMEM((2, width, MOBA_BLOCK), F32),
                        pltpu.VMEM((2, width, MOBA_BLOCK), F32)],
        compiler_params=pltpu.CompilerParams(dimension_semantics=("arbitrary", "arbitrary"),
                                             vmem_limit_bytes=_vmem_limit(resident)),
        name="moba",
    )(qt, qt, cqt, cqt, kaug, vt, kmean, kc, vct)

    return (y_lo, y_hi), (m_lo, m_hi)


def _block_diag_ones(group):
    idx = jnp.arange(V7X_MXU_DIM) // group
    return (idx[:, None] == idx[None, :]).astype(BF16)


def _pack_block_diag(w):
    nb, bw, _ = w.shape
    per = V7X_MXU_DIM // bw
    w = w.reshape(nb // per, per, bw, bw)
    eye = jnp.eye(per, dtype=w.dtype)
    return jnp.einsum("gpij,pq->gpiqj", w, eye).reshape(nb // per, V7X_MXU_DIM, V7X_MXU_DIM)


def _pad_heads(w, heads):
    lead = w.shape[:-1]
    w = w.reshape(lead + (heads, HEAD_DIM))
    w = jnp.pad(w, [(0, 0)] * len(lead) + [(0, 0), (0, AUG_W - HEAD_DIM)])
    return w.reshape(lead + (heads * AUG_W,))


def _layer(x, mem, ffn1_norm, ffn1_w_in, ffn1_w_out, mix_norm, mem_norm, w_in,
           lru_conv_w, lru_conv_b, lru_a_w, lru_a_b, lru_x_w, lru_x_b, lru_lambda,
           moba_q_norm, moba_k_norm, mem_w_kv, mem_q_norm, mem_k_norm, w_out,
           ffn2_norm, ffn2_w_in, ffn2_w_out):
    batch, seq, d = x.shape
    n = batch * seq
    n_lru = lru_lambda.shape[0]
    wq = MOBA_HEADS * HEAD_DIM
    wc = MEM_HEADS * HEAD_DIM
    row = lambda v: v.reshape(1, -1).astype(F32)

    bd64 = _block_diag_ones(HEAD_DIM)
    bd128 = _block_diag_ones(AUG_W)

    w_in_b = w_in.astype(BF16)
    o = 2 * n_lru
    w_lru = w_in_b[:, :o]
    w_k = _pad_heads(w_in_b[:, o + wq:o + 2 * wq], MOBA_HEADS)
    w_qvc = jnp.concatenate([w_in_b[:, o:o + wq], w_in_b[:, o + 2 * wq:]], axis=1)
    gq = row(jnp.tile(moba_q_norm, MOBA_HEADS))
    gk = row(_pad_heads(jnp.tile(moba_k_norm, MOBA_HEADS), MOBA_HEADS))
    gc = row(jnp.tile(mem_q_norm, MEM_HEADS))
    gck = row(_pad_heads(jnp.tile(mem_k_norm, MEM_HEADS), MEM_HEADS))
    w_kv_b = mem_w_kv.astype(BF16)
    w_ck = _pad_heads(w_kv_b[:, :wc], MEM_HEADS)
    w_cv = w_kv_b[:, wc:]
    slopes = 2.0 ** (-8.0 * jnp.arange(1, MOBA_HEADS + 1, dtype=F32) / MOBA_HEADS)
    lane = jnp.arange(AUG_W)
    is_alibi = (lane >= AUG_ALIBI) & (lane < AUG_ALIBI + 3)
    alibi = row(jnp.where(is_alibi[None, :], slopes[:, None] * LOG2E, 0.0))

    x2d = x.reshape(n, d)
    kc, vct = _mem_kv(mem.reshape(-1, d), row(mem_norm), w_ck, w_cv, bd128, gck, batch)

    x1 = _ffn(x2d, row(ffn1_norm), ffn1_w_in, ffn1_w_out)

    lru_x, lru_g, qt, kaug, vt, kmean, cqt = _in_proj(
        x1, row(mix_norm), w_lru, w_qvc, w_k, gq, gk, gc, bd64, bd128, alibi, seq)
    kmean = kmean[:, :IN_PROJ_TILE // MOBA_BLOCK, :].reshape(batch, seq // MOBA_BLOCK, -1)

    y_lru = _lru(lru_x, lru_g, lru_conv_w.astype(F32), row(lru_conv_b),
                 _pack_block_diag(0.5 * lru_a_w).astype(BF16), row(0.5 * lru_a_b),
                 _pack_block_diag(0.5 * lru_x_w).astype(BF16), row(0.5 * lru_x_b),
                 row(lru_lambda), batch, seq)
    y_moba, y_mem = _moba(qt, cqt, kaug, vt, kmean, kc, vct, batch, seq)

    out = _out_ffn(x1, y_lru, y_moba, y_mem, w_out.astype(BF16), row(ffn2_norm),
                   ffn2_w_in, ffn2_w_out, seq)
    return out.reshape(batch, seq, d)


def kernel(x, mem, ffn1_norm, ffn1_w_in, ffn1_w_out, mix_norm, mem_norm, w_in, lru_conv_w, lru_conv_b,
           lru_a_w, lru_a_b, lru_x_w, lru_x_b, lru_lambda, moba_q_norm, moba_k_norm, mem_w_kv,
           mem_q_norm, mem_k_norm, w_out, ffn2_norm, ffn2_w_in, ffn2_w_out):
    params = (ffn1_norm, ffn1_w_in, ffn1_w_out, mix_norm, mem_norm, w_in, lru_conv_w, lru_conv_b,
              lru_a_w, lru_a_b, lru_x_w, lru_x_b, lru_lambda, moba_q_norm, moba_k_norm, mem_w_kv,
              mem_q_norm, mem_k_norm, w_out, ffn2_norm, ffn2_w_in, ffn2_w_out)
    for layer in range(ffn1_norm.shape[0]):
        x = _layer(x, mem, *(p[layer] for p in params))
    return x
```
